```python
import math
import jax
import jax.numpy as jnp
from jax import lax
import numpy as np

D_MODEL = 2048
BATCH = 2
SEQ = 4096
DEPTH = 1
DEC_BATCH = 8
DEC_SEQ = 1
PAST_LEN = 16384
PAGE_SIZE = 128

HEAD_DIM = 128
N_HEADS_TOTAL = D_MODEL // HEAD_DIM
D_MIX = N_HEADS_TOTAL * HEAD_DIM
H_A = N_HEADS_TOTAL // 2
DH_A = HEAD_DIM // 2
D_A = H_A * HEAD_DIM
H_B = N_HEADS_TOTAL - H_A
HKV_B = max(1, H_B // 4)
D_B = H_B * HEAD_DIM
D_BKV = HKV_B * HEAD_DIM
H_IDX = 4
D_IDX = 64
D_PLE = 256
TOPK_MAX = 256
ROPE_THETA = 500000.0
ROPE_FRAC = 4
RMS_EPS = 1e-6
Q_BLOCK = 128
SPLIT_SIZES = (D_A, D_A, D_A, D_A, D_B, D_BKV, D_BKV, D_B, H_IDX * D_IDX, D_IDX, H_IDX)
N_IN = sum(SPLIT_SIZES)

kernel_name = "hymba_diff_dsa_decoder_step"


def _rmsnorm(x, w):
    xf = x.astype(jnp.float32)
    y = xf * lax.rsqrt(jnp.mean(xf * xf, axis=-1, keepdims=True) + RMS_EPS)
    return (y * w.astype(jnp.float32)).astype(x.dtype)


def _rope(x, pos):
    d = x.shape[-1]
    r = d // ROPE_FRAC
    half = r // 2
    inv = ROPE_THETA ** (-(2.0 / r) * jnp.arange(half, dtype=jnp.float32))
    ang = pos.astype(jnp.float32)[:, None] * inv[None, :]
    ang = ang.reshape((pos.shape[0],) + (1,) * (x.ndim - 3) + (half,))
    cos, sin = jnp.cos(ang), jnp.sin(ang)
    xf = x.astype(jnp.float32)
    x1, x2, rest = xf[..., :half], xf[..., half:r], xf[..., r:]
    out = jnp.concatenate([x1 * cos - x2 * sin, x2 * cos + x1 * sin, rest], axis=-1)
    return out.astype(x.dtype)


def _split_cols(z):
    offs = [int(o) for o in np.cumsum(SPLIT_SIZES)[:-1]]
    return jnp.split(z, offs, axis=-1)


def _q_block(t):
    return Q_BLOCK if t % Q_BLOCK == 0 else t


def _to_blocks(a, nb, qb):
    b = a.shape[0]
    return a.reshape((b, nb, qb) + a.shape[2:]).swapaxes(0, 1)


def _from_blocks(o):
    nb, b, qb = o.shape[:3]
    return o.swapaxes(0, 1).reshape((b, nb * qb) + o.shape[3:])


def _diff_attention(q1, q2, k1, k2, v, lam, q_pos, k_pos):
    b, t = q1.shape[:2]
    qb = _q_block(t)
    nb = t // qb
    scale = 1.0 / math.sqrt(q1.shape[-1])

    def block(args):
        a1, a2, qp = args
        vis = (k_pos[None, :] <= qp[:, None])[None, None]
        s1 = jnp.einsum('bqhd,bshd->bhqs', a1, k1).astype(jnp.float32) * scale
        s2 = jnp.einsum('bqhd,bshd->bhqs', a2, k2).astype(jnp.float32) * scale
        s1 = jnp.where(vis, s1, -jnp.inf)
        s2 = jnp.where(vis, s2, -jnp.inf)
        pr = jax.nn.softmax(s1, axis=-1) - lam * jax.nn.softmax(s2, axis=-1)
        return jnp.einsum('bhqs,bshd->bqhd', pr.astype(v.dtype), v)

    out = lax.map(block, (_to_blocks(q1, nb, qb), _to_blocks(q2, nb, qb), q_pos.reshape(nb, qb)))
    return _from_blocks(out)


def _dsa_attention(q, k, v, qi, ki, wi, q_pos, k_pos, topk):
    b, t, h, d = q.shape
    hkv = k.shape[2]
    g = h // hkv
    qb = _q_block(t)
    nb = t // qb
    scale = 1.0 / math.sqrt(d)

    def block(args):
        qq, qqi, ww, qp = args
        rel = jax.nn.relu(jnp.einsum('bqhd,bsd->bqhs', qqi, ki).astype(jnp.float32))
        score = jnp.einsum('bqhs,bqh->bqs', rel, ww.astype(jnp.float32))
        vis = k_pos[None, :] <= qp[:, None]
        score = jnp.where(vis[None], score, -jnp.inf)
        _, idx = lax.top_k(score, topk)
        kg = jax.vmap(lambda kb_, ib: kb_[ib])(k, idx)
        vg = jax.vmap(lambda vb_, ib: vb_[ib])(v, idx)
        ok = k_pos[idx] <= qp[None, :, None]
        s = jnp.einsum('bqngd,bqknd->bqngk', qq.reshape(b, qb, hkv, g, d), kg).astype(jnp.float32) * scale
        s = jnp.where(ok[:, :, None, None, :], s, -jnp.inf)
        pr = jax.nn.softmax(s, axis=-1)
        o = jnp.einsum('bqngk,bqknd->bqngd', pr.astype(v.dtype), vg)
        return o.reshape(b, qb, h, d)

    out = lax.map(block, (_to_blocks(q, nb, qb), _to_blocks(qi, nb, qb),
                          _to_blocks(wi, nb, qb), q_pos.reshape(nb, qb)))
    return _from_blocks(out)


def _layer(x, p, q_pos, k_pos, past, topk, lam_init,
           pre_w, post_w, w_in, lq1, lk1, lq2, lk2, sub_w, w_out, w_pg, b_pg, w_pe):
    b, t, _ = x.shape
    h = _rmsnorm(x, pre_w)
    z = h @ w_in
    qa, ka, va, ga, qs, ks, vs, gs, qi, ki, wi = _split_cols(z)

    qa = _rope(qa.reshape(b, t, H_A, 2, DH_A), q_pos)
    ka = _rope(ka.reshape(b, t, H_A, 2, DH_A), q_pos).reshape(b, t, H_A, HEAD_DIM)
    va = va.reshape(b, t, H_A, HEAD_DIM)
    qs = _rope(qs.reshape(b, t, H_B, HEAD_DIM), q_pos)
    ks = _rope(ks.reshape(b, t, HKV_B, HEAD_DIM), q_pos)
    vs = vs.reshape(b, t, HKV_B, HEAD_DIM)
    qi = _rope(qi.reshape(b, t, H_IDX, D_IDX), q_pos)
    ki = _rope(ki, q_pos)
    new_rows = (ka, va, ks, vs, ki)

    if past is None:
        ka_all, va_all, ks_all, vs_all, ki_all = new_rows
    else:
        ka_all, va_all, ks_all, vs_all, ki_all = [
            jnp.concatenate([pr_, nr_], axis=1) for pr_, nr_ in zip(past, new_rows)]
    s_len = ka_all.shape[1]

    lam = (jnp.exp(jnp.sum(lq1.astype(jnp.float32) * lk1.astype(jnp.float32)))
           - jnp.exp(jnp.sum(lq2.astype(jnp.float32) * lk2.astype(jnp.float32))) + lam_init)
    k_split = ka_all.reshape(b, s_len, H_A, 2, DH_A)
    oa = _diff_attention(qa[..., 0, :], qa[..., 1, :], k_split[..., 0, :], k_split[..., 1, :],
                         va_all, lam, q_pos, k_pos)
    oa = _rmsnorm(oa, sub_w) * (1.0 - lam_init)
    oa = oa.reshape(b, t, D_A) * jax.nn.silu(ga)

    ob = _dsa_attention(qs, ks_all, vs_all, qi, ki_all, wi, q_pos, k_pos, topk)
    ob = ob.reshape(b, t, D_B) * jax.nn.silu(gs)

    o = jnp.concatenate([oa, ob], axis=-1) @ w_out
    x = x + _rmsnorm(o, post_w)
    x = x + jax.nn.sigmoid(x @ w_pg + b_pg) * (p @ w_pe)
    return x, new_rows


def setup_inputs(seed: int = 0) -> dict:
    key = jax.random.key(seed)
    ks = jax.random.split(key, 24)
    n_pages = PAST_LEN // PAGE_SIZE
    n_used = DEC_BATCH * n_pages
    n_pool = n_used + max(1, n_used // 4)

    def nrm(k, shape, s=1.0):
        return s * jax.random.normal(k, shape, jnp.float32)

    page_table = jax.random.permutation(ks[10], n_pool)[:n_used].reshape(DEC_BATCH, n_pages).astype(jnp.int32)
    return {
        'x_prompt': nrm(ks[0], (BATCH, SEQ, D_MODEL)),
        'x_sample': nrm(ks[1], (DEC_BATCH, DEC_SEQ, D_MODEL)),
        'p_prompt': nrm(ks[2], (DEPTH, BATCH, SEQ, D_PLE)),
        'p_sample': nrm(ks[3], (DEPTH, DEC_BATCH, DEC_SEQ, D_PLE)),
        'cache_diff_k': nrm(ks[4], (DEPTH, n_pool, PAGE_SIZE, H_A, HEAD_DIM)),
        'cache_diff_v': nrm(ks[5], (DEPTH, n_pool, PAGE_SIZE, H_A, HEAD_DIM)),
        'cache_dsa_k': nrm(ks[6], (DEPTH, n_pool, PAGE_SIZE, HKV_B, HEAD_DIM)),
        'cache_dsa_v': nrm(ks[7], (DEPTH, n_pool, PAGE_SIZE, HKV_B, HEAD_DIM)),
        'cache_idx_k': nrm(ks[8], (DEPTH, n_pool, PAGE_SIZE, D_IDX)),
        'page_table': page_table,
        'pre_norm_w': 1.0 + nrm(ks[11], (DEPTH, D_MODEL), 0.05),
        'post_norm_w': 1.0 + nrm(ks[12], (DEPTH, D_MODEL), 0.05),
        'w_in': nrm(ks[13], (DEPTH, D_MODEL, N_IN), D_MODEL ** -0.5),
        'lam_q1': nrm(ks[14], (DEPTH, DH_A), 0.1),
        'lam_k1': nrm(ks[15], (DEPTH, DH_A), 0.1),
        'lam_q2': nrm(ks[16], (DEPTH, DH_A), 0.1),
        'lam_k2': nrm(ks[17], (DEPTH, DH_A), 0.1),
        'diff_norm_w': 1.0 + nrm(ks[18], (DEPTH, HEAD_DIM), 0.05),
        'w_out': nrm(ks[19], (DEPTH, D_MIX, D_MODEL), D_MIX ** -0.5),
        'w_ple_gate': nrm(ks[20], (DEPTH, D_MODEL, D_MODEL), D_MODEL ** -0.5),
        'b_ple_gate': nrm(ks[21], (DEPTH, D_MODEL), 0.01),
        'w_ple_proj': nrm(ks[22], (DEPTH, D_PLE, D_MODEL), D_PLE ** -0.5),
    }


def reference(x_prompt, x_sample, p_prompt, p_sample, cache_diff_k, cache_diff_v, cache_dsa_k,
              cache_dsa_v, cache_idx_k, page_table, pre_norm_w, post_norm_w, w_in, lam_q1, lam_k1,
              lam_q2, lam_k2, diff_norm_w, w_out, w_ple_gate, b_ple_gate, w_ple_proj):
    t_p = x_prompt.shape[1]
    t_s = x_sample.shape[1]
    db, n_pages = page_table.shape
    past_len = n_pages * cache_diff_k.shape[2]
    pos_p = jnp.arange(t_p, dtype=jnp.int32)
    pos_s_q = past_len + jnp.arange(t_s, dtype=jnp.int32)
    pos_s_k = jnp.arange(past_len + t_s, dtype=jnp.int32)
    topk_p = min(TOPK_MAX, t_p // 4)
    topk_s = min(TOPK_MAX, (past_len + t_s) // 4)

    hp, hs = x_prompt, x_sample
    rows_p, rows_s = [], []
    for i in range(DEPTH):
        lam_init = 0.8 - 0.6 * math.exp(-0.3 * i)
        weights = (pre_norm_w[i], post_norm_w[i], w_in[i], lam_q1[i], lam_k1[i], lam_q2[i],
                   lam_k2[i], diff_norm_w[i], w_out[i], w_ple_gate[i], b_ple_gate[i], w_ple_proj[i])
        hp, new_p = _layer(hp, p_prompt[i], pos_p, pos_p, None, topk_p, lam_init, *weights)

        def gather(c):
            g = c[i, page_table]
            return g.reshape((db, past_len) + g.shape[3:])

        past = (gather(cache_diff_k), gather(cache_diff_v), gather(cache_dsa_k),
                gather(cache_dsa_v), gather(cache_idx_k))
        hs, new_s = _layer(hs, p_sample[i], pos_s_q, pos_s_k, past, topk_s, lam_init, *weights)
        rows_p.append(new_p)
        rows_s.append(new_s)

    def stack(rows, j):
        return jnp.stack([r[j] for r in rows], axis=0)

    return (hp, hs,
            stack(rows_p, 0), stack(rows_p, 1), stack(rows_p, 2), stack(rows_p, 3), stack(rows_p, 4),
            stack(rows_s, 0), stack(rows_s, 1), stack(rows_s, 2), stack(rows_s, 3), stack(rows_s, 4))
```

```python
import functools
import math

import jax
import jax.numpy as jnp
from jax import lax
from jax.experimental import pallas as pl
from jax.experimental.pallas import tpu as pltpu

F32 = jnp.float32
BF16 = jnp.bfloat16
I32 = jnp.int32

LANES = 128
HEAD_DIM = 128
DH_A = HEAD_DIM // 2
D_IDX = 64
H_IDX = 4
TOPK_MAX = 256
ROPE_THETA = 500000.0
ROPE_FRAC = 4
RMS_EPS = 1e-6
NEG_BIG = -1e30
INT_MIN = -(2 ** 31)
NEG_INF_KEY = INT_MIN + 0x7FFFFF
VMEM_LIMIT = 56 * 1024 * 1024


def _cparams(n_axes):
    return pltpu.CompilerParams(
        dimension_semantics=("arbitrary",) * n_axes, vmem_limit_bytes=VMEM_LIMIT)


def _dot(a, b):
    return jnp.dot(a, b, preferred_element_type=F32)


def _dot_nt(a, b):
    return lax.dot_general(a, b, (((1,), (1,)), ((), ())), preferred_element_type=F32)


def _rope_tables(pos, d):
    r = d // ROPE_FRAC
    half = r // 2
    inv = ROPE_THETA ** (-(2.0 / r) * jnp.arange(half, dtype=F32))
    ang = pos.astype(F32)[:, None] * inv[None, :]
    cos, sin = jnp.cos(ang), jnp.sin(ang)
    t = pos.shape[0]
    ones = jnp.ones((t, d - r), F32)
    zeros_h = jnp.zeros((t, half), F32)
    zeros_r = jnp.zeros((t, d - r), F32)
    c = jnp.concatenate([cos, cos, ones], axis=-1)
    sm = jnp.concatenate([-sin, zeros_h, zeros_r], axis=-1)
    sp = jnp.concatenate([zeros_h, sin, zeros_r], axis=-1)
    rep = LANES // d
    return tuple(jnp.tile(a, (1, rep)) for a in (c, sm, sp)), half


def _rope_chunk(z, c, sm, sp, half):
    return (z * c + pltpu.roll(z, LANES - half, axis=1) * sm
            + pltpu.roll(z, half, axis=1) * sp)


def _silu(z):
    return z * (1.0 / (1.0 + jnp.exp(-z)))


def _rms_rows(x, w):
    return x * lax.rsqrt(jnp.mean(x * x, axis=-1, keepdims=True) + RMS_EPS) * w


def _inproj_a_kernel(x_ref, nw_ref, w_ref, c_ref, sm_ref, sp_ref,
                     qa_ref, ka_ref, va_ref, ga_ref, *, half, d_a):
    h = _rms_rows(x_ref[...], nw_ref[...]).astype(BF16)
    c, sm, sp = c_ref[...], sm_ref[...], sp_ref[...]
    n_chunks = d_a // LANES
    zq = _dot(h, w_ref[:, 0:d_a])
    for j in range(n_chunks):
        sl = slice(j * LANES, (j + 1) * LANES)
        qa_ref[:, sl] = (_rope_chunk(zq[:, sl], c, sm, sp, half) * (1.0 / math.sqrt(DH_A))).astype(BF16)
    zk = _dot(h, w_ref[:, d_a:2 * d_a])
    for j in range(n_chunks):
        sl = slice(j * LANES, (j + 1) * LANES)
        ka_ref[:, sl] = _rope_chunk(zk[:, sl], c, sm, sp, half)
    va_ref[...] = _dot(h, w_ref[:, 2 * d_a:3 * d_a])
    ga_ref[...] = _silu(_dot(h, w_ref[:, 3 * d_a:4 * d_a]))


def _inproj_a(x2d, nw, w_a, tabs64, tm, d_a):
    m, d = x2d.shape
    (c, sm, sp), half = tabs64
    t_blocks = c.shape[0] // tm
    row = lambda i: (i, 0)
    tab = lambda i: (i % t_blocks, 0)
    fixed = lambda i: (0, 0)
    return pl.pallas_call(
        functools.partial(_inproj_a_kernel, half=half, d_a=d_a),
        grid=(m // tm,),
        in_specs=[
            pl.BlockSpec((tm, d), row),
            pl.BlockSpec((1, d), fixed),
            pl.BlockSpec((d, 4 * d_a), fixed, pipeline_mode=pl.Buffered(1)),
            pl.BlockSpec((tm, LANES), tab),
            pl.BlockSpec((tm, LANES), tab),
            pl.BlockSpec((tm, LANES), tab),
        ],
        out_specs=[pl.BlockSpec((tm, d_a), row)] * 4,
        out_shape=[
            jax.ShapeDtypeStruct((m, d_a), BF16),
            jax.ShapeDtypeStruct((m, d_a), F32),
            jax.ShapeDtypeStruct((m, d_a), F32),
            jax.ShapeDtypeStruct((m, d_a), F32),
        ],
        compiler_params=_cparams(1),
        name="inproj_a",
    )(x2d, nw, w_a, c, sm, sp)


def _inproj_b_kernel(x_ref, nw_ref, w_ref, c128_ref, sm128_ref, sp128_ref,
                     c64_ref, sm64_ref, sp64_ref,
                     qs_ref, ks_ref, vs_ref, gs_ref, qi_ref, kiw_ref,
                     *, half128, half64, d_b, d_kv, d_qi):
    h = _rms_rows(x_ref[...], nw_ref[...]).astype(BF16)
    c128, sm128, sp128 = c128_ref[...], sm128_ref[...], sp128_ref[...]
    c64, sm64, sp64 = c64_ref[...], sm64_ref[...], sp64_ref[...]
    o = 0
    zq = _dot(h, w_ref[:, o:o + d_b])
    for j in range(d_b // LANES):
        sl = slice(j * LANES, (j + 1) * LANES)
        qs_ref[:, sl] = (_rope_chunk(zq[:, sl], c128, sm128, sp128, half128)
                         * (1.0 / math.sqrt(HEAD_DIM))).astype(BF16)
    o += d_b
    zk = _dot(h, w_ref[:, o:o + d_kv])
    for j in range(d_kv // LANES):
        sl = slice(j * LANES, (j + 1) * LANES)
        ks_ref[:, sl] = _rope_chunk(zk[:, sl], c128, sm128, sp128, half128)
    o += d_kv
    vs_ref[...] = _dot(h, w_ref[:, o:o + d_kv])
    o += d_kv
    gs_ref[...] = _silu(_dot(h, w_ref[:, o:o + d_b]))
    o += d_b
    zi = _dot(h, w_ref[:, o:o + d_qi])
    for j in range(d_qi // LANES):
        sl = slice(j * LANES, (j + 1) * LANES)
        qi_ref[:, sl] = _rope_chunk(zi[:, sl], c64, sm64, sp64, half64).astype(BF16)
    o += d_qi
    zkw = _dot(h, w_ref[:, o:o + LANES])
    lane = lax.broadcasted_iota(I32, zkw.shape, 1)
    kiw_ref[...] = jnp.where(lane < D_IDX, _rope_chunk(zkw, c64, sm64, sp64, half64), zkw)


def _inproj_b(x2d, nw, w_b, tabs128, tabs64, tm, d_b, d_kv, d_qi):
    m, d = x2d.shape
    (c128, sm128, sp128), half128 = tabs128
    (c64, sm64, sp64), half64 = tabs64
    t_blocks = c128.shape[0] // tm
    row = lambda i: (i, 0)
    tab = lambda i: (i % t_blocks, 0)
    fixed = lambda i: (0, 0)
    n_b = w_b.shape[1]
    return pl.pallas_call(
        functools.partial(_inproj_b_kernel, half128=half128, half64=half64,
                          d_b=d_b, d_kv=d_kv, d_qi=d_qi),
        grid=(m // tm,),
        in_specs=[
            pl.BlockSpec((tm, d), row),
            pl.BlockSpec((1, d), fixed),
            pl.BlockSpec((d, n_b), fixed, pipeline_mode=pl.Buffered(1)),
        ] + [pl.BlockSpec((tm, LANES), tab)] * 6,
        out_specs=[
            pl.BlockSpec((tm, d_b), row),
            pl.BlockSpec((tm, d_kv), row),
            pl.BlockSpec((tm, d_kv), row),
            pl.BlockSpec((tm, d_b), row),
            pl.BlockSpec((tm, d_qi), row),
            pl.BlockSpec((tm, LANES), row),
        ],
        out_shape=[
            jax.ShapeDtypeStruct((m, d_b), BF16),
            jax.ShapeDtypeStruct((m, d_kv), F32),
            jax.ShapeDtypeStruct((m, d_kv), F32),
            jax.ShapeDtypeStruct((m, d_b), F32),
            jax.ShapeDtypeStruct((m, d_qi), BF16),
            jax.ShapeDtypeStruct((m, LANES), F32),
        ],
        compiler_params=_cparams(1),
        name="inproj_b",
    )(x2d, nw, w_b, c128, sm128, sp128, c64, sm64, sp64)


def _flash_init(m_ref, l_ref, acc_ref):
    m_ref[...] = jnp.full(m_ref.shape, NEG_BIG, F32)
    l_ref[...] = jnp.zeros(l_ref.shape, F32)
    acc_ref[...] = jnp.zeros(acc_ref.shape, F32)


def _flash_step(s, v_bf, m_ref, l_ref, acc_ref):
    m_prev = m_ref[...]
    m_new = jnp.maximum(m_prev, jnp.max(s, axis=-1, keepdims=True))
    alpha = jnp.exp(m_prev - m_new)
    p = jnp.exp(s - m_new)
    l_ref[...] = alpha * l_ref[...] + jnp.sum(p, axis=-1, keepdims=True)
    acc_ref[...] = alpha * acc_ref[...] + _dot(p.astype(BF16), v_bf)
    m_ref[...] = m_new


def _lambda_full(lamp, lam_init):
    s1 = jnp.sum(lamp[0:1, :] * lamp[1:2, :], axis=-1, keepdims=True)
    s2 = jnp.sum(lamp[2:3, :] * lamp[3:4, :], axis=-1, keepdims=True)
    return jnp.exp(s1) - jnp.exp(s2) + lam_init


def _diff_attn_kernel(q_ref, k_ref, v_ref, g_ref, subw_ref, lamp_ref, o_ref,
                      kbf, vbf, m1, l1, a1, m2, l2, a2, *, tq, lam_init):
    i = pl.program_id(2)

    @pl.when(i == 0)
    def _():
        kbf[...] = k_ref[0].astype(BF16)
        vbf[...] = v_ref[0].astype(BF16)

    q = q_ref[0]
    lane = lax.broadcasted_iota(I32, q.shape, 1)
    zero = jnp.zeros_like(q)
    q1 = jnp.where(lane < DH_A, q, zero)
    q2 = jnp.where(lane >= DH_A, q, zero)
    _flash_init(m1, l1, a1)
    _flash_init(m2, l2, a2)

    def block(j, masked):
        off = pl.multiple_of(j * tq, tq)
        kb = kbf[pl.ds(off, tq), :]
        vb = vbf[pl.ds(off, tq), :]
        s1 = _dot_nt(q1, kb)
        s2 = _dot_nt(q2, kb)
        if masked:
            r = lax.broadcasted_iota(I32, s1.shape, 0)
            cidx = lax.broadcasted_iota(I32, s1.shape, 1)
            vis = cidx <= r
            s1 = jnp.where(vis, s1, NEG_BIG)
            s2 = jnp.where(vis, s2, NEG_BIG)
        _flash_step(s1, vb, m1, l1, a1)
        _flash_step(s2, vb, m2, l2, a2)

    def body(j, carry):
        block(j, False)
        return carry

    lax.fori_loop(0, i, body, 0)
    block(i, True)

    lam = _lambda_full(lamp_ref[...], lam_init)
    o = a1[...] / l1[...] - lam * (a2[...] / l2[...])
    o = _rms_rows(o, subw_ref[...]) * (1.0 - lam_init)
    o_ref[0] = (o * g_ref[0]).astype(BF16)


def _diff_attn(qa, ka, va, sga, subw, lamp, lam_init, tq):
    b, t, d_a = qa.shape
    n_h = d_a // HEAD_DIM
    qmap = lambda bi, h, i: (bi, i, h)
    kmap = lambda bi, h, i: (bi, 0, h)
    fixed = lambda bi, h, i: (0, 0)
    return pl.pallas_call(
        functools.partial(_diff_attn_kernel, tq=tq, lam_init=lam_init),
        grid=(b, n_h, t // tq),
        in_specs=[
            pl.BlockSpec((1, tq, HEAD_DIM), qmap),
            pl.BlockSpec((1, t, HEAD_DIM), kmap),
            pl.BlockSpec((1, t, HEAD_DIM), kmap),
            pl.BlockSpec((1, tq, HEAD_DIM), qmap),
            pl.BlockSpec((1, HEAD_DIM), fixed),
            pl.BlockSpec((4, DH_A), fixed),
        ],
        out_specs=pl.BlockSpec((1, tq, HEAD_DIM), qmap),
        out_shape=jax.ShapeDtypeStruct((b, t, d_a), BF16),
        scratch_shapes=[
            pltpu.VMEM((t, HEAD_DIM), BF16), pltpu.VMEM((t, HEAD_DIM), BF16),
            pltpu.VMEM((tq, 1), F32), pltpu.VMEM((tq, 1), F32), pltpu.VMEM((tq, HEAD_DIM), F32),
            pltpu.VMEM((tq, 1), F32), pltpu.VMEM((tq, 1), F32), pltpu.VMEM((tq, HEAD_DIM), F32),
        ],
        compiler_params=_cparams(3),
        name="diff_attn",
    )(qa, ka, va, sga, subw, lamp)


def _float_key(s):
    bits = lax.bitcast_convert_type(s, I32)
    return bits ^ ((bits >> 31) & 0x7FFFFFFF)


def _lane_fold(m):
    n = m.shape[1] // LANES
    acc = m[:, 0:LANES]
    for j in range(1, n):
        acc = acc + m[:, j * LANES:(j + 1) * LANES]
    return acc


def _count_rows(key_ref, n_chunks, chunk, pred):
    rows = key_ref.shape[0]

    def body(c, acc):
        off = pl.multiple_of(c * chunk, chunk)
        kc = key_ref[:, pl.ds(off, chunk)]
        return acc + _lane_fold(pred(kc, off))

    acc = lax.fori_loop(0, n_chunks, body, jnp.zeros((rows, LANES), F32))
    return jnp.sum(acc, axis=-1, keepdims=True)


def _select_topk(key_ref, p_ref, n_chunks, chunk, k_sel, idx_bits):
    rows = key_ref.shape[0]
    k_f = float(k_sel)

    def bit_body(it, tu):
        bit = lax.shift_left(jnp.int32(1), 31 - it)
        cand_u = tu | bit
        cand = cand_u ^ INT_MIN
        cnt = _count_rows(key_ref, n_chunks, chunk,
                          lambda kc, off: jnp.where(kc >= cand, 1.0, 0.0))
        return jnp.where(cnt >= k_f, cand_u, tu)

    tu = lax.fori_loop(0, 32, bit_body, jnp.zeros((rows, 1), I32))
    thr = tu ^ INT_MIN
    n_gt = _count_rows(key_ref, n_chunks, chunk,
                       lambda kc, off: jnp.where(kc > thr, 1.0, 0.0))
    n_eq = _count_rows(key_ref, n_chunks, chunk,
                       lambda kc, off: jnp.where(kc == thr, 1.0, 0.0))
    need = k_f - n_gt
    p_ref[...] = jnp.full((rows, 1), 2 ** 30, I32)
    excess = jnp.max(n_eq - need)

    @pl.when(excess > 0.0)
    def _():
        def idx_body(it, p):
            bit = lax.shift_left(jnp.int32(1), idx_bits - 1 - it)
            cand = p | bit

            def pred(kc, off):
                idx = off + lax.broadcasted_iota(I32, kc.shape, 1)
                return jnp.where(kc == thr, jnp.where(idx < cand, 1.0, 0.0), 0.0)

            cnt = _count_rows(key_ref, n_chunks, chunk, pred)
            return jnp.where(cnt < need, cand, p)

        p_ref[...] = lax.fori_loop(0, idx_bits, idx_body, jnp.zeros((rows, 1), I32))

    return thr


def _selected_bias(kc, off, thr, p_max):
    idx = off + lax.broadcasted_iota(I32, kc.shape, 1)
    tie = jnp.where(kc == thr, jnp.where(idx <= p_max, 1.0, 0.0), 0.0)
    sel = jnp.where(kc > thr, 1.0, tie)
    sel = jnp.where(kc == NEG_INF_KEY, 0.0, sel)
    return jnp.where(sel > 0.5, 0.0, NEG_BIG)


def _index_select_kernel(qi_ref, ki_ref, wi_ref, bias_ref, key_ref, p_ref, *, tq, k_sel, idx_bits):
    i = pl.program_id(1)
    n_total = bias_ref.shape[2] // tq
    wi = wi_ref[0]

    def score_body(c, carry):
        off = pl.multiple_of(c * tq, tq)
        kc = ki_ref[0, pl.ds(off, tq), :]
        sc = jnp.zeros((tq, tq), F32)
        for h in range(H_IDX):
            sc = sc + jnp.maximum(_dot_nt(qi_ref[0, h], kc), 0.0) * wi[:, h:h + 1]
        r = lax.broadcasted_iota(I32, sc.shape, 0) + i * tq
        cidx = lax.broadcasted_iota(I32, sc.shape, 1) + off
        sc = jnp.where(cidx <= r, sc, -jnp.inf)
        key_ref[:, pl.ds(off, tq)] = _float_key(sc)
        return carry

    lax.fori_loop(0, i + 1, score_body, 0)
    thr = _select_topk(key_ref, p_ref, i + 1, tq, k_sel, idx_bits)
    p_max = p_ref[...]

    def out_body(c, carry):
        off = pl.multiple_of(c * tq, tq)
        kc = key_ref[:, pl.ds(off, tq)]
        bias_ref[0, :, pl.ds(off, tq)] = _selected_bias(kc, off, thr, p_max).astype(BF16)
        return carry

    lax.fori_loop(0, i + 1, out_body, 0)

    def fill_body(c, carry):
        off = pl.multiple_of(c * tq, tq)
        bias_ref[0, :, pl.ds(off, tq)] = jnp.full((tq, tq), NEG_BIG, BF16)
        return carry

    lax.fori_loop(i + 1, n_total, fill_body, 0)


def _index_select(qi4, ki_bf, wi, tq, k_sel):
    b, _, t, _ = qi4.shape
    idx_bits = max(1, (t - 1).bit_length())
    return pl.pallas_call(
        functools.partial(_index_select_kernel, tq=tq, k_sel=k_sel, idx_bits=idx_bits),
        grid=(b, t // tq),
        in_specs=[
            pl.BlockSpec((1, H_IDX, tq, D_IDX), lambda bi, i: (bi, 0, i, 0)),
            pl.BlockSpec((1, t, D_IDX), lambda bi, i: (bi, 0, 0)),
            pl.BlockSpec((1, tq, H_IDX), lambda bi, i: (bi, i, 0)),
        ],
        out_specs=pl.BlockSpec((1, tq, t), lambda bi, i: (bi, i, 0)),
        out_shape=jax.ShapeDtypeStruct((b, t, t), BF16),
        scratch_shapes=[pltpu.VMEM((tq, t), I32), pltpu.VMEM((tq, 1), I32)],
        compiler_params=_cparams(2),
        name="index_select",
    )(qi4, ki_bf, wi)


def _dsa_attn_kernel(q_ref, k_ref, v_ref, bias_ref, g_ref, o_ref, kbf, vbf, m, l, acc, *, tq, group):
    i = pl.program_id(2)

    @pl.when(i == 0)
    def _():
        kbf[...] = k_ref[0].astype(BF16)
        vbf[...] = v_ref[0].astype(BF16)

    q = q_ref[0]
    q4 = jnp.concatenate([q[:, h * HEAD_DIM:(h + 1) * HEAD_DIM] for h in range(group)], axis=0)
    _flash_init(m, l, acc)

    def body(j, carry):
        off = pl.multiple_of(j * tq, tq)
        kb = kbf[pl.ds(off, tq), :]
        vb = vbf[pl.ds(off, tq), :]
        bias = bias_ref[0, :, pl.ds(off, tq)].astype(F32)
        s = _dot_nt(q4, kb) + jnp.concatenate([bias] * group, axis=0)
        _flash_step(s, vb, m, l, acc)
        return carry

    lax.fori_loop(0, i + 1, body, 0)
    o = acc[...] / l[...]
    g = g_ref[0]
    for h in range(group):
        sl = slice(h * HEAD_DIM, (h + 1) * HEAD_DIM)
        o_ref[0, :, sl] = (o[h * tq:(h + 1) * tq, :] * g[:, sl]).astype(BF16)


def _dsa_attn(qs, ks, vs, bias, sgs, tq):
    b, t, d_b = qs.shape
    n_kv = ks.shape[2] // HEAD_DIM
    group = d_b // HEAD_DIM // n_kv
    gw = group * HEAD_DIM
    qmap = lambda bi, n, i: (bi, i, n)
    kmap = lambda bi, n, i: (bi, 0, n)
    return pl.pallas_call(
        functools.partial(_dsa_attn_kernel, tq=tq, group=group),
        grid=(b, n_kv, t // tq),
        in_specs=[
            pl.BlockSpec((1, tq, gw), qmap),
            pl.BlockSpec((1, t, HEAD_DIM), kmap),
            pl.BlockSpec((1, t, HEAD_DIM), kmap),
            pl.BlockSpec((1, tq, t), lambda bi, n, i: (bi, i, 0)),
            pl.BlockSpec((1, tq, gw), qmap),
        ],
        out_specs=pl.BlockSpec((1, tq, gw), qmap),
        out_shape=jax.ShapeDtypeStruct((b, t, d_b), BF16),
        scratch_shapes=[
            pltpu.VMEM((t, HEAD_DIM), BF16), pltpu.VMEM((t, HEAD_DIM), BF16),
            pltpu.VMEM((group * tq, 1), F32), pltpu.VMEM((group * tq, 1), F32),
            pltpu.VMEM((group * tq, HEAD_DIM), F32),
        ],
        compiler_params=_cparams(3),
        name="dsa_attn",
    )(qs, ks, vs, bias, sgs)


def _out_kernel(x_ref, a1_ref, a2_ref, p_ref, wo_ref, pw_ref, wg_ref, bg_ref, we_ref, y_ref, *, d_a):
    o = _dot(a1_ref[...], wo_ref[0:d_a, :]) + _dot(a2_ref[...], wo_ref[d_a:, :])
    x1 = x_ref[...] + _rms_rows(o, pw_ref[...])
    z = _dot(x1.astype(BF16), wg_ref[...]) + bg_ref[...]
    gate = 1.0 / (1.0 + jnp.exp(-z))
    y_ref[...] = x1 + gate * _dot(p_ref[...].astype(BF16), we_ref[...])


def _out_proj(x2d, a1, a2, p2d, wo, pw, wg, bg, we, tm):
    m, d = x2d.shape
    d_a = a1.shape[1]
    d_mix = wo.shape[0]
    d_ple = p2d.shape[1]
    row = lambda i: (i, 0)
    fixed = lambda i: (0, 0)
    single = dict(pipeline_mode=pl.Buffered(1))
    return pl.pallas_call(
        functools.partial(_out_kernel, d_a=d_a),
        grid=(m // tm,),
        in_specs=[
            pl.BlockSpec((tm, d), row),
            pl.BlockSpec((tm, d_a), row),
            pl.BlockSpec((tm, d_mix - d_a), row),
            pl.BlockSpec((tm, d_ple), row),
            pl.BlockSpec((d_mix, d), fixed, **single),
            pl.BlockSpec((1, d), fixed),
            pl.BlockSpec((d, d), fixed, **single),
            pl.BlockSpec((1, d), fixed),
            pl.BlockSpec((d_ple, d), fixed, **single),
        ],
        out_specs=pl.BlockSpec((tm, d), row),
        out_shape=jax.ShapeDtypeStruct((m, d), F32),
        compiler_params=_cparams(1),
        name="out_proj",
    )(x2d, a1, a2, p2d, wo, pw, wg, bg, we)


def _sample_scores_kernel(pt_ref, qi_ref, wi_ref, *rest, n_group):
    del pt_ref
    page_refs, out_ref = rest[:n_group], rest[n_group]
    q4 = qi_ref[0]
    w = wi_ref[0]
    for r in range(n_group):
        kp = page_refs[r][0].astype(BF16)
        rel = jnp.maximum(_dot_nt(q4, kp), 0.0)
        out_ref[0, r:r + 1, :] = jnp.sum(rel * w, axis=0, keepdims=True)


def _sample_scores(page_table_flat, qi4, wi3, cache_idx, n_pages, n_group):
    nb = qi4.shape[0]
    page = cache_idx.shape[1]

    def page_map(r):
        return lambda bi, g, pt: (pt[bi * n_pages + g * n_group + r], 0, 0)

    grid_spec = pltpu.PrefetchScalarGridSpec(
        num_scalar_prefetch=1,
        grid=(nb, n_pages // n_group),
        in_specs=[
            pl.BlockSpec((1, H_IDX, D_IDX), lambda bi, g, pt: (bi, 0, 0)),
            pl.BlockSpec((1, H_IDX, 1), lambda bi, g, pt: (bi, 0, 0)),
        ] + [pl.BlockSpec((1, page, D_IDX), page_map(r)) for r in range(n_group)],
        out_specs=pl.BlockSpec((1, n_group, page), lambda bi, g, pt: (bi, g, 0)),
    )
    return pl.pallas_call(
        functools.partial(_sample_scores_kernel, n_group=n_group),
        grid_spec=grid_spec,
        out_shape=jax.ShapeDtypeStruct((nb, n_pages, page), F32),
        compiler_params=_cparams(2),
        name="sample_scores",
    )(page_table_flat, qi4, wi3, *([cache_idx] * n_group))


def _sample_select_kernel(sc_ref, qi_ref, kit_ref, wi_ref, bias_ref, key_ref, p_ref, *, k_sel, idx_bits):
    rows, s_past = sc_ref.shape
    s_all = key_ref.shape[1]
    prod = qi_ref[...].astype(F32) * kit_ref[...].astype(F32)
    lane = lax.broadcasted_iota(I32, prod.shape, 1)
    wi = wi_ref[...]
    new = jnp.zeros((rows, 1), F32)
    for h in range(H_IDX):
        dot_h = jnp.sum(jnp.where(lane // D_IDX == h, prod, 0.0), axis=-1, keepdims=True)
        new = new + jnp.maximum(dot_h, 0.0) * wi[:, h:h + 1]
    key_ref[:, 0:s_past] = _float_key(sc_ref[...])
    tail_lane = lax.broadcasted_iota(I32, (rows, s_all - s_past), 1)
    key_ref[:, s_past:s_all] = jnp.where(tail_lane == 0, _float_key(new), NEG_INF_KEY)
    thr = _select_topk(key_ref, p_ref, 1, s_all, k_sel, idx_bits)
    bias_ref[...] = _selected_bias(key_ref[...], 0, thr, p_ref[...])


def _sample_select(scores2d, qi, ki_tiled, wi, k_sel):
    rows, s_past = scores2d.shape
    s_all = s_past + LANES
    idx_bits = max(1, (s_all - 1).bit_length())
    return pl.pallas_call(
        functools.partial(_sample_select_kernel, k_sel=k_sel, idx_bits=idx_bits),
        out_shape=jax.ShapeDtypeStruct((rows, s_all), F32),
        scratch_shapes=[pltpu.VMEM((rows, s_all), I32), pltpu.VMEM((rows, 1), I32)],
        compiler_params=pltpu.CompilerParams(vmem_limit_bytes=VMEM_LIMIT),
        name="sample_select",
    )(scores2d, qi, ki_tiled, wi)


def _sample_attn_kernel(pt_ref, qa_ref, qs_ref, kan_ref, van_ref, ksn_ref, vsn_ref, bnew_ref,
                        ga_ref, gs_ref, subw_ref, lamp_ref,
                        kd_ref, vd_ref, ks_ref, vs_ref, bias_ref,
                        oa_ref, ob_ref,
                        qd, qsb, md, ld, accd, ms, ls, accs, *, n_ha, n_hb, group, lam_init):
    del pt_ref
    p = pl.program_id(1)
    n_p = pl.num_programs(1)
    d_a = n_ha * HEAD_DIM
    d_kv = (n_hb // group) * HEAD_DIM

    @pl.when(p == 0)
    def _():
        qrow = qa_ref[0].astype(F32)
        rd = lax.broadcasted_iota(I32, (2 * n_ha, d_a), 0)
        ld_ = lax.broadcasted_iota(I32, (2 * n_ha, d_a), 1)
        qd_f = jnp.where(ld_ // DH_A == rd, jnp.broadcast_to(qrow, (2 * n_ha, d_a)), 0.0)
        qd[...] = qd_f.astype(BF16)
        srow = qs_ref[0].astype(F32)
        rs = lax.broadcasted_iota(I32, (n_hb, d_kv), 0)
        ls_ = lax.broadcasted_iota(I32, (n_hb, d_kv), 1)
        q_rows = []
        for h in range(n_hb):
            qh = srow[:, h * HEAD_DIM:(h + 1) * HEAD_DIM]
            q_rows.append(jnp.concatenate([qh] * (d_kv // HEAD_DIM), axis=-1))
        qs_full = jnp.concatenate(q_rows, axis=0)
        qs_f = jnp.where(ls_ // HEAD_DIM == rs // group, qs_full, 0.0)
        qsb[...] = qs_f.astype(BF16)
        kan = kan_ref[0].astype(BF16).astype(F32)
        van = van_ref[0].astype(BF16).astype(F32)
        md[...] = jnp.sum(qd_f * kan, axis=-1, keepdims=True)
        ld[...] = jnp.ones(ld.shape, F32)
        accd[...] = jnp.broadcast_to(van, accd.shape)
        ksn = ksn_ref[0].astype(BF16).astype(F32)
        vsn = vsn_ref[0].astype(BF16).astype(F32)
        ms[...] = jnp.sum(qs_f * ksn, axis=-1, keepdims=True) + bnew_ref[0]
        ls[...] = jnp.ones(ls.shape, F32)
        accs[...] = jnp.broadcast_to(vsn, accs.shape)

    kd = kd_ref[0].astype(BF16)
    vd = vd_ref[0].astype(BF16)
    _flash_step(_dot_nt(qd[...], kd), vd, md, ld, accd)
    ksp = ks_ref[0].astype(BF16)
    vsp = vs_ref[0].astype(BF16)
    _flash_step(_dot_nt(qsb[...], ksp) + bias_ref[0, 0], vsp, ms, ls, accs)

    @pl.when(p == n_p - 1)
    def _():
        lam = _lambda_full(lamp_ref[...], lam_init)
        od = accd[...] / ld[...]
        ga = ga_ref[0]
        subw = subw_ref[...]
        for h in range(n_ha):
            sl = slice(h * HEAD_DIM, (h + 1) * HEAD_DIM)
            o = od[2 * h:2 * h + 1, sl] - lam * od[2 * h + 1:2 * h + 2, sl]
            o = _rms_rows(o, subw) * (1.0 - lam_init)
            oa_ref[0, :, sl] = (o * ga[:, sl]).astype(BF16)
        os_ = accs[...] / ls[...]
        gs = gs_ref[0]
        for h in range(n_hb):
            n = h // group
            o = os_[h:h + 1, n * HEAD_DIM:(n + 1) * HEAD_DIM]
            sl = slice(h * HEAD_DIM, (h + 1) * HEAD_DIM)
            ob_ref[0, :, sl] = (o * gs[:, sl]).astype(BF16)


def _sample_attn(page_table_flat, qa, qs, ka_new, va_new, ks_new, vs_new, bias_new, sga, sgs,
                 subw, lamp, cdk, cdv, csk, csv, bias_pages, n_pages, lam_init):
    nb, _, d_a = qa.shape
    d_b = qs.shape[2]
    d_kv = ks_new.shape[2]
    n_ha = d_a // HEAD_DIM
    n_hb = d_b // HEAD_DIM
    group = n_hb // (d_kv // HEAD_DIM)
    page = cdk.shape[1]
    per_b = lambda bi, p, pt: (bi, 0, 0)
    fixed = lambda bi, p, pt: (0, 0)
    paged = lambda bi, p, pt: (pt[bi * n_pages + p], 0, 0)
    grid_spec = pltpu.PrefetchScalarGridSpec(
        num_scalar_prefetch=1,
        grid=(nb, n_pages),
        in_specs=[
            pl.BlockSpec((1, 1, d_a), per_b),
            pl.BlockSpec((1, 1, d_b), per_b),
            pl.BlockSpec((1, 1, d_a), per_b),
            pl.BlockSpec((1, 1, d_a), per_b),
            pl.BlockSpec((1, 1, d_kv), per_b),
            pl.BlockSpec((1, 1, d_kv), per_b),
            pl.BlockSpec((1, 1, 1), per_b),
            pl.BlockSpec((1, 1, d_a), per_b),
            pl.BlockSpec((1, 1, d_b), per_b),
            pl.BlockSpec((1, HEAD_DIM), fixed),
            pl.BlockSpec((4, DH_A), fixed),
            pl.BlockSpec((1, page, d_a), paged),
            pl.BlockSpec((1, page, d_a), paged),
            pl.BlockSpec((1, page, d_kv), paged),
            pl.BlockSpec((1, page, d_kv), paged),
            pl.BlockSpec((1, 1, 1, page), lambda bi, p, pt: (bi, p, 0, 0)),
        ],
        out_specs=[pl.BlockSpec((1, 1, d_a), per_b), pl.BlockSpec((1, 1, d_b), per_b)],
        scratch_shapes=[
            pltpu.VMEM((2 * n_ha, d_a), BF16), pltpu.VMEM((n_hb, d_kv), BF16),
            pltpu.VMEM((2 * n_ha, 1), F32), pltpu.VMEM((2 * n_ha, 1), F32),
            pltpu.VMEM((2 * n_ha, d_a), F32),
            pltpu.VMEM((n_hb, 1), F32), pltpu.VMEM((n_hb, 1), F32), pltpu.VMEM((n_hb, d_kv), F32),
        ],
    )
    return pl.pallas_call(
        functools.partial(_sample_attn_kernel, n_ha=n_ha, n_hb=n_hb, group=group, lam_init=lam_init),
        grid_spec=grid_spec,
        out_shape=[jax.ShapeDtypeStruct((nb, 1, d_a), BF16), jax.ShapeDtypeStruct((nb, 1, d_b), BF16)],
        compiler_params=_cparams(2),
        name="sample_attn",
    )(page_table_flat, qa, qs, ka_new, va_new, ks_new, vs_new, bias_new, sga, sgs, subw, lamp,
      cdk, cdv, csk, csv, bias_pages)


def _row_tile(m, pref):
    return pref if m % pref == 0 else m


def _inproj_all(x2d, t_rows, pos, wts, tm):
    tabs64 = _rope_tables(pos, DH_A)
    tabs128 = _rope_tables(pos, HEAD_DIM)
    d_a, d_b, d_kv, d_qi = wts["d_a"], wts["d_b"], wts["d_kv"], wts["d_qi"]
    del t_rows
    qa, ka, va, sga = _inproj_a(x2d, wts["pre_w"], wts["w_a"], tabs64, tm, d_a)
    qs, ks, vs, sgs, qi, kiw = _inproj_b(x2d, wts["pre_w"], wts["w_b"], tabs128, tabs64, tm,
                                         d_b, d_kv, d_qi)
    return qa, ka, va, sga, qs, ks, vs, sgs, qi, kiw


def kernel(x_prompt, x_sample, p_prompt, p_sample, cache_diff_k, cache_diff_v, cache_dsa_k, cache_dsa_v, cache_idx_k, page_table, pre_norm_w, post_norm_w, w_in, lam_q1, lam_k1, lam_q2, lam_k2, diff_norm_w, w_out, w_ple_gate, b_ple_gate, w_ple_proj):
    depth = w_in.shape[0]
    assert depth == 1, "single-layer stack only"
    bsz, t_p, d = x_prompt.shape
    nb, t_s, _ = x_sample.shape
    assert t_s == 1, "one new token per sample sequence"
    n_pages = page_table.shape[1]
    page = cache_diff_k.shape[2]
    n_ha = cache_diff_k.shape[3]
    n_kv = cache_dsa_k.shape[3]
    d_a = n_ha * HEAD_DIM
    d_kv = n_kv * HEAD_DIM
    d_b = d - d_a
    d_qi = H_IDX * D_IDX
    past_len = n_pages * page
    lam_init = 0.8 - 0.6 * math.exp(-0.3 * 0)
    n_in = w_in.shape[2]

    w = w_in[0]
    w_a = w[:, :4 * d_a].astype(BF16)
    n_b = n_in - 4 * d_a
    n_b_pad = -(-n_b // LANES) * LANES
    w_b = jnp.pad(w[:, 4 * d_a:], ((0, 0), (0, n_b_pad - n_b))).astype(BF16)
    wts = dict(pre_w=pre_norm_w[0][None, :], w_a=w_a, w_b=w_b, d_a=d_a, d_b=d_b, d_kv=d_kv, d_qi=d_qi)
    wo = w_out[0].astype(BF16)
    wg = w_ple_gate[0].astype(BF16)
    we = w_ple_proj[0].astype(BF16)
    pw = post_norm_w[0][None, :]
    bg = b_ple_gate[0][None, :]
    subw = diff_norm_w[0][None, :]
    lamp = jnp.stack([lam_q1[0], lam_k1[0], lam_q2[0], lam_k2[0]], axis=0)

    m_p = bsz * t_p
    tm = _row_tile(t_p, 256)
    tq = _row_tile(t_p, 256)
    xp2 = x_prompt.reshape(m_p, d)
    pos_p = jnp.arange(t_p, dtype=I32)
    qa, ka, va, sga, qs, ks, vs, sgs, qi, kiw = _inproj_all(xp2, t_p, pos_p, wts, tm)
    r3 = lambda a: a.reshape(bsz, t_p, a.shape[-1])
    a1 = _diff_attn(r3(qa), r3(ka), r3(va), r3(sga), subw, lamp, lam_init, tq)
    ki = kiw[:, :D_IDX]
    wi = kiw[:, D_IDX:D_IDX + H_IDX]
    qi4 = qi.reshape(bsz, t_p, H_IDX, D_IDX).transpose(0, 2, 1, 3)
    k_sel_p = min(TOPK_MAX, t_p // 4)
    bias = _index_select(qi4, ki.astype(BF16).reshape(bsz, t_p, D_IDX),
                         wi.reshape(bsz, t_p, H_IDX), tq, k_sel_p)
    a2 = _dsa_attn(r3(qs), r3(ks), r3(vs), bias, r3(sgs), tq)
    y_p = _out_proj(xp2, a1.reshape(m_p, d_a), a2.reshape(m_p, d_b), p_prompt[0].reshape(m_p, -1),
                    wo, pw, wg, bg, we, tm)

    xs2 = x_sample.reshape(nb, d)
    pos_s = jnp.full((nb,), past_len, dtype=I32)
    qa_s, ka_s, va_s, sga_s, qs_s, ks_s, vs_s, sgs_s, qi_s, kiw_s = _inproj_all(xs2, nb, pos_s, wts, nb)
    ki_s = kiw_s[:, :D_IDX]
    wi_s = kiw_s[:, D_IDX:D_IDX + H_IDX]
    pt_flat = page_table.reshape(-1)
    n_group = 8 if n_pages % 8 == 0 else 1
    scores = _sample_scores(pt_flat, qi_s.reshape(nb, H_IDX, D_IDX), wi_s.reshape(nb, H_IDX, 1),
                            cache_idx_k[0], n_pages, n_group)
    k_sel_s = min(TOPK_MAX, (past_len + t_s) // 4)
    bias_s = _sample_select(scores.reshape(nb, past_len), qi_s,
                            jnp.tile(ki_s.astype(BF16), (1, H_IDX)), wi_s, k_sel_s)
    bias_pages = bias_s[:, :past_len].reshape(nb, n_pages, 1, page)
    bias_new = bias_s[:, past_len:past_len + 1].reshape(nb, 1, 1)
    e3 = lambda a: a.reshape(nb, 1, a.shape[-1])
    a1_s, a2_s = _sample_attn(
        pt_flat, e3(qa_s), e3(qs_s), e3(ka_s), e3(va_s), e3(ks_s), e3(vs_s), bias_new,
        e3(sga_s), e3(sgs_s), subw, lamp,
        cache_diff_k[0].reshape(-1, page, d_a), cache_diff_v[0].reshape(-1, page, d_a),
        cache_dsa_k[0].reshape(-1, page, d_kv), cache_dsa_v[0].reshape(-1, page, d_kv),
        bias_pages, n_pages, lam_init)
    y_s = _out_proj(xs2, a1_s.reshape(nb, d_a), a2_s.reshape(nb, d_b), p_sample[0].reshape(nb, -1),
                    wo, pw, wg, bg, we, nb)

    return (
        y_p.reshape(bsz, t_p, d), y_s.reshape(nb, t_s, d),
        ka.reshape(1, bsz, t_p, n_ha, HEAD_DIM), va.reshape(1, bsz, t_p, n_ha, HEAD_DIM),
        ks.reshape(1, bsz, t_p, n_kv, HEAD_DIM), vs.reshape(1, bsz, t_p, n_kv, HEAD_DIM),
        ki.reshape(1, bsz, t_p, D_IDX),
        ka_s.reshape(1, nb, t_s, n_ha, HEAD_DIM), va_s.reshape(1, nb, t_s, n_ha, HEAD_DIM),
        ks_s.reshape(1, nb, t_s, n_kv, HEAD_DIM), vs_s.reshape(1, nb, t_s, n_kv, HEAD_DIM),
        ki_s.reshape(1, nb, t_s, D_IDX),
    )
```

```python
import functools
import math

import jax
import jax.numpy as jnp
from jax import lax
from jax.experimental import pallas as pl
from jax.experimental.pallas import tpu as pltpu

F32 = jnp.float32
BF16 = jnp.bfloat16
I32 = jnp.int32

LANES = 128
SUBLANES = 8
HEAD_DIM = 128
DH_A = HEAD_DIM // 2
D_IDX = 64
H_IDX = 4
TOPK_MAX = 256
ROPE_THETA = 500000.0
ROPE_FRAC = 4
RMS_EPS = 1e-6
NEG_BIG = -1e30
INT_MIN = -(2 ** 31)
NEG_INF_KEY = INT_MIN + 0x7FFFFF
VMEM_LIMIT = 56 * 1024 * 1024
ROW_TILE = 256
ATTN_TILE = 256
SAMPLE_PAGES_PER_STEP = 4


def _cparams(n_axes):
    return pltpu.CompilerParams(
        dimension_semantics=("arbitrary",) * n_axes, vmem_limit_bytes=VMEM_LIMIT)


def _dot(a, b):
    return jnp.dot(a, b, preferred_element_type=F32)


def _dot_nt(a, b):
    return lax.dot_general(a, b, (((1,), (1,)), ((), ())), preferred_element_type=F32)


def _rope_tables(pos, d):
    r = d // ROPE_FRAC
    half = r // 2
    inv = ROPE_THETA ** (-(2.0 / r) * jnp.arange(half, dtype=F32))
    ang = pos.astype(F32)[:, None] * inv[None, :]
    cos, sin = jnp.cos(ang), jnp.sin(ang)
    t = pos.shape[0]
    ones = jnp.ones((t, d - r), F32)
    zeros_h = jnp.zeros((t, half), F32)
    zeros_r = jnp.zeros((t, d - r), F32)
    c = jnp.concatenate([cos, cos, ones], axis=-1)
    sm = jnp.concatenate([-sin, zeros_h, zeros_r], axis=-1)
    sp = jnp.concatenate([zeros_h, sin, zeros_r], axis=-1)
    rep = LANES // d
    return tuple(jnp.tile(a, (1, rep)) for a in (c, sm, sp)), half


def _rope_chunk(z, c, sm, sp, half):
    return (z * c + pltpu.roll(z, LANES - half, axis=1) * sm
            + pltpu.roll(z, half, axis=1) * sp)


def _silu(z):
    return z * (1.0 / (1.0 + jnp.exp(-z)))


def _rms_rows(x, w):
    return x * lax.rsqrt(jnp.mean(x * x, axis=-1, keepdims=True) + RMS_EPS) * w


def _inproj_a_kernel(x_ref, nw_ref, w_ref, c_ref, sm_ref, sp_ref,
                     qa_ref, ka_ref, va_ref, ga_ref, *maybe_vt_ref, half, d_a):
    h = _rms_rows(x_ref[...], nw_ref[...]).astype(BF16)
    c, sm, sp = c_ref[...], sm_ref[...], sp_ref[...]
    n_chunks = d_a // LANES
    zq = _dot(h, w_ref[:, 0:d_a])
    for j in range(n_chunks):
        sl = slice(j * LANES, (j + 1) * LANES)
        qa_ref[:, sl] = (_rope_chunk(zq[:, sl], c, sm, sp, half) * (1.0 / math.sqrt(DH_A))).astype(BF16)
    zk = _dot(h, w_ref[:, d_a:2 * d_a])
    for j in range(n_chunks):
        sl = slice(j * LANES, (j + 1) * LANES)
        ka_ref[:, sl] = _rope_chunk(zk[:, sl], c, sm, sp, half)
    zv = _dot(h, w_ref[:, 2 * d_a:3 * d_a])
    va_ref[...] = zv
    if maybe_vt_ref:
        maybe_vt_ref[0][...] = zv.T.astype(BF16)
    ga_ref[...] = _silu(_dot(h, w_ref[:, 3 * d_a:4 * d_a]))


def _inproj_a(x2d, nw, w_a, tabs64, tm, d_a, emit_vt):
    m, d = x2d.shape
    (c, sm, sp), half = tabs64
    t_blocks = c.shape[0] // tm
    row = lambda i: (i, 0)
    tab = lambda i: (i % t_blocks, 0)
    fixed = lambda i: (0, 0)
    out_specs = [pl.BlockSpec((tm, d_a), row)] * 4
    out_shape = [
        jax.ShapeDtypeStruct((m, d_a), BF16),
        jax.ShapeDtypeStruct((m, d_a), F32),
        jax.ShapeDtypeStruct((m, d_a), F32),
        jax.ShapeDtypeStruct((m, d_a), F32),
    ]
    if emit_vt:
        out_specs.append(pl.BlockSpec((d_a, tm), lambda i: (0, i)))
        out_shape.append(jax.ShapeDtypeStruct((d_a, m), BF16))
    return pl.pallas_call(
        functools.partial(_inproj_a_kernel, half=half, d_a=d_a),
        grid=(m // tm,),
        in_specs=[
            pl.BlockSpec((tm, d), row),
            pl.BlockSpec((1, d), fixed),
            pl.BlockSpec((d, 4 * d_a), fixed, pipeline_mode=pl.Buffered(1)),
            pl.BlockSpec((tm, LANES), tab),
            pl.BlockSpec((tm, LANES), tab),
            pl.BlockSpec((tm, LANES), tab),
        ],
        out_specs=out_specs,
        out_shape=out_shape,
        compiler_params=_cparams(1),
        name="inproj_a",
    )(x2d, nw, w_a, c, sm, sp)


def _inproj_b_kernel(x_ref, nw_ref, w_ref, c128_ref, sm128_ref, sp128_ref,
                     c64_ref, sm64_ref, sp64_ref,
                     qs_ref, ks_ref, vs_ref, gs_ref, qi_ref, kiw_ref, *maybe_vt_ref,
                     half128, half64, d_b, d_kv, d_qi):
    h = _rms_rows(x_ref[...], nw_ref[...]).astype(BF16)
    c128, sm128, sp128 = c128_ref[...], sm128_ref[...], sp128_ref[...]
    c64, sm64, sp64 = c64_ref[...], sm64_ref[...], sp64_ref[...]
    o = 0
    zq = _dot(h, w_ref[:, o:o + d_b])
    for j in range(d_b // LANES):
        sl = slice(j * LANES, (j + 1) * LANES)
        qs_ref[:, sl] = (_rope_chunk(zq[:, sl], c128, sm128, sp128, half128)
                         * (1.0 / math.sqrt(HEAD_DIM))).astype(BF16)
    o += d_b
    zk = _dot(h, w_ref[:, o:o + d_kv])
    for j in range(d_kv // LANES):
        sl = slice(j * LANES, (j + 1) * LANES)
        ks_ref[:, sl] = _rope_chunk(zk[:, sl], c128, sm128, sp128, half128)
    o += d_kv
    zv = _dot(h, w_ref[:, o:o + d_kv])
    vs_ref[...] = zv
    if maybe_vt_ref:
        maybe_vt_ref[0][...] = zv.T.astype(BF16)
    o += d_kv
    gs_ref[...] = _silu(_dot(h, w_ref[:, o:o + d_b]))
    o += d_b
    zi = _dot(h, w_ref[:, o:o + d_qi])
    for j in range(d_qi // LANES):
        sl = slice(j * LANES, (j + 1) * LANES)
        qi_ref[:, sl] = _rope_chunk(zi[:, sl], c64, sm64, sp64, half64).astype(BF16)
    o += d_qi
    zkw = _dot(h, w_ref[:, o:o + LANES])
    lane = lax.broadcasted_iota(I32, zkw.shape, 1)
    kiw_ref[...] = jnp.where(lane < D_IDX, _rope_chunk(zkw, c64, sm64, sp64, half64), zkw)


def _inproj_b(x2d, nw, w_b, tabs128, tabs64, tm, d_b, d_kv, d_qi, emit_vt):
    m, d = x2d.shape
    (c128, sm128, sp128), half128 = tabs128
    (c64, sm64, sp64), half64 = tabs64
    t_blocks = c128.shape[0] // tm
    row = lambda i: (i, 0)
    tab = lambda i: (i % t_blocks, 0)
    fixed = lambda i: (0, 0)
    n_b = w_b.shape[1]
    out_specs = [
        pl.BlockSpec((tm, d_b), row),
        pl.BlockSpec((tm, d_kv), row),
        pl.BlockSpec((tm, d_kv), row),
        pl.BlockSpec((tm, d_b), row),
        pl.BlockSpec((tm, d_qi), row),
        pl.BlockSpec((tm, LANES), row),
    ]
    out_shape = [
        jax.ShapeDtypeStruct((m, d_b), BF16),
        jax.ShapeDtypeStruct((m, d_kv), F32),
        jax.ShapeDtypeStruct((m, d_kv), F32),
        jax.ShapeDtypeStruct((m, d_b), F32),
        jax.ShapeDtypeStruct((m, d_qi), BF16),
        jax.ShapeDtypeStruct((m, LANES), F32),
    ]
    if emit_vt:
        out_specs.append(pl.BlockSpec((d_kv, tm), lambda i: (0, i)))
        out_shape.append(jax.ShapeDtypeStruct((d_kv, m), BF16))
    return pl.pallas_call(
        functools.partial(_inproj_b_kernel, half128=half128, half64=half64,
                          d_b=d_b, d_kv=d_kv, d_qi=d_qi),
        grid=(m // tm,),
        in_specs=[
            pl.BlockSpec((tm, d), row),
            pl.BlockSpec((1, d), fixed),
            pl.BlockSpec((d, n_b), fixed, pipeline_mode=pl.Buffered(1)),
        ] + [pl.BlockSpec((tm, LANES), tab)] * 6,
        out_specs=out_specs,
        out_shape=out_shape,
        compiler_params=_cparams(1),
        name="inproj_b",
    )(x2d, nw, w_b, c128, sm128, sp128, c64, sm64, sp64)


def _flash_init(m_ref, l_ref, acc_ref):
    m_ref[...] = jnp.full(m_ref.shape, NEG_BIG, F32)
    l_ref[...] = jnp.zeros(l_ref.shape, F32)
    acc_ref[...] = jnp.zeros(acc_ref.shape, F32)


def _flash_step(s, v_bf, m_ref, l_ref, acc_ref, axis=1):
    m_prev = m_ref[...]
    m_new = jnp.maximum(m_prev, jnp.max(s, axis=axis, keepdims=True))
    alpha = jnp.exp(m_prev - m_new)
    p = jnp.exp(s - m_new)
    l_ref[...] = alpha * l_ref[...] + jnp.sum(p, axis=axis, keepdims=True)
    pv = _dot(p.astype(BF16), v_bf) if axis == 1 else _dot(v_bf, p.astype(BF16))
    acc_ref[...] = alpha * acc_ref[...] + pv
    m_ref[...] = m_new


def _flash_step_multi(s_list, v_list, m_ref, l_ref, acc_ref):
    m_prev = m_ref[...]
    m_new = m_prev
    for s in s_list:
        m_new = jnp.maximum(m_new, jnp.max(s, axis=1, keepdims=True))
    alpha = jnp.exp(m_prev - m_new)
    l_new = alpha * l_ref[...]
    acc = alpha * acc_ref[...]
    for s, v_bf in zip(s_list, v_list):
        p = jnp.exp(s - m_new)
        l_new = l_new + jnp.sum(p, axis=1, keepdims=True)
        acc = acc + _dot(p.astype(BF16), v_bf)
    l_ref[...] = l_new
    acc_ref[...] = acc
    m_ref[...] = m_new


def _lambda_full(lamp, lam_init):
    s1 = jnp.sum(lamp[0:1, :] * lamp[1:2, :], axis=-1, keepdims=True)
    s2 = jnp.sum(lamp[2:3, :] * lamp[3:4, :], axis=-1, keepdims=True)
    return jnp.exp(s1) - jnp.exp(s2) + lam_init


def _diff_attn_kernel(q_ref, k_ref, vt_ref, g_ref, subw_ref, lamp_ref, o_ref,
                      kbf, m1, l1, a1, m2, l2, a2, *, tq, lam_init):
    i = pl.program_id(2)

    @pl.when(i == 0)
    def _():
        kbf[...] = k_ref[0].astype(BF16)

    q = q_ref[0]
    lane = lax.broadcasted_iota(I32, q.shape, 1)
    zero = jnp.zeros_like(q)
    q1 = jnp.where(lane < DH_A, q, zero)
    q2 = jnp.where(lane >= DH_A, q, zero)
    _flash_init(m1, l1, a1)
    _flash_init(m2, l2, a2)

    def block(j, masked):
        off = pl.multiple_of(j * tq, tq)
        kb = kbf[pl.ds(off, tq), :]
        vtb = vt_ref[:, pl.ds(off, tq)]
        s1 = _dot_nt(kb, q1)
        s2 = _dot_nt(kb, q2)
        if masked:
            kidx = lax.broadcasted_iota(I32, s1.shape, 0)
            qidx = lax.broadcasted_iota(I32, s1.shape, 1)
            vis = kidx <= qidx
            s1 = jnp.where(vis, s1, NEG_BIG)
            s2 = jnp.where(vis, s2, NEG_BIG)
        _flash_step(s1, vtb, m1, l1, a1, axis=0)
        _flash_step(s2, vtb, m2, l2, a2, axis=0)

    def body(j, carry):
        block(j, False)
        return carry

    lax.fori_loop(0, i, body, 0)
    block(i, True)

    lam = _lambda_full(lamp_ref[...], lam_init)
    o_t = a1[...] / l1[...] - lam * (a2[...] / l2[...])
    o = _rms_rows(o_t.T, subw_ref[...]) * (1.0 - lam_init)
    o_ref[0] = (o * g_ref[0]).astype(BF16)


def _diff_attn(qa, ka, va_t, sga, subw, lamp, lam_init, tq):
    b, t, d_a = qa.shape
    n_h = d_a // HEAD_DIM
    qmap = lambda bi, h, i: (bi, i, h)
    kmap = lambda bi, h, i: (bi, 0, h)
    fixed = lambda bi, h, i: (0, 0)
    return pl.pallas_call(
        functools.partial(_diff_attn_kernel, tq=tq, lam_init=lam_init),
        grid=(b, n_h, t // tq),
        in_specs=[
            pl.BlockSpec((1, tq, HEAD_DIM), qmap),
            pl.BlockSpec((1, t, HEAD_DIM), kmap),
            pl.BlockSpec((HEAD_DIM, t), lambda bi, h, i: (h, bi)),
            pl.BlockSpec((1, tq, HEAD_DIM), qmap),
            pl.BlockSpec((1, HEAD_DIM), fixed),
            pl.BlockSpec((4, DH_A), fixed),
        ],
        out_specs=pl.BlockSpec((1, tq, HEAD_DIM), qmap),
        out_shape=jax.ShapeDtypeStruct((b, t, d_a), BF16),
        scratch_shapes=[
            pltpu.VMEM((t, HEAD_DIM), BF16),
            pltpu.VMEM((1, tq), F32), pltpu.VMEM((1, tq), F32), pltpu.VMEM((HEAD_DIM, tq), F32),
            pltpu.VMEM((1, tq), F32), pltpu.VMEM((1, tq), F32), pltpu.VMEM((HEAD_DIM, tq), F32),
        ],
        compiler_params=_cparams(3),
        name="diff_attn",
    )(qa, ka, va_t, sga, subw, lamp)


def _float_key(s):
    bits = lax.bitcast_convert_type(s, I32)
    return bits ^ ((bits >> 31) & 0x7FFFFFFF)


def _fold(m, axis):
    if axis == 1:
        n = m.shape[1] // LANES
        acc = m[:, 0:LANES]
        for j in range(1, n):
            acc = acc + m[:, j * LANES:(j + 1) * LANES]
        return acc
    return jnp.sum(m.reshape(m.shape[0] // SUBLANES, SUBLANES, m.shape[1]), axis=0)


def _count(key_ref, n_chunks, chunk, pred, axis):
    other = key_ref.shape[1 - axis]
    acc_shape = (other, LANES) if axis == 1 else (SUBLANES, other)

    def body(c, acc):
        off = pl.multiple_of(c * chunk, chunk)
        kc = key_ref[:, pl.ds(off, chunk)] if axis == 1 else key_ref[pl.ds(off, chunk), :]
        return acc + _fold(pred(kc, off), axis)

    acc = lax.fori_loop(0, n_chunks, body, jnp.zeros(acc_shape, F32))
    return jnp.sum(acc, axis=axis, keepdims=True)


def _select_topk(key_ref, p_ref, n_chunks, chunk, k_sel, idx_bits, axis):
    k_f = float(k_sel)
    q_shape = p_ref.shape

    def bit_body(it, tu):
        bit = lax.shift_left(jnp.int32(1), 31 - it)
        cand_u = tu | bit
        cand = cand_u ^ INT_MIN
        cnt = _count(key_ref, n_chunks, chunk,
                     lambda kc, off: jnp.where(kc >= cand, 1.0, 0.0), axis)
        return jnp.where(cnt >= k_f, cand_u, tu)

    tu = lax.fori_loop(0, 32, bit_body, jnp.zeros(q_shape, I32))
    thr = tu ^ INT_MIN
    n_gt = _count(key_ref, n_chunks, chunk, lambda kc, off: jnp.where(kc > thr, 1.0, 0.0), axis)
    n_eq = _count(key_ref, n_chunks, chunk, lambda kc, off: jnp.where(kc == thr, 1.0, 0.0), axis)
    need = k_f - n_gt
    p_ref[...] = jnp.full(q_shape, 2 ** 30, I32)
    excess = jnp.max(n_eq - need)

    @pl.when(excess > 0.0)
    def _():
        def idx_body(it, p):
            bit = lax.shift_left(jnp.int32(1), idx_bits - 1 - it)
            cand = p | bit

            def pred(kc, off):
                idx = off + lax.broadcasted_iota(I32, kc.shape, axis)
                return jnp.where(kc == thr, jnp.where(idx < cand, 1.0, 0.0), 0.0)

            cnt = _count(key_ref, n_chunks, chunk, pred, axis)
            return jnp.where(cnt < need, cand, p)

        p_ref[...] = lax.fori_loop(0, idx_bits, idx_body, jnp.zeros(q_shape, I32))

    return thr


def _selected_bias(kc, off, thr, p_max, axis):
    idx = off + lax.broadcasted_iota(I32, kc.shape, axis)
    tie = jnp.where(kc == thr, jnp.where(idx <= p_max, 1.0, 0.0), 0.0)
    sel = jnp.where(kc > thr, 1.0, tie)
    sel = jnp.where(kc == NEG_INF_KEY, 0.0, sel)
    return jnp.where(sel > 0.5, 0.0, NEG_BIG)


def _index_select_kernel(qi_ref, ki_ref, wit_ref, bias_ref, key_ref, p_ref, *, tq, k_sel, idx_bits):
    i = pl.program_id(1)
    n_total = bias_ref.shape[1] // tq
    wit = wit_ref[0]

    def score_body(c, carry):
        off = pl.multiple_of(c * tq, tq)
        kc = ki_ref[0, pl.ds(off, tq), :]
        sc = jnp.zeros((tq, tq), F32)
        for h in range(H_IDX):
            sc = sc + jnp.maximum(_dot_nt(kc, qi_ref[0, h]), 0.0) * wit[h:h + 1, :]
        kidx = lax.broadcasted_iota(I32, sc.shape, 0) + off
        qidx = lax.broadcasted_iota(I32, sc.shape, 1) + i * tq
        sc = jnp.where(kidx <= qidx, sc, -jnp.inf)
        key_ref[pl.ds(off, tq), :] = _float_key(sc)
        return carry

    lax.fori_loop(0, i + 1, score_body, 0)
    thr = _select_topk(key_ref, p_ref, i + 1, tq, k_sel, idx_bits, axis=0)
    p_max = p_ref[...]

    def out_body(c, carry):
        off = pl.multiple_of(c * tq, tq)
        kc = key_ref[pl.ds(off, tq), :]
        bias_ref[0, pl.ds(off, tq), :] = _selected_bias(kc, off, thr, p_max, 0).astype(BF16)
        return carry

    lax.fori_loop(0, i + 1, out_body, 0)

    def fill_body(c, carry):
        off = pl.multiple_of(c * tq, tq)
        bias_ref[0, pl.ds(off, tq), :] = jnp.full((tq, tq), NEG_BIG, BF16)
        return carry

    lax.fori_loop(i + 1, n_total, fill_body, 0)


def _index_select(qi4, ki_bf, wi_t, tq, k_sel):
    b, _, t, _ = qi4.shape
    idx_bits = max(1, (t - 1).bit_length())
    return pl.pallas_call(
        functools.partial(_index_select_kernel, tq=tq, k_sel=k_sel, idx_bits=idx_bits),
        grid=(b, t // tq),
        in_specs=[
            pl.BlockSpec((1, H_IDX, tq, D_IDX), lambda bi, i: (bi, 0, i, 0)),
            pl.BlockSpec((1, t, D_IDX), lambda bi, i: (bi, 0, 0)),
            pl.BlockSpec((1, H_IDX, tq), lambda bi, i: (bi, 0, i)),
        ],
        out_specs=pl.BlockSpec((1, t, tq), lambda bi, i: (bi, 0, i)),
        out_shape=jax.ShapeDtypeStruct((b, t, t), BF16),
        scratch_shapes=[pltpu.VMEM((t, tq), I32), pltpu.VMEM((1, tq), I32)],
        compiler_params=_cparams(2),
        name="index_select",
    )(qi4, ki_bf, wi_t)


def _dsa_attn_kernel(q_ref, k_ref, vt_ref, bias_ref, g_ref, o_ref, kbf, m, l, acc, *, tq, group):
    i = pl.program_id(2)

    @pl.when(i == 0)
    def _():
        kbf[...] = k_ref[0].astype(BF16)

    q = q_ref[0]
    q4 = jnp.concatenate([q[:, h * HEAD_DIM:(h + 1) * HEAD_DIM] for h in range(group)], axis=0)
    _flash_init(m, l, acc)

    def body(j, carry):
        off = pl.multiple_of(j * tq, tq)
        kb = kbf[pl.ds(off, tq), :]
        vtb = vt_ref[:, pl.ds(off, tq)]
        bias = bias_ref[0, pl.ds(off, tq), :].astype(F32)
        s = _dot_nt(kb, q4) + jnp.concatenate([bias] * group, axis=1)
        _flash_step(s, vtb, m, l, acc, axis=0)
        return carry

    lax.fori_loop(0, i + 1, body, 0)
    o_t = acc[...] / l[...]
    g = g_ref[0]
    for h in range(group):
        sl = slice(h * HEAD_DIM, (h + 1) * HEAD_DIM)
        o_ref[0, :, sl] = (o_t[:, h * tq:(h + 1) * tq].T * g[:, sl]).astype(BF16)


def _dsa_attn(qs, ks, vs_t, bias_t, sgs, tq):
    b, t, d_b = qs.shape
    n_kv = ks.shape[2] // HEAD_DIM
    group = d_b // HEAD_DIM // n_kv
    gw = group * HEAD_DIM
    qmap = lambda bi, n, i: (bi, i, n)
    kmap = lambda bi, n, i: (bi, 0, n)
    return pl.pallas_call(
        functools.partial(_dsa_attn_kernel, tq=tq, group=group),
        grid=(b, n_kv, t // tq),
        in_specs=[
            pl.BlockSpec((1, tq, gw), qmap),
            pl.BlockSpec((1, t, HEAD_DIM), kmap),
            pl.BlockSpec((HEAD_DIM, t), lambda bi, n, i: (n, bi)),
            pl.BlockSpec((1, t, tq), lambda bi, n, i: (bi, 0, i)),
            pl.BlockSpec((1, tq, gw), qmap),
        ],
        out_specs=pl.BlockSpec((1, tq, gw), qmap),
        out_shape=jax.ShapeDtypeStruct((b, t, d_b), BF16),
        scratch_shapes=[
            pltpu.VMEM((t, HEAD_DIM), BF16),
            pltpu.VMEM((1, group * tq), F32), pltpu.VMEM((1, group * tq), F32),
            pltpu.VMEM((HEAD_DIM, group * tq), F32),
        ],
        compiler_params=_cparams(3),
        name="dsa_attn",
    )(qs, ks, vs_t, bias_t, sgs)


def _out_kernel(x_ref, a1_ref, a2_ref, p_ref, wo_ref, pw_ref, wg_ref, bg_ref, we_ref, y_ref, *, d_a):
    o = _dot(a1_ref[...], wo_ref[0:d_a, :]) + _dot(a2_ref[...], wo_ref[d_a:, :])
    x1 = x_ref[...] + _rms_rows(o, pw_ref[...])
    z = _dot(x1.astype(BF16), wg_ref[...]) + bg_ref[...]
    gate = 1.0 / (1.0 + jnp.exp(-z))
    y_ref[...] = x1 + gate * _dot(p_ref[...].astype(BF16), we_ref[...])


def _out_proj(x2d, a1, a2, p2d, wo, pw, wg, bg, we, tm):
    m, d = x2d.shape
    d_a = a1.shape[1]
    d_mix = wo.shape[0]
    d_ple = p2d.shape[1]
    row = lambda i: (i, 0)
    fixed = lambda i: (0, 0)
    single = dict(pipeline_mode=pl.Buffered(1))
    return pl.pallas_call(
        functools.partial(_out_kernel, d_a=d_a),
        grid=(m // tm,),
        in_specs=[
            pl.BlockSpec((tm, d), row),
            pl.BlockSpec((tm, d_a), row),
            pl.BlockSpec((tm, d_mix - d_a), row),
            pl.BlockSpec((tm, d_ple), row),
            pl.BlockSpec((d_mix, d), fixed, **single),
            pl.BlockSpec((1, d), fixed),
            pl.BlockSpec((d, d), fixed, **single),
            pl.BlockSpec((1, d), fixed),
            pl.BlockSpec((d_ple, d), fixed, **single),
        ],
        out_specs=pl.BlockSpec((tm, d), row),
        out_shape=jax.ShapeDtypeStruct((m, d), F32),
        compiler_params=_cparams(1),
        name="out_proj",
    )(x2d, a1, a2, p2d, wo, pw, wg, bg, we)


def _sample_scores_kernel(pt_ref, qi_ref, wi_ref, *rest, n_group):
    del pt_ref
    page_refs, out_ref = rest[:n_group], rest[n_group]
    q4 = qi_ref[0]
    w = wi_ref[0]
    for r in range(n_group):
        kp_t = page_refs[r][0].astype(BF16)
        rel = jnp.maximum(_dot(q4, kp_t), 0.0)
        out_ref[0, r:r + 1, :] = jnp.sum(rel * w, axis=0, keepdims=True)


def _sample_scores(page_table_flat, qi4, wi3, cache_idx_t, n_pages, n_group):
    nb = qi4.shape[0]
    page = cache_idx_t.shape[2]

    def page_map(r):
        return lambda bi, g, pt: (pt[bi * n_pages + g * n_group + r], 0, 0)

    grid_spec = pltpu.PrefetchScalarGridSpec(
        num_scalar_prefetch=1,
        grid=(nb, n_pages // n_group),
        in_specs=[
            pl.BlockSpec((1, H_IDX, D_IDX), lambda bi, g, pt: (bi, 0, 0)),
            pl.BlockSpec((1, H_IDX, 1), lambda bi, g, pt: (bi, 0, 0)),
        ] + [pl.BlockSpec((1, D_IDX, page), page_map(r)) for r in range(n_group)],
        out_specs=pl.BlockSpec((1, n_group, page), lambda bi, g, pt: (bi, g, 0)),
    )
    return pl.pallas_call(
        functools.partial(_sample_scores_kernel, n_group=n_group),
        grid_spec=grid_spec,
        out_shape=jax.ShapeDtypeStruct((nb, n_pages, page), F32),
        compiler_params=_cparams(2),
        name="sample_scores",
    )(page_table_flat, qi4, wi3, *([cache_idx_t] * n_group))


def _sample_select_kernel(sc_ref, qi_ref, kit_ref, wi_ref, bias_ref, key_ref, p_ref, *, k_sel, idx_bits):
    rows, s_past = sc_ref.shape
    s_all = key_ref.shape[1]
    prod = qi_ref[...].astype(F32) * kit_ref[...].astype(F32)
    lane = lax.broadcasted_iota(I32, prod.shape, 1)
    wi = wi_ref[...]
    new = jnp.zeros((rows, 1), F32)
    for h in range(H_IDX):
        dot_h = jnp.sum(jnp.where(lane // D_IDX == h, prod, 0.0), axis=-1, keepdims=True)
        new = new + jnp.maximum(dot_h, 0.0) * wi[:, h:h + 1]
    key_ref[:, 0:s_past] = _float_key(sc_ref[...])
    tail_lane = lax.broadcasted_iota(I32, (rows, s_all - s_past), 1)
    key_ref[:, s_past:s_all] = jnp.where(tail_lane == 0, _float_key(new), NEG_INF_KEY)
    thr = _select_topk(key_ref, p_ref, 1, s_all, k_sel, idx_bits, axis=1)
    bias_ref[...] = _selected_bias(key_ref[...], 0, thr, p_ref[...], 1)


def _sample_select(scores2d, qi, ki_tiled, wi, k_sel):
    rows, s_past = scores2d.shape
    s_all = s_past + LANES
    idx_bits = max(1, (s_all - 1).bit_length())
    return pl.pallas_call(
        functools.partial(_sample_select_kernel, k_sel=k_sel, idx_bits=idx_bits),
        out_shape=jax.ShapeDtypeStruct((rows, s_all), F32),
        scratch_shapes=[pltpu.VMEM((rows, s_all), I32), pltpu.VMEM((rows, 1), I32)],
        compiler_params=pltpu.CompilerParams(vmem_limit_bytes=VMEM_LIMIT),
        name="sample_select",
    )(scores2d, qi, ki_tiled, wi)


def _head_rows(row, n_rows, rows_per_head):
    return jnp.concatenate(
        [row[:, (c // rows_per_head) * HEAD_DIM:(c // rows_per_head + 1) * HEAD_DIM]
         for c in range(n_rows)], axis=0)


def _sample_attn_kernel(pt_ref, qa_ref, qs_ref, kan_ref, van_ref, ksn_ref, vsn_ref, bnew_ref,
                        ga_ref, gs_ref, subw_ref, lamp_ref, *rest, n_pg, n_ha, n_hb, group, lam_init):
    del pt_ref
    kd_refs, vd_refs = rest[0:n_pg], rest[n_pg:2 * n_pg]
    ks_refs, vs_refs = rest[2 * n_pg:3 * n_pg], rest[3 * n_pg:4 * n_pg]
    bias_ref, oa_ref, ob_ref, qd, qsb, md, ld, accd, ms, ls, accs = rest[4 * n_pg:]
    p = pl.program_id(1)
    n_p = pl.num_programs(1)
    n_kv = n_hb // group

    @pl.when(p == 0)
    def _():
        qd_f = _head_rows(qa_ref[0].astype(F32), 2 * n_ha, 2)
        rd = lax.broadcasted_iota(I32, qd_f.shape, 0)
        ln = lax.broadcasted_iota(I32, qd_f.shape, 1)
        qd_f = jnp.where(ln // DH_A == rd % 2, qd_f, 0.0)
        qd[...] = qd_f.astype(BF16)
        qs_f = _head_rows(qs_ref[0].astype(F32), n_hb, 1)
        qsb[...] = qs_f.astype(BF16)
        kan = _head_rows(kan_ref[0].astype(BF16).astype(F32), 2 * n_ha, 2)
        van = _head_rows(van_ref[0].astype(BF16).astype(F32), 2 * n_ha, 2)
        md[...] = jnp.sum(qd_f * kan, axis=-1, keepdims=True)
        ld[...] = jnp.ones(ld.shape, F32)
        accd[...] = van
        ksn = _head_rows(ksn_ref[0].astype(BF16).astype(F32), n_hb, group)
        vsn = _head_rows(vsn_ref[0].astype(BF16).astype(F32), n_hb, group)
        ms[...] = jnp.sum(qs_f * ksn, axis=-1, keepdims=True) + bnew_ref[0]
        ls[...] = jnp.ones(ls.shape, F32)
        accs[...] = vsn

    sd_list, vd_list, ss_list, vs_list = [], [], [], []
    for r in range(n_pg):
        kd = kd_refs[r][0].astype(BF16)
        sd = _dot_nt(qd[...], kd)
        rq = lax.broadcasted_iota(I32, sd.shape, 0)
        ck = lax.broadcasted_iota(I32, sd.shape, 1)
        sd_list.append(jnp.where(ck % n_ha == rq // 2, sd, NEG_BIG))
        vd_list.append(vd_refs[r][0].astype(BF16))
        ksp = ks_refs[r][0].astype(BF16)
        ss = _dot_nt(qsb[...], ksp) + bias_ref[0, r]
        rq = lax.broadcasted_iota(I32, ss.shape, 0)
        ck = lax.broadcasted_iota(I32, ss.shape, 1)
        ss_list.append(jnp.where(ck % n_kv == rq // group, ss, NEG_BIG))
        vs_list.append(vs_refs[r][0].astype(BF16))
    _flash_step_multi(sd_list, vd_list, md, ld, accd)
    _flash_step_multi(ss_list, vs_list, ms, ls, accs)

    @pl.when(p == n_p - 1)
    def _():
        lam = _lambda_full(lamp_ref[...], lam_init)
        od = accd[...] / ld[...]
        ga = ga_ref[0]
        subw = subw_ref[...]
        for h in range(n_ha):
            sl = slice(h * HEAD_DIM, (h + 1) * HEAD_DIM)
            o = od[2 * h:2 * h + 1, :] - lam * od[2 * h + 1:2 * h + 2, :]
            o = _rms_rows(o, subw) * (1.0 - lam_init)
            oa_ref[0, :, sl] = (o * ga[:, sl]).astype(BF16)
        os_ = accs[...] / ls[...]
        gs = gs_ref[0]
        for h in range(n_hb):
            sl = slice(h * HEAD_DIM, (h + 1) * HEAD_DIM)
            ob_ref[0, :, sl] = (os_[h:h + 1, :] * gs[:, sl]).astype(BF16)


def _sample_attn(page_table_flat, qa, qs, ka_new, va_new, ks_new, vs_new, bias_new, sga, sgs,
                 subw, lamp, cdk, cdv, csk, csv, bias_pages, n_pages, lam_init):
    nb, _, d_a = qa.shape
    d_b = qs.shape[2]
    d_kv = ks_new.shape[2]
    n_ha = d_a // HEAD_DIM
    n_hb = d_b // HEAD_DIM
    n_kv = d_kv // HEAD_DIM
    group = n_hb // n_kv
    rows_d = cdk.shape[1]
    rows_s = csk.shape[1]
    per_b = lambda bi, p, pt: (bi, 0, 0)
    fixed = lambda bi, p, pt: (0, 0)
    n_pg = SAMPLE_PAGES_PER_STEP if n_pages % SAMPLE_PAGES_PER_STEP == 0 else 1

    def paged(r):
        return lambda bi, p, pt: (pt[bi * n_pages + p * n_pg + r], 0, 0)

    grid_spec = pltpu.PrefetchScalarGridSpec(
        num_scalar_prefetch=1,
        grid=(nb, n_pages // n_pg),
        in_specs=[
            pl.BlockSpec((1, 1, d_a), per_b),
            pl.BlockSpec((1, 1, d_b), per_b),
            pl.BlockSpec((1, 1, d_a), per_b),
            pl.BlockSpec((1, 1, d_a), per_b),
            pl.BlockSpec((1, 1, d_kv), per_b),
            pl.BlockSpec((1, 1, d_kv), per_b),
            pl.BlockSpec((1, 1, 1), per_b),
            pl.BlockSpec((1, 1, d_a), per_b),
            pl.BlockSpec((1, 1, d_b), per_b),
            pl.BlockSpec((1, HEAD_DIM), fixed),
            pl.BlockSpec((4, DH_A), fixed),
        ] + [pl.BlockSpec((1, rows_d, HEAD_DIM), paged(r)) for r in range(n_pg)] * 2
        + [pl.BlockSpec((1, rows_s, HEAD_DIM), paged(r)) for r in range(n_pg)] * 2
        + [pl.BlockSpec((1, n_pg, 1, rows_s), lambda bi, p, pt: (bi, p, 0, 0))],
        out_specs=[pl.BlockSpec((1, 1, d_a), per_b), pl.BlockSpec((1, 1, d_b), per_b)],
        scratch_shapes=[
            pltpu.VMEM((2 * n_ha, HEAD_DIM), BF16), pltpu.VMEM((n_hb, HEAD_DIM), BF16),
            pltpu.VMEM((2 * n_ha, 1), F32), pltpu.VMEM((2 * n_ha, 1), F32),
            pltpu.VMEM((2 * n_ha, HEAD_DIM), F32),
            pltpu.VMEM((n_hb, 1), F32), pltpu.VMEM((n_hb, 1), F32), pltpu.VMEM((n_hb, HEAD_DIM), F32),
        ],
    )
    return pl.pallas_call(
        functools.partial(_sample_attn_kernel, n_pg=n_pg, n_ha=n_ha, n_hb=n_hb, group=group,
                          lam_init=lam_init),
        grid_spec=grid_spec,
        out_shape=[jax.ShapeDtypeStruct((nb, 1, d_a), BF16), jax.ShapeDtypeStruct((nb, 1, d_b), BF16)],
        compiler_params=_cparams(2),
        name="sample_attn",
    )(page_table_flat, qa, qs, ka_new, va_new, ks_new, vs_new, bias_new, sga, sgs, subw, lamp,
      *([cdk] * n_pg + [cdv] * n_pg + [csk] * n_pg + [csv] * n_pg), bias_pages)


def _row_tile(m, pref):
    return pref if m % pref == 0 else m


def _inproj_all(x2d, pos, wts, tm, emit_vt):
    tabs64 = _rope_tables(pos, DH_A)
    tabs128 = _rope_tables(pos, HEAD_DIM)
    d_a, d_b, d_kv, d_qi = wts["d_a"], wts["d_b"], wts["d_kv"], wts["d_qi"]
    outs_a = _inproj_a(x2d, wts["pre_w"], wts["w_a"], tabs64, tm, d_a, emit_vt)
    outs_b = _inproj_b(x2d, wts["pre_w"], wts["w_b"], tabs128, tabs64, tm, d_b, d_kv, d_qi, emit_vt)
    return outs_a, outs_b


def kernel(x_prompt, x_sample, p_prompt, p_sample, cache_diff_k, cache_diff_v, cache_dsa_k, cache_dsa_v, cache_idx_k, page_table, pre_norm_w, post_norm_w, w_in, lam_q1, lam_k1, lam_q2, lam_k2, diff_norm_w, w_out, w_ple_gate, b_ple_gate, w_ple_proj):
    depth = w_in.shape[0]
    assert depth == 1, "single-layer stack only"
    bsz, t_p, d = x_prompt.shape
    nb, t_s, _ = x_sample.shape
    assert t_s == 1, "one new token per sample sequence"
    n_pages = page_table.shape[1]
    n_pool = cache_diff_k.shape[1]
    page = cache_diff_k.shape[2]
    n_ha = cache_diff_k.shape[3]
    n_kv = cache_dsa_k.shape[3]
    d_a = n_ha * HEAD_DIM
    d_kv = n_kv * HEAD_DIM
    d_b = d - d_a
    d_qi = H_IDX * D_IDX
    past_len = n_pages * page
    lam_init = 0.8 - 0.6 * math.exp(-0.3 * 0)
    n_in = w_in.shape[2]

    w = w_in[0]
    w_a = w[:, :4 * d_a].astype(BF16)
    n_b = n_in - 4 * d_a
    n_b_pad = -(-n_b // LANES) * LANES
    w_b = jnp.pad(w[:, 4 * d_a:], ((0, 0), (0, n_b_pad - n_b))).astype(BF16)
    wts = dict(pre_w=pre_norm_w[0][None, :], w_a=w_a, w_b=w_b, d_a=d_a, d_b=d_b, d_kv=d_kv, d_qi=d_qi)
    wo = w_out[0].astype(BF16)
    wg = w_ple_gate[0].astype(BF16)
    we = w_ple_proj[0].astype(BF16)
    pw = post_norm_w[0][None, :]
    bg = b_ple_gate[0][None, :]
    subw = diff_norm_w[0][None, :]
    lamp = jnp.stack([lam_q1[0], lam_k1[0], lam_q2[0], lam_k2[0]], axis=0)

    m_p = bsz * t_p
    tm = _row_tile(t_p, ROW_TILE)
    tq = _row_tile(t_p, ATTN_TILE)
    xp2 = x_prompt.reshape(m_p, d)
    pos_p = jnp.arange(t_p, dtype=I32)
    (qa, ka, va, sga, va_t), (qs, ks, vs, sgs, qi, kiw, vs_t) = _inproj_all(xp2, pos_p, wts, tm, True)
    r3 = lambda a: a.reshape(bsz, t_p, a.shape[-1])
    a1 = _diff_attn(r3(qa), r3(ka), va_t, r3(sga), subw, lamp, lam_init, tq)
    ki = kiw[:, :D_IDX]
    wi = kiw[:, D_IDX:D_IDX + H_IDX]
    qi4 = qi.reshape(bsz, t_p, H_IDX, D_IDX).transpose(0, 2, 1, 3)
    wi_t = wi.reshape(bsz, t_p, H_IDX).transpose(0, 2, 1)
    k_sel_p = min(TOPK_MAX, t_p // 4)
    bias_t = _index_select(qi4, ki.astype(BF16).reshape(bsz, t_p, D_IDX), wi_t, tq, k_sel_p)
    a2 = _dsa_attn(r3(qs), r3(ks), vs_t, bias_t, r3(sgs), tq)
    y_p = _out_proj(xp2, a1.reshape(m_p, d_a), a2.reshape(m_p, d_b), p_prompt[0].reshape(m_p, -1),
                    wo, pw, wg, bg, we, tm)

    xs2 = x_sample.reshape(nb, d)
    pos_s = jnp.full((nb,), past_len, dtype=I32)
    (qa_s, ka_s, va_s, sga_s), (qs_s, ks_s, vs_s, sgs_s, qi_s, kiw_s) = _inproj_all(xs2, pos_s, wts, nb, False)
    ki_s = kiw_s[:, :D_IDX]
    wi_s = kiw_s[:, D_IDX:D_IDX + H_IDX]
    pt_flat = page_table.reshape(-1)
    n_group = 8 if n_pages % 8 == 0 else 1
    cache_idx_t = jnp.swapaxes(cache_idx_k.reshape(n_pool, page, D_IDX), 1, 2)
    scores = _sample_scores(pt_flat, qi_s.reshape(nb, H_IDX, D_IDX), wi_s.reshape(nb, H_IDX, 1),
                            cache_idx_t, n_pages, n_group)
    k_sel_s = min(TOPK_MAX, (past_len + t_s) // 4)
    bias_s = _sample_select(scores.reshape(nb, past_len), qi_s,
                            jnp.tile(ki_s.astype(BF16), (1, H_IDX)), wi_s, k_sel_s)
    bias_pages = jnp.repeat(bias_s[:, :past_len], n_kv, axis=1).reshape(nb, n_pages, 1, page * n_kv)
    bias_new = bias_s[:, past_len:past_len + 1].reshape(nb, 1, 1)
    e3 = lambda a: a.reshape(nb, 1, a.shape[-1])
    a1_s, a2_s = _sample_attn(
        pt_flat, e3(qa_s), e3(qs_s), e3(ka_s), e3(va_s), e3(ks_s), e3(vs_s), bias_new,
        e3(sga_s), e3(sgs_s), subw, lamp,
        cache_diff_k.reshape(n_pool, page * n_ha, HEAD_DIM), cache_diff_v.reshape(n_pool, page * n_ha, HEAD_DIM),
        cache_dsa_k.reshape(n_pool, page * n_kv, HEAD_DIM), cache_dsa_v.reshape(n_pool, page * n_kv, HEAD_DIM),
        bias_pages, n_pages, lam_init)
    y_s = _out_proj(xs2, a1_s.reshape(nb, d_a), a2_s.reshape(nb, d_b), p_sample[0].reshape(nb, -1),
                    wo, pw, wg, bg, we, nb)

    return (
        y_p.reshape(bsz, t_p, d), y_s.reshape(nb, t_s, d),
        ka.reshape(1, bsz, t_p, n_ha, HEAD_DIM), va.reshape(1, bsz, t_p, n_ha, HEAD_DIM),
        ks.reshape(1, bsz, t_p, n_kv, HEAD_DIM), vs.reshape(1, bsz, t_p, n_kv, HEAD_DIM),
        ki.reshape(1, bsz, t_p, D_IDX),
        ka_s.reshape(1, nb, t_s, n_ha, HEAD_DIM), va_s.reshape(1, nb, t_s, n_ha, HEAD_DIM),
        ks_s.reshape(1, nb, t_s, n_kv, HEAD_DIM), vs_s.reshape(1, nb, t_s, n_kv, HEAD_DIM),
        ki_s.reshape(1, nb, t_s, D_IDX),
    )
```

```python
import functools
import math

import jax
import jax.numpy as jnp
from jax import lax
from jax.experimental import pallas as pl
from jax.experimental.pallas import tpu as pltpu

F32 = jnp.float32
BF16 = jnp.bfloat16
I32 = jnp.int32

LANES = 128
SUBLANES = 8
HEAD_DIM = 128
DH_A = HEAD_DIM // 2
D_IDX = 64
H_IDX = 4
TOPK_MAX = 256
ROPE_THETA = 500000.0
ROPE_FRAC = 4
RMS_EPS = 1e-6
NEG_BIG = -1e30
INT_MIN = -(2 ** 31)
NEG_INF_KEY = INT_MIN + 0x7FFFFF
VMEM_LIMIT = 56 * 1024 * 1024
ROW_TILE = 256
LOG2E = math.log2(math.e)
DIFF_TILE = 512
DSA_TILE = 256
IDX_TILE = 512
SAMPLE_PAGES_PER_STEP = 4


def _cparams(n_axes):
    return pltpu.CompilerParams(
        dimension_semantics=("arbitrary",) * n_axes, vmem_limit_bytes=VMEM_LIMIT)


def _dot(a, b):
    return jnp.dot(a, b, preferred_element_type=F32)


def _dot_nt(a, b):
    return lax.dot_general(a, b, (((1,), (1,)), ((), ())), preferred_element_type=F32)


def _rope_tables(pos, d):
    r = d // ROPE_FRAC
    half = r // 2
    inv = ROPE_THETA ** (-(2.0 / r) * jnp.arange(half, dtype=F32))
    ang = pos.astype(F32)[:, None] * inv[None, :]
    cos, sin = jnp.cos(ang), jnp.sin(ang)
    t = pos.shape[0]
    ones = jnp.ones((t, d - r), F32)
    zeros_h = jnp.zeros((t, half), F32)
    zeros_r = jnp.zeros((t, d - r), F32)
    c = jnp.concatenate([cos, cos, ones], axis=-1)
    sm = jnp.concatenate([-sin, zeros_h, zeros_r], axis=-1)
    sp = jnp.concatenate([zeros_h, sin, zeros_r], axis=-1)
    rep = LANES // d
    return tuple(jnp.tile(a, (1, rep)) for a in (c, sm, sp)), half


def _rope_chunk(z, c, sm, sp, half):
    return (z * c + pltpu.roll(z, LANES - half, axis=1) * sm
            + pltpu.roll(z, half, axis=1) * sp)


def _silu(z):
    return z * (1.0 / (1.0 + jnp.exp(-z)))


def _rms_rows(x, w):
    return x * lax.rsqrt(jnp.mean(x * x, axis=-1, keepdims=True) + RMS_EPS) * w


def _inproj_a_kernel(x_ref, nw_ref, w_ref, c_ref, sm_ref, sp_ref,
                     qa_ref, ka_ref, va_ref, ga_ref, *maybe_vt_ref, half, d_a):
    h = _rms_rows(x_ref[...], nw_ref[...]).astype(BF16)
    c, sm, sp = c_ref[...], sm_ref[...], sp_ref[...]
    n_chunks = d_a // LANES
    zq = _dot(h, w_ref[:, 0:d_a])
    for j in range(n_chunks):
        sl = slice(j * LANES, (j + 1) * LANES)
        qa_ref[:, sl] = (_rope_chunk(zq[:, sl], c, sm, sp, half) * (LOG2E / math.sqrt(DH_A))).astype(BF16)
    zk = _dot(h, w_ref[:, d_a:2 * d_a])
    for j in range(n_chunks):
        sl = slice(j * LANES, (j + 1) * LANES)
        ka_ref[:, sl] = _rope_chunk(zk[:, sl], c, sm, sp, half)
    zv = _dot(h, w_ref[:, 2 * d_a:3 * d_a])
    va_ref[...] = zv
    if maybe_vt_ref:
        maybe_vt_ref[0][...] = zv.T.astype(BF16)
    ga_ref[...] = _silu(_dot(h, w_ref[:, 3 * d_a:4 * d_a]))


def _inproj_a(x2d, nw, w_a, tabs64, tm, d_a, emit_vt):
    m, d = x2d.shape
    (c, sm, sp), half = tabs64
    t_blocks = c.shape[0] // tm
    row = lambda i: (i, 0)
    tab = lambda i: (i % t_blocks, 0)
    fixed = lambda i: (0, 0)
    out_specs = [pl.BlockSpec((tm, d_a), row)] * 4
    out_shape = [
        jax.ShapeDtypeStruct((m, d_a), BF16),
        jax.ShapeDtypeStruct((m, d_a), F32),
        jax.ShapeDtypeStruct((m, d_a), F32),
        jax.ShapeDtypeStruct((m, d_a), F32),
    ]
    if emit_vt:
        out_specs.append(pl.BlockSpec((d_a, tm), lambda i: (0, i)))
        out_shape.append(jax.ShapeDtypeStruct((d_a, m), BF16))
    return pl.pallas_call(
        functools.partial(_inproj_a_kernel, half=half, d_a=d_a),
        grid=(m // tm,),
        in_specs=[
            pl.BlockSpec((tm, d), row),
            pl.BlockSpec((1, d), fixed),
            pl.BlockSpec((d, 4 * d_a), fixed, pipeline_mode=pl.Buffered(1)),
            pl.BlockSpec((tm, LANES), tab),
            pl.BlockSpec((tm, LANES), tab),
            pl.BlockSpec((tm, LANES), tab),
        ],
        out_specs=out_specs,
        out_shape=out_shape,
        compiler_params=_cparams(1),
        name="inproj_a",
    )(x2d, nw, w_a, c, sm, sp)


def _inproj_b_kernel(x_ref, nw_ref, w_ref, c128_ref, sm128_ref, sp128_ref,
                     c64_ref, sm64_ref, sp64_ref,
                     qs_ref, ks_ref, vs_ref, gs_ref, qi_ref, kiw_ref, *maybe_vt_ref,
                     half128, half64, d_b, d_kv, d_qi):
    h = _rms_rows(x_ref[...], nw_ref[...]).astype(BF16)
    c128, sm128, sp128 = c128_ref[...], sm128_ref[...], sp128_ref[...]
    c64, sm64, sp64 = c64_ref[...], sm64_ref[...], sp64_ref[...]
    o = 0
    zq = _dot(h, w_ref[:, o:o + d_b])
    for j in range(d_b // LANES):
        sl = slice(j * LANES, (j + 1) * LANES)
        qs_ref[:, sl] = (_rope_chunk(zq[:, sl], c128, sm128, sp128, half128)
                         * (LOG2E / math.sqrt(HEAD_DIM))).astype(BF16)
    o += d_b
    zk = _dot(h, w_ref[:, o:o + d_kv])
    for j in range(d_kv // LANES):
        sl = slice(j * LANES, (j + 1) * LANES)
        ks_ref[:, sl] = _rope_chunk(zk[:, sl], c128, sm128, sp128, half128)
    o += d_kv
    zv = _dot(h, w_ref[:, o:o + d_kv])
    vs_ref[...] = zv
    if maybe_vt_ref:
        maybe_vt_ref[0][...] = zv.T.astype(BF16)
    o += d_kv
    gs_ref[...] = _silu(_dot(h, w_ref[:, o:o + d_b]))
    o += d_b
    zi = _dot(h, w_ref[:, o:o + d_qi])
    for j in range(d_qi // LANES):
        sl = slice(j * LANES, (j + 1) * LANES)
        qi_ref[:, sl] = _rope_chunk(zi[:, sl], c64, sm64, sp64, half64).astype(BF16)
    o += d_qi
    zkw = _dot(h, w_ref[:, o:o + LANES])
    lane = lax.broadcasted_iota(I32, zkw.shape, 1)
    kiw_ref[...] = jnp.where(lane < D_IDX, _rope_chunk(zkw, c64, sm64, sp64, half64), zkw)


def _inproj_b(x2d, nw, w_b, tabs128, tabs64, tm, d_b, d_kv, d_qi, emit_vt):
    m, d = x2d.shape
    (c128, sm128, sp128), half128 = tabs128
    (c64, sm64, sp64), half64 = tabs64
    t_blocks = c128.shape[0] // tm
    row = lambda i: (i, 0)
    tab = lambda i: (i % t_blocks, 0)
    fixed = lambda i: (0, 0)
    n_b = w_b.shape[1]
    out_specs = [
        pl.BlockSpec((tm, d_b), row),
        pl.BlockSpec((tm, d_kv), row),
        pl.BlockSpec((tm, d_kv), row),
        pl.BlockSpec((tm, d_b), row),
        pl.BlockSpec((tm, d_qi), row),
        pl.BlockSpec((tm, LANES), row),
    ]
    out_shape = [
        jax.ShapeDtypeStruct((m, d_b), BF16),
        jax.ShapeDtypeStruct((m, d_kv), F32),
        jax.ShapeDtypeStruct((m, d_kv), F32),
        jax.ShapeDtypeStruct((m, d_b), F32),
        jax.ShapeDtypeStruct((m, d_qi), BF16),
        jax.ShapeDtypeStruct((m, LANES), F32),
    ]
    if emit_vt:
        out_specs.append(pl.BlockSpec((d_kv, tm), lambda i: (0, i)))
        out_shape.append(jax.ShapeDtypeStruct((d_kv, m), BF16))
    return pl.pallas_call(
        functools.partial(_inproj_b_kernel, half128=half128, half64=half64,
                          d_b=d_b, d_kv=d_kv, d_qi=d_qi),
        grid=(m // tm,),
        in_specs=[
            pl.BlockSpec((tm, d), row),
            pl.BlockSpec((1, d), fixed),
            pl.BlockSpec((d, n_b), fixed, pipeline_mode=pl.Buffered(1)),
        ] + [pl.BlockSpec((tm, LANES), tab)] * 6,
        out_specs=out_specs,
        out_shape=out_shape,
        compiler_params=_cparams(1),
        name="inproj_b",
    )(x2d, nw, w_b, c128, sm128, sp128, c64, sm64, sp64)


def _flash_init(m_ref, l_ref, acc_ref):
    m_ref[...] = jnp.full(m_ref.shape, NEG_BIG, F32)
    l_ref[...] = jnp.zeros(l_ref.shape, F32)
    acc_ref[...] = jnp.zeros(acc_ref.shape, F32)


def _flash_step(s, v_bf, m_ref, l_ref, acc_ref, axis=1):
    m_prev = m_ref[...]
    m_new = jnp.maximum(m_prev, jnp.max(s, axis=axis, keepdims=True))
    alpha = jnp.exp2(m_prev - m_new)
    p = jnp.exp2(s - m_new)
    l_ref[...] = alpha * l_ref[...] + jnp.sum(p, axis=axis, keepdims=True)
    pv = _dot(p.astype(BF16), v_bf) if axis == 1 else _dot(v_bf, p.astype(BF16))
    acc_ref[...] = alpha * acc_ref[...] + pv
    m_ref[...] = m_new


def _flash_step_multi(s_list, v_list, m_ref, l_ref, acc_ref):
    m_prev = m_ref[...]
    m_new = m_prev
    for s in s_list:
        m_new = jnp.maximum(m_new, jnp.max(s, axis=1, keepdims=True))
    alpha = jnp.exp2(m_prev - m_new)
    l_new = alpha * l_ref[...]
    acc = alpha * acc_ref[...]
    for s, v_bf in zip(s_list, v_list):
        p = jnp.exp2(s - m_new)
        l_new = l_new + jnp.sum(p, axis=1, keepdims=True)
        acc = acc + _dot(p.astype(BF16), v_bf)
    l_ref[...] = l_new
    acc_ref[...] = acc
    m_ref[...] = m_new


def _lambda_full(lamp, lam_init):
    s1 = jnp.sum(lamp[0:1, :] * lamp[1:2, :], axis=-1, keepdims=True)
    s2 = jnp.sum(lamp[2:3, :] * lamp[3:4, :], axis=-1, keepdims=True)
    return jnp.exp(s1) - jnp.exp(s2) + lam_init


def _diff_attn_kernel(q_ref, k_ref, vt_ref, g_ref, subw_ref, lamp_ref, o_ref,
                      kbf, s1b, s2b, m1, l1, a1, m2, l2, a2, *, tq, lam_init):
    i = pl.program_id(2)

    @pl.when(i == 0)
    def _():
        kbf[...] = k_ref[0].astype(BF16)

    q = q_ref[0]
    lane = lax.broadcasted_iota(I32, q.shape, 1)
    zero = jnp.zeros_like(q)
    q1 = jnp.where(lane < DH_A, q, zero)
    q2 = jnp.where(lane >= DH_A, q, zero)
    _flash_init(m1, l1, a1)
    _flash_init(m2, l2, a2)

    def scores(j, slot):
        kb = kbf[pl.ds(pl.multiple_of(j * tq, tq), tq), :]
        s1b[slot] = _dot_nt(kb, q1)
        s2b[slot] = _dot_nt(kb, q2)

    def softmax_pv(j, slot, masked):
        vtb = vt_ref[:, pl.ds(pl.multiple_of(j * tq, tq), tq)]
        s1 = s1b[slot]
        s2 = s2b[slot]
        if masked:
            kidx = lax.broadcasted_iota(I32, s1.shape, 0)
            qidx = lax.broadcasted_iota(I32, s1.shape, 1)
            vis = kidx <= qidx
            s1 = jnp.where(vis, s1, NEG_BIG)
            s2 = jnp.where(vis, s2, NEG_BIG)
        _flash_step(s1, vtb, m1, l1, a1, axis=0)
        _flash_step(s2, vtb, m2, l2, a2, axis=0)

    scores(0, 0)

    def pair(jj, carry):
        j = 2 * jj
        scores(j + 1, 1)
        softmax_pv(j, 0, False)
        scores(j + 2, 0)
        softmax_pv(j + 1, 1, False)
        return carry

    lax.fori_loop(0, i // 2, pair, 0)

    @pl.when(i % 2 == 1)
    def _():
        scores(i, 1)
        softmax_pv(i - 1, 0, False)
        softmax_pv(i, 1, True)

    @pl.when(i % 2 == 0)
    def _():
        softmax_pv(i, 0, True)

    lam = _lambda_full(lamp_ref[...], lam_init)
    o_t = a1[...] / l1[...] - lam * (a2[...] / l2[...])
    o = _rms_rows(o_t.T, subw_ref[...]) * (1.0 - lam_init)
    o_ref[0] = (o * g_ref[0]).astype(BF16)


def _diff_attn(qa, ka, va_t, sga, subw, lamp, lam_init, tq):
    b, t, d_a = qa.shape
    n_h = d_a // HEAD_DIM
    qmap = lambda bi, h, i: (bi, i, h)
    kmap = lambda bi, h, i: (bi, 0, h)
    fixed = lambda bi, h, i: (0, 0)
    return pl.pallas_call(
        functools.partial(_diff_attn_kernel, tq=tq, lam_init=lam_init),
        grid=(b, n_h, t // tq),
        in_specs=[
            pl.BlockSpec((1, tq, HEAD_DIM), qmap),
            pl.BlockSpec((1, t, HEAD_DIM), kmap),
            pl.BlockSpec((HEAD_DIM, t), lambda bi, h, i: (h, bi)),
            pl.BlockSpec((1, tq, HEAD_DIM), qmap),
            pl.BlockSpec((1, HEAD_DIM), fixed),
            pl.BlockSpec((4, DH_A), fixed),
        ],
        out_specs=pl.BlockSpec((1, tq, HEAD_DIM), qmap),
        out_shape=jax.ShapeDtypeStruct((b, t, d_a), BF16),
        scratch_shapes=[
            pltpu.VMEM((t, HEAD_DIM), BF16),
            pltpu.VMEM((2, tq, tq), F32), pltpu.VMEM((2, tq, tq), F32),
            pltpu.VMEM((1, tq), F32), pltpu.VMEM((1, tq), F32), pltpu.VMEM((HEAD_DIM, tq), F32),
            pltpu.VMEM((1, tq), F32), pltpu.VMEM((1, tq), F32), pltpu.VMEM((HEAD_DIM, tq), F32),
        ],
        compiler_params=_cparams(3),
        name="diff_attn",
    )(qa, ka, va_t, sga, subw, lamp)


def _key_to_float(key):
    key = jnp.maximum(key, NEG_INF_KEY)
    return lax.bitcast_convert_type(key ^ ((key >> 31) & 0x7FFFFFFF), F32)


def _fold(m, axis):
    if axis == 1:
        n = m.shape[1] // LANES
        acc = m[:, 0:LANES]
        for j in range(1, n):
            acc = acc + m[:, j * LANES:(j + 1) * LANES]
        return acc
    return jnp.sum(m.reshape(m.shape[0] // SUBLANES, SUBLANES, m.shape[1]), axis=0)


def _count(key_ref, n_chunks, chunk, pred, axis):
    other = key_ref.shape[1 - axis]
    acc_shape = (other, LANES) if axis == 1 else (SUBLANES, other)

    def body(c, acc):
        off = pl.multiple_of(c * chunk, chunk)
        kc = key_ref[:, pl.ds(off, chunk)] if axis == 1 else key_ref[pl.ds(off, chunk), :]
        return acc + _fold(pred(kc, off), axis)

    acc = lax.fori_loop(0, n_chunks, body, jnp.zeros(acc_shape, F32))
    return jnp.sum(acc, axis=axis, keepdims=True)


def _select_topk(key_ref, p_ref, n_chunks, chunk, k_sel, idx_bits, axis):
    k_f = float(k_sel)
    q_shape = p_ref.shape

    def bit_body(it, tu):
        bit = lax.shift_left(jnp.int32(1), 31 - it)
        cand_u = tu | bit
        cand = _key_to_float(cand_u ^ INT_MIN)
        cnt = _count(key_ref, n_chunks, chunk,
                     lambda kc, off: jnp.where(kc >= cand, 1.0, 0.0), axis)
        return jnp.where(cnt >= k_f, cand_u, tu)

    tu = lax.fori_loop(0, 32, bit_body, jnp.zeros(q_shape, I32))
    thr = _key_to_float(tu ^ INT_MIN)
    n_gt = _count(key_ref, n_chunks, chunk, lambda kc, off: jnp.where(kc > thr, 1.0, 0.0), axis)
    n_eq = _count(key_ref, n_chunks, chunk, lambda kc, off: jnp.where(kc == thr, 1.0, 0.0), axis)
    need = k_f - n_gt
    p_ref[...] = jnp.full(q_shape, 2 ** 30, I32)
    excess = jnp.max(n_eq - need)

    @pl.when(excess > 0.0)
    def _():
        def idx_body(it, p):
            bit = lax.shift_left(jnp.int32(1), idx_bits - 1 - it)
            cand = p | bit

            def pred(kc, off):
                idx = off + lax.broadcasted_iota(I32, kc.shape, axis)
                return jnp.where(kc == thr, jnp.where(idx < cand, 1.0, 0.0), 0.0)

            cnt = _count(key_ref, n_chunks, chunk, pred, axis)
            return jnp.where(cnt < need, cand, p)

        p_ref[...] = lax.fori_loop(0, idx_bits, idx_body, jnp.zeros(q_shape, I32))

    return thr


def _selected_bias(kc, off, thr, p_max, axis):
    idx = off + lax.broadcasted_iota(I32, kc.shape, axis)
    tie = jnp.where(kc == thr, jnp.where(idx <= p_max, 1.0, 0.0), 0.0)
    sel = jnp.where(kc > thr, 1.0, tie)
    sel = jnp.where(kc == -jnp.inf, 0.0, sel)
    return jnp.where(sel > 0.5, 0.0, NEG_BIG)


def _index_select_kernel(qi_ref, ki_ref, wit_ref, bias_ref, key_ref, p_ref, *, tq, k_sel, idx_bits):
    i = pl.program_id(1)
    n_total = bias_ref.shape[1] // tq
    wit = wit_ref[0]

    def score_body(c, carry):
        off = pl.multiple_of(c * tq, tq)
        kc = ki_ref[0, pl.ds(off, tq), :]
        sc = jnp.zeros((tq, tq), F32)
        for h in range(H_IDX):
            sc = sc + jnp.maximum(_dot_nt(kc, qi_ref[0, h]), 0.0) * wit[h:h + 1, :]
        kidx = lax.broadcasted_iota(I32, sc.shape, 0) + off
        qidx = lax.broadcasted_iota(I32, sc.shape, 1) + i * tq
        sc = jnp.where(kidx <= qidx, sc, -jnp.inf)
        key_ref[pl.ds(off, tq), :] = sc
        return carry

    lax.fori_loop(0, i + 1, score_body, 0)
    thr = _select_topk(key_ref, p_ref, i + 1, tq, k_sel, idx_bits, axis=0)
    p_max = p_ref[...]

    def out_body(c, carry):
        off = pl.multiple_of(c * tq, tq)
        kc = key_ref[pl.ds(off, tq), :]
        bias_ref[0, pl.ds(off, tq), :] = _selected_bias(kc, off, thr, p_max, 0).astype(BF16)
        return carry

    lax.fori_loop(0, i + 1, out_body, 0)

    def fill_body(c, carry):
        off = pl.multiple_of(c * tq, tq)
        bias_ref[0, pl.ds(off, tq), :] = jnp.full((tq, tq), NEG_BIG, BF16)
        return carry

    lax.fori_loop(i + 1, n_total, fill_body, 0)


def _index_select(qi4, ki_bf, wi_t, tq, k_sel):
    b, _, t, _ = qi4.shape
    idx_bits = max(1, (t - 1).bit_length())
    return pl.pallas_call(
        functools.partial(_index_select_kernel, tq=tq, k_sel=k_sel, idx_bits=idx_bits),
        grid=(b, t // tq),
        in_specs=[
            pl.BlockSpec((1, H_IDX, tq, D_IDX), lambda bi, i: (bi, 0, i, 0)),
            pl.BlockSpec((1, t, D_IDX), lambda bi, i: (bi, 0, 0)),
            pl.BlockSpec((1, H_IDX, tq), lambda bi, i: (bi, 0, i)),
        ],
        out_specs=pl.BlockSpec((1, t, tq), lambda bi, i: (bi, 0, i)),
        out_shape=jax.ShapeDtypeStruct((b, t, t), BF16),
        scratch_shapes=[pltpu.VMEM((t, tq), F32), pltpu.VMEM((1, tq), I32)],
        compiler_params=_cparams(2),
        name="index_select",
    )(qi4, ki_bf, wi_t)


def _dsa_attn_kernel(q_ref, k_ref, vt_ref, bias_ref, g_ref, o_ref, kbf, sb, m, l, acc, *, tq, group):
    i = pl.program_id(2)

    @pl.when(i == 0)
    def _():
        kbf[...] = k_ref[0].astype(BF16)

    q = q_ref[0]
    q4 = jnp.concatenate([q[:, h * HEAD_DIM:(h + 1) * HEAD_DIM] for h in range(group)], axis=0)
    _flash_init(m, l, acc)

    def scores(j, slot):
        kb = kbf[pl.ds(pl.multiple_of(j * tq, tq), tq), :]
        sb[slot] = _dot_nt(kb, q4)

    def softmax_pv(j, slot):
        off = pl.multiple_of(j * tq, tq)
        vtb = vt_ref[:, pl.ds(off, tq)]
        bias = bias_ref[0, pl.ds(off, tq), :].astype(F32)
        s = sb[slot] + jnp.concatenate([bias] * group, axis=1)
        _flash_step(s, vtb, m, l, acc, axis=0)

    scores(0, 0)

    def pair(jj, carry):
        j = 2 * jj
        scores(j + 1, 1)
        softmax_pv(j, 0)
        scores(jnp.minimum(j + 2, i), 0)
        softmax_pv(j + 1, 1)
        return carry

    lax.fori_loop(0, (i + 1) // 2, pair, 0)

    @pl.when(i % 2 == 0)
    def _():
        softmax_pv(i, 0)
    o_t = acc[...] / l[...]
    g = g_ref[0]
    for h in range(group):
        sl = slice(h * HEAD_DIM, (h + 1) * HEAD_DIM)
        o_ref[0, :, sl] = (o_t[:, h * tq:(h + 1) * tq].T * g[:, sl]).astype(BF16)


def _dsa_attn(qs, ks, vs_t, bias_t, sgs, tq):
    b, t, d_b = qs.shape
    n_kv = ks.shape[2] // HEAD_DIM
    group = d_b // HEAD_DIM // n_kv
    gw = group * HEAD_DIM
    qmap = lambda bi, n, i: (bi, i, n)
    kmap = lambda bi, n, i: (bi, 0, n)
    return pl.pallas_call(
        functools.partial(_dsa_attn_kernel, tq=tq, group=group),
        grid=(b, n_kv, t // tq),
        in_specs=[
            pl.BlockSpec((1, tq, gw), qmap),
            pl.BlockSpec((1, t, HEAD_DIM), kmap),
            pl.BlockSpec((HEAD_DIM, t), lambda bi, n, i: (n, bi)),
            pl.BlockSpec((1, t, tq), lambda bi, n, i: (bi, 0, i)),
            pl.BlockSpec((1, tq, gw), qmap),
        ],
        out_specs=pl.BlockSpec((1, tq, gw), qmap),
        out_shape=jax.ShapeDtypeStruct((b, t, d_b), BF16),
        scratch_shapes=[
            pltpu.VMEM((t, HEAD_DIM), BF16),
            pltpu.VMEM((2, tq, group * tq), F32),
            pltpu.VMEM((1, group * tq), F32), pltpu.VMEM((1, group * tq), F32),
            pltpu.VMEM((HEAD_DIM, group * tq), F32),
        ],
        compiler_params=_cparams(3),
        name="dsa_attn",
    )(qs, ks, vs_t, bias_t, sgs)


def _out_kernel(x_ref, a1_ref, a2_ref, p_ref, wo_ref, pw_ref, wg_ref, bg_ref, we_ref, y_ref, *, d_a):
    o = _dot(a1_ref[...], wo_ref[0:d_a, :]) + _dot(a2_ref[...], wo_ref[d_a:, :])
    x1 = x_ref[...] + _rms_rows(o, pw_ref[...])
    z = _dot(x1.astype(BF16), wg_ref[...]) + bg_ref[...]
    gate = 1.0 / (1.0 + jnp.exp(-z))
    y_ref[...] = x1 + gate * _dot(p_ref[...].astype(BF16), we_ref[...])


def _out_proj(x2d, a1, a2, p2d, wo, pw, wg, bg, we, tm):
    m, d = x2d.shape
    d_a = a1.shape[1]
    d_mix = wo.shape[0]
    d_ple = p2d.shape[1]
    row = lambda i: (i, 0)
    fixed = lambda i: (0, 0)
    single = dict(pipeline_mode=pl.Buffered(1))
    return pl.pallas_call(
        functools.partial(_out_kernel, d_a=d_a),
        grid=(m // tm,),
        in_specs=[
            pl.BlockSpec((tm, d), row),
            pl.BlockSpec((tm, d_a), row),
            pl.BlockSpec((tm, d_mix - d_a), row),
            pl.BlockSpec((tm, d_ple), row),
            pl.BlockSpec((d_mix, d), fixed, **single),
            pl.BlockSpec((1, d), fixed),
            pl.BlockSpec((d, d), fixed, **single),
            pl.BlockSpec((1, d), fixed),
            pl.BlockSpec((d_ple, d), fixed, **single),
        ],
        out_specs=pl.BlockSpec((tm, d), row),
        out_shape=jax.ShapeDtypeStruct((m, d), F32),
        compiler_params=_cparams(1),
        name="out_proj",
    )(x2d, a1, a2, p2d, wo, pw, wg, bg, we)


def _sample_scores_kernel(pt_ref, qi_ref, wi_ref, *rest, n_group):
    del pt_ref
    page_refs, out_ref = rest[:n_group], rest[n_group]
    q4 = qi_ref[0]
    w = wi_ref[0]
    for r in range(n_group):
        kp_t = page_refs[r][0].astype(BF16)
        rel = jnp.maximum(_dot(q4, kp_t), 0.0)
        out_ref[0, r:r + 1, :] = jnp.sum(rel * w, axis=0, keepdims=True)


def _sample_scores(page_table_flat, qi4, wi3, cache_idx_t, n_pages, n_group):
    nb = qi4.shape[0]
    page = cache_idx_t.shape[2]

    def page_map(r):
        return lambda bi, g, pt: (pt[bi * n_pages + g * n_group + r], 0, 0)

    grid_spec = pltpu.PrefetchScalarGridSpec(
        num_scalar_prefetch=1,
        grid=(nb, n_pages // n_group),
        in_specs=[
            pl.BlockSpec((1, H_IDX, D_IDX), lambda bi, g, pt: (bi, 0, 0)),
            pl.BlockSpec((1, H_IDX, 1), lambda bi, g, pt: (bi, 0, 0)),
        ] + [pl.BlockSpec((1, D_IDX, page), page_map(r)) for r in range(n_group)],
        out_specs=pl.BlockSpec((1, n_group, page), lambda bi, g, pt: (bi, g, 0)),
    )
    return pl.pallas_call(
        functools.partial(_sample_scores_kernel, n_group=n_group),
        grid_spec=grid_spec,
        out_shape=jax.ShapeDtypeStruct((nb, n_pages, page), F32),
        compiler_params=_cparams(2),
        name="sample_scores",
    )(page_table_flat, qi4, wi3, *([cache_idx_t] * n_group))


def _sample_select_kernel(sc_ref, qi_ref, kit_ref, wi_ref, bias_ref, key_ref, p_ref, *, k_sel, idx_bits):
    rows, s_past = sc_ref.shape
    s_all = key_ref.shape[1]
    prod = qi_ref[...].astype(F32) * kit_ref[...].astype(F32)
    lane = lax.broadcasted_iota(I32, prod.shape, 1)
    wi = wi_ref[...]
    new = jnp.zeros((rows, 1), F32)
    for h in range(H_IDX):
        dot_h = jnp.sum(jnp.where(lane // D_IDX == h, prod, 0.0), axis=-1, keepdims=True)
        new = new + jnp.maximum(dot_h, 0.0) * wi[:, h:h + 1]
    key_ref[:, 0:s_past] = sc_ref[...]
    tail_lane = lax.broadcasted_iota(I32, (rows, s_all - s_past), 1)
    key_ref[:, s_past:s_all] = jnp.where(tail_lane == 0, new, -jnp.inf)
    thr = _select_topk(key_ref, p_ref, 1, s_all, k_sel, idx_bits, axis=1)
    bias_ref[...] = _selected_bias(key_ref[...], 0, thr, p_ref[...], 1)


def _sample_select(scores2d, qi, ki_tiled, wi, k_sel):
    rows, s_past = scores2d.shape
    s_all = s_past + LANES
    idx_bits = max(1, (s_all - 1).bit_length())
    return pl.pallas_call(
        functools.partial(_sample_select_kernel, k_sel=k_sel, idx_bits=idx_bits),
        out_shape=jax.ShapeDtypeStruct((rows, s_all), F32),
        scratch_shapes=[pltpu.VMEM((rows, s_all), F32), pltpu.VMEM((rows, 1), I32)],
        compiler_params=pltpu.CompilerParams(vmem_limit_bytes=VMEM_LIMIT),
        name="sample_select",
    )(scores2d, qi, ki_tiled, wi)


def _head_rows(row, n_rows, rows_per_head):
    return jnp.concatenate(
        [row[:, (c // rows_per_head) * HEAD_DIM:(c // rows_per_head + 1) * HEAD_DIM]
         for c in range(n_rows)], axis=0)


def _sample_attn_kernel(pt_ref, qa_ref, qs_ref, kan_ref, van_ref, ksn_ref, vsn_ref, bnew_ref,
                        ga_ref, gs_ref, subw_ref, lamp_ref, *rest, n_pg, n_ha, n_hb, group, lam_init):
    del pt_ref
    kd_refs, vd_refs = rest[0:n_pg], rest[n_pg:2 * n_pg]
    ks_refs, vs_refs = rest[2 * n_pg:3 * n_pg], rest[3 * n_pg:4 * n_pg]
    bias_ref, oa_ref, ob_ref, qd, qsb, md, ld, accd, ms, ls, accs = rest[4 * n_pg:]
    p = pl.program_id(1)
    n_p = pl.num_programs(1)
    n_kv = n_hb // group

    @pl.when(p == 0)
    def _():
        qd_f = _head_rows(qa_ref[0].astype(F32), 2 * n_ha, 2)
        rd = lax.broadcasted_iota(I32, qd_f.shape, 0)
        ln = lax.broadcasted_iota(I32, qd_f.shape, 1)
        qd_f = jnp.where(ln // DH_A == rd % 2, qd_f, 0.0)
        qd[...] = qd_f.astype(BF16)
        qs_f = _head_rows(qs_ref[0].astype(F32), n_hb, 1)
        qsb[...] = qs_f.astype(BF16)
        kan = _head_rows(kan_ref[0].astype(BF16).astype(F32), 2 * n_ha, 2)
        van = _head_rows(van_ref[0].astype(BF16).astype(F32), 2 * n_ha, 2)
        md[...] = jnp.sum(qd_f * kan, axis=-1, keepdims=True)
        ld[...] = jnp.ones(ld.shape, F32)
        accd[...] = van
        ksn = _head_rows(ksn_ref[0].astype(BF16).astype(F32), n_hb, group)
        vsn = _head_rows(vsn_ref[0].astype(BF16).astype(F32), n_hb, group)
        ms[...] = jnp.sum(qs_f * ksn, axis=-1, keepdims=True) + bnew_ref[0]
        ls[...] = jnp.ones(ls.shape, F32)
        accs[...] = vsn

    sd_list, vd_list, ss_list, vs_list = [], [], [], []
    for r in range(n_pg):
        kd = kd_refs[r][0].astype(BF16)
        sd = _dot_nt(qd[...], kd)
        rq = lax.broadcasted_iota(I32, sd.shape, 0)
        ck = lax.broadcasted_iota(I32, sd.shape, 1)
        sd_list.append(jnp.where(ck % n_ha == rq // 2, sd, NEG_BIG))
        vd_list.append(vd_refs[r][0].astype(BF16))
        ksp = ks_refs[r][0].astype(BF16)
        ss = _dot_nt(qsb[...], ksp) + bias_ref[0, r]
        rq = lax.broadcasted_iota(I32, ss.shape, 0)
        ck = lax.broadcasted_iota(I32, ss.shape, 1)
        ss_list.append(jnp.where(ck % n_kv == rq // group, ss, NEG_BIG))
        vs_list.append(vs_refs[r][0].astype(BF16))
    _flash_step_multi(sd_list, vd_list, md, ld, accd)
    _flash_step_multi(ss_list, vs_list, ms, ls, accs)

    @pl.when(p == n_p - 1)
    def _():
        lam = _lambda_full(lamp_ref[...], lam_init)
        od = accd[...] / ld[...]
        ga = ga_ref[0]
        subw = subw_ref[...]
        for h in range(n_ha):
            sl = slice(h * HEAD_DIM, (h + 1) * HEAD_DIM)
            o = od[2 * h:2 * h + 1, :] - lam * od[2 * h + 1:2 * h + 2, :]
            o = _rms_rows(o, subw) * (1.0 - lam_init)
            oa_ref[0, :, sl] = (o * ga[:, sl]).astype(BF16)
        os_ = accs[...] / ls[...]
        gs = gs_ref[0]
        for h in range(n_hb):
            sl = slice(h * HEAD_DIM, (h + 1) * HEAD_DIM)
            ob_ref[0, :, sl] = (os_[h:h + 1, :] * gs[:, sl]).astype(BF16)


def _sample_attn(page_table_flat, qa, qs, ka_new, va_new, ks_new, vs_new, bias_new, sga, sgs,
                 subw, lamp, cdk, cdv, csk, csv, bias_pages, n_pages, lam_init):
    nb, _, d_a = qa.shape
    d_b = qs.shape[2]
    d_kv = ks_new.shape[2]
    n_ha = d_a // HEAD_DIM
    n_hb = d_b // HEAD_DIM
    n_kv = d_kv // HEAD_DIM
    group = n_hb // n_kv
    rows_d = cdk.shape[1]
    rows_s = csk.shape[1]
    per_b = lambda bi, p, pt: (bi, 0, 0)
    fixed = lambda bi, p, pt: (0, 0)
    n_pg = SAMPLE_PAGES_PER_STEP if n_pages % SAMPLE_PAGES_PER_STEP == 0 else 1

    def paged(r):
        return lambda bi, p, pt: (pt[bi * n_pages + p * n_pg + r], 0, 0)

    grid_spec = pltpu.PrefetchScalarGridSpec(
        num_scalar_prefetch=1,
        grid=(nb, n_pages // n_pg),
        in_specs=[
            pl.BlockSpec((1, 1, d_a), per_b),
            pl.BlockSpec((1, 1, d_b), per_b),
            pl.BlockSpec((1, 1, d_a), per_b),
            pl.BlockSpec((1, 1, d_a), per_b),
            pl.BlockSpec((1, 1, d_kv), per_b),
            pl.BlockSpec((1, 1, d_kv), per_b),
            pl.BlockSpec((1, 1, 1), per_b),
            pl.BlockSpec((1, 1, d_a), per_b),
            pl.BlockSpec((1, 1, d_b), per_b),
            pl.BlockSpec((1, HEAD_DIM), fixed),
            pl.BlockSpec((4, DH_A), fixed),
        ] + [pl.BlockSpec((1, rows_d, HEAD_DIM), paged(r)) for r in range(n_pg)] * 2
        + [pl.BlockSpec((1, rows_s, HEAD_DIM), paged(r)) for r in range(n_pg)] * 2
        + [pl.BlockSpec((1, n_pg, 1, rows_s), lambda bi, p, pt: (bi, p, 0, 0))],
        out_specs=[pl.BlockSpec((1, 1, d_a), per_b), pl.BlockSpec((1, 1, d_b), per_b)],
        scratch_shapes=[
            pltpu.VMEM((2 * n_ha, HEAD_DIM), BF16), pltpu.VMEM((n_hb, HEAD_DIM), BF16),
            pltpu.VMEM((2 * n_ha, 1), F32), pltpu.VMEM((2 * n_ha, 1), F32),
            pltpu.VMEM((2 * n_ha, HEAD_DIM), F32),
            pltpu.VMEM((n_hb, 1), F32), pltpu.VMEM((n_hb, 1), F32), pltpu.VMEM((n_hb, HEAD_DIM), F32),
        ],
    )
    return pl.pallas_call(
        functools.partial(_sample_attn_kernel, n_pg=n_pg, n_ha=n_ha, n_hb=n_hb, group=group,
                          lam_init=lam_init),
        grid_spec=grid_spec,
        out_shape=[jax.ShapeDtypeStruct((nb, 1, d_a), BF16), jax.ShapeDtypeStruct((nb, 1, d_b), BF16)],
        compiler_params=_cparams(2),
        name="sample_attn",
    )(page_table_flat, qa, qs, ka_new, va_new, ks_new, vs_new, bias_new, sga, sgs, subw, lamp,
      *([cdk] * n_pg + [cdv] * n_pg + [csk] * n_pg + [csv] * n_pg), bias_pages)


def _row_tile(m, pref):
    return pref if m % pref == 0 else m


def _inproj_all(x2d, pos, wts, tm, emit_vt):
    tabs64 = _rope_tables(pos, DH_A)
    tabs128 = _rope_tables(pos, HEAD_DIM)
    d_a, d_b, d_kv, d_qi = wts["d_a"], wts["d_b"], wts["d_kv"], wts["d_qi"]
    outs_a = _inproj_a(x2d, wts["pre_w"], wts["w_a"], tabs64, tm, d_a, emit_vt)
    outs_b = _inproj_b(x2d, wts["pre_w"], wts["w_b"], tabs128, tabs64, tm, d_b, d_kv, d_qi, emit_vt)
    return outs_a, outs_b


def kernel(x_prompt, x_sample, p_prompt, p_sample, cache_diff_k, cache_diff_v, cache_dsa_k, cache_dsa_v, cache_idx_k, page_table, pre_norm_w, post_norm_w, w_in, lam_q1, lam_k1, lam_q2, lam_k2, diff_norm_w, w_out, w_ple_gate, b_ple_gate, w_ple_proj):
    depth = w_in.shape[0]
    assert depth == 1, "single-layer stack only"
    bsz, t_p, d = x_prompt.shape
    nb, t_s, _ = x_sample.shape
    assert t_s == 1, "one new token per sample sequence"
    n_pages = page_table.shape[1]
    n_pool = cache_diff_k.shape[1]
    page = cache_diff_k.shape[2]
    n_ha = cache_diff_k.shape[3]
    n_kv = cache_dsa_k.shape[3]
    d_a = n_ha * HEAD_DIM
    d_kv = n_kv * HEAD_DIM
    d_b = d - d_a
    d_qi = H_IDX * D_IDX
    past_len = n_pages * page
    lam_init = 0.8 - 0.6 * math.exp(-0.3 * 0)
    n_in = w_in.shape[2]

    w = w_in[0]
    w_a = w[:, :4 * d_a].astype(BF16)
    n_b = n_in - 4 * d_a
    n_b_pad = -(-n_b // LANES) * LANES
    w_b = jnp.pad(w[:, 4 * d_a:], ((0, 0), (0, n_b_pad - n_b))).astype(BF16)
    wts = dict(pre_w=pre_norm_w[0][None, :], w_a=w_a, w_b=w_b, d_a=d_a, d_b=d_b, d_kv=d_kv, d_qi=d_qi)
    wo = w_out[0].astype(BF16)
    wg = w_ple_gate[0].astype(BF16)
    we = w_ple_proj[0].astype(BF16)
    pw = post_norm_w[0][None, :]
    bg = b_ple_gate[0][None, :]
    subw = diff_norm_w[0][None, :]
    lamp = jnp.stack([lam_q1[0], lam_k1[0], lam_q2[0], lam_k2[0]], axis=0)

    m_p = bsz * t_p
    tm = _row_tile(t_p, ROW_TILE)
    tq_diff = _row_tile(t_p, DIFF_TILE)
    tq_dsa = _row_tile(t_p, DSA_TILE)
    tq_idx = _row_tile(t_p, IDX_TILE)
    xp2 = x_prompt.reshape(m_p, d)
    pos_p = jnp.arange(t_p, dtype=I32)
    (qa, ka, va, sga, va_t), (qs, ks, vs, sgs, qi, kiw, vs_t) = _inproj_all(xp2, pos_p, wts, tm, True)
    r3 = lambda a: a.reshape(bsz, t_p, a.shape[-1])
    a1 = _diff_attn(r3(qa), r3(ka), va_t, r3(sga), subw, lamp, lam_init, tq_diff)
    ki = kiw[:, :D_IDX]
    wi = kiw[:, D_IDX:D_IDX + H_IDX]
    qi4 = qi.reshape(bsz, t_p, H_IDX, D_IDX).transpose(0, 2, 1, 3)
    wi_t = wi.reshape(bsz, t_p, H_IDX).transpose(0, 2, 1)
    k_sel_p = min(TOPK_MAX, t_p // 4)
    bias_t = _index_select(qi4, ki.astype(BF16).reshape(bsz, t_p, D_IDX), wi_t, tq_idx, k_sel_p)
    a2 = _dsa_attn(r3(qs), r3(ks), vs_t, bias_t, r3(sgs), tq_dsa)
    y_p = _out_proj(xp2, a1.reshape(m_p, d_a), a2.reshape(m_p, d_b), p_prompt[0].reshape(m_p, -1),
                    wo, pw, wg, bg, we, tm)

    xs2 = x_sample.reshape(nb, d)
    pos_s = jnp.full((nb,), past_len, dtype=I32)
    (qa_s, ka_s, va_s, sga_s), (qs_s, ks_s, vs_s, sgs_s, qi_s, kiw_s) = _inproj_all(xs2, pos_s, wts, nb, False)
    ki_s = kiw_s[:, :D_IDX]
    wi_s = kiw_s[:, D_IDX:D_IDX + H_IDX]
    pt_flat = page_table.reshape(-1)
    n_group = 8 if n_pages % 8 == 0 else 1
    cache_idx_t = jnp.swapaxes(cache_idx_k.reshape(n_pool, page, D_IDX), 1, 2)
    scores = _sample_scores(pt_flat, qi_s.reshape(nb, H_IDX, D_IDX), wi_s.reshape(nb, H_IDX, 1),
                            cache_idx_t, n_pages, n_group)
    k_sel_s = min(TOPK_MAX, (past_len + t_s) // 4)
    bias_s = _sample_select(scores.reshape(nb, past_len), qi_s,
                            jnp.tile(ki_s.astype(BF16), (1, H_IDX)), wi_s, k_sel_s)
    bias_pages = jnp.repeat(bias_s[:, :past_len], n_kv, axis=1).reshape(nb, n_pages, 1, page * n_kv)
    bias_new = bias_s[:, past_len:past_len + 1].reshape(nb, 1, 1)
    e3 = lambda a: a.reshape(nb, 1, a.shape[-1])
    a1_s, a2_s = _sample_attn(
        pt_flat, e3(qa_s), e3(qs_s), e3(ka_s), e3(va_s), e3(ks_s), e3(vs_s), bias_new,
        e3(sga_s), e3(sgs_s), subw, lamp,
        cache_diff_k.reshape(n_pool, page * n_ha, HEAD_DIM), cache_diff_v.reshape(n_pool, page * n_ha, HEAD_DIM),
        cache_dsa_k.reshape(n_pool, page * n_kv, HEAD_DIM), cache_dsa_v.reshape(n_pool, page * n_kv, HEAD_DIM),
        bias_pages, n_pages, lam_init)
    y_s = _out_proj(xs2, a1_s.reshape(nb, d_a), a2_s.reshape(nb, d_b), p_sample[0].reshape(nb, -1),
                    wo, pw, wg, bg, we, nb)

    return (
        y_p.reshape(bsz, t_p, d), y_s.reshape(nb, t_s, d),
        ka.reshape(1, bsz, t_p, n_ha, HEAD_DIM), va.reshape(1, bsz, t_p, n_ha, HEAD_DIM),
        ks.reshape(1, bsz, t_p, n_kv, HEAD_DIM), vs.reshape(1, bsz, t_p, n_kv, HEAD_DIM),
        ki.reshape(1, bsz, t_p, D_IDX),
        ka_s.reshape(1, nb, t_s, n_ha, HEAD_DIM), va_s.reshape(1, nb, t_s, n_ha, HEAD_DIM),
        ks_s.reshape(1, nb, t_s, n_kv, HEAD_DIM), vs_s.reshape(1, nb, t_s, n_kv, HEAD_DIM),
        ki_s.reshape(1, nb, t_s, D_IDX),
    )
```

```python
import functools
import math

import jax
import jax.numpy as jnp
from jax import lax
from jax.experimental import pallas as pl
from jax.experimental.pallas import tpu as pltpu

F32 = jnp.float32
BF16 = jnp.bfloat16
I32 = jnp.int32

LANES = 128
SUBLANES = 8
HEAD_DIM = 128
DH_A = HEAD_DIM // 2
D_IDX = 64
H_IDX = 4
TOPK_MAX = 256
ROPE_THETA = 500000.0
ROPE_FRAC = 4
RMS_EPS = 1e-6
NEG_BIG = -1e30
INT_MIN = -(2 ** 31)
NEG_INF_KEY = INT_MIN + 0x7FFFFF
VMEM_LIMIT = 56 * 1024 * 1024
ROW_TILE = 256
LOG2E = math.log2(math.e)
DIFF_TILE = 512
DSA_TILE = 256
IDX_TILE = 512
SAMPLE_PAGES_PER_STEP = 4
SCORE_PAGES_PER_STEP = 16


def _cparams(n_axes):
    return pltpu.CompilerParams(
        dimension_semantics=("arbitrary",) * n_axes, vmem_limit_bytes=VMEM_LIMIT)


def _dot(a, b):
    return jnp.dot(a, b, preferred_element_type=F32)


def _dot_nt(a, b):
    return lax.dot_general(a, b, (((1,), (1,)), ((), ())), preferred_element_type=F32)


def _rope_tables(pos, d):
    r = d // ROPE_FRAC
    half = r // 2
    inv = ROPE_THETA ** (-(2.0 / r) * jnp.arange(half, dtype=F32))
    ang = pos.astype(F32)[:, None] * inv[None, :]
    cos, sin = jnp.cos(ang), jnp.sin(ang)
    t = pos.shape[0]
    ones = jnp.ones((t, d - r), F32)
    zeros_h = jnp.zeros((t, half), F32)
    zeros_r = jnp.zeros((t, d - r), F32)
    c = jnp.concatenate([cos, cos, ones], axis=-1)
    sm = jnp.concatenate([-sin, zeros_h, zeros_r], axis=-1)
    sp = jnp.concatenate([zeros_h, sin, zeros_r], axis=-1)
    rep = LANES // d
    return tuple(jnp.tile(a, (1, rep)) for a in (c, sm, sp)), half


def _rope_chunk(z, c, sm, sp, half):
    return (z * c + pltpu.roll(z, LANES - half, axis=1) * sm
            + pltpu.roll(z, half, axis=1) * sp)


def _silu(z):
    return z * (1.0 / (1.0 + jnp.exp(-z)))


def _rms_rows(x, w):
    return x * lax.rsqrt(jnp.mean(x * x, axis=-1, keepdims=True) + RMS_EPS) * w


def _cast_rows_kernel(w_ref, o_ref):
    o_ref[...] = w_ref[...].astype(BF16)


def _cast_rows(w_t, row0, n_rows):
    d = w_t.shape[1]
    rb = ROW_TILE
    assert row0 % rb == 0 and n_rows % rb == 0
    b0 = row0 // rb
    return pl.pallas_call(
        _cast_rows_kernel,
        grid=(n_rows // rb,),
        in_specs=[pl.BlockSpec((rb, d), lambda i: (i + b0, 0))],
        out_specs=pl.BlockSpec((rb, d), lambda i: (i, 0)),
        out_shape=jax.ShapeDtypeStruct((n_rows, d), BF16),
        compiler_params=_cparams(1),
        name="cast_rows",
    )(w_t)


def _inproj_a_kernel(x_ref, nw_ref, w_ref, c_ref, sm_ref, sp_ref,
                     qa_ref, ka_ref, va_ref, ga_ref, *maybe_vt_ref, half, d_a):
    h = _rms_rows(x_ref[...], nw_ref[...]).astype(BF16)
    c, sm, sp = c_ref[...], sm_ref[...], sp_ref[...]
    n_chunks = d_a // LANES
    zq = _dot_nt(h, w_ref[0:d_a, :])
    for j in range(n_chunks):
        sl = slice(j * LANES, (j + 1) * LANES)
        qa_ref[:, sl] = (_rope_chunk(zq[:, sl], c, sm, sp, half) * (LOG2E / math.sqrt(DH_A))).astype(BF16)
    zk = _dot_nt(h, w_ref[d_a:2 * d_a, :])
    for j in range(n_chunks):
        sl = slice(j * LANES, (j + 1) * LANES)
        ka_ref[:, sl] = _rope_chunk(zk[:, sl], c, sm, sp, half)
    zv = _dot_nt(h, w_ref[2 * d_a:3 * d_a, :])
    va_ref[...] = zv
    if maybe_vt_ref:
        maybe_vt_ref[0][...] = zv.T.astype(BF16)
    ga_ref[...] = _silu(_dot_nt(h, w_ref[3 * d_a:4 * d_a, :]))


def _inproj_a(x2d, nw, w_a, tabs64, tm, d_a, emit_vt):
    m, d = x2d.shape
    (c, sm, sp), half = tabs64
    t_blocks = c.shape[0] // tm
    row = lambda i: (i, 0)
    tab = lambda i: (i % t_blocks, 0)
    fixed = lambda i: (0, 0)
    out_specs = [pl.BlockSpec((tm, d_a), row)] * 4
    out_shape = [
        jax.ShapeDtypeStruct((m, d_a), BF16),
        jax.ShapeDtypeStruct((m, d_a), F32),
        jax.ShapeDtypeStruct((m, d_a), F32),
        jax.ShapeDtypeStruct((m, d_a), F32),
    ]
    if emit_vt:
        out_specs.append(pl.BlockSpec((d_a, tm), lambda i: (0, i)))
        out_shape.append(jax.ShapeDtypeStruct((d_a, m), BF16))
    return pl.pallas_call(
        functools.partial(_inproj_a_kernel, half=half, d_a=d_a),
        grid=(m // tm,),
        in_specs=[
            pl.BlockSpec((tm, d), row),
            pl.BlockSpec((1, d), fixed),
            pl.BlockSpec((4 * d_a, d), fixed, pipeline_mode=pl.Buffered(1)),
            pl.BlockSpec((tm, LANES), tab),
            pl.BlockSpec((tm, LANES), tab),
            pl.BlockSpec((tm, LANES), tab),
        ],
        out_specs=out_specs,
        out_shape=out_shape,
        compiler_params=_cparams(1),
        name="inproj_a",
    )(x2d, nw, w_a, c, sm, sp)


def _inproj_b_kernel(x_ref, nw_ref, w_ref, wt_ref, c128_ref, sm128_ref, sp128_ref,
                     c64_ref, sm64_ref, sp64_ref,
                     qs_ref, ks_ref, vs_ref, gs_ref, qi_ref, kiw_ref, *maybe_vt_ref,
                     half128, half64, d_b, d_kv, d_qi):
    h = _rms_rows(x_ref[...], nw_ref[...]).astype(BF16)
    c128, sm128, sp128 = c128_ref[...], sm128_ref[...], sp128_ref[...]
    c64, sm64, sp64 = c64_ref[...], sm64_ref[...], sp64_ref[...]
    o = 0
    zq = _dot_nt(h, w_ref[o:o + d_b, :])
    for j in range(d_b // LANES):
        sl = slice(j * LANES, (j + 1) * LANES)
        qs_ref[:, sl] = (_rope_chunk(zq[:, sl], c128, sm128, sp128, half128)
                         * (LOG2E / math.sqrt(HEAD_DIM))).astype(BF16)
    o += d_b
    zk = _dot_nt(h, w_ref[o:o + d_kv, :])
    for j in range(d_kv // LANES):
        sl = slice(j * LANES, (j + 1) * LANES)
        ks_ref[:, sl] = _rope_chunk(zk[:, sl], c128, sm128, sp128, half128)
    o += d_kv
    zv = _dot_nt(h, w_ref[o:o + d_kv, :])
    vs_ref[...] = zv
    if maybe_vt_ref:
        maybe_vt_ref[0][...] = zv.T.astype(BF16)
    o += d_kv
    gs_ref[...] = _silu(_dot_nt(h, w_ref[o:o + d_b, :]))
    o += d_b
    zi = _dot_nt(h, w_ref[o:o + d_qi, :])
    for j in range(d_qi // LANES):
        sl = slice(j * LANES, (j + 1) * LANES)
        qi_ref[:, sl] = _rope_chunk(zi[:, sl], c64, sm64, sp64, half64).astype(BF16)
    zkw = _dot_nt(h, wt_ref[...])
    lane = lax.broadcasted_iota(I32, zkw.shape, 1)
    kiw_ref[...] = jnp.where(lane < D_IDX, _rope_chunk(zkw, c64, sm64, sp64, half64), zkw)


def _inproj_b(x2d, nw, w_b, w_tail, tabs128, tabs64, tm, d_b, d_kv, d_qi, emit_vt):
    m, d = x2d.shape
    (c128, sm128, sp128), half128 = tabs128
    (c64, sm64, sp64), half64 = tabs64
    t_blocks = c128.shape[0] // tm
    row = lambda i: (i, 0)
    tab = lambda i: (i % t_blocks, 0)
    fixed = lambda i: (0, 0)
    n_b = w_b.shape[0]
    out_specs = [
        pl.BlockSpec((tm, d_b), row),
        pl.BlockSpec((tm, d_kv), row),
        pl.BlockSpec((tm, d_kv), row),
        pl.BlockSpec((tm, d_b), row),
        pl.BlockSpec((tm, d_qi), row),
        pl.BlockSpec((tm, LANES), row),
    ]
    out_shape = [
        jax.ShapeDtypeStruct((m, d_b), BF16),
        jax.ShapeDtypeStruct((m, d_kv), F32),
        jax.ShapeDtypeStruct((m, d_kv), F32),
        jax.ShapeDtypeStruct((m, d_b), F32),
        jax.ShapeDtypeStruct((m, d_qi), BF16),
        jax.ShapeDtypeStruct((m, LANES), F32),
    ]
    if emit_vt:
        out_specs.append(pl.BlockSpec((d_kv, tm), lambda i: (0, i)))
        out_shape.append(jax.ShapeDtypeStruct((d_kv, m), BF16))
    return pl.pallas_call(
        functools.partial(_inproj_b_kernel, half128=half128, half64=half64,
                          d_b=d_b, d_kv=d_kv, d_qi=d_qi),
        grid=(m // tm,),
        in_specs=[
            pl.BlockSpec((tm, d), row),
            pl.BlockSpec((1, d), fixed),
            pl.BlockSpec((n_b, d), fixed, pipeline_mode=pl.Buffered(1)),
            pl.BlockSpec((LANES, d), fixed),
        ] + [pl.BlockSpec((tm, LANES), tab)] * 6,
        out_specs=out_specs,
        out_shape=out_shape,
        compiler_params=_cparams(1),
        name="inproj_b",
    )(x2d, nw, w_b, w_tail, c128, sm128, sp128, c64, sm64, sp64)


def _flash_init(m_ref, l_ref, acc_ref):
    m_ref[...] = jnp.full(m_ref.shape, NEG_BIG, F32)
    l_ref[...] = jnp.zeros(l_ref.shape, F32)
    acc_ref[...] = jnp.zeros(acc_ref.shape, F32)


def _flash_step(s, v_bf, m_ref, l_ref, acc_ref, axis=1):
    m_prev = m_ref[...]
    m_new = jnp.maximum(m_prev, jnp.max(s, axis=axis, keepdims=True))
    alpha = jnp.exp2(m_prev - m_new)
    p = jnp.exp2(s - m_new)
    l_ref[...] = alpha * l_ref[...] + jnp.sum(p, axis=axis, keepdims=True)
    pv = _dot(p.astype(BF16), v_bf) if axis == 1 else _dot(v_bf, p.astype(BF16))
    acc_ref[...] = alpha * acc_ref[...] + pv
    m_ref[...] = m_new


def _flash_step_multi(s_list, v_list, m_ref, l_ref, acc_ref):
    m_prev = m_ref[...]
    m_new = m_prev
    for s in s_list:
        m_new = jnp.maximum(m_new, jnp.max(s, axis=1, keepdims=True))
    alpha = jnp.exp2(m_prev - m_new)
    l_new = alpha * l_ref[...]
    acc = alpha * acc_ref[...]
    for s, v_bf in zip(s_list, v_list):
        p = jnp.exp2(s - m_new)
        l_new = l_new + jnp.sum(p, axis=1, keepdims=True)
        acc = acc + _dot(p.astype(BF16), v_bf)
    l_ref[...] = l_new
    acc_ref[...] = acc
    m_ref[...] = m_new


def _lambda_full(lamp, lam_init):
    s1 = jnp.sum(lamp[0:1, :] * lamp[1:2, :], axis=-1, keepdims=True)
    s2 = jnp.sum(lamp[2:3, :] * lamp[3:4, :], axis=-1, keepdims=True)
    return jnp.exp(s1) - jnp.exp(s2) + lam_init


def _diff_attn_kernel(q_ref, k_ref, vt_ref, g_ref, subw_ref, lamp_ref, o_ref,
                      kbf, s1b, s2b, m1, l1, a1, m2, l2, a2, *, tq, lam_init):
    i = pl.program_id(2)

    @pl.when(i == 0)
    def _():
        kbf[...] = k_ref[0].astype(BF16)

    q = q_ref[0]
    lane = lax.broadcasted_iota(I32, q.shape, 1)
    zero = jnp.zeros_like(q)
    q1 = jnp.where(lane < DH_A, q, zero)
    q2 = jnp.where(lane >= DH_A, q, zero)
    _flash_init(m1, l1, a1)
    _flash_init(m2, l2, a2)

    def scores(j, slot):
        kb = kbf[pl.ds(pl.multiple_of(j * tq, tq), tq), :]
        s1b[slot] = _dot_nt(kb, q1)
        s2b[slot] = _dot_nt(kb, q2)

    def softmax_pv(j, slot, masked):
        vtb = vt_ref[:, pl.ds(pl.multiple_of(j * tq, tq), tq)]
        s1 = s1b[slot]
        s2 = s2b[slot]
        if masked:
            kidx = lax.broadcasted_iota(I32, s1.shape, 0)
            qidx = lax.broadcasted_iota(I32, s1.shape, 1)
            vis = kidx <= qidx
            s1 = jnp.where(vis, s1, NEG_BIG)
            s2 = jnp.where(vis, s2, NEG_BIG)
        _flash_step(s1, vtb, m1, l1, a1, axis=0)
        _flash_step(s2, vtb, m2, l2, a2, axis=0)

    scores(0, 0)

    def pair(jj, carry):
        j = 2 * jj
        scores(j + 1, 1)
        softmax_pv(j, 0, False)
        scores(j + 2, 0)
        softmax_pv(j + 1, 1, False)
        return carry

    lax.fori_loop(0, i // 2, pair, 0)

    @pl.when(i % 2 == 1)
    def _():
        scores(i, 1)
        softmax_pv(i - 1, 0, False)
        softmax_pv(i, 1, True)

    @pl.when(i % 2 == 0)
    def _():
        softmax_pv(i, 0, True)

    lam = _lambda_full(lamp_ref[...], lam_init)
    o_t = a1[...] / l1[...] - lam * (a2[...] / l2[...])
    o = _rms_rows(o_t.T, subw_ref[...]) * (1.0 - lam_init)
    o_ref[0] = (o * g_ref[0]).astype(BF16)


def _diff_attn(qa, ka, va_t, sga, subw, lamp, lam_init, tq):
    b, t, d_a = qa.shape
    n_h = d_a // HEAD_DIM
    qmap = lambda bi, h, i: (bi, i, h)
    kmap = lambda bi, h, i: (bi, 0, h)
    fixed = lambda bi, h, i: (0, 0)
    return pl.pallas_call(
        functools.partial(_diff_attn_kernel, tq=tq, lam_init=lam_init),
        grid=(b, n_h, t // tq),
        in_specs=[
            pl.BlockSpec((1, tq, HEAD_DIM), qmap),
            pl.BlockSpec((1, t, HEAD_DIM), kmap),
            pl.BlockSpec((HEAD_DIM, t), lambda bi, h, i: (h, bi)),
            pl.BlockSpec((1, tq, HEAD_DIM), qmap),
            pl.BlockSpec((1, HEAD_DIM), fixed),
            pl.BlockSpec((4, DH_A), fixed),
        ],
        out_specs=pl.BlockSpec((1, tq, HEAD_DIM), qmap),
        out_shape=jax.ShapeDtypeStruct((b, t, d_a), BF16),
        scratch_shapes=[
            pltpu.VMEM((t, HEAD_DIM), BF16),
            pltpu.VMEM((2, tq, tq), F32), pltpu.VMEM((2, tq, tq), F32),
            pltpu.VMEM((1, tq), F32), pltpu.VMEM((1, tq), F32), pltpu.VMEM((HEAD_DIM, tq), F32),
            pltpu.VMEM((1, tq), F32), pltpu.VMEM((1, tq), F32), pltpu.VMEM((HEAD_DIM, tq), F32),
        ],
        compiler_params=_cparams(3),
        name="diff_attn",
    )(qa, ka, va_t, sga, subw, lamp)


def _key_to_float(key):
    key = jnp.maximum(key, NEG_INF_KEY)
    return lax.bitcast_convert_type(key ^ ((key >> 31) & 0x7FFFFFFF), F32)


def _fold(m, axis):
    if axis == 1:
        n = m.shape[1] // LANES
        acc = m[:, 0:LANES]
        for j in range(1, n):
            acc = acc + m[:, j * LANES:(j + 1) * LANES]
        return acc
    return jnp.sum(m.reshape(m.shape[0] // SUBLANES, SUBLANES, m.shape[1]), axis=0)


def _count(key_ref, n_chunks, chunk, pred, axis):
    other = key_ref.shape[1 - axis]
    acc_shape = (other, LANES) if axis == 1 else (SUBLANES, other)

    def body(c, acc):
        off = pl.multiple_of(c * chunk, chunk)
        kc = key_ref[:, pl.ds(off, chunk)] if axis == 1 else key_ref[pl.ds(off, chunk), :]
        return acc + _fold(pred(kc, off), axis)

    acc = lax.fori_loop(0, n_chunks, body, jnp.zeros(acc_shape, F32))
    return jnp.sum(acc, axis=axis, keepdims=True)


def _select_topk(key_ref, p_ref, n_chunks, chunk, k_sel, idx_bits, axis):
    k_f = float(k_sel)
    q_shape = p_ref.shape

    def bit_body(it, tu):
        bit = lax.shift_left(jnp.int32(1), 31 - it)
        cand_u = tu | bit
        cand = _key_to_float(cand_u ^ INT_MIN)
        cnt = _count(key_ref, n_chunks, chunk,
                     lambda kc, off: jnp.where(kc >= cand, 1.0, 0.0), axis)
        return jnp.where(cnt >= k_f, cand_u, tu)

    tu = lax.fori_loop(0, 32, bit_body, jnp.zeros(q_shape, I32))
    thr = _key_to_float(tu ^ INT_MIN)
    n_gt = _count(key_ref, n_chunks, chunk, lambda kc, off: jnp.where(kc > thr, 1.0, 0.0), axis)
    n_eq = _count(key_ref, n_chunks, chunk, lambda kc, off: jnp.where(kc == thr, 1.0, 0.0), axis)
    need = k_f - n_gt
    p_ref[...] = jnp.full(q_shape, 2 ** 30, I32)
    excess = jnp.max(n_eq - need)

    @pl.when(excess > 0.0)
    def _():
        def idx_body(it, p):
            bit = lax.shift_left(jnp.int32(1), idx_bits - 1 - it)
            cand = p | bit

            def pred(kc, off):
                idx = off + lax.broadcasted_iota(I32, kc.shape, axis)
                return jnp.where(kc == thr, jnp.where(idx < cand, 1.0, 0.0), 0.0)

            cnt = _count(key_ref, n_chunks, chunk, pred, axis)
            return jnp.where(cnt < need, cand, p)

        p_ref[...] = lax.fori_loop(0, idx_bits, idx_body, jnp.zeros(q_shape, I32))

    return thr


def _selected_bias(kc, off, thr, p_max, axis):
    idx = off + lax.broadcasted_iota(I32, kc.shape, axis)
    tie = jnp.where(kc == thr, jnp.where(idx <= p_max, 1.0, 0.0), 0.0)
    sel = jnp.where(kc > thr, 1.0, tie)
    sel = jnp.where(kc == -jnp.inf, 0.0, sel)
    return jnp.where(sel > 0.5, 0.0, NEG_BIG)


def _index_select_kernel(qi_ref, ki_ref, wit_ref, bias_ref, key_ref, p_ref, *, tq, k_sel, idx_bits):
    i = pl.program_id(1)
    n_total = bias_ref.shape[1] // tq
    wit = wit_ref[0]

    def score_body(c, carry):
        off = pl.multiple_of(c * tq, tq)
        kc = ki_ref[0, pl.ds(off, tq), :]
        sc = jnp.zeros((tq, tq), F32)
        for h in range(H_IDX):
            sc = sc + jnp.maximum(_dot_nt(kc, qi_ref[0, h]), 0.0) * wit[h:h + 1, :]
        kidx = lax.broadcasted_iota(I32, sc.shape, 0) + off
        qidx = lax.broadcasted_iota(I32, sc.shape, 1) + i * tq
        sc = jnp.where(kidx <= qidx, sc, -jnp.inf)
        key_ref[pl.ds(off, tq), :] = sc
        return carry

    lax.fori_loop(0, i + 1, score_body, 0)
    thr = _select_topk(key_ref, p_ref, i + 1, tq, k_sel, idx_bits, axis=0)
    p_max = p_ref[...]

    def out_body(c, carry):
        off = pl.multiple_of(c * tq, tq)
        kc = key_ref[pl.ds(off, tq), :]
        bias_ref[0, pl.ds(off, tq), :] = _selected_bias(kc, off, thr, p_max, 0).astype(BF16)
        return carry

    lax.fori_loop(0, i + 1, out_body, 0)

    def fill_body(c, carry):
        off = pl.multiple_of(c * tq, tq)
        bias_ref[0, pl.ds(off, tq), :] = jnp.full((tq, tq), NEG_BIG, BF16)
        return carry

    lax.fori_loop(i + 1, n_total, fill_body, 0)


def _index_select(qi4, ki_bf, wi_t, tq, k_sel):
    b, _, t, _ = qi4.shape
    idx_bits = max(1, (t - 1).bit_length())
    return pl.pallas_call(
        functools.partial(_index_select_kernel, tq=tq, k_sel=k_sel, idx_bits=idx_bits),
        grid=(b, t // tq),
        in_specs=[
            pl.BlockSpec((1, H_IDX, tq, D_IDX), lambda bi, i: (bi, 0, i, 0)),
            pl.BlockSpec((1, t, D_IDX), lambda bi, i: (bi, 0, 0)),
            pl.BlockSpec((1, H_IDX, tq), lambda bi, i: (bi, 0, i)),
        ],
        out_specs=pl.BlockSpec((1, t, tq), lambda bi, i: (bi, 0, i)),
        out_shape=jax.ShapeDtypeStruct((b, t, t), BF16),
        scratch_shapes=[pltpu.VMEM((t, tq), F32), pltpu.VMEM((1, tq), I32)],
        compiler_params=_cparams(2),
        name="index_select",
    )(qi4, ki_bf, wi_t)


def _dsa_attn_kernel(q_ref, k_ref, vt_ref, bias_ref, g_ref, o_ref, kbf, sb, m, l, acc, *, tq, group):
    i = pl.program_id(2)

    @pl.when(i == 0)
    def _():
        kbf[...] = k_ref[0].astype(BF16)

    q = q_ref[0]
    q4 = jnp.concatenate([q[:, h * HEAD_DIM:(h + 1) * HEAD_DIM] for h in range(group)], axis=0)
    _flash_init(m, l, acc)

    def scores(j, slot):
        kb = kbf[pl.ds(pl.multiple_of(j * tq, tq), tq), :]
        sb[slot] = _dot_nt(kb, q4)

    def softmax_pv(j, slot):
        off = pl.multiple_of(j * tq, tq)
        vtb = vt_ref[:, pl.ds(off, tq)]
        bias = bias_ref[0, pl.ds(off, tq), :].astype(F32)
        s = sb[slot] + jnp.concatenate([bias] * group, axis=1)
        _flash_step(s, vtb, m, l, acc, axis=0)

    scores(0, 0)

    def pair(jj, carry):
        j = 2 * jj
        scores(j + 1, 1)
        softmax_pv(j, 0)
        scores(jnp.minimum(j + 2, i), 0)
        softmax_pv(j + 1, 1)
        return carry

    lax.fori_loop(0, (i + 1) // 2, pair, 0)

    @pl.when(i % 2 == 0)
    def _():
        softmax_pv(i, 0)
    o_t = acc[...] / l[...]
    g = g_ref[0]
    for h in range(group):
        sl = slice(h * HEAD_DIM, (h + 1) * HEAD_DIM)
        o_ref[0, :, sl] = (o_t[:, h * tq:(h + 1) * tq].T * g[:, sl]).astype(BF16)


def _dsa_attn(qs, ks, vs_t, bias_t, sgs, tq):
    b, t, d_b = qs.shape
    n_kv = ks.shape[2] // HEAD_DIM
    group = d_b // HEAD_DIM // n_kv
    gw = group * HEAD_DIM
    qmap = lambda bi, n, i: (bi, i, n)
    kmap = lambda bi, n, i: (bi, 0, n)
    return pl.pallas_call(
        functools.partial(_dsa_attn_kernel, tq=tq, group=group),
        grid=(b, n_kv, t // tq),
        in_specs=[
            pl.BlockSpec((1, tq, gw), qmap),
            pl.BlockSpec((1, t, HEAD_DIM), kmap),
            pl.BlockSpec((HEAD_DIM, t), lambda bi, n, i: (n, bi)),
            pl.BlockSpec((1, t, tq), lambda bi, n, i: (bi, 0, i)),
            pl.BlockSpec((1, tq, gw), qmap),
        ],
        out_specs=pl.BlockSpec((1, tq, gw), qmap),
        out_shape=jax.ShapeDtypeStruct((b, t, d_b), BF16),
        scratch_shapes=[
            pltpu.VMEM((t, HEAD_DIM), BF16),
            pltpu.VMEM((2, tq, group * tq), F32),
            pltpu.VMEM((1, group * tq), F32), pltpu.VMEM((1, group * tq), F32),
            pltpu.VMEM((HEAD_DIM, group * tq), F32),
        ],
        compiler_params=_cparams(3),
        name="dsa_attn",
    )(qs, ks, vs_t, bias_t, sgs)


def _out_kernel(x_ref, a1_ref, a2_ref, p_ref, wo_ref, pw_ref, wg_ref, bg_ref, we_ref, y_ref, *, d_a):
    o = _dot(a1_ref[...], wo_ref[0:d_a, :]) + _dot(a2_ref[...], wo_ref[d_a:, :])
    x1 = x_ref[...] + _rms_rows(o, pw_ref[...])
    z = _dot(x1.astype(BF16), wg_ref[...]) + bg_ref[...]
    gate = 1.0 / (1.0 + jnp.exp(-z))
    y_ref[...] = x1 + gate * _dot(p_ref[...].astype(BF16), we_ref[...])


def _out_proj(x2d, a1, a2, p2d, wo, pw, wg, bg, we, tm):
    m, d = x2d.shape
    d_a = a1.shape[1]
    d_mix = wo.shape[0]
    d_ple = p2d.shape[1]
    row = lambda i: (i, 0)
    fixed = lambda i: (0, 0)
    single = dict(pipeline_mode=pl.Buffered(1))
    return pl.pallas_call(
        functools.partial(_out_kernel, d_a=d_a),
        grid=(m // tm,),
        in_specs=[
            pl.BlockSpec((tm, d), row),
            pl.BlockSpec((tm, d_a), row),
            pl.BlockSpec((tm, d_mix - d_a), row),
            pl.BlockSpec((tm, d_ple), row),
            pl.BlockSpec((d_mix, d), fixed, **single),
            pl.BlockSpec((1, d), fixed),
            pl.BlockSpec((d, d), fixed, **single),
            pl.BlockSpec((1, d), fixed),
            pl.BlockSpec((d_ple, d), fixed, **single),
        ],
        out_specs=pl.BlockSpec((tm, d), row),
        out_shape=jax.ShapeDtypeStruct((m, d), F32),
        compiler_params=_cparams(1),
        name="out_proj",
    )(x2d, a1, a2, p2d, wo, pw, wg, bg, we)


def _sample_scores_kernel(pt_ref, qi_ref, wi_ref, *rest, n_group):
    del pt_ref
    page_refs, out_ref = rest[:n_group], rest[n_group]
    q4 = qi_ref[0]
    w = wi_ref[0]
    for r in range(n_group):
        kp_t = page_refs[r][0].astype(BF16)
        rel = jnp.maximum(_dot(q4, kp_t), 0.0)
        out_ref[0, r:r + 1, :] = jnp.sum(rel * w, axis=0, keepdims=True)


def _sample_scores(page_table_flat, qi4, wi3, cache_idx_t, n_pages, n_group):
    nb = qi4.shape[0]
    page = cache_idx_t.shape[2]

    def page_map(r):
        return lambda bi, g, pt: (pt[bi * n_pages + g * n_group + r], 0, 0)

    grid_spec = pltpu.PrefetchScalarGridSpec(
        num_scalar_prefetch=1,
        grid=(nb, n_pages // n_group),
        in_specs=[
            pl.BlockSpec((1, H_IDX, D_IDX), lambda bi, g, pt: (bi, 0, 0)),
            pl.BlockSpec((1, H_IDX, 1), lambda bi, g, pt: (bi, 0, 0)),
        ] + [pl.BlockSpec((1, D_IDX, page), page_map(r)) for r in range(n_group)],
        out_specs=pl.BlockSpec((1, n_group, page), lambda bi, g, pt: (bi, g, 0)),
    )
    return pl.pallas_call(
        functools.partial(_sample_scores_kernel, n_group=n_group),
        grid_spec=grid_spec,
        out_shape=jax.ShapeDtypeStruct((nb, n_pages, page), F32),
        compiler_params=_cparams(2),
        name="sample_scores",
    )(page_table_flat, qi4, wi3, *([cache_idx_t] * n_group))


def _sample_select_kernel(sc_ref, qi_ref, kit_ref, wi_ref, bias_ref, key_ref, p_ref, *, k_sel, idx_bits):
    rows, s_past = sc_ref.shape
    s_all = key_ref.shape[1]
    prod = qi_ref[...].astype(F32) * kit_ref[...].astype(F32)
    lane = lax.broadcasted_iota(I32, prod.shape, 1)
    wi = wi_ref[...]
    new = jnp.zeros((rows, 1), F32)
    for h in range(H_IDX):
        dot_h = jnp.sum(jnp.where(lane // D_IDX == h, prod, 0.0), axis=-1, keepdims=True)
        new = new + jnp.maximum(dot_h, 0.0) * wi[:, h:h + 1]
    key_ref[:, 0:s_past] = sc_ref[...]
    tail_lane = lax.broadcasted_iota(I32, (rows, s_all - s_past), 1)
    key_ref[:, s_past:s_all] = jnp.where(tail_lane == 0, new, -jnp.inf)
    thr = _select_topk(key_ref, p_ref, 1, s_all, k_sel, idx_bits, axis=1)
    bias_ref[...] = _selected_bias(key_ref[...], 0, thr, p_ref[...], 1)


def _sample_select(scores2d, qi, ki_tiled, wi, k_sel):
    rows, s_past = scores2d.shape
    s_all = s_past + LANES
    idx_bits = max(1, (s_all - 1).bit_length())
    return pl.pallas_call(
        functools.partial(_sample_select_kernel, k_sel=k_sel, idx_bits=idx_bits),
        out_shape=jax.ShapeDtypeStruct((rows, s_all), F32),
        scratch_shapes=[pltpu.VMEM((rows, s_all), F32), pltpu.VMEM((rows, 1), I32)],
        compiler_params=pltpu.CompilerParams(vmem_limit_bytes=VMEM_LIMIT),
        name="sample_select",
    )(scores2d, qi, ki_tiled, wi)


def _head_rows(row, n_rows, rows_per_head):
    return jnp.concatenate(
        [row[:, (c // rows_per_head) * HEAD_DIM:(c // rows_per_head + 1) * HEAD_DIM]
         for c in range(n_rows)], axis=0)


def _sample_attn_kernel(pt_ref, qa_ref, qs_ref, kan_ref, van_ref, ksn_ref, vsn_ref, bnew_ref,
                        ga_ref, gs_ref, subw_ref, lamp_ref, *rest, n_pg, n_ha, n_hb, group, lam_init):
    del pt_ref
    kd_refs, vd_refs = rest[0:n_pg], rest[n_pg:2 * n_pg]
    ks_refs, vs_refs = rest[2 * n_pg:3 * n_pg], rest[3 * n_pg:4 * n_pg]
    bias_ref, oa_ref, ob_ref, qd, qsb, md, ld, accd, ms, ls, accs = rest[4 * n_pg:]
    p = pl.program_id(1)
    n_p = pl.num_programs(1)
    n_kv = n_hb // group

    @pl.when(p == 0)
    def _():
        qd_f = _head_rows(qa_ref[0].astype(F32), 2 * n_ha, 2)
        rd = lax.broadcasted_iota(I32, qd_f.shape, 0)
        ln = lax.broadcasted_iota(I32, qd_f.shape, 1)
        qd_f = jnp.where(ln // DH_A == rd % 2, qd_f, 0.0)
        qd[...] = qd_f.astype(BF16)
        qs_f = _head_rows(qs_ref[0].astype(F32), n_hb, 1)
        qsb[...] = qs_f.astype(BF16)
        kan = _head_rows(kan_ref[0].astype(BF16).astype(F32), 2 * n_ha, 2)
        van = _head_rows(van_ref[0].astype(BF16).astype(F32), 2 * n_ha, 2)
        md[...] = jnp.sum(qd_f * kan, axis=-1, keepdims=True)
        ld[...] = jnp.ones(ld.shape, F32)
        accd[...] = van
        ksn = _head_rows(ksn_ref[0].astype(BF16).astype(F32), n_hb, group)
        vsn = _head_rows(vsn_ref[0].astype(BF16).astype(F32), n_hb, group)
        ms[...] = jnp.sum(qs_f * ksn, axis=-1, keepdims=True) + bnew_ref[0]
        ls[...] = jnp.ones(ls.shape, F32)
        accs[...] = vsn

    sd_list, vd_list, ss_list, vs_list = [], [], [], []
    for r in range(n_pg):
        kd = kd_refs[r][0].astype(BF16)
        sd = _dot_nt(qd[...], kd)
        rq = lax.broadcasted_iota(I32, sd.shape, 0)
        ck = lax.broadcasted_iota(I32, sd.shape, 1)
        sd_list.append(jnp.where(ck % n_ha == rq // 2, sd, NEG_BIG))
        vd_list.append(vd_refs[r][0].astype(BF16))
        ksp = ks_refs[r][0].astype(BF16)
        ss = _dot_nt(qsb[...], ksp) + bias_ref[0, r]
        rq = lax.broadcasted_iota(I32, ss.shape, 0)
        ck = lax.broadcasted_iota(I32, ss.shape, 1)
        ss_list.append(jnp.where(ck % n_kv == rq // group, ss, NEG_BIG))
        vs_list.append(vs_refs[r][0].astype(BF16))
    _flash_step_multi(sd_list, vd_list, md, ld, accd)
    _flash_step_multi(ss_list, vs_list, ms, ls, accs)

    @pl.when(p == n_p - 1)
    def _():
        lam = _lambda_full(lamp_ref[...], lam_init)
        od = accd[...] / ld[...]
        ga = ga_ref[0]
        subw = subw_ref[...]
        for h in range(n_ha):
            sl = slice(h * HEAD_DIM, (h + 1) * HEAD_DIM)
            o = od[2 * h:2 * h + 1, :] - lam * od[2 * h + 1:2 * h + 2, :]
            o = _rms_rows(o, subw) * (1.0 - lam_init)
            oa_ref[0, :, sl] = (o * ga[:, sl]).astype(BF16)
        os_ = accs[...] / ls[...]
        gs = gs_ref[0]
        for h in range(n_hb):
            sl = slice(h * HEAD_DIM, (h + 1) * HEAD_DIM)
            ob_ref[0, :, sl] = (os_[h:h + 1, :] * gs[:, sl]).astype(BF16)


def _sample_attn(page_table_flat, qa, qs, ka_new, va_new, ks_new, vs_new, bias_new, sga, sgs,
                 subw, lamp, cdk, cdv, csk, csv, bias_pages, n_pages, lam_init):
    nb, _, d_a = qa.shape
    d_b = qs.shape[2]
    d_kv = ks_new.shape[2]
    n_ha = d_a // HEAD_DIM
    n_hb = d_b // HEAD_DIM
    n_kv = d_kv // HEAD_DIM
    group = n_hb // n_kv
    rows_d = cdk.shape[1]
    rows_s = csk.shape[1]
    per_b = lambda bi, p, pt: (bi, 0, 0)
    fixed = lambda bi, p, pt: (0, 0)
    n_pg = SAMPLE_PAGES_PER_STEP if n_pages % SAMPLE_PAGES_PER_STEP == 0 else 1

    def paged(r):
        return lambda bi, p, pt: (pt[bi * n_pages + p * n_pg + r], 0, 0)

    grid_spec = pltpu.PrefetchScalarGridSpec(
        num_scalar_prefetch=1,
        grid=(nb, n_pages // n_pg),
        in_specs=[
            pl.BlockSpec((1, 1, d_a), per_b),
            pl.BlockSpec((1, 1, d_b), per_b),
            pl.BlockSpec((1, 1, d_a), per_b),
            pl.BlockSpec((1, 1, d_a), per_b),
            pl.BlockSpec((1, 1, d_kv), per_b),
            pl.BlockSpec((1, 1, d_kv), per_b),
            pl.BlockSpec((1, 1, 1), per_b),
            pl.BlockSpec((1, 1, d_a), per_b),
            pl.BlockSpec((1, 1, d_b), per_b),
            pl.BlockSpec((1, HEAD_DIM), fixed),
            pl.BlockSpec((4, DH_A), fixed),
        ] + [pl.BlockSpec((1, rows_d, HEAD_DIM), paged(r)) for r in range(n_pg)] * 2
        + [pl.BlockSpec((1, rows_s, HEAD_DIM), paged(r)) for r in range(n_pg)] * 2
        + [pl.BlockSpec((1, n_pg, 1, rows_s), lambda bi, p, pt: (bi, p, 0, 0))],
        out_specs=[pl.BlockSpec((1, 1, d_a), per_b), pl.BlockSpec((1, 1, d_b), per_b)],
        scratch_shapes=[
            pltpu.VMEM((2 * n_ha, HEAD_DIM), BF16), pltpu.VMEM((n_hb, HEAD_DIM), BF16),
            pltpu.VMEM((2 * n_ha, 1), F32), pltpu.VMEM((2 * n_ha, 1), F32),
            pltpu.VMEM((2 * n_ha, HEAD_DIM), F32),
            pltpu.VMEM((n_hb, 1), F32), pltpu.VMEM((n_hb, 1), F32), pltpu.VMEM((n_hb, HEAD_DIM), F32),
        ],
    )
    return pl.pallas_call(
        functools.partial(_sample_attn_kernel, n_pg=n_pg, n_ha=n_ha, n_hb=n_hb, group=group,
                          lam_init=lam_init),
        grid_spec=grid_spec,
        out_shape=[jax.ShapeDtypeStruct((nb, 1, d_a), BF16), jax.ShapeDtypeStruct((nb, 1, d_b), BF16)],
        compiler_params=_cparams(2),
        name="sample_attn",
    )(page_table_flat, qa, qs, ka_new, va_new, ks_new, vs_new, bias_new, sga, sgs, subw, lamp,
      *([cdk] * n_pg + [cdv] * n_pg + [csk] * n_pg + [csv] * n_pg), bias_pages)


def _row_tile(m, pref):
    return pref if m % pref == 0 else m


def _inproj_all(x2d, pos, wts, tm, emit_vt):
    tabs64 = _rope_tables(pos, DH_A)
    tabs128 = _rope_tables(pos, HEAD_DIM)
    d_a, d_b, d_kv, d_qi = wts["d_a"], wts["d_b"], wts["d_kv"], wts["d_qi"]
    outs_a = _inproj_a(x2d, wts["pre_w"], wts["w_a"], tabs64, tm, d_a, emit_vt)
    outs_b = _inproj_b(x2d, wts["pre_w"], wts["w_b"], wts["w_tail"], tabs128, tabs64, tm,
                       d_b, d_kv, d_qi, emit_vt)
    return outs_a, outs_b


def kernel(x_prompt, x_sample, p_prompt, p_sample, cache_diff_k, cache_diff_v, cache_dsa_k, cache_dsa_v, cache_idx_k, page_table, pre_norm_w, post_norm_w, w_in, lam_q1, lam_k1, lam_q2, lam_k2, diff_norm_w, w_out, w_ple_gate, b_ple_gate, w_ple_proj):
    depth = w_in.shape[0]
    assert depth == 1, "single-layer stack only"
    bsz, t_p, d = x_prompt.shape
    nb, t_s, _ = x_sample.shape
    assert t_s == 1, "one new token per sample sequence"
    n_pages = page_table.shape[1]
    n_pool = cache_diff_k.shape[1]
    page = cache_diff_k.shape[2]
    n_ha = cache_diff_k.shape[3]
    n_kv = cache_dsa_k.shape[3]
    d_a = n_ha * HEAD_DIM
    d_kv = n_kv * HEAD_DIM
    d_b = d - d_a
    d_qi = H_IDX * D_IDX
    past_len = n_pages * page
    lam_init = 0.8 - 0.6 * math.exp(-0.3 * 0)
    n_in = w_in.shape[2]

    w_t = jnp.swapaxes(w_in[0], 0, 1)
    n_main = 4 * d_a + 2 * d_b + 2 * d_kv + d_qi
    assert n_in - n_main == D_IDX + H_IDX
    w_a = _cast_rows(w_t, 0, 4 * d_a)
    w_b = _cast_rows(w_t, 4 * d_a, n_main - 4 * d_a)
    w_tail = jnp.pad(w_t[n_main:], ((0, LANES - (n_in - n_main)), (0, 0))).astype(BF16)
    wts = dict(pre_w=pre_norm_w[0][None, :], w_a=w_a, w_b=w_b, w_tail=w_tail,
               d_a=d_a, d_b=d_b, d_kv=d_kv, d_qi=d_qi)
    wo = w_out[0].astype(BF16)
    wg = w_ple_gate[0].astype(BF16)
    we = w_ple_proj[0].astype(BF16)
    pw = post_norm_w[0][None, :]
    bg = b_ple_gate[0][None, :]
    subw = diff_norm_w[0][None, :]
    lamp = jnp.stack([lam_q1[0], lam_k1[0], lam_q2[0], lam_k2[0]], axis=0)

    m_p = bsz * t_p
    tm = _row_tile(t_p, ROW_TILE)
    tq_diff = _row_tile(t_p, DIFF_TILE)
    tq_dsa = _row_tile(t_p, DSA_TILE)
    tq_idx = _row_tile(t_p, IDX_TILE)
    xp2 = x_prompt.reshape(m_p, d)
    pos_p = jnp.arange(t_p, dtype=I32)
    (qa, ka, va, sga, va_t), (qs, ks, vs, sgs, qi, kiw, vs_t) = _inproj_all(xp2, pos_p, wts, tm, True)
    r3 = lambda a: a.reshape(bsz, t_p, a.shape[-1])
    a1 = _diff_attn(r3(qa), r3(ka), va_t, r3(sga), subw, lamp, lam_init, tq_diff)
    ki = kiw[:, :D_IDX]
    wi = kiw[:, D_IDX:D_IDX + H_IDX]
    qi4 = qi.reshape(bsz, t_p, H_IDX, D_IDX).transpose(0, 2, 1, 3)
    wi_t = wi.reshape(bsz, t_p, H_IDX).transpose(0, 2, 1)
    k_sel_p = min(TOPK_MAX, t_p // 4)
    bias_t = _index_select(qi4, ki.astype(BF16).reshape(bsz, t_p, D_IDX), wi_t, tq_idx, k_sel_p)
    a2 = _dsa_attn(r3(qs), r3(ks), vs_t, bias_t, r3(sgs), tq_dsa)
    y_p = _out_proj(xp2, a1.reshape(m_p, d_a), a2.reshape(m_p, d_b), p_prompt[0].reshape(m_p, -1),
                    wo, pw, wg, bg, we, tm)

    xs2 = x_sample.reshape(nb, d)
    pos_s = jnp.full((nb,), past_len, dtype=I32)
    (qa_s, ka_s, va_s, sga_s), (qs_s, ks_s, vs_s, sgs_s, qi_s, kiw_s) = _inproj_all(xs2, pos_s, wts, nb, False)
    ki_s = kiw_s[:, :D_IDX]
    wi_s = kiw_s[:, D_IDX:D_IDX + H_IDX]
    pt_flat = page_table.reshape(-1)
    n_group = SCORE_PAGES_PER_STEP if n_pages % SCORE_PAGES_PER_STEP == 0 else 1
    cache_idx_t = jnp.swapaxes(cache_idx_k.reshape(n_pool, page, D_IDX), 1, 2)
    scores = _sample_scores(pt_flat, qi_s.reshape(nb, H_IDX, D_IDX), wi_s.reshape(nb, H_IDX, 1),
                            cache_idx_t, n_pages, n_group)
    k_sel_s = min(TOPK_MAX, (past_len + t_s) // 4)
    bias_s = _sample_select(scores.reshape(nb, past_len), qi_s,
                            jnp.tile(ki_s.astype(BF16), (1, H_IDX)), wi_s, k_sel_s)
    bias_pages = jnp.repeat(bias_s[:, :past_len], n_kv, axis=1).reshape(nb, n_pages, 1, page * n_kv)
    bias_new = bias_s[:, past_len:past_len + 1].reshape(nb, 1, 1)
    e3 = lambda a: a.reshape(nb, 1, a.shape[-1])
    a1_s, a2_s = _sample_attn(
        pt_flat, e3(qa_s), e3(qs_s), e3(ka_s), e3(va_s), e3(ks_s), e3(vs_s), bias_new,
        e3(sga_s), e3(sgs_s), subw, lamp,
        cache_diff_k.reshape(n_pool, page * n_ha, HEAD_DIM), cache_diff_v.reshape(n_pool, page * n_ha, HEAD_DIM),
        cache_dsa_k.reshape(n_pool, page * n_kv, HEAD_DIM), cache_dsa_v.reshape(n_pool, page * n_kv, HEAD_DIM),
        bias_pages, n_pages, lam_init)
    y_s = _out_proj(xs2, a1_s.reshape(nb, d_a), a2_s.reshape(nb, d_b), p_sample[0].reshape(nb, -1),
                    wo, pw, wg, bg, we, nb)

    return (
        y_p.reshape(bsz, t_p, d), y_s.reshape(nb, t_s, d),
        ka.reshape(1, bsz, t_p, n_ha, HEAD_DIM), va.reshape(1, bsz, t_p, n_ha, HEAD_DIM),
        ks.reshape(1, bsz, t_p, n_kv, HEAD_DIM), vs.reshape(1, bsz, t_p, n_kv, HEAD_DIM),
        ki.reshape(1, bsz, t_p, D_IDX),
        ka_s.reshape(1, nb, t_s, n_ha, HEAD_DIM), va_s.reshape(1, nb, t_s, n_ha, HEAD_DIM),
        ks_s.reshape(1, nb, t_s, n_kv, HEAD_DIM), vs_s.reshape(1, nb, t_s, n_kv, HEAD_DIM),
        ki_s.reshape(1, nb, t_s, D_IDX),
    )
```

```python
import functools
import math

import jax
import jax.numpy as jnp
from jax import lax
from jax.experimental import pallas as pl
from jax.experimental.pallas import tpu as pltpu

F32 = jnp.float32
BF16 = jnp.bfloat16
I32 = jnp.int32

LANES = 128
SUBLANES = 8
HEAD_DIM = 128
DH_A = HEAD_DIM // 2
D_IDX = 64
H_IDX = 4
TOPK_MAX = 256
ROPE_THETA = 500000.0
ROPE_FRAC = 4
RMS_EPS = 1e-6
NEG_BIG = -1e30
INT_MIN = -(2 ** 31)
NEG_INF_KEY = INT_MIN + 0x7FFFFF
VMEM_LIMIT = 56 * 1024 * 1024
ROW_TILE = 256
LOG2E = math.log2(math.e)
DIFF_TILE = 512
DIFF_HEADS_PER_STEP = 2
DSA_TILE = 256
IDX_TILE = 512
SAMPLE_PAGES_PER_STEP = 8
SCORE_PAGES_PER_STEP = 16


def _cparams(n_axes):
    return pltpu.CompilerParams(
        dimension_semantics=("arbitrary",) * n_axes, vmem_limit_bytes=VMEM_LIMIT)


def _dot(a, b):
    return jnp.dot(a, b, preferred_element_type=F32)


def _dot_nt(a, b):
    return lax.dot_general(a, b, (((1,), (1,)), ((), ())), preferred_element_type=F32)


def _rope_tables(pos, d):
    r = d // ROPE_FRAC
    half = r // 2
    inv = ROPE_THETA ** (-(2.0 / r) * jnp.arange(half, dtype=F32))
    ang = pos.astype(F32)[:, None] * inv[None, :]
    cos, sin = jnp.cos(ang), jnp.sin(ang)
    t = pos.shape[0]
    ones = jnp.ones((t, d - r), F32)
    zeros_h = jnp.zeros((t, half), F32)
    zeros_r = jnp.zeros((t, d - r), F32)
    c = jnp.concatenate([cos, cos, ones], axis=-1)
    sm = jnp.concatenate([-sin, zeros_h, zeros_r], axis=-1)
    sp = jnp.concatenate([zeros_h, sin, zeros_r], axis=-1)
    rep = LANES // d
    return tuple(jnp.tile(a, (1, rep)) for a in (c, sm, sp)), half


def _rope_chunk(z, c, sm, sp, half):
    return (z * c + pltpu.roll(z, LANES - half, axis=1) * sm
            + pltpu.roll(z, half, axis=1) * sp)


def _silu(z):
    return z * (1.0 / (1.0 + jnp.exp(-z)))


def _rms_rows(x, w):
    return x * lax.rsqrt(jnp.mean(x * x, axis=-1, keepdims=True) + RMS_EPS) * w


def _inproj_a_kernel(x_ref, nw_ref, w_ref, c_ref, sm_ref, sp_ref,
                     qa_ref, ka_ref, va_ref, ga_ref, *maybe_vt_ref, half, d_a):
    h = _rms_rows(x_ref[...], nw_ref[...]).astype(BF16)
    c, sm, sp = c_ref[...], sm_ref[...], sp_ref[...]
    n_chunks = d_a // LANES
    zq = _dot_nt(h, w_ref[0:d_a, :])
    for j in range(n_chunks):
        sl = slice(j * LANES, (j + 1) * LANES)
        qa_ref[:, sl] = (_rope_chunk(zq[:, sl], c, sm, sp, half) * (LOG2E / math.sqrt(DH_A))).astype(BF16)
    zk = _dot_nt(h, w_ref[d_a:2 * d_a, :])
    for j in range(n_chunks):
        sl = slice(j * LANES, (j + 1) * LANES)
        ka_ref[:, sl] = _rope_chunk(zk[:, sl], c, sm, sp, half)
    zv = _dot_nt(h, w_ref[2 * d_a:3 * d_a, :])
    va_ref[...] = zv
    if maybe_vt_ref:
        maybe_vt_ref[0][...] = zv.T.astype(BF16)
    ga_ref[...] = _silu(_dot_nt(h, w_ref[3 * d_a:4 * d_a, :]))


def _inproj_a(x2d, nw, w_a, tabs64, tm, d_a, emit_vt):
    m, d = x2d.shape
    (c, sm, sp), half = tabs64
    t_blocks = c.shape[0] // tm
    row = lambda i: (i, 0)
    tab = lambda i: (i % t_blocks, 0)
    fixed = lambda i: (0, 0)
    out_specs = [pl.BlockSpec((tm, d_a), row)] * 4
    out_shape = [
        jax.ShapeDtypeStruct((m, d_a), BF16),
        jax.ShapeDtypeStruct((m, d_a), F32),
        jax.ShapeDtypeStruct((m, d_a), F32),
        jax.ShapeDtypeStruct((m, d_a), F32),
    ]
    if emit_vt:
        out_specs.append(pl.BlockSpec((d_a, tm), lambda i: (0, i)))
        out_shape.append(jax.ShapeDtypeStruct((d_a, m), BF16))
    return pl.pallas_call(
        functools.partial(_inproj_a_kernel, half=half, d_a=d_a),
        grid=(m // tm,),
        in_specs=[
            pl.BlockSpec((tm, d), row),
            pl.BlockSpec((1, d), fixed),
            pl.BlockSpec((4 * d_a, d), fixed, pipeline_mode=pl.Buffered(1)),
            pl.BlockSpec((tm, LANES), tab),
            pl.BlockSpec((tm, LANES), tab),
            pl.BlockSpec((tm, LANES), tab),
        ],
        out_specs=out_specs,
        out_shape=out_shape,
        compiler_params=_cparams(1),
        name="inproj_a",
    )(x2d, nw, w_a, c, sm, sp)


def _inproj_b_kernel(x_ref, nw_ref, w_ref, wt_ref, c128_ref, sm128_ref, sp128_ref,
                     c64_ref, sm64_ref, sp64_ref,
                     qs_ref, ks_ref, vs_ref, gs_ref, qi_ref, kiw_ref, *maybe_vt_ref,
                     half128, half64, d_b, d_kv, d_qi):
    h = _rms_rows(x_ref[...], nw_ref[...]).astype(BF16)
    c128, sm128, sp128 = c128_ref[...], sm128_ref[...], sp128_ref[...]
    c64, sm64, sp64 = c64_ref[...], sm64_ref[...], sp64_ref[...]
    o = 0
    zq = _dot_nt(h, w_ref[o:o + d_b, :])
    for j in range(d_b // LANES):
        sl = slice(j * LANES, (j + 1) * LANES)
        qs_ref[:, sl] = (_rope_chunk(zq[:, sl], c128, sm128, sp128, half128)
                         * (LOG2E / math.sqrt(HEAD_DIM))).astype(BF16)
    o += d_b
    zk = _dot_nt(h, w_ref[o:o + d_kv, :])
    for j in range(d_kv // LANES):
        sl = slice(j * LANES, (j + 1) * LANES)
        ks_ref[:, sl] = _rope_chunk(zk[:, sl], c128, sm128, sp128, half128)
    o += d_kv
    zv = _dot_nt(h, w_ref[o:o + d_kv, :])
    vs_ref[...] = zv
    if maybe_vt_ref:
        maybe_vt_ref[0][...] = zv.T.astype(BF16)
    o += d_kv
    gs_ref[...] = _silu(_dot_nt(h, w_ref[o:o + d_b, :]))
    o += d_b
    zi = _dot_nt(h, w_ref[o:o + d_qi, :])
    for j in range(d_qi // LANES):
        sl = slice(j * LANES, (j + 1) * LANES)
        qi_ref[:, sl] = _rope_chunk(zi[:, sl], c64, sm64, sp64, half64).astype(BF16)
    zkw = _dot_nt(h, wt_ref[...])
    lane = lax.broadcasted_iota(I32, zkw.shape, 1)
    kiw_ref[...] = jnp.where(lane < D_IDX, _rope_chunk(zkw, c64, sm64, sp64, half64), zkw)


def _inproj_b(x2d, nw, w_b, w_tail, tabs128, tabs64, tm, d_b, d_kv, d_qi, emit_vt):
    m, d = x2d.shape
    (c128, sm128, sp128), half128 = tabs128
    (c64, sm64, sp64), half64 = tabs64
    t_blocks = c128.shape[0] // tm
    row = lambda i: (i, 0)
    tab = lambda i: (i % t_blocks, 0)
    fixed = lambda i: (0, 0)
    n_b = w_b.shape[0]
    out_specs = [
        pl.BlockSpec((tm, d_b), row),
        pl.BlockSpec((tm, d_kv), row),
        pl.BlockSpec((tm, d_kv), row),
        pl.BlockSpec((tm, d_b), row),
        pl.BlockSpec((tm, d_qi), row),
        pl.BlockSpec((tm, LANES), row),
    ]
    out_shape = [
        jax.ShapeDtypeStruct((m, d_b), BF16),
        jax.ShapeDtypeStruct((m, d_kv), F32),
        jax.ShapeDtypeStruct((m, d_kv), F32),
        jax.ShapeDtypeStruct((m, d_b), F32),
        jax.ShapeDtypeStruct((m, d_qi), BF16),
        jax.ShapeDtypeStruct((m, LANES), F32),
    ]
    if emit_vt:
        out_specs.append(pl.BlockSpec((d_kv, tm), lambda i: (0, i)))
        out_shape.append(jax.ShapeDtypeStruct((d_kv, m), BF16))
    return pl.pallas_call(
        functools.partial(_inproj_b_kernel, half128=half128, half64=half64,
                          d_b=d_b, d_kv=d_kv, d_qi=d_qi),
        grid=(m // tm,),
        in_specs=[
            pl.BlockSpec((tm, d), row),
            pl.BlockSpec((1, d), fixed),
            pl.BlockSpec((n_b, d), fixed, pipeline_mode=pl.Buffered(1)),
            pl.BlockSpec((LANES, d), fixed),
        ] + [pl.BlockSpec((tm, LANES), tab)] * 6,
        out_specs=out_specs,
        out_shape=out_shape,
        compiler_params=_cparams(1),
        name="inproj_b",
    )(x2d, nw, w_b, w_tail, c128, sm128, sp128, c64, sm64, sp64)


def _flash_init(m_ref, l_ref, acc_ref):
    m_ref[...] = jnp.full(m_ref.shape, NEG_BIG, F32)
    l_ref[...] = jnp.zeros(l_ref.shape, F32)
    acc_ref[...] = jnp.zeros(acc_ref.shape, F32)


def _flash_step(s, v_bf, m_ref, l_ref, acc_ref, axis=1):
    m_prev = m_ref[...]
    m_new = jnp.maximum(m_prev, jnp.max(s, axis=axis, keepdims=True))
    alpha = jnp.exp2(m_prev - m_new)
    p = jnp.exp2(s - m_new)
    l_ref[...] = alpha * l_ref[...] + jnp.sum(p, axis=axis, keepdims=True)
    pv = _dot(p.astype(BF16), v_bf) if axis == 1 else _dot(v_bf, p.astype(BF16))
    acc_ref[...] = alpha * acc_ref[...] + pv
    m_ref[...] = m_new


def _flash_step_multi(s_list, v_list, m_ref, l_ref, acc_ref):
    m_prev = m_ref[...]
    m_new = m_prev
    for s in s_list:
        m_new = jnp.maximum(m_new, jnp.max(s, axis=1, keepdims=True))
    alpha = jnp.exp2(m_prev - m_new)
    l_new = alpha * l_ref[...]
    acc = alpha * acc_ref[...]
    for s, v_bf in zip(s_list, v_list):
        p = jnp.exp2(s - m_new)
        l_new = l_new + jnp.sum(p, axis=1, keepdims=True)
        acc = acc + _dot(p.astype(BF16), v_bf)
    l_ref[...] = l_new
    acc_ref[...] = acc
    m_ref[...] = m_new


def _lambda_full(lamp, lam_init):
    s1 = jnp.sum(lamp[0:1, :] * lamp[1:2, :], axis=-1, keepdims=True)
    s2 = jnp.sum(lamp[2:3, :] * lamp[3:4, :], axis=-1, keepdims=True)
    return jnp.exp(s1) - jnp.exp(s2) + lam_init


def _diff_attn_kernel(q_ref, k_ref, vt_ref, g_ref, subw_ref, lamp_ref, o_ref,
                      kbf, sb, mb, lb, accb, *, tq, n_hs, lam_init):
    i = pl.program_id(2)

    @pl.when(i == 0)
    def _():
        kbf[...] = k_ref[0].astype(BF16)

    q = q_ref[0]
    lane = lax.broadcasted_iota(I32, (tq, HEAD_DIM), 1)
    qm = []
    for h in range(n_hs):
        qh = q[:, h * HEAD_DIM:(h + 1) * HEAD_DIM]
        zero = jnp.zeros_like(qh)
        qm.append((jnp.where(lane < DH_A, qh, zero), jnp.where(lane >= DH_A, qh, zero)))
        for mp in range(2):
            _flash_init(mb.at[h, mp], lb.at[h, mp], accb.at[h, mp])

    def scores(j, slot):
        off = pl.multiple_of(j * tq, tq)
        for h in range(n_hs):
            kb = kbf[pl.ds(off, tq), h * HEAD_DIM:(h + 1) * HEAD_DIM]
            for mp in range(2):
                sb[h, mp, slot] = _dot_nt(kb, qm[h][mp])

    def softmax_pv(j, slot, masked):
        off = pl.multiple_of(j * tq, tq)
        for h in range(n_hs):
            vtb = vt_ref[h * HEAD_DIM:(h + 1) * HEAD_DIM, pl.ds(off, tq)]
            for mp in range(2):
                s = sb[h, mp, slot]
                if masked:
                    kidx = lax.broadcasted_iota(I32, s.shape, 0)
                    qidx = lax.broadcasted_iota(I32, s.shape, 1)
                    s = jnp.where(kidx <= qidx, s, NEG_BIG)
                _flash_step(s, vtb, mb.at[h, mp], lb.at[h, mp], accb.at[h, mp], axis=0)

    scores(0, 0)

    def pair(jj, carry):
        j = 2 * jj
        scores(j + 1, 1)
        softmax_pv(j, 0, False)
        scores(j + 2, 0)
        softmax_pv(j + 1, 1, False)
        return carry

    lax.fori_loop(0, i // 2, pair, 0)

    @pl.when(i % 2 == 1)
    def _():
        scores(i, 1)
        softmax_pv(i - 1, 0, False)
        softmax_pv(i, 1, True)

    @pl.when(i % 2 == 0)
    def _():
        softmax_pv(i, 0, True)

    lam = _lambda_full(lamp_ref[...], lam_init)
    g = g_ref[0]
    for h in range(n_hs):
        sl = slice(h * HEAD_DIM, (h + 1) * HEAD_DIM)
        o_t = accb[h, 0] / lb[h, 0] - lam * (accb[h, 1] / lb[h, 1])
        o = _rms_rows(o_t.T, subw_ref[...]) * (1.0 - lam_init)
        o_ref[0, :, sl] = (o * g[:, sl]).astype(BF16)


def _diff_attn(qa, ka, va_t, sga, subw, lamp, lam_init, tq):
    b, t, d_a = qa.shape
    n_h = d_a // HEAD_DIM
    n_hs = DIFF_HEADS_PER_STEP if n_h % DIFF_HEADS_PER_STEP == 0 else 1
    hw = n_hs * HEAD_DIM
    qmap = lambda bi, h, i: (bi, i, h)
    kmap = lambda bi, h, i: (bi, 0, h)
    fixed = lambda bi, h, i: (0, 0)
    return pl.pallas_call(
        functools.partial(_diff_attn_kernel, tq=tq, n_hs=n_hs, lam_init=lam_init),
        grid=(b, n_h // n_hs, t // tq),
        in_specs=[
            pl.BlockSpec((1, tq, hw), qmap),
            pl.BlockSpec((1, t, hw), kmap),
            pl.BlockSpec((hw, t), lambda bi, h, i: (h, bi)),
            pl.BlockSpec((1, tq, hw), qmap),
            pl.BlockSpec((1, HEAD_DIM), fixed),
            pl.BlockSpec((4, DH_A), fixed),
        ],
        out_specs=pl.BlockSpec((1, tq, hw), qmap),
        out_shape=jax.ShapeDtypeStruct((b, t, d_a), BF16),
        scratch_shapes=[
            pltpu.VMEM((t, hw), BF16),
            pltpu.VMEM((n_hs, 2, 2, tq, tq), F32),
            pltpu.VMEM((n_hs, 2, 1, tq), F32), pltpu.VMEM((n_hs, 2, 1, tq), F32),
            pltpu.VMEM((n_hs, 2, HEAD_DIM, tq), F32),
        ],
        compiler_params=_cparams(3),
        name="diff_attn",
    )(qa, ka, va_t, sga, subw, lamp)


def _key_to_float(key):
    key = jnp.maximum(key, NEG_INF_KEY)
    return lax.bitcast_convert_type(key ^ ((key >> 31) & 0x7FFFFFFF), F32)


def _fold(m, axis):
    if axis == 1:
        n = m.shape[1] // LANES
        acc = m[:, 0:LANES]
        for j in range(1, n):
            acc = acc + m[:, j * LANES:(j + 1) * LANES]
        return acc
    return jnp.sum(m.reshape(m.shape[0] // SUBLANES, SUBLANES, m.shape[1]), axis=0)


def _count(key_ref, n_chunks, chunk, pred, axis):
    other = key_ref.shape[1 - axis]
    acc_shape = (other, LANES) if axis == 1 else (SUBLANES, other)

    def body(c, acc):
        off = pl.multiple_of(c * chunk, chunk)
        kc = key_ref[:, pl.ds(off, chunk)] if axis == 1 else key_ref[pl.ds(off, chunk), :]
        return acc + _fold(pred(kc, off), axis)

    acc = lax.fori_loop(0, n_chunks, body, jnp.zeros(acc_shape, F32))
    return jnp.sum(acc, axis=axis, keepdims=True)


def _select_topk(key_ref, p_ref, n_chunks, chunk, k_sel, idx_bits, axis):
    k_f = float(k_sel)
    q_shape = p_ref.shape

    def bit_body(it, tu):
        bit = lax.shift_left(jnp.int32(1), 31 - it)
        cand_u = tu | bit
        cand = _key_to_float(cand_u ^ INT_MIN)
        cnt = _count(key_ref, n_chunks, chunk,
                     lambda kc, off: jnp.where(kc >= cand, 1.0, 0.0), axis)
        return jnp.where(cnt >= k_f, cand_u, tu)

    tu = lax.fori_loop(0, 32, bit_body, jnp.zeros(q_shape, I32))
    thr = _key_to_float(tu ^ INT_MIN)
    n_gt = _count(key_ref, n_chunks, chunk, lambda kc, off: jnp.where(kc > thr, 1.0, 0.0), axis)
    n_eq = _count(key_ref, n_chunks, chunk, lambda kc, off: jnp.where(kc == thr, 1.0, 0.0), axis)
    need = k_f - n_gt
    p_ref[...] = jnp.full(q_shape, 2 ** 30, I32)
    excess = jnp.max(n_eq - need)

    @pl.when(excess > 0.0)
    def _():
        def idx_body(it, p):
            bit = lax.shift_left(jnp.int32(1), idx_bits - 1 - it)
            cand = p | bit

            def pred(kc, off):
                idx = off + lax.broadcasted_iota(I32, kc.shape, axis)
                return jnp.where(kc == thr, jnp.where(idx < cand, 1.0, 0.0), 0.0)

            cnt = _count(key_ref, n_chunks, chunk, pred, axis)
            return jnp.where(cnt < need, cand, p)

        p_ref[...] = lax.fori_loop(0, idx_bits, idx_body, jnp.zeros(q_shape, I32))

    return thr


def _selected_bias(kc, off, thr, p_max, axis):
    idx = off + lax.broadcasted_iota(I32, kc.shape, axis)
    tie = jnp.where(kc == thr, jnp.where(idx <= p_max, 1.0, 0.0), 0.0)
    sel = jnp.where(kc > thr, 1.0, tie)
    sel = jnp.where(kc == -jnp.inf, 0.0, sel)
    return jnp.where(sel > 0.5, 0.0, NEG_BIG)


def _index_select_kernel(qi_ref, ki_ref, wit_ref, bias_ref, key_ref, p_ref, *, tq, k_sel, idx_bits):
    i = pl.program_id(1)
    n_total = bias_ref.shape[1] // tq
    wit = wit_ref[0]

    def score_body(c, carry):
        off = pl.multiple_of(c * tq, tq)
        kc = ki_ref[0, pl.ds(off, tq), :]
        sc = jnp.zeros((tq, tq), F32)
        for h in range(H_IDX):
            sc = sc + jnp.maximum(_dot_nt(kc, qi_ref[0, h]), 0.0) * wit[h:h + 1, :]
        kidx = lax.broadcasted_iota(I32, sc.shape, 0) + off
        qidx = lax.broadcasted_iota(I32, sc.shape, 1) + i * tq
        sc = jnp.where(kidx <= qidx, sc, -jnp.inf)
        key_ref[pl.ds(off, tq), :] = sc
        return carry

    lax.fori_loop(0, i + 1, score_body, 0)
    thr = _select_topk(key_ref, p_ref, i + 1, tq, k_sel, idx_bits, axis=0)
    p_max = p_ref[...]

    def out_body(c, carry):
        off = pl.multiple_of(c * tq, tq)
        kc = key_ref[pl.ds(off, tq), :]
        bias_ref[0, pl.ds(off, tq), :] = _selected_bias(kc, off, thr, p_max, 0).astype(BF16)
        return carry

    lax.fori_loop(0, i + 1, out_body, 0)

    def fill_body(c, carry):
        off = pl.multiple_of(c * tq, tq)
        bias_ref[0, pl.ds(off, tq), :] = jnp.full((tq, tq), NEG_BIG, BF16)
        return carry

    lax.fori_loop(i + 1, n_total, fill_body, 0)


def _index_select(qi4, ki_bf, wi_t, tq, k_sel):
    b, _, t, _ = qi4.shape
    idx_bits = max(1, (t - 1).bit_length())
    return pl.pallas_call(
        functools.partial(_index_select_kernel, tq=tq, k_sel=k_sel, idx_bits=idx_bits),
        grid=(b, t // tq),
        in_specs=[
            pl.BlockSpec((1, H_IDX, tq, D_IDX), lambda bi, i: (bi, 0, i, 0)),
            pl.BlockSpec((1, t, D_IDX), lambda bi, i: (bi, 0, 0)),
            pl.BlockSpec((1, H_IDX, tq), lambda bi, i: (bi, 0, i)),
        ],
        out_specs=pl.BlockSpec((1, t, tq), lambda bi, i: (bi, 0, i)),
        out_shape=jax.ShapeDtypeStruct((b, t, t), BF16),
        scratch_shapes=[pltpu.VMEM((t, tq), F32), pltpu.VMEM((1, tq), I32)],
        compiler_params=_cparams(2),
        name="index_select",
    )(qi4, ki_bf, wi_t)


def _dsa_attn_kernel(q_ref, k_ref, vt_ref, bias_ref, g_ref, o_ref, kbf, sb, m, l, acc, *, tq, group):
    i = pl.program_id(2)

    @pl.when(i == 0)
    def _():
        kbf[...] = k_ref[0].astype(BF16)

    q = q_ref[0]
    q4 = jnp.concatenate([q[:, h * HEAD_DIM:(h + 1) * HEAD_DIM] for h in range(group)], axis=0)
    _flash_init(m, l, acc)

    def scores(j, slot):
        kb = kbf[pl.ds(pl.multiple_of(j * tq, tq), tq), :]
        sb[slot] = _dot_nt(kb, q4)

    def softmax_pv(j, slot):
        off = pl.multiple_of(j * tq, tq)
        vtb = vt_ref[:, pl.ds(off, tq)]
        bias = bias_ref[0, pl.ds(off, tq), :].astype(F32)
        s = sb[slot] + jnp.concatenate([bias] * group, axis=1)
        _flash_step(s, vtb, m, l, acc, axis=0)

    scores(0, 0)

    def pair(jj, carry):
        j = 2 * jj
        scores(j + 1, 1)
        softmax_pv(j, 0)
        scores(jnp.minimum(j + 2, i), 0)
        softmax_pv(j + 1, 1)
        return carry

    lax.fori_loop(0, (i + 1) // 2, pair, 0)

    @pl.when(i % 2 == 0)
    def _():
        softmax_pv(i, 0)
    o_t = acc[...] / l[...]
    g = g_ref[0]
    for h in range(group):
        sl = slice(h * HEAD_DIM, (h + 1) * HEAD_DIM)
        o_ref[0, :, sl] = (o_t[:, h * tq:(h + 1) * tq].T * g[:, sl]).astype(BF16)


def _dsa_attn(qs, ks, vs_t, bias_t, sgs, tq):
    b, t, d_b = qs.shape
    n_kv = ks.shape[2] // HEAD_DIM
    group = d_b // HEAD_DIM // n_kv
    gw = group * HEAD_DIM
    qmap = lambda bi, n, i: (bi, i, n)
    kmap = lambda bi, n, i: (bi, 0, n)
    return pl.pallas_call(
        functools.partial(_dsa_attn_kernel, tq=tq, group=group),
        grid=(b, n_kv, t // tq),
        in_specs=[
            pl.BlockSpec((1, tq, gw), qmap),
            pl.BlockSpec((1, t, HEAD_DIM), kmap),
            pl.BlockSpec((HEAD_DIM, t), lambda bi, n, i: (n, bi)),
            pl.BlockSpec((1, t, tq), lambda bi, n, i: (bi, 0, i)),
            pl.BlockSpec((1, tq, gw), qmap),
        ],
        out_specs=pl.BlockSpec((1, tq, gw), qmap),
        out_shape=jax.ShapeDtypeStruct((b, t, d_b), BF16),
        scratch_shapes=[
            pltpu.VMEM((t, HEAD_DIM), BF16),
            pltpu.VMEM((2, tq, group * tq), F32),
            pltpu.VMEM((1, group * tq), F32), pltpu.VMEM((1, group * tq), F32),
            pltpu.VMEM((HEAD_DIM, group * tq), F32),
        ],
        compiler_params=_cparams(3),
        name="dsa_attn",
    )(qs, ks, vs_t, bias_t, sgs)


def _out_kernel(x_ref, a1_ref, a2_ref, p_ref, wo_ref, pw_ref, wg_ref, bg_ref, we_ref, y_ref, *, d_a):
    o = _dot(a1_ref[...], wo_ref[0:d_a, :]) + _dot(a2_ref[...], wo_ref[d_a:, :])
    x1 = x_ref[...] + _rms_rows(o, pw_ref[...])
    z = _dot(x1.astype(BF16), wg_ref[...]) + bg_ref[...]
    gate = 1.0 / (1.0 + jnp.exp(-z))
    y_ref[...] = x1 + gate * _dot(p_ref[...].astype(BF16), we_ref[...])


def _out_proj(x2d, a1, a2, p2d, wo, pw, wg, bg, we, tm):
    m, d = x2d.shape
    d_a = a1.shape[1]
    d_mix = wo.shape[0]
    d_ple = p2d.shape[1]
    row = lambda i: (i, 0)
    fixed = lambda i: (0, 0)
    single = dict(pipeline_mode=pl.Buffered(1))
    return pl.pallas_call(
        functools.partial(_out_kernel, d_a=d_a),
        grid=(m // tm,),
        in_specs=[
            pl.BlockSpec((tm, d), row),
            pl.BlockSpec((tm, d_a), row),
            pl.BlockSpec((tm, d_mix - d_a), row),
            pl.BlockSpec((tm, d_ple), row),
            pl.BlockSpec((d_mix, d), fixed, **single),
            pl.BlockSpec((1, d), fixed),
            pl.BlockSpec((d, d), fixed, **single),
            pl.BlockSpec((1, d), fixed),
            pl.BlockSpec((d_ple, d), fixed, **single),
        ],
        out_specs=pl.BlockSpec((tm, d), row),
        out_shape=jax.ShapeDtypeStruct((m, d), F32),
        compiler_params=_cparams(1),
        name="out_proj",
    )(x2d, a1, a2, p2d, wo, pw, wg, bg, we)


def _sample_scores_kernel(pt_ref, qi_ref, wi_ref, *rest, n_group):
    del pt_ref
    page_refs, out_ref = rest[:n_group], rest[n_group]
    q4 = qi_ref[0]
    w = wi_ref[0]
    for r in range(n_group):
        kp_t = page_refs[r][0].astype(BF16)
        rel = jnp.maximum(_dot(q4, kp_t), 0.0)
        out_ref[0, r:r + 1, :] = jnp.sum(rel * w, axis=0, keepdims=True)


def _sample_scores(page_table_flat, qi4, wi3, cache_idx_t, n_pages, n_group):
    nb = qi4.shape[0]
    page = cache_idx_t.shape[2]

    def page_map(r):
        return lambda bi, g, pt: (pt[bi * n_pages + g * n_group + r], 0, 0)

    grid_spec = pltpu.PrefetchScalarGridSpec(
        num_scalar_prefetch=1,
        grid=(nb, n_pages // n_group),
        in_specs=[
            pl.BlockSpec((1, H_IDX, D_IDX), lambda bi, g, pt: (bi, 0, 0)),
            pl.BlockSpec((1, H_IDX, 1), lambda bi, g, pt: (bi, 0, 0)),
        ] + [pl.BlockSpec((1, D_IDX, page), page_map(r)) for r in range(n_group)],
        out_specs=pl.BlockSpec((1, n_group, page), lambda bi, g, pt: (bi, g, 0)),
    )
    return pl.pallas_call(
        functools.partial(_sample_scores_kernel, n_group=n_group),
        grid_spec=grid_spec,
        out_shape=jax.ShapeDtypeStruct((nb, n_pages, page), F32),
        compiler_params=_cparams(2),
        name="sample_scores",
    )(page_table_flat, qi4, wi3, *([cache_idx_t] * n_group))


def _sample_select_kernel(sc_ref, qi_ref, kit_ref, wi_ref, bias_ref, key_ref, p_ref, *, k_sel, idx_bits):
    rows, s_past = sc_ref.shape
    s_all = key_ref.shape[1]
    prod = qi_ref[...].astype(F32) * kit_ref[...].astype(F32)
    lane = lax.broadcasted_iota(I32, prod.shape, 1)
    wi = wi_ref[...]
    new = jnp.zeros((rows, 1), F32)
    for h in range(H_IDX):
        dot_h = jnp.sum(jnp.where(lane // D_IDX == h, prod, 0.0), axis=-1, keepdims=True)
        new = new + jnp.maximum(dot_h, 0.0) * wi[:, h:h + 1]
    key_ref[:, 0:s_past] = sc_ref[...]
    tail_lane = lax.broadcasted_iota(I32, (rows, s_all - s_past), 1)
    key_ref[:, s_past:s_all] = jnp.where(tail_lane == 0, new, -jnp.inf)
    thr = _select_topk(key_ref, p_ref, 1, s_all, k_sel, idx_bits, axis=1)
    bias_ref[...] = _selected_bias(key_ref[...], 0, thr, p_ref[...], 1)


def _sample_select(scores2d, qi, ki_tiled, wi, k_sel):
    rows, s_past = scores2d.shape
    s_all = s_past + LANES
    idx_bits = max(1, (s_all - 1).bit_length())
    return pl.pallas_call(
        functools.partial(_sample_select_kernel, k_sel=k_sel, idx_bits=idx_bits),
        out_shape=jax.ShapeDtypeStruct((rows, s_all), F32),
        scratch_shapes=[pltpu.VMEM((rows, s_all), F32), pltpu.VMEM((rows, 1), I32)],
        compiler_params=pltpu.CompilerParams(vmem_limit_bytes=VMEM_LIMIT),
        name="sample_select",
    )(scores2d, qi, ki_tiled, wi)


def _head_rows(row, n_rows, rows_per_head):
    return jnp.concatenate(
        [row[:, (c // rows_per_head) * HEAD_DIM:(c // rows_per_head + 1) * HEAD_DIM]
         for c in range(n_rows)], axis=0)


def _sample_attn_kernel(pt_ref, qa_ref, qs_ref, kan_ref, van_ref, ksn_ref, vsn_ref, bnew_ref,
                        ga_ref, gs_ref, subw_ref, lamp_ref, *rest, n_pg, n_ha, n_hb, group, lam_init):
    del pt_ref
    kd_refs, vd_refs = rest[0:n_pg], rest[n_pg:2 * n_pg]
    ks_refs, vs_refs = rest[2 * n_pg:3 * n_pg], rest[3 * n_pg:4 * n_pg]
    bias_ref, oa_ref, ob_ref, qd, qsb, md, ld, accd, ms, ls, accs = rest[4 * n_pg:]
    p = pl.program_id(1)
    n_p = pl.num_programs(1)
    n_kv = n_hb // group

    @pl.when(p == 0)
    def _():
        qd_f = _head_rows(qa_ref[0].astype(F32), 2 * n_ha, 2)
        rd = lax.broadcasted_iota(I32, qd_f.shape, 0)
        ln = lax.broadcasted_iota(I32, qd_f.shape, 1)
        qd_f = jnp.where(ln // DH_A == rd % 2, qd_f, 0.0)
        qd[...] = qd_f.astype(BF16)
        qs_f = _head_rows(qs_ref[0].astype(F32), n_hb, 1)
        qsb[...] = qs_f.astype(BF16)
        kan = _head_rows(kan_ref[0].astype(BF16).astype(F32), 2 * n_ha, 2)
        van = _head_rows(van_ref[0].astype(BF16).astype(F32), 2 * n_ha, 2)
        md[...] = jnp.sum(qd_f * kan, axis=-1, keepdims=True)
        ld[...] = jnp.ones(ld.shape, F32)
        accd[...] = van
        ksn = _head_rows(ksn_ref[0].astype(BF16).astype(F32), n_hb, group)
        vsn = _head_rows(vsn_ref[0].astype(BF16).astype(F32), n_hb, group)
        ms[...] = jnp.sum(qs_f * ksn, axis=-1, keepdims=True) + bnew_ref[0]
        ls[...] = jnp.ones(ls.shape, F32)
        accs[...] = vsn

    sd_list, vd_list, ss_list, vs_list = [], [], [], []
    for r in range(n_pg):
        kd = kd_refs[r][0].astype(BF16)
        sd = _dot_nt(qd[...], kd)
        rq = lax.broadcasted_iota(I32, sd.shape, 0)
        ck = lax.broadcasted_iota(I32, sd.shape, 1)
        sd_list.append(jnp.where(ck % n_ha == rq // 2, sd, NEG_BIG))
        vd_list.append(vd_refs[r][0].astype(BF16))
        ksp = ks_refs[r][0].astype(BF16)
        ss = _dot_nt(qsb[...], ksp) + bias_ref[0, r]
        rq = lax.broadcasted_iota(I32, ss.shape, 0)
        ck = lax.broadcasted_iota(I32, ss.shape, 1)
        ss_list.append(jnp.where(ck % n_kv == rq // group, ss, NEG_BIG))
        vs_list.append(vs_refs[r][0].astype(BF16))
    _flash_step_multi(sd_list, vd_list, md, ld, accd)
    _flash_step_multi(ss_list, vs_list, ms, ls, accs)

    @pl.when(p == n_p - 1)
    def _():
        lam = _lambda_full(lamp_ref[...], lam_init)
        od = accd[...] / ld[...]
        ga = ga_ref[0]
        subw = subw_ref[...]
        for h in range(n_ha):
            sl = slice(h * HEAD_DIM, (h + 1) * HEAD_DIM)
            o = od[2 * h:2 * h + 1, :] - lam * od[2 * h + 1:2 * h + 2, :]
            o = _rms_rows(o, subw) * (1.0 - lam_init)
            oa_ref[0, :, sl] = (o * ga[:, sl]).astype(BF16)
        os_ = accs[...] / ls[...]
        gs = gs_ref[0]
        for h in range(n_hb):
            sl = slice(h * HEAD_DIM, (h + 1) * HEAD_DIM)
            ob_ref[0, :, sl] = (os_[h:h + 1, :] * gs[:, sl]).astype(BF16)


def _sample_attn(page_table_flat, qa, qs, ka_new, va_new, ks_new, vs_new, bias_new, sga, sgs,
                 subw, lamp, cdk, cdv, csk, csv, bias_pages, n_pages, lam_init):
    nb, _, d_a = qa.shape
    d_b = qs.shape[2]
    d_kv = ks_new.shape[2]
    n_ha = d_a // HEAD_DIM
    n_hb = d_b // HEAD_DIM
    n_kv = d_kv // HEAD_DIM
    group = n_hb // n_kv
    rows_d = cdk.shape[1]
    rows_s = csk.shape[1]
    per_b = lambda bi, p, pt: (bi, 0, 0)
    fixed = lambda bi, p, pt: (0, 0)
    n_pg = SAMPLE_PAGES_PER_STEP if n_pages % SAMPLE_PAGES_PER_STEP == 0 else 1

    def paged(r):
        return lambda bi, p, pt: (pt[bi * n_pages + p * n_pg + r], 0, 0)

    grid_spec = pltpu.PrefetchScalarGridSpec(
        num_scalar_prefetch=1,
        grid=(nb, n_pages // n_pg),
        in_specs=[
            pl.BlockSpec((1, 1, d_a), per_b),
            pl.BlockSpec((1, 1, d_b), per_b),
            pl.BlockSpec((1, 1, d_a), per_b),
            pl.BlockSpec((1, 1, d_a), per_b),
            pl.BlockSpec((1, 1, d_kv), per_b),
            pl.BlockSpec((1, 1, d_kv), per_b),
            pl.BlockSpec((1, 1, 1), per_b),
            pl.BlockSpec((1, 1, d_a), per_b),
            pl.BlockSpec((1, 1, d_b), per_b),
            pl.BlockSpec((1, HEAD_DIM), fixed),
            pl.BlockSpec((4, DH_A), fixed),
        ] + [pl.BlockSpec((1, rows_d, HEAD_DIM), paged(r)) for r in range(n_pg)] * 2
        + [pl.BlockSpec((1, rows_s, HEAD_DIM), paged(r)) for r in range(n_pg)] * 2
        + [pl.BlockSpec((1, n_pg, 1, rows_s), lambda bi, p, pt: (bi, p, 0, 0))],
        out_specs=[pl.BlockSpec((1, 1, d_a), per_b), pl.BlockSpec((1, 1, d_b), per_b)],
        scratch_shapes=[
            pltpu.VMEM((2 * n_ha, HEAD_DIM), BF16), pltpu.VMEM((n_hb, HEAD_DIM), BF16),
            pltpu.VMEM((2 * n_ha, 1), F32), pltpu.VMEM((2 * n_ha, 1), F32),
            pltpu.VMEM((2 * n_ha, HEAD_DIM), F32),
            pltpu.VMEM((n_hb, 1), F32), pltpu.VMEM((n_hb, 1), F32), pltpu.VMEM((n_hb, HEAD_DIM), F32),
        ],
    )
    return pl.pallas_call(
        functools.partial(_sample_attn_kernel, n_pg=n_pg, n_ha=n_ha, n_hb=n_hb, group=group,
                          lam_init=lam_init),
        grid_spec=grid_spec,
        out_shape=[jax.ShapeDtypeStruct((nb, 1, d_a), BF16), jax.ShapeDtypeStruct((nb, 1, d_b), BF16)],
        compiler_params=_cparams(2),
        name="sample_attn",
    )(page_table_flat, qa, qs, ka_new, va_new, ks_new, vs_new, bias_new, sga, sgs, subw, lamp,
      *([cdk] * n_pg + [cdv] * n_pg + [csk] * n_pg + [csv] * n_pg), bias_pages)


def _row_tile(m, pref):
    return pref if m % pref == 0 else m


def _inproj_all(x2d, pos, wts, tm, emit_vt):
    tabs64 = _rope_tables(pos, DH_A)
    tabs128 = _rope_tables(pos, HEAD_DIM)
    d_a, d_b, d_kv, d_qi = wts["d_a"], wts["d_b"], wts["d_kv"], wts["d_qi"]
    outs_a = _inproj_a(x2d, wts["pre_w"], wts["w_a"], tabs64, tm, d_a, emit_vt)
    outs_b = _inproj_b(x2d, wts["pre_w"], wts["w_b"], wts["w_tail"], tabs128, tabs64, tm,
                       d_b, d_kv, d_qi, emit_vt)
    return outs_a, outs_b


def kernel(x_prompt, x_sample, p_prompt, p_sample, cache_diff_k, cache_diff_v, cache_dsa_k, cache_dsa_v, cache_idx_k, page_table, pre_norm_w, post_norm_w, w_in, lam_q1, lam_k1, lam_q2, lam_k2, diff_norm_w, w_out, w_ple_gate, b_ple_gate, w_ple_proj):
    depth = w_in.shape[0]
    assert depth == 1, "single-layer stack only"
    bsz, t_p, d = x_prompt.shape
    nb, t_s, _ = x_sample.shape
    assert t_s == 1, "one new token per sample sequence"
    n_pages = page_table.shape[1]
    n_pool = cache_diff_k.shape[1]
    page = cache_diff_k.shape[2]
    n_ha = cache_diff_k.shape[3]
    n_kv = cache_dsa_k.shape[3]
    d_a = n_ha * HEAD_DIM
    d_kv = n_kv * HEAD_DIM
    d_b = d - d_a
    d_qi = H_IDX * D_IDX
    past_len = n_pages * page
    lam_init = 0.8 - 0.6 * math.exp(-0.3 * 0)
    n_in = w_in.shape[2]

    w_t = jnp.swapaxes(w_in[0], 0, 1)
    n_main = 4 * d_a + 2 * d_b + 2 * d_kv + d_qi
    assert n_in - n_main == D_IDX + H_IDX
    w_a = w_t[:4 * d_a].astype(BF16)
    w_b = w_t[4 * d_a:n_main].astype(BF16)
    w_tail = jnp.pad(w_t[n_main:], ((0, LANES - (n_in - n_main)), (0, 0))).astype(BF16)
    wts = dict(pre_w=pre_norm_w[0][None, :], w_a=w_a, w_b=w_b, w_tail=w_tail,
               d_a=d_a, d_b=d_b, d_kv=d_kv, d_qi=d_qi)
    wo = w_out[0].astype(BF16)
    wg = w_ple_gate[0].astype(BF16)
    we = w_ple_proj[0].astype(BF16)
    pw = post_norm_w[0][None, :]
    bg = b_ple_gate[0][None, :]
    subw = diff_norm_w[0][None, :]
    lamp = jnp.stack([lam_q1[0], lam_k1[0], lam_q2[0], lam_k2[0]], axis=0)

    m_p = bsz * t_p
    tm = _row_tile(t_p, ROW_TILE)
    tq_diff = _row_tile(t_p, DIFF_TILE)
    tq_dsa = _row_tile(t_p, DSA_TILE)
    tq_idx = _row_tile(t_p, IDX_TILE)
    xp2 = x_prompt.reshape(m_p, d)
    pos_p = jnp.arange(t_p, dtype=I32)
    (qa, ka, va, sga, va_t), (qs, ks, vs, sgs, qi, kiw, vs_t) = _inproj_all(xp2, pos_p, wts, tm, True)
    r3 = lambda a: a.reshape(bsz, t_p, a.shape[-1])
    a1 = _diff_attn(r3(qa), r3(ka), va_t, r3(sga), subw, lamp, lam_init, tq_diff)
    ki = kiw[:, :D_IDX]
    wi = kiw[:, D_IDX:D_IDX + H_IDX]
    qi4 = qi.reshape(bsz, t_p, H_IDX, D_IDX).transpose(0, 2, 1, 3)
    wi_t = wi.reshape(bsz, t_p, H_IDX).transpose(0, 2, 1)
    k_sel_p = min(TOPK_MAX, t_p // 4)
    bias_t = _index_select(qi4, ki.astype(BF16).reshape(bsz, t_p, D_IDX), wi_t, tq_idx, k_sel_p)
    a2 = _dsa_attn(r3(qs), r3(ks), vs_t, bias_t, r3(sgs), tq_dsa)
    y_p = _out_proj(xp2, a1.reshape(m_p, d_a), a2.reshape(m_p, d_b), p_prompt[0].reshape(m_p, -1),
                    wo, pw, wg, bg, we, tm)

    xs2 = x_sample.reshape(nb, d)
    pos_s = jnp.full((nb,), past_len, dtype=I32)
    (qa_s, ka_s, va_s, sga_s), (qs_s, ks_s, vs_s, sgs_s, qi_s, kiw_s) = _inproj_all(xs2, pos_s, wts, nb, False)
    ki_s = kiw_s[:, :D_IDX]
    wi_s = kiw_s[:, D_IDX:D_IDX + H_IDX]
    pt_flat = page_table.reshape(-1)
    n_group = SCORE_PAGES_PER_STEP if n_pages % SCORE_PAGES_PER_STEP == 0 else 1
    cache_idx_t = jnp.swapaxes(cache_idx_k.reshape(n_pool, page, D_IDX), 1, 2)
    scores = _sample_scores(pt_flat, qi_s.reshape(nb, H_IDX, D_IDX), wi_s.reshape(nb, H_IDX, 1),
                            cache_idx_t, n_pages, n_group)
    k_sel_s = min(TOPK_MAX, (past_len + t_s) // 4)
    bias_s = _sample_select(scores.reshape(nb, past_len), qi_s,
                            jnp.tile(ki_s.astype(BF16), (1, H_IDX)), wi_s, k_sel_s)
    bias_pages = jnp.repeat(bias_s[:, :past_len], n_kv, axis=1).reshape(nb, n_pages, 1, page * n_kv)
    bias_new = bias_s[:, past_len:past_len + 1].reshape(nb, 1, 1)
    e3 = lambda a: a.reshape(nb, 1, a.shape[-1])
    a1_s, a2_s = _sample_attn(
        pt_flat, e3(qa_s), e3(qs_s), e3(ka_s), e3(va_s), e3(ks_s), e3(vs_s), bias_new,
        e3(sga_s), e3(sgs_s), subw, lamp,
        cache_diff_k.reshape(n_pool, page * n_ha, HEAD_DIM), cache_diff_v.reshape(n_pool, page * n_ha, HEAD_DIM),
        cache_dsa_k.reshape(n_pool, page * n_kv, HEAD_DIM), cache_dsa_v.reshape(n_pool, page * n_kv, HEAD_DIM),
        bias_pages, n_pages, lam_init)
    y_s = _out_proj(xs2, a1_s.reshape(nb, d_a), a2_s.reshape(nb, d_b), p_sample[0].reshape(nb, -1),
                    wo, pw, wg, bg, we, nb)

    return (
        y_p.reshape(bsz, t_p, d), y_s.reshape(nb, t_s, d),
        ka.reshape(1, bsz, t_p, n_ha, HEAD_DIM), va.reshape(1, bsz, t_p, n_ha, HEAD_DIM),
        ks.reshape(1, bsz, t_p, n_kv, HEAD_DIM), vs.reshape(1, bsz, t_p, n_kv, HEAD_DIM),
        ki.reshape(1, bsz, t_p, D_IDX),
        ka_s.reshape(1, nb, t_s, n_ha, HEAD_DIM), va_s.reshape(1, nb, t_s, n_ha, HEAD_DIM),
        ks_s.reshape(1, nb, t_s, n_kv, HEAD_DIM), vs_s.reshape(1, nb, t_s, n_kv, HEAD_DIM),
        ki_s.reshape(1, nb, t_s, D_IDX),
    )
```

```python
import functools
import math

import jax
import jax.numpy as jnp
from jax import lax
from jax.experimental import pallas as pl
from jax.experimental.pallas import tpu as pltpu

F32 = jnp.float32
BF16 = jnp.bfloat16
I32 = jnp.int32

LANES = 128
SUBLANES = 8
HEAD_DIM = 128
DH_A = HEAD_DIM // 2
D_IDX = 64
H_IDX = 4
TOPK_MAX = 256
ROPE_THETA = 500000.0
ROPE_FRAC = 4
RMS_EPS = 1e-6
NEG_BIG = -1e30
INT_MIN = -(2 ** 31)
NEG_INF_KEY = INT_MIN + 0x7FFFFF
VMEM_LIMIT = 56 * 1024 * 1024
ROW_TILE = 256
LOG2E = math.log2(math.e)
DIFF_TILE = 512
DIFF_HEADS_PER_STEP = 2
DSA_TILE = 256
IDX_TILE = 512
SAMPLE_PAGES_PER_STEP = 8
SCORE_PAGES_PER_STEP = 32


def _cparams(n_axes):
    return pltpu.CompilerParams(
        dimension_semantics=("arbitrary",) * n_axes, vmem_limit_bytes=VMEM_LIMIT)


def _dot(a, b):
    return jnp.dot(a, b, preferred_element_type=F32)


def _dot_nt(a, b):
    return lax.dot_general(a, b, (((1,), (1,)), ((), ())), preferred_element_type=F32)


def _rope_tables(pos, d):
    r = d // ROPE_FRAC
    half = r // 2
    inv = ROPE_THETA ** (-(2.0 / r) * jnp.arange(half, dtype=F32))
    ang = pos.astype(F32)[:, None] * inv[None, :]
    cos, sin = jnp.cos(ang), jnp.sin(ang)
    t = pos.shape[0]
    ones = jnp.ones((t, d - r), F32)
    zeros_h = jnp.zeros((t, half), F32)
    zeros_r = jnp.zeros((t, d - r), F32)
    c = jnp.concatenate([cos, cos, ones], axis=-1)
    sm = jnp.concatenate([-sin, zeros_h, zeros_r], axis=-1)
    sp = jnp.concatenate([zeros_h, sin, zeros_r], axis=-1)
    rep = LANES // d
    return tuple(jnp.tile(a, (1, rep)) for a in (c, sm, sp)), half


def _rope_chunk(z, c, sm, sp, half):
    return (z * c + pltpu.roll(z, LANES - half, axis=1) * sm
            + pltpu.roll(z, half, axis=1) * sp)


def _silu(z):
    return z * (1.0 / (1.0 + jnp.exp(-z)))


def _rms_rows(x, w):
    return x * lax.rsqrt(jnp.mean(x * x, axis=-1, keepdims=True) + RMS_EPS) * w


def _inproj_a_kernel(x_ref, nw_ref, w_ref, c_ref, sm_ref, sp_ref,
                     qa_ref, ka_ref, va_ref, ga_ref, *maybe_vt_ref, half, d_a):
    h = _rms_rows(x_ref[...], nw_ref[...]).astype(BF16)
    c, sm, sp = c_ref[...], sm_ref[...], sp_ref[...]
    n_chunks = d_a // LANES
    zq = _dot_nt(h, w_ref[0:d_a, :])
    for j in range(n_chunks):
        sl = slice(j * LANES, (j + 1) * LANES)
        qa_ref[:, sl] = (_rope_chunk(zq[:, sl], c, sm, sp, half) * (LOG2E / math.sqrt(DH_A))).astype(BF16)
    zk = _dot_nt(h, w_ref[d_a:2 * d_a, :])
    for j in range(n_chunks):
        sl = slice(j * LANES, (j + 1) * LANES)
        ka_ref[:, sl] = _rope_chunk(zk[:, sl], c, sm, sp, half)
    zv = _dot_nt(h, w_ref[2 * d_a:3 * d_a, :])
    va_ref[...] = zv
    if maybe_vt_ref:
        maybe_vt_ref[0][...] = zv.T.astype(BF16)
    ga_ref[...] = _silu(_dot_nt(h, w_ref[3 * d_a:4 * d_a, :]))


def _inproj_a(x2d, nw, w_a, tabs64, tm, d_a, emit_vt):
    m, d = x2d.shape
    (c, sm, sp), half = tabs64
    t_blocks = c.shape[0] // tm
    row = lambda i: (i, 0)
    tab = lambda i: (i % t_blocks, 0)
    fixed = lambda i: (0, 0)
    out_specs = [pl.BlockSpec((tm, d_a), row)] * 4
    out_shape = [
        jax.ShapeDtypeStruct((m, d_a), BF16),
        jax.ShapeDtypeStruct((m, d_a), F32),
        jax.ShapeDtypeStruct((m, d_a), F32),
        jax.ShapeDtypeStruct((m, d_a), F32),
    ]
    if emit_vt:
        out_specs.append(pl.BlockSpec((d_a, tm), lambda i: (0, i)))
        out_shape.append(jax.ShapeDtypeStruct((d_a, m), BF16))
    return pl.pallas_call(
        functools.partial(_inproj_a_kernel, half=half, d_a=d_a),
        grid=(m // tm,),
        in_specs=[
            pl.BlockSpec((tm, d), row),
            pl.BlockSpec((1, d), fixed),
            pl.BlockSpec((4 * d_a, d), fixed, pipeline_mode=pl.Buffered(1)),
            pl.BlockSpec((tm, LANES), tab),
            pl.BlockSpec((tm, LANES), tab),
            pl.BlockSpec((tm, LANES), tab),
        ],
        out_specs=out_specs,
        out_shape=out_shape,
        compiler_params=_cparams(1),
        name="inproj_a",
    )(x2d, nw, w_a, c, sm, sp)


def _inproj_b_kernel(x_ref, nw_ref, w_ref, wt_ref, c128_ref, sm128_ref, sp128_ref,
                     c64_ref, sm64_ref, sp64_ref,
                     qs_ref, ks_ref, vs_ref, gs_ref, qi_ref, kiw_ref, *maybe_vt_ref,
                     half128, half64, d_b, d_kv, d_qi):
    h = _rms_rows(x_ref[...], nw_ref[...]).astype(BF16)
    c128, sm128, sp128 = c128_ref[...], sm128_ref[...], sp128_ref[...]
    c64, sm64, sp64 = c64_ref[...], sm64_ref[...], sp64_ref[...]
    o = 0
    zq = _dot_nt(h, w_ref[o:o + d_b, :])
    for j in range(d_b // LANES):
        sl = slice(j * LANES, (j + 1) * LANES)
        qs_ref[:, sl] = (_rope_chunk(zq[:, sl], c128, sm128, sp128, half128)
                         * (LOG2E / math.sqrt(HEAD_DIM))).astype(BF16)
    o += d_b
    zk = _dot_nt(h, w_ref[o:o + d_kv, :])
    for j in range(d_kv // LANES):
        sl = slice(j * LANES, (j + 1) * LANES)
        ks_ref[:, sl] = _rope_chunk(zk[:, sl], c128, sm128, sp128, half128)
    o += d_kv
    zv = _dot_nt(h, w_ref[o:o + d_kv, :])
    vs_ref[...] = zv
    if maybe_vt_ref:
        maybe_vt_ref[0][...] = zv.T.astype(BF16)
    o += d_kv
    gs_ref[...] = _silu(_dot_nt(h, w_ref[o:o + d_b, :]))
    o += d_b
    zi = _dot_nt(h, w_ref[o:o + d_qi, :])
    for j in range(d_qi // LANES):
        sl = slice(j * LANES, (j + 1) * LANES)
        qi_ref[:, sl] = _rope_chunk(zi[:, sl], c64, sm64, sp64, half64).astype(BF16)
    zkw = _dot_nt(h, wt_ref[...])
    lane = lax.broadcasted_iota(I32, zkw.shape, 1)
    kiw_ref[...] = jnp.where(lane < D_IDX, _rope_chunk(zkw, c64, sm64, sp64, half64), zkw)


def _inproj_b(x2d, nw, w_b, w_tail, tabs128, tabs64, tm, d_b, d_kv, d_qi, emit_vt):
    m, d = x2d.shape
    (c128, sm128, sp128), half128 = tabs128
    (c64, sm64, sp64), half64 = tabs64
    t_blocks = c128.shape[0] // tm
    row = lambda i: (i, 0)
    tab = lambda i: (i % t_blocks, 0)
    fixed = lambda i: (0, 0)
    n_b = w_b.shape[0]
    out_specs = [
        pl.BlockSpec((tm, d_b), row),
        pl.BlockSpec((tm, d_kv), row),
        pl.BlockSpec((tm, d_kv), row),
        pl.BlockSpec((tm, d_b), row),
        pl.BlockSpec((tm, d_qi), row),
        pl.BlockSpec((tm, LANES), row),
    ]
    out_shape = [
        jax.ShapeDtypeStruct((m, d_b), BF16),
        jax.ShapeDtypeStruct((m, d_kv), F32),
        jax.ShapeDtypeStruct((m, d_kv), F32),
        jax.ShapeDtypeStruct((m, d_b), F32),
        jax.ShapeDtypeStruct((m, d_qi), BF16),
        jax.ShapeDtypeStruct((m, LANES), F32),
    ]
    if emit_vt:
        out_specs.append(pl.BlockSpec((d_kv, tm), lambda i: (0, i)))
        out_shape.append(jax.ShapeDtypeStruct((d_kv, m), BF16))
    return pl.pallas_call(
        functools.partial(_inproj_b_kernel, half128=half128, half64=half64,
                          d_b=d_b, d_kv=d_kv, d_qi=d_qi),
        grid=(m // tm,),
        in_specs=[
            pl.BlockSpec((tm, d), row),
            pl.BlockSpec((1, d), fixed),
            pl.BlockSpec((n_b, d), fixed, pipeline_mode=pl.Buffered(1)),
            pl.BlockSpec((LANES, d), fixed),
        ] + [pl.BlockSpec((tm, LANES), tab)] * 6,
        out_specs=out_specs,
        out_shape=out_shape,
        compiler_params=_cparams(1),
        name="inproj_b",
    )(x2d, nw, w_b, w_tail, c128, sm128, sp128, c64, sm64, sp64)


def _flash_init(m_ref, l_ref, acc_ref):
    m_ref[...] = jnp.full(m_ref.shape, NEG_BIG, F32)
    l_ref[...] = jnp.zeros(l_ref.shape, F32)
    acc_ref[...] = jnp.zeros(acc_ref.shape, F32)


def _flash_step(s, v_bf, m_ref, l_ref, acc_ref, axis=1):
    m_prev = m_ref[...]
    m_new = jnp.maximum(m_prev, jnp.max(s, axis=axis, keepdims=True))
    alpha = jnp.exp2(m_prev - m_new)
    p = jnp.exp2(s - m_new)
    l_ref[...] = alpha * l_ref[...] + jnp.sum(p, axis=axis, keepdims=True)
    pv = _dot(p.astype(BF16), v_bf) if axis == 1 else _dot(v_bf, p.astype(BF16))
    acc_ref[...] = alpha * acc_ref[...] + pv
    m_ref[...] = m_new


def _flash_step_multi(s_list, v_list, m_ref, l_ref, acc_ref):
    m_prev = m_ref[...]
    m_new = m_prev
    for s in s_list:
        m_new = jnp.maximum(m_new, jnp.max(s, axis=1, keepdims=True))
    alpha = jnp.exp2(m_prev - m_new)
    l_new = alpha * l_ref[...]
    acc = alpha * acc_ref[...]
    for s, v_bf in zip(s_list, v_list):
        p = jnp.exp2(s - m_new)
        l_new = l_new + jnp.sum(p, axis=1, keepdims=True)
        acc = acc + _dot(p.astype(BF16), v_bf)
    l_ref[...] = l_new
    acc_ref[...] = acc
    m_ref[...] = m_new


def _lambda_full(lamp, lam_init):
    s1 = jnp.sum(lamp[0:1, :] * lamp[1:2, :], axis=-1, keepdims=True)
    s2 = jnp.sum(lamp[2:3, :] * lamp[3:4, :], axis=-1, keepdims=True)
    return jnp.exp(s1) - jnp.exp(s2) + lam_init


def _diff_attn_kernel(q_ref, k_ref, vt_ref, g_ref, subw_ref, lamp_ref, o_ref,
                      kbf, sb, mb, lb, accb, *, tq, n_hs, lam_init):
    i = pl.program_id(2)

    @pl.when(i == 0)
    def _():
        kbf[...] = k_ref[0].astype(BF16)

    q = q_ref[0]
    lane = lax.broadcasted_iota(I32, (tq, HEAD_DIM), 1)
    qm = []
    for h in range(n_hs):
        qh = q[:, h * HEAD_DIM:(h + 1) * HEAD_DIM]
        zero = jnp.zeros_like(qh)
        qm.append((jnp.where(lane < DH_A, qh, zero), jnp.where(lane >= DH_A, qh, zero)))
        for mp in range(2):
            _flash_init(mb.at[h, mp], lb.at[h, mp], accb.at[h, mp])

    def scores(j, slot):
        off = pl.multiple_of(j * tq, tq)
        for h in range(n_hs):
            kb = kbf[pl.ds(off, tq), h * HEAD_DIM:(h + 1) * HEAD_DIM]
            for mp in range(2):
                sb[h, mp, slot] = _dot_nt(kb, qm[h][mp])

    def softmax_pv(j, slot, masked):
        off = pl.multiple_of(j * tq, tq)
        for h in range(n_hs):
            vtb = vt_ref[h * HEAD_DIM:(h + 1) * HEAD_DIM, pl.ds(off, tq)]
            for mp in range(2):
                s = sb[h, mp, slot]
                if masked:
                    kidx = lax.broadcasted_iota(I32, s.shape, 0)
                    qidx = lax.broadcasted_iota(I32, s.shape, 1)
                    s = jnp.where(kidx <= qidx, s, NEG_BIG)
                _flash_step(s, vtb, mb.at[h, mp], lb.at[h, mp], accb.at[h, mp], axis=0)

    scores(0, 0)

    def pair(jj, carry):
        j = 2 * jj
        scores(j + 1, 1)
        softmax_pv(j, 0, False)
        scores(j + 2, 0)
        softmax_pv(j + 1, 1, False)
        return carry

    lax.fori_loop(0, i // 2, pair, 0)

    @pl.when(i % 2 == 1)
    def _():
        scores(i, 1)
        softmax_pv(i - 1, 0, False)
        softmax_pv(i, 1, True)

    @pl.when(i % 2 == 0)
    def _():
        softmax_pv(i, 0, True)

    lam = _lambda_full(lamp_ref[...], lam_init)
    g = g_ref[0]
    for h in range(n_hs):
        sl = slice(h * HEAD_DIM, (h + 1) * HEAD_DIM)
        o_t = accb[h, 0] / lb[h, 0] - lam * (accb[h, 1] / lb[h, 1])
        o = _rms_rows(o_t.T, subw_ref[...]) * (1.0 - lam_init)
        o_ref[0, :, sl] = (o * g[:, sl]).astype(BF16)


def _diff_attn(qa, ka, va_t, sga, subw, lamp, lam_init, tq):
    b, t, d_a = qa.shape
    n_h = d_a // HEAD_DIM
    n_hs = DIFF_HEADS_PER_STEP if n_h % DIFF_HEADS_PER_STEP == 0 else 1
    hw = n_hs * HEAD_DIM
    qmap = lambda bi, h, i: (bi, i, h)
    kmap = lambda bi, h, i: (bi, 0, h)
    fixed = lambda bi, h, i: (0, 0)
    return pl.pallas_call(
        functools.partial(_diff_attn_kernel, tq=tq, n_hs=n_hs, lam_init=lam_init),
        grid=(b, n_h // n_hs, t // tq),
        in_specs=[
            pl.BlockSpec((1, tq, hw), qmap),
            pl.BlockSpec((1, t, hw), kmap),
            pl.BlockSpec((hw, t), lambda bi, h, i: (h, bi)),
            pl.BlockSpec((1, tq, hw), qmap),
            pl.BlockSpec((1, HEAD_DIM), fixed),
            pl.BlockSpec((4, DH_A), fixed),
        ],
        out_specs=pl.BlockSpec((1, tq, hw), qmap),
        out_shape=jax.ShapeDtypeStruct((b, t, d_a), BF16),
        scratch_shapes=[
            pltpu.VMEM((t, hw), BF16),
            pltpu.VMEM((n_hs, 2, 2, tq, tq), F32),
            pltpu.VMEM((n_hs, 2, 1, tq), F32), pltpu.VMEM((n_hs, 2, 1, tq), F32),
            pltpu.VMEM((n_hs, 2, HEAD_DIM, tq), F32),
        ],
        compiler_params=_cparams(3),
        name="diff_attn",
    )(qa, ka, va_t, sga, subw, lamp)


def _key_to_float(key):
    key = jnp.maximum(key, NEG_INF_KEY)
    return lax.bitcast_convert_type(key ^ ((key >> 31) & 0x7FFFFFFF), F32)


def _fold(m, axis):
    if axis == 1:
        n = m.shape[1] // LANES
        acc = m[:, 0:LANES]
        for j in range(1, n):
            acc = acc + m[:, j * LANES:(j + 1) * LANES]
        return acc
    return jnp.sum(m.reshape(m.shape[0] // SUBLANES, SUBLANES, m.shape[1]), axis=0)


def _count(key_ref, n_chunks, chunk, pred, axis):
    other = key_ref.shape[1 - axis]
    acc_shape = (other, LANES) if axis == 1 else (SUBLANES, other)

    def body(c, acc):
        off = pl.multiple_of(c * chunk, chunk)
        kc = key_ref[:, pl.ds(off, chunk)] if axis == 1 else key_ref[pl.ds(off, chunk), :]
        return acc + _fold(pred(kc, off, slice(None)), axis)

    acc = lax.fori_loop(0, n_chunks, body, jnp.zeros(acc_shape, F32))
    return jnp.sum(acc, axis=axis, keepdims=True)


def _valu_counter(key_ref, n_chunks, chunk, axis):
    return lambda preds: [_count(key_ref, n_chunks, chunk, p, axis) for p in preds]


def _prefix_key(pu):
    key = lax.shift_left(pu, 16) ^ INT_MIN
    return key | ((key >> 31) & 0xFFFF)


def _count_bf16(hb_ref, n_chunks, chunk, cand):
    tq = hb_ref.shape[1]
    pack = 2 * SUBLANES
    candb = jnp.broadcast_to(cand.astype(BF16), (pack, tq))
    one, zero = jnp.ones((), BF16), jnp.zeros((), BF16)

    def body(c, acc):
        off = pl.multiple_of(c * chunk, chunk)
        kc = hb_ref[pl.ds(off, chunk), :].reshape(chunk // pack, pack, tq)
        part = jnp.where(kc[0] >= candb, one, zero)
        for r in range(1, chunk // pack):
            part = part + jnp.where(kc[r] >= candb, one, zero)
        return acc + part.astype(F32)

    acc = lax.fori_loop(0, n_chunks, body, jnp.zeros((pack, tq), F32))
    return jnp.sum(acc, axis=0, keepdims=True)


def _select_topk(key_ref, p_ref, counter, k_sel, idx_bits, axis, coarse_count=None):
    k_f = float(k_sel)
    q_shape = p_ref.shape

    def count_ge(cand):
        (cnt,) = counter([lambda kc, off, cs: jnp.where(kc >= cand[:, cs], 1.0, 0.0)])
        return cnt

    if coarse_count is None:
        def bit_body(it, tu):
            cand_u = tu | lax.shift_left(jnp.int32(1), 31 - it)
            cnt = count_ge(_key_to_float(cand_u ^ INT_MIN))
            return jnp.where(cnt >= k_f, cand_u, tu)

        key = lax.fori_loop(0, 32, bit_body, jnp.zeros(q_shape, I32)) ^ INT_MIN
    else:
        def coarse_body(it, pu):
            cand_p = pu | lax.shift_left(jnp.int32(1), 15 - it)
            cnt = coarse_count(_key_to_float(_prefix_key(cand_p)))
            return jnp.where(cnt >= k_f, cand_p, pu)

        pu = lax.fori_loop(0, 16, coarse_body, jnp.zeros(q_shape, I32))
        base = _prefix_key(pu) - 2 ** 16

        def fine_body(it, x):
            cand_x = x | lax.shift_left(jnp.int32(1), 16 - it)
            cnt = count_ge(_key_to_float(base + cand_x))
            return jnp.where(cnt >= k_f, cand_x, x)

        key = base + lax.fori_loop(0, 17, fine_body, jnp.zeros(q_shape, I32))

    thr = _key_to_float(key)
    n_gt, n_eq = counter([lambda kc, off, cs: jnp.where(kc > thr[:, cs], 1.0, 0.0),
                          lambda kc, off, cs: jnp.where(kc == thr[:, cs], 1.0, 0.0)])
    need = k_f - n_gt
    p_ref[...] = jnp.full(q_shape, 2 ** 30, I32)
    excess = jnp.max(n_eq - need)

    @pl.when(excess > 0.0)
    def _():
        def idx_body(it, p):
            bit = lax.shift_left(jnp.int32(1), idx_bits - 1 - it)
            cand = p | bit

            def pred(kc, off, cs):
                idx = off + lax.broadcasted_iota(I32, kc.shape, axis)
                return jnp.where(kc == thr[:, cs], jnp.where(idx < cand[:, cs], 1.0, 0.0), 0.0)

            (cnt,) = counter([pred])
            return jnp.where(cnt < need, cand, p)

        p_ref[...] = lax.fori_loop(0, idx_bits, idx_body, jnp.zeros(q_shape, I32))

    return thr


def _selected_bias(kc, off, thr, p_max, axis):
    idx = off + lax.broadcasted_iota(I32, kc.shape, axis)
    tie = jnp.where(kc == thr, jnp.where(idx <= p_max, 1.0, 0.0), 0.0)
    sel = jnp.where(kc > thr, 1.0, tie)
    sel = jnp.where(kc == -jnp.inf, 0.0, sel)
    return jnp.where(sel > 0.5, 0.0, NEG_BIG)


def _index_select_kernel(qi_ref, ki_ref, wit_ref, bias_ref, key_ref, hb_ref, p_ref, *, tq, k_sel, idx_bits):
    i = pl.program_id(1)
    n_total = bias_ref.shape[1] // tq
    wit = wit_ref[0]

    def score_body(c, carry):
        off = pl.multiple_of(c * tq, tq)
        kc = ki_ref[0, pl.ds(off, tq), :]
        sc = jnp.zeros((tq, tq), F32)
        for h in range(H_IDX):
            sc = sc + jnp.maximum(_dot_nt(kc, qi_ref[0, h]), 0.0) * wit[h:h + 1, :]
        kidx = lax.broadcasted_iota(I32, sc.shape, 0) + off
        qidx = lax.broadcasted_iota(I32, sc.shape, 1) + i * tq
        sc = jnp.where(kidx <= qidx, sc, -jnp.inf)
        key_ref[pl.ds(off, tq), :] = sc
        hb_ref[pl.ds(off, tq), :] = sc.astype(BF16)
        return carry

    lax.fori_loop(0, i + 1, score_body, 0)

    thr = _select_topk(key_ref, p_ref, _valu_counter(key_ref, i + 1, tq, 0), k_sel, idx_bits, axis=0,
                       coarse_count=functools.partial(_count_bf16, hb_ref, i + 1, tq))
    p_max = p_ref[...]

    def out_body(c, carry):
        off = pl.multiple_of(c * tq, tq)
        kc = key_ref[pl.ds(off, tq), :]
        bias_ref[0, pl.ds(off, tq), :] = _selected_bias(kc, off, thr, p_max, 0).astype(BF16)
        return carry

    lax.fori_loop(0, i + 1, out_body, 0)

    def fill_body(c, carry):
        off = pl.multiple_of(c * tq, tq)
        bias_ref[0, pl.ds(off, tq), :] = jnp.full((tq, tq), NEG_BIG, BF16)
        return carry

    lax.fori_loop(i + 1, n_total, fill_body, 0)


def _index_select(qi4, ki_bf, wi_t, tq, k_sel):
    b, _, t, _ = qi4.shape
    idx_bits = max(1, (t - 1).bit_length())
    return pl.pallas_call(
        functools.partial(_index_select_kernel, tq=tq, k_sel=k_sel, idx_bits=idx_bits),
        grid=(b, t // tq),
        in_specs=[
            pl.BlockSpec((1, H_IDX, tq, D_IDX), lambda bi, i: (bi, 0, i, 0)),
            pl.BlockSpec((1, t, D_IDX), lambda bi, i: (bi, 0, 0)),
            pl.BlockSpec((1, H_IDX, tq), lambda bi, i: (bi, 0, i)),
        ],
        out_specs=pl.BlockSpec((1, t, tq), lambda bi, i: (bi, 0, i)),
        out_shape=jax.ShapeDtypeStruct((b, t, t), BF16),
        scratch_shapes=[pltpu.VMEM((t, tq), F32), pltpu.VMEM((t, tq), BF16), pltpu.VMEM((1, tq), I32)],
        compiler_params=_cparams(2),
        name="index_select",
    )(qi4, ki_bf, wi_t)


def _dsa_attn_kernel(q_ref, k_ref, vt_ref, bias_ref, g_ref, o_ref, kbf, sb, m, l, acc, *, tq, group):
    i = pl.program_id(2)

    @pl.when(i == 0)
    def _():
        kbf[...] = k_ref[0].astype(BF16)

    q = q_ref[0]
    q4 = jnp.concatenate([q[:, h * HEAD_DIM:(h + 1) * HEAD_DIM] for h in range(group)], axis=0)
    _flash_init(m, l, acc)

    def scores(j, slot):
        kb = kbf[pl.ds(pl.multiple_of(j * tq, tq), tq), :]
        sb[slot] = _dot_nt(kb, q4)

    def softmax_pv(j, slot):
        off = pl.multiple_of(j * tq, tq)
        vtb = vt_ref[:, pl.ds(off, tq)]
        bias = bias_ref[0, pl.ds(off, tq), :].astype(F32)
        s = sb[slot] + jnp.concatenate([bias] * group, axis=1)
        _flash_step(s, vtb, m, l, acc, axis=0)

    scores(0, 0)

    def pair(jj, carry):
        j = 2 * jj
        scores(j + 1, 1)
        softmax_pv(j, 0)
        scores(jnp.minimum(j + 2, i), 0)
        softmax_pv(j + 1, 1)
        return carry

    lax.fori_loop(0, (i + 1) // 2, pair, 0)

    @pl.when(i % 2 == 0)
    def _():
        softmax_pv(i, 0)
    o_t = acc[...] / l[...]
    g = g_ref[0]
    for h in range(group):
        sl = slice(h * HEAD_DIM, (h + 1) * HEAD_DIM)
        o_ref[0, :, sl] = (o_t[:, h * tq:(h + 1) * tq].T * g[:, sl]).astype(BF16)


def _dsa_attn(qs, ks, vs_t, bias_t, sgs, tq):
    b, t, d_b = qs.shape
    n_kv = ks.shape[2] // HEAD_DIM
    group = d_b // HEAD_DIM // n_kv
    gw = group * HEAD_DIM
    qmap = lambda bi, n, i: (bi, i, n)
    kmap = lambda bi, n, i: (bi, 0, n)
    return pl.pallas_call(
        functools.partial(_dsa_attn_kernel, tq=tq, group=group),
        grid=(b, n_kv, t // tq),
        in_specs=[
            pl.BlockSpec((1, tq, gw), qmap),
            pl.BlockSpec((1, t, HEAD_DIM), kmap),
            pl.BlockSpec((HEAD_DIM, t), lambda bi, n, i: (n, bi)),
            pl.BlockSpec((1, t, tq), lambda bi, n, i: (bi, 0, i)),
            pl.BlockSpec((1, tq, gw), qmap),
        ],
        out_specs=pl.BlockSpec((1, tq, gw), qmap),
        out_shape=jax.ShapeDtypeStruct((b, t, d_b), BF16),
        scratch_shapes=[
            pltpu.VMEM((t, HEAD_DIM), BF16),
            pltpu.VMEM((2, tq, group * tq), F32),
            pltpu.VMEM((1, group * tq), F32), pltpu.VMEM((1, group * tq), F32),
            pltpu.VMEM((HEAD_DIM, group * tq), F32),
        ],
        compiler_params=_cparams(3),
        name="dsa_attn",
    )(qs, ks, vs_t, bias_t, sgs)


def _out_kernel(x_ref, a1_ref, a2_ref, p_ref, wo_ref, pw_ref, wg_ref, bg_ref, we_ref, y_ref, *, d_a):
    o = _dot(a1_ref[...], wo_ref[0:d_a, :]) + _dot(a2_ref[...], wo_ref[d_a:, :])
    x1 = x_ref[...] + _rms_rows(o, pw_ref[...])
    z = _dot(x1.astype(BF16), wg_ref[...]) + bg_ref[...]
    gate = 1.0 / (1.0 + jnp.exp(-z))
    y_ref[...] = x1 + gate * _dot(p_ref[...].astype(BF16), we_ref[...])


def _out_proj(x2d, a1, a2, p2d, wo, pw, wg, bg, we, tm):
    m, d = x2d.shape
    d_a = a1.shape[1]
    d_mix = wo.shape[0]
    d_ple = p2d.shape[1]
    row = lambda i: (i, 0)
    fixed = lambda i: (0, 0)
    single = dict(pipeline_mode=pl.Buffered(1))
    return pl.pallas_call(
        functools.partial(_out_kernel, d_a=d_a),
        grid=(m // tm,),
        in_specs=[
            pl.BlockSpec((tm, d), row),
            pl.BlockSpec((tm, d_a), row),
            pl.BlockSpec((tm, d_mix - d_a), row),
            pl.BlockSpec((tm, d_ple), row),
            pl.BlockSpec((d_mix, d), fixed, **single),
            pl.BlockSpec((1, d), fixed),
            pl.BlockSpec((d, d), fixed, **single),
            pl.BlockSpec((1, d), fixed),
            pl.BlockSpec((d_ple, d), fixed, **single),
        ],
        out_specs=pl.BlockSpec((tm, d), row),
        out_shape=jax.ShapeDtypeStruct((m, d), F32),
        compiler_params=_cparams(1),
        name="out_proj",
    )(x2d, a1, a2, p2d, wo, pw, wg, bg, we)


def _sample_scores_kernel(pt_ref, qi_ref, wi_ref, *rest, n_group):
    del pt_ref
    page_refs, out_ref = rest[:n_group], rest[n_group]
    q4 = qi_ref[0]
    w = wi_ref[0]
    for r in range(n_group):
        kp_t = page_refs[r][0].astype(BF16)
        rel = jnp.maximum(_dot(q4, kp_t), 0.0)
        out_ref[0, r:r + 1, :] = jnp.sum(rel * w, axis=0, keepdims=True)


def _sample_scores(page_table_flat, qi4, wi3, cache_idx_t, n_pages, n_group):
    nb = qi4.shape[0]
    page = cache_idx_t.shape[2]

    def page_map(r):
        return lambda bi, g, pt: (pt[bi * n_pages + g * n_group + r], 0, 0)

    grid_spec = pltpu.PrefetchScalarGridSpec(
        num_scalar_prefetch=1,
        grid=(nb, n_pages // n_group),
        in_specs=[
            pl.BlockSpec((1, H_IDX, D_IDX), lambda bi, g, pt: (bi, 0, 0)),
            pl.BlockSpec((1, H_IDX, 1), lambda bi, g, pt: (bi, 0, 0)),
        ] + [pl.BlockSpec((1, D_IDX, page), page_map(r)) for r in range(n_group)],
        out_specs=pl.BlockSpec((1, n_group, page), lambda bi, g, pt: (bi, g, 0)),
    )
    return pl.pallas_call(
        functools.partial(_sample_scores_kernel, n_group=n_group),
        grid_spec=grid_spec,
        out_shape=jax.ShapeDtypeStruct((nb, n_pages, page), F32),
        compiler_params=_cparams(2),
        name="sample_scores",
    )(page_table_flat, qi4, wi3, *([cache_idx_t] * n_group))


def _sample_select_kernel(sc_ref, qi_ref, kit_ref, wi_ref, bias_ref, key_ref, p_ref, *, k_sel, idx_bits):
    rows, s_past = sc_ref.shape
    s_all = key_ref.shape[1]
    prod = qi_ref[...].astype(F32) * kit_ref[...].astype(F32)
    lane = lax.broadcasted_iota(I32, prod.shape, 1)
    wi = wi_ref[...]
    new = jnp.zeros((rows, 1), F32)
    for h in range(H_IDX):
        dot_h = jnp.sum(jnp.where(lane // D_IDX == h, prod, 0.0), axis=-1, keepdims=True)
        new = new + jnp.maximum(dot_h, 0.0) * wi[:, h:h + 1]
    key_ref[:, 0:s_past] = sc_ref[...]
    tail_lane = lax.broadcasted_iota(I32, (rows, s_all - s_past), 1)
    key_ref[:, s_past:s_all] = jnp.where(tail_lane == 0, new, -jnp.inf)
    thr = _select_topk(key_ref, p_ref, _valu_counter(key_ref, 1, s_all, 1), k_sel, idx_bits, axis=1)
    bias_ref[...] = _selected_bias(key_ref[...], 0, thr, p_ref[...], 1)


def _sample_select(scores2d, qi, ki_tiled, wi, k_sel):
    rows, s_past = scores2d.shape
    s_all = s_past + LANES
    idx_bits = max(1, (s_all - 1).bit_length())
    return pl.pallas_call(
        functools.partial(_sample_select_kernel, k_sel=k_sel, idx_bits=idx_bits),
        out_shape=jax.ShapeDtypeStruct((rows, s_all), F32),
        scratch_shapes=[pltpu.VMEM((rows, s_all), F32), pltpu.VMEM((rows, 1), I32)],
        compiler_params=pltpu.CompilerParams(vmem_limit_bytes=VMEM_LIMIT),
        name="sample_select",
    )(scores2d, qi, ki_tiled, wi)


def _head_rows(row, n_rows, rows_per_head):
    return jnp.concatenate(
        [row[:, (c // rows_per_head) * HEAD_DIM:(c // rows_per_head + 1) * HEAD_DIM]
         for c in range(n_rows)], axis=0)


def _sample_attn_kernel(pt_ref, qa_ref, qs_ref, kan_ref, van_ref, ksn_ref, vsn_ref, bnew_ref,
                        ga_ref, gs_ref, subw_ref, lamp_ref, *rest, n_pg, n_ha, n_hb, group, lam_init):
    del pt_ref
    kd_refs, vd_refs = rest[0:n_pg], rest[n_pg:2 * n_pg]
    ks_refs, vs_refs = rest[2 * n_pg:3 * n_pg], rest[3 * n_pg:4 * n_pg]
    bias_ref, oa_ref, ob_ref, qd, qsb, md, ld, accd, ms, ls, accs = rest[4 * n_pg:]
    p = pl.program_id(1)
    n_p = pl.num_programs(1)
    n_kv = n_hb // group

    @pl.when(p == 0)
    def _():
        qd_f = _head_rows(qa_ref[0].astype(F32), 2 * n_ha, 2)
        rd = lax.broadcasted_iota(I32, qd_f.shape, 0)
        ln = lax.broadcasted_iota(I32, qd_f.shape, 1)
        qd_f = jnp.where(ln // DH_A == rd % 2, qd_f, 0.0)
        qd[...] = qd_f.astype(BF16)
        qs_f = _head_rows(qs_ref[0].astype(F32), n_hb, 1)
        qsb[...] = qs_f.astype(BF16)
        kan = _head_rows(kan_ref[0].astype(BF16).astype(F32), 2 * n_ha, 2)
        van = _head_rows(van_ref[0].astype(BF16).astype(F32), 2 * n_ha, 2)
        md[...] = jnp.sum(qd_f * kan, axis=-1, keepdims=True)
        ld[...] = jnp.ones(ld.shape, F32)
        accd[...] = van
        ksn = _head_rows(ksn_ref[0].astype(BF16).astype(F32), n_hb, group)
        vsn = _head_rows(vsn_ref[0].astype(BF16).astype(F32), n_hb, group)
        ms[...] = jnp.sum(qs_f * ksn, axis=-1, keepdims=True) + bnew_ref[0]
        ls[...] = jnp.ones(ls.shape, F32)
        accs[...] = vsn

    sd_list, vd_list, ss_list, vs_list = [], [], [], []
    for r in range(n_pg):
        kd = kd_refs[r][0].astype(BF16)
        sd = _dot_nt(qd[...], kd)
        rq = lax.broadcasted_iota(I32, sd.shape, 0)
        ck = lax.broadcasted_iota(I32, sd.shape, 1)
        sd_list.append(jnp.where(ck % n_ha == rq // 2, sd, NEG_BIG))
        vd_list.append(vd_refs[r][0].astype(BF16))
        ksp = ks_refs[r][0].astype(BF16)
        ss = _dot_nt(qsb[...], ksp) + bias_ref[0, r]
        rq = lax.broadcasted_iota(I32, ss.shape, 0)
        ck = lax.broadcasted_iota(I32, ss.shape, 1)
        ss_list.append(jnp.where(ck % n_kv == rq // group, ss, NEG_BIG))
        vs_list.append(vs_refs[r][0].astype(BF16))
    _flash_step_multi(sd_list, vd_list, md, ld, accd)
    _flash_step_multi(ss_list, vs_list, ms, ls, accs)

    @pl.when(p == n_p - 1)
    def _():
        lam = _lambda_full(lamp_ref[...], lam_init)
        od = accd[...] / ld[...]
        ga = ga_ref[0]
        subw = subw_ref[...]
        for h in range(n_ha):
            sl = slice(h * HEAD_DIM, (h + 1) * HEAD_DIM)
            o = od[2 * h:2 * h + 1, :] - lam * od[2 * h + 1:2 * h + 2, :]
            o = _rms_rows(o, subw) * (1.0 - lam_init)
            oa_ref[0, :, sl] = (o * ga[:, sl]).astype(BF16)
        os_ = accs[...] / ls[...]
        gs = gs_ref[0]
        for h in range(n_hb):
            sl = slice(h * HEAD_DIM, (h + 1) * HEAD_DIM)
            ob_ref[0, :, sl] = (os_[h:h + 1, :] * gs[:, sl]).astype(BF16)


def _sample_attn(page_table_flat, qa, qs, ka_new, va_new, ks_new, vs_new, bias_new, sga, sgs,
                 subw, lamp, cdk, cdv, csk, csv, bias_pages, n_pages, lam_init):
    nb, _, d_a = qa.shape
    d_b = qs.shape[2]
    d_kv = ks_new.shape[2]
    n_ha = d_a // HEAD_DIM
    n_hb = d_b // HEAD_DIM
    n_kv = d_kv // HEAD_DIM
    group = n_hb // n_kv
    rows_d = cdk.shape[1]
    rows_s = csk.shape[1]
    per_b = lambda bi, p, pt: (bi, 0, 0)
    fixed = lambda bi, p, pt: (0, 0)
    n_pg = SAMPLE_PAGES_PER_STEP if n_pages % SAMPLE_PAGES_PER_STEP == 0 else 1

    def paged(r):
        return lambda bi, p, pt: (pt[bi * n_pages + p * n_pg + r], 0, 0)

    grid_spec = pltpu.PrefetchScalarGridSpec(
        num_scalar_prefetch=1,
        grid=(nb, n_pages // n_pg),
        in_specs=[
            pl.BlockSpec((1, 1, d_a), per_b),
            pl.BlockSpec((1, 1, d_b), per_b),
            pl.BlockSpec((1, 1, d_a), per_b),
            pl.BlockSpec((1, 1, d_a), per_b),
            pl.BlockSpec((1, 1, d_kv), per_b),
            pl.BlockSpec((1, 1, d_kv), per_b),
            pl.BlockSpec((1, 1, 1), per_b),
            pl.BlockSpec((1, 1, d_a), per_b),
            pl.BlockSpec((1, 1, d_b), per_b),
            pl.BlockSpec((1, HEAD_DIM), fixed),
            pl.BlockSpec((4, DH_A), fixed),
        ] + [pl.BlockSpec((1, rows_d, HEAD_DIM), paged(r)) for r in range(n_pg)] * 2
        + [pl.BlockSpec((1, rows_s, HEAD_DIM), paged(r)) for r in range(n_pg)] * 2
        + [pl.BlockSpec((1, n_pg, 1, rows_s), lambda bi, p, pt: (bi, p, 0, 0))],
        out_specs=[pl.BlockSpec((1, 1, d_a), per_b), pl.BlockSpec((1, 1, d_b), per_b)],
        scratch_shapes=[
            pltpu.VMEM((2 * n_ha, HEAD_DIM), BF16), pltpu.VMEM((n_hb, HEAD_DIM), BF16),
            pltpu.VMEM((2 * n_ha, 1), F32), pltpu.VMEM((2 * n_ha, 1), F32),
            pltpu.VMEM((2 * n_ha, HEAD_DIM), F32),
            pltpu.VMEM((n_hb, 1), F32), pltpu.VMEM((n_hb, 1), F32), pltpu.VMEM((n_hb, HEAD_DIM), F32),
        ],
    )
    return pl.pallas_call(
        functools.partial(_sample_attn_kernel, n_pg=n_pg, n_ha=n_ha, n_hb=n_hb, group=group,
                          lam_init=lam_init),
        grid_spec=grid_spec,
        out_shape=[jax.ShapeDtypeStruct((nb, 1, d_a), BF16), jax.ShapeDtypeStruct((nb, 1, d_b), BF16)],
        compiler_params=_cparams(2),
        name="sample_attn",
    )(page_table_flat, qa, qs, ka_new, va_new, ks_new, vs_new, bias_new, sga, sgs, subw, lamp,
      *([cdk] * n_pg + [cdv] * n_pg + [csk] * n_pg + [csv] * n_pg), bias_pages)


def _row_tile(m, pref):
    return pref if m % pref == 0 else m


def _inproj_all(x2d, pos, wts, tm, emit_vt):
    tabs64 = _rope_tables(pos, DH_A)
    tabs128 = _rope_tables(pos, HEAD_DIM)
    d_a, d_b, d_kv, d_qi = wts["d_a"], wts["d_b"], wts["d_kv"], wts["d_qi"]
    outs_a = _inproj_a(x2d, wts["pre_w"], wts["w_a"], tabs64, tm, d_a, emit_vt)
    outs_b = _inproj_b(x2d, wts["pre_w"], wts["w_b"], wts["w_tail"], tabs128, tabs64, tm,
                       d_b, d_kv, d_qi, emit_vt)
    return outs_a, outs_b


def kernel(x_prompt, x_sample, p_prompt, p_sample, cache_diff_k, cache_diff_v, cache_dsa_k, cache_dsa_v, cache_idx_k, page_table, pre_norm_w, post_norm_w, w_in, lam_q1, lam_k1, lam_q2, lam_k2, diff_norm_w, w_out, w_ple_gate, b_ple_gate, w_ple_proj):
    depth = w_in.shape[0]
    assert depth == 1, "single-layer stack only"
    bsz, t_p, d = x_prompt.shape
    nb, t_s, _ = x_sample.shape
    assert t_s == 1, "one new token per sample sequence"
    n_pages = page_table.shape[1]
    n_pool = cache_diff_k.shape[1]
    page = cache_diff_k.shape[2]
    n_ha = cache_diff_k.shape[3]
    n_kv = cache_dsa_k.shape[3]
    d_a = n_ha * HEAD_DIM
    d_kv = n_kv * HEAD_DIM
    d_b = d - d_a
    d_qi = H_IDX * D_IDX
    past_len = n_pages * page
    lam_init = 0.8 - 0.6 * math.exp(-0.3 * 0)
    n_in = w_in.shape[2]

    n_main = 4 * d_a + 2 * d_b + 2 * d_kv + d_qi
    assert n_in - n_main == D_IDX + H_IDX
    w_a = jnp.swapaxes(w_in[0, :, :4 * d_a], 0, 1).astype(BF16)
    w_b = jnp.swapaxes(w_in[0, :, 4 * d_a:n_main], 0, 1).astype(BF16)
    w_tail = jnp.pad(jnp.swapaxes(w_in[0, :, n_main:], 0, 1),
                     ((0, LANES - (n_in - n_main)), (0, 0))).astype(BF16)
    wts = dict(pre_w=pre_norm_w[0][None, :], w_a=w_a, w_b=w_b, w_tail=w_tail,
               d_a=d_a, d_b=d_b, d_kv=d_kv, d_qi=d_qi)
    wo = w_out[0].astype(BF16)
    wg = w_ple_gate[0].astype(BF16)
    we = w_ple_proj[0].astype(BF16)
    pw = post_norm_w[0][None, :]
    bg = b_ple_gate[0][None, :]
    subw = diff_norm_w[0][None, :]
    lamp = jnp.stack([lam_q1[0], lam_k1[0], lam_q2[0], lam_k2[0]], axis=0)

    m_p = bsz * t_p
    tm = _row_tile(t_p, ROW_TILE)
    tq_diff = _row_tile(t_p, DIFF_TILE)
    tq_dsa = _row_tile(t_p, DSA_TILE)
    tq_idx = _row_tile(t_p, IDX_TILE)
    xp2 = x_prompt.reshape(m_p, d)
    pos_p = jnp.arange(t_p, dtype=I32)
    (qa, ka, va, sga, va_t), (qs, ks, vs, sgs, qi, kiw, vs_t) = _inproj_all(xp2, pos_p, wts, tm, True)
    r3 = lambda a: a.reshape(bsz, t_p, a.shape[-1])
    a1 = _diff_attn(r3(qa), r3(ka), va_t, r3(sga), subw, lamp, lam_init, tq_diff)
    ki = kiw[:, :D_IDX]
    wi = kiw[:, D_IDX:D_IDX + H_IDX]
    qi4 = qi.reshape(bsz, t_p, H_IDX, D_IDX).transpose(0, 2, 1, 3)
    wi_t = wi.reshape(bsz, t_p, H_IDX).transpose(0, 2, 1)
    k_sel_p = min(TOPK_MAX, t_p // 4)
    bias_t = _index_select(qi4, ki.astype(BF16).reshape(bsz, t_p, D_IDX), wi_t, tq_idx, k_sel_p)
    a2 = _dsa_attn(r3(qs), r3(ks), vs_t, bias_t, r3(sgs), tq_dsa)
    y_p = _out_proj(xp2, a1.reshape(m_p, d_a), a2.reshape(m_p, d_b), p_prompt[0].reshape(m_p, -1),
                    wo, pw, wg, bg, we, tm)

    xs2 = x_sample.reshape(nb, d)
    pos_s = jnp.full((nb,), past_len, dtype=I32)
    (qa_s, ka_s, va_s, sga_s), (qs_s, ks_s, vs_s, sgs_s, qi_s, kiw_s) = _inproj_all(xs2, pos_s, wts, nb, False)
    ki_s = kiw_s[:, :D_IDX]
    wi_s = kiw_s[:, D_IDX:D_IDX + H_IDX]
    pt_flat = page_table.reshape(-1)
    n_group = SCORE_PAGES_PER_STEP if n_pages % SCORE_PAGES_PER_STEP == 0 else 1
    cache_idx_t = jnp.swapaxes(cache_idx_k.reshape(n_pool, page, D_IDX), 1, 2)
    scores = _sample_scores(pt_flat, qi_s.reshape(nb, H_IDX, D_IDX), wi_s.reshape(nb, H_IDX, 1),
                            cache_idx_t, n_pages, n_group)
    k_sel_s = min(TOPK_MAX, (past_len + t_s) // 4)
    bias_s = _sample_select(scores.reshape(nb, past_len), qi_s,
                            jnp.tile(ki_s.astype(BF16), (1, H_IDX)), wi_s, k_sel_s)
    bias_pages = jnp.repeat(bias_s[:, :past_len], n_kv, axis=1).reshape(nb, n_pages, 1, page * n_kv)
    bias_new = bias_s[:, past_len:past_len + 1].reshape(nb, 1, 1)
    e3 = lambda a: a.reshape(nb, 1, a.shape[-1])
    a1_s, a2_s = _sample_attn(
        pt_flat, e3(qa_s), e3(qs_s), e3(ka_s), e3(va_s), e3(ks_s), e3(vs_s), bias_new,
        e3(sga_s), e3(sgs_s), subw, lamp,
        cache_diff_k.reshape(n_pool, page * n_ha, HEAD_DIM), cache_diff_v.reshape(n_pool, page * n_ha, HEAD_DIM),
        cache_dsa_k.reshape(n_pool, page * n_kv, HEAD_DIM), cache_dsa_v.reshape(n_pool, page * n_kv, HEAD_DIM),
        bias_pages, n_pages, lam_init)
    y_s = _out_proj(xs2, a1_s.reshape(nb, d_a), a2_s.reshape(nb, d_b), p_sample[0].reshape(nb, -1),
                    wo, pw, wg, bg, we, nb)

    return (
        y_p.reshape(bsz, t_p, d), y_s.reshape(nb, t_s, d),
        ka.reshape(1, bsz, t_p, n_ha, HEAD_DIM), va.reshape(1, bsz, t_p, n_ha, HEAD_DIM),
        ks.reshape(1, bsz, t_p, n_kv, HEAD_DIM), vs.reshape(1, bsz, t_p, n_kv, HEAD_DIM),
        ki.reshape(1, bsz, t_p, D_IDX),
        ka_s.reshape(1, nb, t_s, n_ha, HEAD_DIM), va_s.reshape(1, nb, t_s, n_ha, HEAD_DIM),
        ks_s.reshape(1, nb, t_s, n_kv, HEAD_DIM), vs_s.reshape(1, nb, t_s, n_kv, HEAD_DIM),
        ki_s.reshape(1, nb, t_s, D_IDX),
    )
```

```python
import functools
import math

import jax
import jax.numpy as jnp
from jax import lax
from jax.experimental import pallas as pl
from jax.experimental.pallas import tpu as pltpu

F32 = jnp.float32
BF16 = jnp.bfloat16
I32 = jnp.int32

LANES = 128
SUBLANES = 8
HEAD_DIM = 128
DH_A = HEAD_DIM // 2
D_IDX = 64
H_IDX = 4
TOPK_MAX = 256
ROPE_THETA = 500000.0
ROPE_FRAC = 4
RMS_EPS = 1e-6
NEG_BIG = -1e30
INT_MIN = -(2 ** 31)
NEG_INF_KEY = INT_MIN + 0x7FFFFF
VMEM_LIMIT = 56 * 1024 * 1024
ROW_TILE = 256
LOG2E = math.log2(math.e)
DIFF_TILE = 512
DIFF_HEADS_PER_STEP = 2
DSA_TILE = 256
DSA_KV_HEADS_PER_STEP = 1
IDX_TILE = 512
SAMPLE_PAGES_PER_STEP = 8
SCORE_PAGES_PER_STEP = 32


def _cparams(n_axes):
    return pltpu.CompilerParams(
        dimension_semantics=("arbitrary",) * n_axes, vmem_limit_bytes=VMEM_LIMIT)


def _dot(a, b):
    return jnp.dot(a, b, preferred_element_type=F32)


def _dot_nt(a, b):
    return lax.dot_general(a, b, (((1,), (1,)), ((), ())), preferred_element_type=F32)


def _rope_tables(pos, d):
    r = d // ROPE_FRAC
    half = r // 2
    inv = ROPE_THETA ** (-(2.0 / r) * jnp.arange(half, dtype=F32))
    ang = pos.astype(F32)[:, None] * inv[None, :]
    cos, sin = jnp.cos(ang), jnp.sin(ang)
    t = pos.shape[0]
    ones = jnp.ones((t, d - r), F32)
    zeros_h = jnp.zeros((t, half), F32)
    zeros_r = jnp.zeros((t, d - r), F32)
    c = jnp.concatenate([cos, cos, ones], axis=-1)
    sm = jnp.concatenate([-sin, zeros_h, zeros_r], axis=-1)
    sp = jnp.concatenate([zeros_h, sin, zeros_r], axis=-1)
    rep = LANES // d
    return tuple(jnp.tile(a, (1, rep)) for a in (c, sm, sp)), half


def _rope_chunk(z, c, sm, sp, half):
    return (z * c + pltpu.roll(z, LANES - half, axis=1) * sm
            + pltpu.roll(z, half, axis=1) * sp)


def _silu(z):
    return z * (1.0 / (1.0 + jnp.exp(-z)))


def _rms_rows(x, w):
    return x * lax.rsqrt(jnp.mean(x * x, axis=-1, keepdims=True) + RMS_EPS) * w


def _inproj_a_kernel(x_ref, nw_ref, w_ref, c_ref, sm_ref, sp_ref,
                     qa_ref, ka_ref, va_ref, ga_ref, *maybe_vt_ref, half, d_a):
    h = _rms_rows(x_ref[...], nw_ref[...]).astype(BF16)
    c, sm, sp = c_ref[...], sm_ref[...], sp_ref[...]
    n_chunks = d_a // LANES
    zq = _dot_nt(h, w_ref[0:d_a, :])
    for j in range(n_chunks):
        sl = slice(j * LANES, (j + 1) * LANES)
        qa_ref[:, sl] = (_rope_chunk(zq[:, sl], c, sm, sp, half) * (LOG2E / math.sqrt(DH_A))).astype(BF16)
    zk = _dot_nt(h, w_ref[d_a:2 * d_a, :])
    for j in range(n_chunks):
        sl = slice(j * LANES, (j + 1) * LANES)
        ka_ref[:, sl] = _rope_chunk(zk[:, sl], c, sm, sp, half)
    zv = _dot_nt(h, w_ref[2 * d_a:3 * d_a, :])
    va_ref[...] = zv
    if maybe_vt_ref:
        maybe_vt_ref[0][...] = zv.T.astype(BF16)
    ga_ref[...] = _silu(_dot_nt(h, w_ref[3 * d_a:4 * d_a, :]))


def _inproj_a(x2d, nw, w_a, tabs64, tm, d_a, emit_vt):
    m, d = x2d.shape
    (c, sm, sp), half = tabs64
    t_blocks = c.shape[0] // tm
    row = lambda i: (i, 0)
    tab = lambda i: (i % t_blocks, 0)
    fixed = lambda i: (0, 0)
    out_specs = [pl.BlockSpec((tm, d_a), row)] * 4
    out_shape = [
        jax.ShapeDtypeStruct((m, d_a), BF16),
        jax.ShapeDtypeStruct((m, d_a), F32),
        jax.ShapeDtypeStruct((m, d_a), F32),
        jax.ShapeDtypeStruct((m, d_a), F32),
    ]
    if emit_vt:
        out_specs.append(pl.BlockSpec((d_a, tm), lambda i: (0, i)))
        out_shape.append(jax.ShapeDtypeStruct((d_a, m), BF16))
    return pl.pallas_call(
        functools.partial(_inproj_a_kernel, half=half, d_a=d_a),
        grid=(m // tm,),
        in_specs=[
            pl.BlockSpec((tm, d), row),
            pl.BlockSpec((1, d), fixed),
            pl.BlockSpec((4 * d_a, d), fixed, pipeline_mode=pl.Buffered(1)),
            pl.BlockSpec((tm, LANES), tab),
            pl.BlockSpec((tm, LANES), tab),
            pl.BlockSpec((tm, LANES), tab),
        ],
        out_specs=out_specs,
        out_shape=out_shape,
        compiler_params=_cparams(1),
        name="inproj_a",
    )(x2d, nw, w_a, c, sm, sp)


def _inproj_b_kernel(x_ref, nw_ref, w_ref, wt_ref, c128_ref, sm128_ref, sp128_ref,
                     c64_ref, sm64_ref, sp64_ref,
                     qs_ref, ks_ref, vs_ref, gs_ref, qi_ref, kiw_ref, *maybe_vt_ref,
                     half128, half64, d_b, d_kv, d_qi):
    h = _rms_rows(x_ref[...], nw_ref[...]).astype(BF16)
    c128, sm128, sp128 = c128_ref[...], sm128_ref[...], sp128_ref[...]
    c64, sm64, sp64 = c64_ref[...], sm64_ref[...], sp64_ref[...]
    o = 0
    zq = _dot_nt(h, w_ref[o:o + d_b, :])
    for j in range(d_b // LANES):
        sl = slice(j * LANES, (j + 1) * LANES)
        qs_ref[:, sl] = (_rope_chunk(zq[:, sl], c128, sm128, sp128, half128)
                         * (LOG2E / math.sqrt(HEAD_DIM))).astype(BF16)
    o += d_b
    zk = _dot_nt(h, w_ref[o:o + d_kv, :])
    for j in range(d_kv // LANES):
        sl = slice(j * LANES, (j + 1) * LANES)
        ks_ref[:, sl] = _rope_chunk(zk[:, sl], c128, sm128, sp128, half128)
    o += d_kv
    zv = _dot_nt(h, w_ref[o:o + d_kv, :])
    vs_ref[...] = zv
    if maybe_vt_ref:
        maybe_vt_ref[0][...] = zv.T.astype(BF16)
    o += d_kv
    gs_ref[...] = _silu(_dot_nt(h, w_ref[o:o + d_b, :]))
    o += d_b
    zi = _dot_nt(h, w_ref[o:o + d_qi, :])
    for j in range(d_qi // LANES):
        sl = slice(j * LANES, (j + 1) * LANES)
        qi_ref[:, sl] = _rope_chunk(zi[:, sl], c64, sm64, sp64, half64).astype(BF16)
    zkw = _dot_nt(h, wt_ref[...])
    lane = lax.broadcasted_iota(I32, zkw.shape, 1)
    kiw_ref[...] = jnp.where(lane < D_IDX, _rope_chunk(zkw, c64, sm64, sp64, half64), zkw)


def _inproj_b(x2d, nw, w_b, w_tail, tabs128, tabs64, tm, d_b, d_kv, d_qi, emit_vt):
    m, d = x2d.shape
    (c128, sm128, sp128), half128 = tabs128
    (c64, sm64, sp64), half64 = tabs64
    t_blocks = c128.shape[0] // tm
    row = lambda i: (i, 0)
    tab = lambda i: (i % t_blocks, 0)
    fixed = lambda i: (0, 0)
    n_b = w_b.shape[0]
    out_specs = [
        pl.BlockSpec((tm, d_b), row),
        pl.BlockSpec((tm, d_kv), row),
        pl.BlockSpec((tm, d_kv), row),
        pl.BlockSpec((tm, d_b), row),
        pl.BlockSpec((tm, d_qi), row),
        pl.BlockSpec((tm, LANES), row),
    ]
    out_shape = [
        jax.ShapeDtypeStruct((m, d_b), BF16),
        jax.ShapeDtypeStruct((m, d_kv), F32),
        jax.ShapeDtypeStruct((m, d_kv), F32),
        jax.ShapeDtypeStruct((m, d_b), F32),
        jax.ShapeDtypeStruct((m, d_qi), BF16),
        jax.ShapeDtypeStruct((m, LANES), F32),
    ]
    if emit_vt:
        out_specs.append(pl.BlockSpec((d_kv, tm), lambda i: (0, i)))
        out_shape.append(jax.ShapeDtypeStruct((d_kv, m), BF16))
    return pl.pallas_call(
        functools.partial(_inproj_b_kernel, half128=half128, half64=half64,
                          d_b=d_b, d_kv=d_kv, d_qi=d_qi),
        grid=(m // tm,),
        in_specs=[
            pl.BlockSpec((tm, d), row),
            pl.BlockSpec((1, d), fixed),
            pl.BlockSpec((n_b, d), fixed, pipeline_mode=pl.Buffered(1)),
            pl.BlockSpec((LANES, d), fixed),
        ] + [pl.BlockSpec((tm, LANES), tab)] * 6,
        out_specs=out_specs,
        out_shape=out_shape,
        compiler_params=_cparams(1),
        name="inproj_b",
    )(x2d, nw, w_b, w_tail, c128, sm128, sp128, c64, sm64, sp64)


def _flash_init(m_ref, l_ref, acc_ref):
    m_ref[...] = jnp.full(m_ref.shape, NEG_BIG, F32)
    l_ref[...] = jnp.zeros(l_ref.shape, F32)
    acc_ref[...] = jnp.zeros(acc_ref.shape, F32)


def _flash_step(s, v_bf, m_ref, l_ref, acc_ref, axis=1):
    m_prev = m_ref[...]
    m_new = jnp.maximum(m_prev, jnp.max(s, axis=axis, keepdims=True))
    alpha = jnp.exp2(m_prev - m_new)
    p = jnp.exp2(s - m_new)
    l_ref[...] = alpha * l_ref[...] + jnp.sum(p, axis=axis, keepdims=True)
    pv = _dot(p.astype(BF16), v_bf) if axis == 1 else _dot(v_bf, p.astype(BF16))
    acc_ref[...] = alpha * acc_ref[...] + pv
    m_ref[...] = m_new


def _flash_step_multi(s_list, v_list, m_ref, l_ref, acc_ref):
    m_prev = m_ref[...]
    m_new = m_prev
    for s in s_list:
        m_new = jnp.maximum(m_new, jnp.max(s, axis=1, keepdims=True))
    alpha = jnp.exp2(m_prev - m_new)
    l_new = alpha * l_ref[...]
    acc = alpha * acc_ref[...]
    for s, v_bf in zip(s_list, v_list):
        p = jnp.exp2(s - m_new)
        l_new = l_new + jnp.sum(p, axis=1, keepdims=True)
        acc = acc + _dot(p.astype(BF16), v_bf)
    l_ref[...] = l_new
    acc_ref[...] = acc
    m_ref[...] = m_new


def _lambda_full(lamp, lam_init):
    s1 = jnp.sum(lamp[0:1, :] * lamp[1:2, :], axis=-1, keepdims=True)
    s2 = jnp.sum(lamp[2:3, :] * lamp[3:4, :], axis=-1, keepdims=True)
    return jnp.exp(s1) - jnp.exp(s2) + lam_init


def _diff_attn_kernel(q_ref, k_ref, vt_ref, g_ref, subw_ref, lamp_ref, o_ref,
                      kbf, sb, mb, lb, accb, *, tq, n_hs, lam_init):
    i = pl.program_id(2)

    @pl.when(i == 0)
    def _():
        kbf[...] = k_ref[0].astype(BF16)

    q = q_ref[0]
    lane = lax.broadcasted_iota(I32, (tq, HEAD_DIM), 1)
    qm = []
    for h in range(n_hs):
        qh = q[:, h * HEAD_DIM:(h + 1) * HEAD_DIM]
        zero = jnp.zeros_like(qh)
        qm.append((jnp.where(lane < DH_A, qh, zero), jnp.where(lane >= DH_A, qh, zero)))
        for mp in range(2):
            _flash_init(mb.at[h, mp], lb.at[h, mp], accb.at[h, mp])

    def scores(j, slot):
        off = pl.multiple_of(j * tq, tq)
        for h in range(n_hs):
            kb = kbf[pl.ds(off, tq), h * HEAD_DIM:(h + 1) * HEAD_DIM]
            for mp in range(2):
                sb[h, mp, slot] = _dot_nt(kb, qm[h][mp])

    def softmax_pv(j, slot, masked):
        off = pl.multiple_of(j * tq, tq)
        for h in range(n_hs):
            vtb = vt_ref[h * HEAD_DIM:(h + 1) * HEAD_DIM, pl.ds(off, tq)]
            for mp in range(2):
                s = sb[h, mp, slot]
                if masked:
                    kidx = lax.broadcasted_iota(I32, s.shape, 0)
                    qidx = lax.broadcasted_iota(I32, s.shape, 1)
                    s = jnp.where(kidx <= qidx, s, NEG_BIG)
                _flash_step(s, vtb, mb.at[h, mp], lb.at[h, mp], accb.at[h, mp], axis=0)

    scores(0, 0)

    def pair(jj, carry):
        j = 2 * jj
        scores(j + 1, 1)
        softmax_pv(j, 0, False)
        scores(j + 2, 0)
        softmax_pv(j + 1, 1, False)
        return carry

    lax.fori_loop(0, i // 2, pair, 0)

    @pl.when(i % 2 == 1)
    def _():
        scores(i, 1)
        softmax_pv(i - 1, 0, False)
        softmax_pv(i, 1, True)

    @pl.when(i % 2 == 0)
    def _():
        softmax_pv(i, 0, True)

    lam = _lambda_full(lamp_ref[...], lam_init)
    g = g_ref[0]
    for h in range(n_hs):
        sl = slice(h * HEAD_DIM, (h + 1) * HEAD_DIM)
        o_t = accb[h, 0] / lb[h, 0] - lam * (accb[h, 1] / lb[h, 1])
        o = _rms_rows(o_t.T, subw_ref[...]) * (1.0 - lam_init)
        o_ref[0, :, sl] = (o * g[:, sl]).astype(BF16)


def _diff_attn(qa, ka, va_t, sga, subw, lamp, lam_init, tq):
    b, t, d_a = qa.shape
    n_h = d_a // HEAD_DIM
    n_hs = DIFF_HEADS_PER_STEP if n_h % DIFF_HEADS_PER_STEP == 0 else 1
    hw = n_hs * HEAD_DIM
    qmap = lambda bi, h, i: (bi, i, h)
    kmap = lambda bi, h, i: (bi, 0, h)
    fixed = lambda bi, h, i: (0, 0)
    return pl.pallas_call(
        functools.partial(_diff_attn_kernel, tq=tq, n_hs=n_hs, lam_init=lam_init),
        grid=(b, n_h // n_hs, t // tq),
        in_specs=[
            pl.BlockSpec((1, tq, hw), qmap),
            pl.BlockSpec((1, t, hw), kmap),
            pl.BlockSpec((hw, t), lambda bi, h, i: (h, bi)),
            pl.BlockSpec((1, tq, hw), qmap),
            pl.BlockSpec((1, HEAD_DIM), fixed),
            pl.BlockSpec((4, DH_A), fixed),
        ],
        out_specs=pl.BlockSpec((1, tq, hw), qmap),
        out_shape=jax.ShapeDtypeStruct((b, t, d_a), BF16),
        scratch_shapes=[
            pltpu.VMEM((t, hw), BF16),
            pltpu.VMEM((n_hs, 2, 2, tq, tq), F32),
            pltpu.VMEM((n_hs, 2, 1, tq), F32), pltpu.VMEM((n_hs, 2, 1, tq), F32),
            pltpu.VMEM((n_hs, 2, HEAD_DIM, tq), F32),
        ],
        compiler_params=_cparams(3),
        name="diff_attn",
    )(qa, ka, va_t, sga, subw, lamp)


def _key_to_float(key):
    key = jnp.maximum(key, NEG_INF_KEY)
    return lax.bitcast_convert_type(key ^ ((key >> 31) & 0x7FFFFFFF), F32)


def _fold(m, axis):
    if axis == 1:
        n = m.shape[1] // LANES
        acc = m[:, 0:LANES]
        for j in range(1, n):
            acc = acc + m[:, j * LANES:(j + 1) * LANES]
        return acc
    return jnp.sum(m.reshape(m.shape[0] // SUBLANES, SUBLANES, m.shape[1]), axis=0)


def _count(key_ref, n_chunks, chunk, pred, axis):
    other = key_ref.shape[1 - axis]
    acc_shape = (other, LANES) if axis == 1 else (SUBLANES, other)

    def body(c, acc):
        off = pl.multiple_of(c * chunk, chunk)
        kc = key_ref[:, pl.ds(off, chunk)] if axis == 1 else key_ref[pl.ds(off, chunk), :]
        return acc + _fold(pred(kc, off, slice(None)), axis)

    acc = lax.fori_loop(0, n_chunks, body, jnp.zeros(acc_shape, F32))
    return jnp.sum(acc, axis=axis, keepdims=True)


def _valu_counter(key_ref, n_chunks, chunk, axis):
    return lambda preds: [_count(key_ref, n_chunks, chunk, p, axis) for p in preds]


def _prefix_key(pu):
    key = lax.shift_left(pu, 16) ^ INT_MIN
    return key | ((key >> 31) & 0xFFFF)


def _count_bf16(hb_ref, n_chunks, chunk, cand):
    tq = hb_ref.shape[1]
    pack = 2 * SUBLANES
    candb = jnp.broadcast_to(cand.astype(BF16), (pack, tq))
    one, zero = jnp.ones((), BF16), jnp.zeros((), BF16)

    def body(c, acc):
        off = pl.multiple_of(c * chunk, chunk)
        kc = hb_ref[pl.ds(off, chunk), :].reshape(chunk // pack, pack, tq)
        part = jnp.where(kc[0] >= candb, one, zero)
        for r in range(1, chunk // pack):
            part = part + jnp.where(kc[r] >= candb, one, zero)
        return acc + part.astype(F32)

    acc = lax.fori_loop(0, n_chunks, body, jnp.zeros((pack, tq), F32))
    return jnp.sum(acc, axis=0, keepdims=True)


def _select_topk(key_ref, p_ref, counter, k_sel, idx_bits, axis, coarse_count=None):
    k_f = float(k_sel)
    q_shape = p_ref.shape

    def count_ge(cand):
        (cnt,) = counter([lambda kc, off, cs: jnp.where(kc >= cand[:, cs], 1.0, 0.0)])
        return cnt

    if coarse_count is None:
        def bit_body(it, carry):
            tu, cnt_tu = carry
            cand_u = tu | lax.shift_left(jnp.int32(1), 31 - it)
            cnt = count_ge(_key_to_float(cand_u ^ INT_MIN))
            ok = cnt >= k_f
            return jnp.where(ok, cand_u, tu), jnp.where(ok, cnt, cnt_tu)

        lowest = jnp.full(q_shape, -jnp.inf, F32)
        tu, cnt_ge = lax.fori_loop(0, 32, bit_body, (jnp.zeros(q_shape, I32), count_ge(lowest)))
        key = tu ^ INT_MIN
    else:
        def coarse_body(it, pu):
            cand_p = pu | lax.shift_left(jnp.int32(1), 15 - it)
            cnt = coarse_count(_key_to_float(_prefix_key(cand_p)))
            return jnp.where(cnt >= k_f, cand_p, pu)

        pu = lax.fori_loop(0, 16, coarse_body, jnp.zeros(q_shape, I32))
        base = _prefix_key(pu) - 2 ** 16

        def fine_body(it, carry):
            x, cnt_x = carry
            cand_x = x | lax.shift_left(jnp.int32(1), 16 - it)
            cnt = count_ge(_key_to_float(base + cand_x))
            ok = cnt >= k_f
            return jnp.where(ok, cand_x, x), jnp.where(ok, cnt, cnt_x)

        x, cnt_ge = lax.fori_loop(0, 17, fine_body,
                                  (jnp.zeros(q_shape, I32), count_ge(_key_to_float(base))))
        key = base + x

    thr = _key_to_float(key)
    p_ref[...] = jnp.full(q_shape, 2 ** 30, I32)
    has_ties = jnp.max(cnt_ge) > k_f

    @pl.when(has_ties)
    def _():
        (n_gt,) = counter([lambda kc, off, cs: jnp.where(kc > thr[:, cs], 1.0, 0.0)])
        need = k_f - n_gt

        def idx_body(it, p):
            bit = lax.shift_left(jnp.int32(1), idx_bits - 1 - it)
            cand = p | bit

            def pred(kc, off, cs):
                idx = off + lax.broadcasted_iota(I32, kc.shape, axis)
                return jnp.where(kc == thr[:, cs], jnp.where(idx < cand[:, cs], 1.0, 0.0), 0.0)

            (cnt,) = counter([pred])
            return jnp.where(cnt < need, cand, p)

        p_ref[...] = lax.fori_loop(0, idx_bits, idx_body, jnp.zeros(q_shape, I32))

    return thr, has_ties


def _selected_bias(kc, off, thr, p_max, axis):
    idx = off + lax.broadcasted_iota(I32, kc.shape, axis)
    tie = jnp.where(kc == thr, jnp.where(idx <= p_max, 1.0, 0.0), 0.0)
    sel = jnp.where(kc > thr, 1.0, tie)
    sel = jnp.where(kc == -jnp.inf, 0.0, sel)
    return jnp.where(sel > 0.5, 0.0, NEG_BIG)


def _index_select_kernel(qi_ref, ki_ref, wit_ref, bias_ref, key_ref, hb_ref, p_ref, *, tq, k_sel, idx_bits):
    i = pl.program_id(1)
    n_total = bias_ref.shape[1] // tq
    wit = wit_ref[0]

    def score_body(c, carry):
        off = pl.multiple_of(c * tq, tq)
        kc = ki_ref[0, pl.ds(off, tq), :]
        sc = jnp.zeros((tq, tq), F32)
        for h in range(H_IDX):
            sc = sc + jnp.maximum(_dot_nt(kc, qi_ref[0, h]), 0.0) * wit[h:h + 1, :]
        kidx = lax.broadcasted_iota(I32, sc.shape, 0) + off
        qidx = lax.broadcasted_iota(I32, sc.shape, 1) + i * tq
        sc = jnp.where(kidx <= qidx, sc, -jnp.inf)
        key_ref[pl.ds(off, tq), :] = sc
        hb_ref[pl.ds(off, tq), :] = sc.astype(BF16)
        return carry

    lax.fori_loop(0, i + 1, score_body, 0)

    thr, has_ties = _select_topk(key_ref, p_ref, _valu_counter(key_ref, i + 1, tq, 0), k_sel, idx_bits,
                                 axis=0, coarse_count=functools.partial(_count_bf16, hb_ref, i + 1, tq))
    p_max = p_ref[...]

    def out_body(c, carry):
        off = pl.multiple_of(c * tq, tq)
        kc = key_ref[pl.ds(off, tq), :]
        bias_ref[0, pl.ds(off, tq), :] = _selected_bias(kc, off, thr, p_max, 0).astype(BF16)
        return carry

    def out_body_no_ties(c, carry):
        off = pl.multiple_of(c * tq, tq)
        kc = key_ref[pl.ds(off, tq), :]
        keep = jnp.where(kc == -jnp.inf, NEG_BIG, 0.0)
        bias_ref[0, pl.ds(off, tq), :] = jnp.where(kc >= thr, keep, NEG_BIG).astype(BF16)
        return carry

    @pl.when(has_ties)
    def _():
        lax.fori_loop(0, i + 1, out_body, 0)

    @pl.when(jnp.logical_not(has_ties))
    def _():
        lax.fori_loop(0, i + 1, out_body_no_ties, 0)

    def fill_body(c, carry):
        off = pl.multiple_of(c * tq, tq)
        bias_ref[0, pl.ds(off, tq), :] = jnp.full((tq, tq), NEG_BIG, BF16)
        return carry

    lax.fori_loop(i + 1, n_total, fill_body, 0)


def _index_select(qi4, ki_bf, wi_t, tq, k_sel):
    b, _, t, _ = qi4.shape
    idx_bits = max(1, (t - 1).bit_length())
    return pl.pallas_call(
        functools.partial(_index_select_kernel, tq=tq, k_sel=k_sel, idx_bits=idx_bits),
        grid=(b, t // tq),
        in_specs=[
            pl.BlockSpec((1, H_IDX, tq, D_IDX), lambda bi, i: (bi, 0, i, 0)),
            pl.BlockSpec((1, t, D_IDX), lambda bi, i: (bi, 0, 0)),
            pl.BlockSpec((1, H_IDX, tq), lambda bi, i: (bi, 0, i)),
        ],
        out_specs=pl.BlockSpec((1, t, tq), lambda bi, i: (bi, 0, i)),
        out_shape=jax.ShapeDtypeStruct((b, t, t), BF16),
        scratch_shapes=[pltpu.VMEM((t, tq), F32), pltpu.VMEM((t, tq), BF16), pltpu.VMEM((1, tq), I32)],
        compiler_params=_cparams(2),
        name="index_select",
    )(qi4, ki_bf, wi_t)


def _dsa_attn_kernel(q_ref, k_ref, vt_ref, bias_ref, g_ref, o_ref, kbf, sb, mb, lb, accb,
                     *, tq, group, n_kvs):
    i = pl.program_id(2)

    @pl.when(i == 0)
    def _():
        kbf[...] = k_ref[0].astype(BF16)

    q = q_ref[0]
    q4 = []
    for n in range(n_kvs):
        q4.append(jnp.concatenate(
            [q[:, (n * group + h) * HEAD_DIM:(n * group + h + 1) * HEAD_DIM] for h in range(group)],
            axis=0))
        _flash_init(mb.at[n], lb.at[n], accb.at[n])

    def scores(j, slot):
        off = pl.multiple_of(j * tq, tq)
        for n in range(n_kvs):
            kb = kbf[pl.ds(off, tq), n * HEAD_DIM:(n + 1) * HEAD_DIM]
            sb[n, slot] = _dot_nt(kb, q4[n])

    def softmax_pv(j, slot):
        off = pl.multiple_of(j * tq, tq)
        bias = bias_ref[0, pl.ds(off, tq), :].astype(F32)
        bias_g = jnp.concatenate([bias] * group, axis=1)
        for n in range(n_kvs):
            vtb = vt_ref[n * HEAD_DIM:(n + 1) * HEAD_DIM, pl.ds(off, tq)]
            _flash_step(sb[n, slot] + bias_g, vtb, mb.at[n], lb.at[n], accb.at[n], axis=0)

    scores(0, 0)

    def pair(jj, carry):
        j = 2 * jj
        scores(j + 1, 1)
        softmax_pv(j, 0)
        scores(jnp.minimum(j + 2, i), 0)
        softmax_pv(j + 1, 1)
        return carry

    lax.fori_loop(0, (i + 1) // 2, pair, 0)

    @pl.when(i % 2 == 0)
    def _():
        softmax_pv(i, 0)

    g = g_ref[0]
    for n in range(n_kvs):
        o_t = accb[n] / lb[n]
        for h in range(group):
            sl = slice((n * group + h) * HEAD_DIM, (n * group + h + 1) * HEAD_DIM)
            o_ref[0, :, sl] = (o_t[:, h * tq:(h + 1) * tq].T * g[:, sl]).astype(BF16)


def _dsa_attn(qs, ks, vs_t, bias_t, sgs, tq):
    b, t, d_b = qs.shape
    n_kv = ks.shape[2] // HEAD_DIM
    group = d_b // HEAD_DIM // n_kv
    n_kvs = DSA_KV_HEADS_PER_STEP if n_kv % DSA_KV_HEADS_PER_STEP == 0 else 1
    gw = n_kvs * group * HEAD_DIM
    kw = n_kvs * HEAD_DIM
    qmap = lambda bi, n, i: (bi, i, n)
    kmap = lambda bi, n, i: (bi, 0, n)
    return pl.pallas_call(
        functools.partial(_dsa_attn_kernel, tq=tq, group=group, n_kvs=n_kvs),
        grid=(b, n_kv // n_kvs, t // tq),
        in_specs=[
            pl.BlockSpec((1, tq, gw), qmap),
            pl.BlockSpec((1, t, kw), kmap),
            pl.BlockSpec((kw, t), lambda bi, n, i: (n, bi)),
            pl.BlockSpec((1, t, tq), lambda bi, n, i: (bi, 0, i)),
            pl.BlockSpec((1, tq, gw), qmap),
        ],
        out_specs=pl.BlockSpec((1, tq, gw), qmap),
        out_shape=jax.ShapeDtypeStruct((b, t, d_b), BF16),
        scratch_shapes=[
            pltpu.VMEM((t, kw), BF16),
            pltpu.VMEM((n_kvs, 2, tq, group * tq), F32),
            pltpu.VMEM((n_kvs, 1, group * tq), F32), pltpu.VMEM((n_kvs, 1, group * tq), F32),
            pltpu.VMEM((n_kvs, HEAD_DIM, group * tq), F32),
        ],
        compiler_params=_cparams(3),
        name="dsa_attn",
    )(qs, ks, vs_t, bias_t, sgs)


def _out_kernel(x_ref, a1_ref, a2_ref, p_ref, wo_ref, pw_ref, wg_ref, bg_ref, we_ref, y_ref, *, d_a):
    o = _dot(a1_ref[...], wo_ref[0:d_a, :]) + _dot(a2_ref[...], wo_ref[d_a:, :])
    x1 = x_ref[...] + _rms_rows(o, pw_ref[...])
    z = _dot(x1.astype(BF16), wg_ref[...]) + bg_ref[...]
    gate = 1.0 / (1.0 + jnp.exp(-z))
    y_ref[...] = x1 + gate * _dot(p_ref[...].astype(BF16), we_ref[...])


def _out_proj(x2d, a1, a2, p2d, wo, pw, wg, bg, we, tm):
    m, d = x2d.shape
    d_a = a1.shape[1]
    d_mix = wo.shape[0]
    d_ple = p2d.shape[1]
    row = lambda i: (i, 0)
    fixed = lambda i: (0, 0)
    single = dict(pipeline_mode=pl.Buffered(1))
    return pl.pallas_call(
        functools.partial(_out_kernel, d_a=d_a),
        grid=(m // tm,),
        in_specs=[
            pl.BlockSpec((tm, d), row),
            pl.BlockSpec((tm, d_a), row),
            pl.BlockSpec((tm, d_mix - d_a), row),
            pl.BlockSpec((tm, d_ple), row),
            pl.BlockSpec((d_mix, d), fixed, **single),
            pl.BlockSpec((1, d), fixed),
            pl.BlockSpec((d, d), fixed, **single),
            pl.BlockSpec((1, d), fixed),
            pl.BlockSpec((d_ple, d), fixed, **single),
        ],
        out_specs=pl.BlockSpec((tm, d), row),
        out_shape=jax.ShapeDtypeStruct((m, d), F32),
        compiler_params=_cparams(1),
        name="out_proj",
    )(x2d, a1, a2, p2d, wo, pw, wg, bg, we)


def _sample_scores_kernel(pt_ref, qi_ref, wi_ref, *rest, n_group):
    del pt_ref
    page_refs, out_ref = rest[:n_group], rest[n_group]
    q4 = qi_ref[0]
    w = wi_ref[0]
    for r in range(n_group):
        kp_t = page_refs[r][0].astype(BF16)
        rel = jnp.maximum(_dot(q4, kp_t), 0.0)
        out_ref[0, r:r + 1, :] = jnp.sum(rel * w, axis=0, keepdims=True)


def _sample_scores(page_table_flat, qi4, wi3, cache_idx_t, n_pages, n_group):
    nb = qi4.shape[0]
    page = cache_idx_t.shape[2]

    def page_map(r):
        return lambda bi, g, pt: (pt[bi * n_pages + g * n_group + r], 0, 0)

    grid_spec = pltpu.PrefetchScalarGridSpec(
        num_scalar_prefetch=1,
        grid=(nb, n_pages // n_group),
        in_specs=[
            pl.BlockSpec((1, H_IDX, D_IDX), lambda bi, g, pt: (bi, 0, 0)),
            pl.BlockSpec((1, H_IDX, 1), lambda bi, g, pt: (bi, 0, 0)),
        ] + [pl.BlockSpec((1, D_IDX, page), page_map(r)) for r in range(n_group)],
        out_specs=pl.BlockSpec((1, n_group, page), lambda bi, g, pt: (bi, g, 0)),
    )
    return pl.pallas_call(
        functools.partial(_sample_scores_kernel, n_group=n_group),
        grid_spec=grid_spec,
        out_shape=jax.ShapeDtypeStruct((nb, n_pages, page), F32),
        compiler_params=_cparams(2),
        name="sample_scores",
    )(page_table_flat, qi4, wi3, *([cache_idx_t] * n_group))


def _sample_select_kernel(sc_ref, qi_ref, kit_ref, wi_ref, bias_ref, key_ref, p_ref, *, k_sel, idx_bits):
    rows, s_past = sc_ref.shape
    s_all = key_ref.shape[1]
    prod = qi_ref[...].astype(F32) * kit_ref[...].astype(F32)
    lane = lax.broadcasted_iota(I32, prod.shape, 1)
    wi = wi_ref[...]
    new = jnp.zeros((rows, 1), F32)
    for h in range(H_IDX):
        dot_h = jnp.sum(jnp.where(lane // D_IDX == h, prod, 0.0), axis=-1, keepdims=True)
        new = new + jnp.maximum(dot_h, 0.0) * wi[:, h:h + 1]
    key_ref[:, 0:s_past] = sc_ref[...]
    tail_lane = lax.broadcasted_iota(I32, (rows, s_all - s_past), 1)
    key_ref[:, s_past:s_all] = jnp.where(tail_lane == 0, new, -jnp.inf)
    thr, _ = _select_topk(key_ref, p_ref, _valu_counter(key_ref, 1, s_all, 1), k_sel, idx_bits, axis=1)
    bias_ref[...] = _selected_bias(key_ref[...], 0, thr, p_ref[...], 1)


def _sample_select(scores2d, qi, ki_tiled, wi, k_sel):
    rows, s_past = scores2d.shape
    s_all = s_past + LANES
    idx_bits = max(1, (s_all - 1).bit_length())
    return pl.pallas_call(
        functools.partial(_sample_select_kernel, k_sel=k_sel, idx_bits=idx_bits),
        out_shape=jax.ShapeDtypeStruct((rows, s_all), F32),
        scratch_shapes=[pltpu.VMEM((rows, s_all), F32), pltpu.VMEM((rows, 1), I32)],
        compiler_params=pltpu.CompilerParams(vmem_limit_bytes=VMEM_LIMIT),
        name="sample_select",
    )(scores2d, qi, ki_tiled, wi)


def _head_rows(row, n_rows, rows_per_head):
    return jnp.concatenate(
        [row[:, (c // rows_per_head) * HEAD_DIM:(c // rows_per_head + 1) * HEAD_DIM]
         for c in range(n_rows)], axis=0)


def _sample_attn_kernel(pt_ref, qa_ref, qs_ref, kan_ref, van_ref, ksn_ref, vsn_ref, bnew_ref,
                        ga_ref, gs_ref, subw_ref, lamp_ref, *rest, n_pg, n_ha, n_hb, group, lam_init):
    del pt_ref
    kd_refs, vd_refs = rest[0:n_pg], rest[n_pg:2 * n_pg]
    ks_refs, vs_refs = rest[2 * n_pg:3 * n_pg], rest[3 * n_pg:4 * n_pg]
    bias_ref, oa_ref, ob_ref, qd, qsb, md, ld, accd, ms, ls, accs = rest[4 * n_pg:]
    p = pl.program_id(1)
    n_p = pl.num_programs(1)
    n_kv = n_hb // group

    @pl.when(p == 0)
    def _():
        qd_f = _head_rows(qa_ref[0].astype(F32), 2 * n_ha, 2)
        rd = lax.broadcasted_iota(I32, qd_f.shape, 0)
        ln = lax.broadcasted_iota(I32, qd_f.shape, 1)
        qd_f = jnp.where(ln // DH_A == rd % 2, qd_f, 0.0)
        qd[...] = qd_f.astype(BF16)
        qs_f = _head_rows(qs_ref[0].astype(F32), n_hb, 1)
        qsb[...] = qs_f.astype(BF16)
        kan = _head_rows(kan_ref[0].astype(BF16).astype(F32), 2 * n_ha, 2)
        van = _head_rows(van_ref[0].astype(BF16).astype(F32), 2 * n_ha, 2)
        md[...] = jnp.sum(qd_f * kan, axis=-1, keepdims=True)
        ld[...] = jnp.ones(ld.shape, F32)
        accd[...] = van
        ksn = _head_rows(ksn_ref[0].astype(BF16).astype(F32), n_hb, group)
        vsn = _head_rows(vsn_ref[0].astype(BF16).astype(F32), n_hb, group)
        ms[...] = jnp.sum(qs_f * ksn, axis=-1, keepdims=True) + bnew_ref[0]
        ls[...] = jnp.ones(ls.shape, F32)
        accs[...] = vsn

    sd_list, vd_list, ss_list, vs_list = [], [], [], []
    for r in range(n_pg):
        kd = kd_refs[r][0].astype(BF16)
        sd = _dot_nt(qd[...], kd)
        rq = lax.broadcasted_iota(I32, sd.shape, 0)
        ck = lax.broadcasted_iota(I32, sd.shape, 1)
        sd_list.append(jnp.where(ck % n_ha == rq // 2, sd, NEG_BIG))
        vd_list.append(vd_refs[r][0].astype(BF16))
        ksp = ks_refs[r][0].astype(BF16)
        ss = _dot_nt(qsb[...], ksp) + bias_ref[0, r]
        rq = lax.broadcasted_iota(I32, ss.shape, 0)
        ck = lax.broadcasted_iota(I32, ss.shape, 1)
        ss_list.append(jnp.where(ck % n_kv == rq // group, ss, NEG_BIG))
        vs_list.append(vs_refs[r][0].astype(BF16))
    _flash_step_multi(sd_list, vd_list, md, ld, accd)
    _flash_step_multi(ss_list, vs_list, ms, ls, accs)

    @pl.when(p == n_p - 1)
    def _():
        lam = _lambda_full(lamp_ref[...], lam_init)
        od = accd[...] / ld[...]
        ga = ga_ref[0]
        subw = subw_ref[...]
        for h in range(n_ha):
            sl = slice(h * HEAD_DIM, (h + 1) * HEAD_DIM)
            o = od[2 * h:2 * h + 1, :] - lam * od[2 * h + 1:2 * h + 2, :]
            o = _rms_rows(o, subw) * (1.0 - lam_init)
            oa_ref[0, :, sl] = (o * ga[:, sl]).astype(BF16)
        os_ = accs[...] / ls[...]
        gs = gs_ref[0]
        for h in range(n_hb):
            sl = slice(h * HEAD_DIM, (h + 1) * HEAD_DIM)
            ob_ref[0, :, sl] = (os_[h:h + 1, :] * gs[:, sl]).astype(BF16)


def _sample_attn(page_table_flat, qa, qs, ka_new, va_new, ks_new, vs_new, bias_new, sga, sgs,
                 subw, lamp, cdk, cdv, csk, csv, bias_pages, n_pages, lam_init):
    nb, _, d_a = qa.shape
    d_b = qs.shape[2]
    d_kv = ks_new.shape[2]
    n_ha = d_a // HEAD_DIM
    n_hb = d_b // HEAD_DIM
    n_kv = d_kv // HEAD_DIM
    group = n_hb // n_kv
    rows_d = cdk.shape[1]
    rows_s = csk.shape[1]
    per_b = lambda bi, p, pt: (bi, 0, 0)
    fixed = lambda bi, p, pt: (0, 0)
    n_pg = SAMPLE_PAGES_PER_STEP if n_pages % SAMPLE_PAGES_PER_STEP == 0 else 1

    def paged(r):
        return lambda bi, p, pt: (pt[bi * n_pages + p * n_pg + r], 0, 0)

    grid_spec = pltpu.PrefetchScalarGridSpec(
        num_scalar_prefetch=1,
        grid=(nb, n_pages // n_pg),
        in_specs=[
            pl.BlockSpec((1, 1, d_a), per_b),
            pl.BlockSpec((1, 1, d_b), per_b),
            pl.BlockSpec((1, 1, d_a), per_b),
            pl.BlockSpec((1, 1, d_a), per_b),
            pl.BlockSpec((1, 1, d_kv), per_b),
            pl.BlockSpec((1, 1, d_kv), per_b),
            pl.BlockSpec((1, 1, 1), per_b),
            pl.BlockSpec((1, 1, d_a), per_b),
            pl.BlockSpec((1, 1, d_b), per_b),
            pl.BlockSpec((1, HEAD_DIM), fixed),
            pl.BlockSpec((4, DH_A), fixed),
        ] + [pl.BlockSpec((1, rows_d, HEAD_DIM), paged(r)) for r in range(n_pg)] * 2
        + [pl.BlockSpec((1, rows_s, HEAD_DIM), paged(r)) for r in range(n_pg)] * 2
        + [pl.BlockSpec((1, n_pg, 1, rows_s), lambda bi, p, pt: (bi, p, 0, 0))],
        out_specs=[pl.BlockSpec((1, 1, d_a), per_b), pl.BlockSpec((1, 1, d_b), per_b)],
        scratch_shapes=[
            pltpu.VMEM((2 * n_ha, HEAD_DIM), BF16), pltpu.VMEM((n_hb, HEAD_DIM), BF16),
            pltpu.VMEM((2 * n_ha, 1), F32), pltpu.VMEM((2 * n_ha, 1), F32),
            pltpu.VMEM((2 * n_ha, HEAD_DIM), F32),
            pltpu.VMEM((n_hb, 1), F32), pltpu.VMEM((n_hb, 1), F32), pltpu.VMEM((n_hb, HEAD_DIM), F32),
        ],
    )
    return pl.pallas_call(
        functools.partial(_sample_attn_kernel, n_pg=n_pg, n_ha=n_ha, n_hb=n_hb, group=group,
                          lam_init=lam_init),
        grid_spec=grid_spec,
        out_shape=[jax.ShapeDtypeStruct((nb, 1, d_a), BF16), jax.ShapeDtypeStruct((nb, 1, d_b), BF16)],
        compiler_params=_cparams(2),
        name="sample_attn",
    )(page_table_flat, qa, qs, ka_new, va_new, ks_new, vs_new, bias_new, sga, sgs, subw, lamp,
      *([cdk] * n_pg + [cdv] * n_pg + [csk] * n_pg + [csv] * n_pg), bias_pages)


def _row_tile(m, pref):
    return pref if m % pref == 0 else m


def _inproj_all(x2d, pos, wts, tm, emit_vt):
    tabs64 = _rope_tables(pos, DH_A)
    tabs128 = _rope_tables(pos, HEAD_DIM)
    d_a, d_b, d_kv, d_qi = wts["d_a"], wts["d_b"], wts["d_kv"], wts["d_qi"]
    outs_a = _inproj_a(x2d, wts["pre_w"], wts["w_a"], tabs64, tm, d_a, emit_vt)
    outs_b = _inproj_b(x2d, wts["pre_w"], wts["w_b"], wts["w_tail"], tabs128, tabs64, tm,
                       d_b, d_kv, d_qi, emit_vt)
    return outs_a, outs_b


def kernel(x_prompt, x_sample, p_prompt, p_sample, cache_diff_k, cache_diff_v, cache_dsa_k, cache_dsa_v, cache_idx_k, page_table, pre_norm_w, post_norm_w, w_in, lam_q1, lam_k1, lam_q2, lam_k2, diff_norm_w, w_out, w_ple_gate, b_ple_gate, w_ple_proj):
    depth = w_in.shape[0]
    assert depth == 1, "single-layer stack only"
    bsz, t_p, d = x_prompt.shape
    nb, t_s, _ = x_sample.shape
    assert t_s == 1, "one new token per sample sequence"
    n_pages = page_table.shape[1]
    n_pool = cache_diff_k.shape[1]
    page = cache_diff_k.shape[2]
    n_ha = cache_diff_k.shape[3]
    n_kv = cache_dsa_k.shape[3]
    d_a = n_ha * HEAD_DIM
    d_kv = n_kv * HEAD_DIM
    d_b = d - d_a
    d_qi = H_IDX * D_IDX
    past_len = n_pages * page
    lam_init = 0.8 - 0.6 * math.exp(-0.3 * 0)
    n_in = w_in.shape[2]

    n_main = 4 * d_a + 2 * d_b + 2 * d_kv + d_qi
    assert n_in - n_main == D_IDX + H_IDX
    w_a = jnp.swapaxes(w_in[0, :, :4 * d_a], 0, 1).astype(BF16)
    w_b = jnp.swapaxes(w_in[0, :, 4 * d_a:n_main], 0, 1).astype(BF16)
    w_tail = jnp.pad(jnp.swapaxes(w_in[0, :, n_main:], 0, 1),
                     ((0, LANES - (n_in - n_main)), (0, 0))).astype(BF16)
    wts = dict(pre_w=pre_norm_w[0][None, :], w_a=w_a, w_b=w_b, w_tail=w_tail,
               d_a=d_a, d_b=d_b, d_kv=d_kv, d_qi=d_qi)
    wo = w_out[0].astype(BF16)
    wg = w_ple_gate[0].astype(BF16)
    we = w_ple_proj[0].astype(BF16)
    pw = post_norm_w[0][None, :]
    bg = b_ple_gate[0][None, :]
    subw = diff_norm_w[0][None, :]
    lamp = jnp.stack([lam_q1[0], lam_k1[0], lam_q2[0], lam_k2[0]], axis=0)

    m_p = bsz * t_p
    tm = _row_tile(t_p, ROW_TILE)
    tq_diff = _row_tile(t_p, DIFF_TILE)
    tq_dsa = _row_tile(t_p, DSA_TILE)
    tq_idx = _row_tile(t_p, IDX_TILE)
    xp2 = x_prompt.reshape(m_p, d)
    pos_p = jnp.arange(t_p, dtype=I32)
    (qa, ka, va, sga, va_t), (qs, ks, vs, sgs, qi, kiw, vs_t) = _inproj_all(xp2, pos_p, wts, tm, True)
    r3 = lambda a: a.reshape(bsz, t_p, a.shape[-1])
    a1 = _diff_attn(r3(qa), r3(ka), va_t, r3(sga), subw, lamp, lam_init, tq_diff)
    ki = kiw[:, :D_IDX]
    wi = kiw[:, D_IDX:D_IDX + H_IDX]
    qi4 = qi.reshape(bsz, t_p, H_IDX, D_IDX).transpose(0, 2, 1, 3)
    wi_t = wi.reshape(bsz, t_p, H_IDX).transpose(0, 2, 1)
    k_sel_p = min(TOPK_MAX, t_p // 4)
    bias_t = _index_select(qi4, ki.astype(BF16).reshape(bsz, t_p, D_IDX), wi_t, tq_idx, k_sel_p)
    a2 = _dsa_attn(r3(qs), r3(ks), vs_t, bias_t, r3(sgs), tq_dsa)
    y_p = _out_proj(xp2, a1.reshape(m_p, d_a), a2.reshape(m_p, d_b), p_prompt[0].reshape(m_p, -1),
                    wo, pw, wg, bg, we, tm)

    xs2 = x_sample.reshape(nb, d)
    pos_s = jnp.full((nb,), past_len, dtype=I32)
    (qa_s, ka_s, va_s, sga_s), (qs_s, ks_s, vs_s, sgs_s, qi_s, kiw_s) = _inproj_all(xs2, pos_s, wts, nb, False)
    ki_s = kiw_s[:, :D_IDX]
    wi_s = kiw_s[:, D_IDX:D_IDX + H_IDX]
    pt_flat = page_table.reshape(-1)
    n_group = SCORE_PAGES_PER_STEP if n_pages % SCORE_PAGES_PER_STEP == 0 else 1
    cache_idx_t = jnp.swapaxes(cache_idx_k.reshape(n_pool, page, D_IDX), 1, 2)
    scores = _sample_scores(pt_flat, qi_s.reshape(nb, H_IDX, D_IDX), wi_s.reshape(nb, H_IDX, 1),
                            cache_idx_t, n_pages, n_group)
    k_sel_s = min(TOPK_MAX, (past_len + t_s) // 4)
    bias_s = _sample_select(scores.reshape(nb, past_len), qi_s,
                            jnp.tile(ki_s.astype(BF16), (1, H_IDX)), wi_s, k_sel_s)
    bias_pages = jnp.repeat(bias_s[:, :past_len], n_kv, axis=1).reshape(nb, n_pages, 1, page * n_kv)
    bias_new = bias_s[:, past_len:past_len + 1].reshape(nb, 1, 1)
    e3 = lambda a: a.reshape(nb, 1, a.shape[-1])
    a1_s, a2_s = _sample_attn(
        pt_flat, e3(qa_s), e3(qs_s), e3(ka_s), e3(va_s), e3(ks_s), e3(vs_s), bias_new,
        e3(sga_s), e3(sgs_s), subw, lamp,
        cache_diff_k.reshape(n_pool, page * n_ha, HEAD_DIM), cache_diff_v.reshape(n_pool, page * n_ha, HEAD_DIM),
        cache_dsa_k.reshape(n_pool, page * n_kv, HEAD_DIM), cache_dsa_v.reshape(n_pool, page * n_kv, HEAD_DIM),
        bias_pages, n_pages, lam_init)
    y_s = _out_proj(xs2, a1_s.reshape(nb, d_a), a2_s.reshape(nb, d_b), p_sample[0].reshape(nb, -1),
                    wo, pw, wg, bg, we, nb)

    return (
        y_p.reshape(bsz, t_p, d), y_s.reshape(nb, t_s, d),
        ka.reshape(1, bsz, t_p, n_ha, HEAD_DIM), va.reshape(1, bsz, t_p, n_ha, HEAD_DIM),
        ks.reshape(1, bsz, t_p, n_kv, HEAD_DIM), vs.reshape(1, bsz, t_p, n_kv, HEAD_DIM),
        ki.reshape(1, bsz, t_p, D_IDX),
        ka_s.reshape(1, nb, t_s, n_ha, HEAD_DIM), va_s.reshape(1, nb, t_s, n_ha, HEAD_DIM),
        ks_s.reshape(1, nb, t_s, n_kv, HEAD_DIM), vs_s.reshape(1, nb, t_s, n_kv, HEAD_DIM),
        ki_s.reshape(1, nb, t_s, D_IDX),
    )
```

```python
import functools
import math

import jax
import jax.numpy as jnp
from jax import lax
from jax.experimental import pallas as pl
from jax.experimental.pallas import tpu as pltpu

F32 = jnp.float32
BF16 = jnp.bfloat16
I32 = jnp.int32

LANES = 128
SUBLANES = 8
HEAD_DIM = 128
DH_A = HEAD_DIM // 2
D_IDX = 64
H_IDX = 4
TOPK_MAX = 256
ROPE_THETA = 500000.0
ROPE_FRAC = 4
RMS_EPS = 1e-6
NEG_BIG = -1e30
INT_MIN = -(2 ** 31)
NEG_INF_KEY = INT_MIN + 0x7FFFFF
VMEM_LIMIT = 56 * 1024 * 1024
ROW_TILE = 256
LOG2E = math.log2(math.e)
DIFF_TILE = 512
DIFF_HEADS_PER_STEP = 2
DSA_TILE = 256
DSA_KV_HEADS_PER_STEP = 1
IDX_TILE = 512
SAMPLE_PAGES_PER_STEP = 8
SCORE_PAGES_PER_STEP = 32


def _cparams(n_axes):
    return pltpu.CompilerParams(
        dimension_semantics=("arbitrary",) * n_axes, vmem_limit_bytes=VMEM_LIMIT)


def _dot(a, b):
    return jnp.dot(a, b, preferred_element_type=F32)


def _dot_nt(a, b):
    return lax.dot_general(a, b, (((1,), (1,)), ((), ())), preferred_element_type=F32)


def _rope_tables(pos, d):
    r = d // ROPE_FRAC
    half = r // 2
    inv = ROPE_THETA ** (-(2.0 / r) * jnp.arange(half, dtype=F32))
    ang = pos.astype(F32)[:, None] * inv[None, :]
    cos, sin = jnp.cos(ang), jnp.sin(ang)
    t = pos.shape[0]
    ones = jnp.ones((t, d - r), F32)
    zeros_h = jnp.zeros((t, half), F32)
    zeros_r = jnp.zeros((t, d - r), F32)
    c = jnp.concatenate([cos, cos, ones], axis=-1)
    sm = jnp.concatenate([-sin, zeros_h, zeros_r], axis=-1)
    sp = jnp.concatenate([zeros_h, sin, zeros_r], axis=-1)
    rep = LANES // d
    return tuple(jnp.tile(a, (1, rep)) for a in (c, sm, sp)), half


def _rope_chunk(z, c, sm, sp, half):
    return (z * c + pltpu.roll(z, LANES - half, axis=1) * sm
            + pltpu.roll(z, half, axis=1) * sp)


def _silu(z):
    return z * (1.0 / (1.0 + jnp.exp(-z)))


def _rms_rows(x, w):
    return x * lax.rsqrt(jnp.mean(x * x, axis=-1, keepdims=True) + RMS_EPS) * w


def _inproj_a_kernel(x_ref, nw_ref, w_ref, c_ref, sm_ref, sp_ref,
                     qa_ref, ka_ref, va_ref, ga_ref, *maybe_vt_ref, half, d_a):
    h = _rms_rows(x_ref[...], nw_ref[...]).astype(BF16)
    c, sm, sp = c_ref[...], sm_ref[...], sp_ref[...]
    n_chunks = d_a // LANES
    zq = _dot_nt(h, w_ref[0:d_a, :])
    for j in range(n_chunks):
        sl = slice(j * LANES, (j + 1) * LANES)
        qa_ref[:, sl] = (_rope_chunk(zq[:, sl], c, sm, sp, half) * (LOG2E / math.sqrt(DH_A))).astype(BF16)
    zk = _dot_nt(h, w_ref[d_a:2 * d_a, :])
    for j in range(n_chunks):
        sl = slice(j * LANES, (j + 1) * LANES)
        ka_ref[:, sl] = _rope_chunk(zk[:, sl], c, sm, sp, half)
    zv = _dot_nt(h, w_ref[2 * d_a:3 * d_a, :])
    va_ref[...] = zv
    if maybe_vt_ref:
        maybe_vt_ref[0][...] = zv.T.astype(BF16)
    ga_ref[...] = _silu(_dot_nt(h, w_ref[3 * d_a:4 * d_a, :]))


def _inproj_a(x2d, nw, w_a, tabs64, tm, d_a, emit_vt):
    m, d = x2d.shape
    (c, sm, sp), half = tabs64
    t_blocks = c.shape[0] // tm
    row = lambda i: (i, 0)
    tab = lambda i: (i % t_blocks, 0)
    fixed = lambda i: (0, 0)
    out_specs = [pl.BlockSpec((tm, d_a), row)] * 4
    out_shape = [
        jax.ShapeDtypeStruct((m, d_a), BF16),
        jax.ShapeDtypeStruct((m, d_a), F32),
        jax.ShapeDtypeStruct((m, d_a), F32),
        jax.ShapeDtypeStruct((m, d_a), F32),
    ]
    if emit_vt:
        out_specs.append(pl.BlockSpec((d_a, tm), lambda i: (0, i)))
        out_shape.append(jax.ShapeDtypeStruct((d_a, m), BF16))
    return pl.pallas_call(
        functools.partial(_inproj_a_kernel, half=half, d_a=d_a),
        grid=(m // tm,),
        in_specs=[
            pl.BlockSpec((tm, d), row),
            pl.BlockSpec((1, d), fixed),
            pl.BlockSpec((4 * d_a, d), fixed, pipeline_mode=pl.Buffered(1)),
            pl.BlockSpec((tm, LANES), tab),
            pl.BlockSpec((tm, LANES), tab),
            pl.BlockSpec((tm, LANES), tab),
        ],
        out_specs=out_specs,
        out_shape=out_shape,
        compiler_params=_cparams(1),
        name="inproj_a",
    )(x2d, nw, w_a, c, sm, sp)


def _inproj_b_kernel(x_ref, nw_ref, w_ref, wt_ref, c128_ref, sm128_ref, sp128_ref,
                     c64_ref, sm64_ref, sp64_ref,
                     qs_ref, ks_ref, vs_ref, gs_ref, qi_ref, kiw_ref, *maybe_vt_ref,
                     half128, half64, d_b, d_kv, d_qi):
    h = _rms_rows(x_ref[...], nw_ref[...]).astype(BF16)
    c128, sm128, sp128 = c128_ref[...], sm128_ref[...], sp128_ref[...]
    c64, sm64, sp64 = c64_ref[...], sm64_ref[...], sp64_ref[...]
    o = 0
    zq = _dot_nt(h, w_ref[o:o + d_b, :])
    for j in range(d_b // LANES):
        sl = slice(j * LANES, (j + 1) * LANES)
        qs_ref[:, sl] = (_rope_chunk(zq[:, sl], c128, sm128, sp128, half128)
                         * (LOG2E / math.sqrt(HEAD_DIM))).astype(BF16)
    o += d_b
    zk = _dot_nt(h, w_ref[o:o + d_kv, :])
    for j in range(d_kv // LANES):
        sl = slice(j * LANES, (j + 1) * LANES)
        ks_ref[:, sl] = _rope_chunk(zk[:, sl], c128, sm128, sp128, half128)
    o += d_kv
    zv = _dot_nt(h, w_ref[o:o + d_kv, :])
    vs_ref[...] = zv
    if maybe_vt_ref:
        maybe_vt_ref[0][...] = zv.T.astype(BF16)
    o += d_kv
    gs_ref[...] = _silu(_dot_nt(h, w_ref[o:o + d_b, :]))
    o += d_b
    zi = _dot_nt(h, w_ref[o:o + d_qi, :])
    for j in range(d_qi // LANES):
        sl = slice(j * LANES, (j + 1) * LANES)
        qi_ref[:, sl] = _rope_chunk(zi[:, sl], c64, sm64, sp64, half64).astype(BF16)
    zkw = _dot_nt(h, wt_ref[...])
    lane = lax.broadcasted_iota(I32, zkw.shape, 1)
    kiw_ref[...] = jnp.where(lane < D_IDX, _rope_chunk(zkw, c64, sm64, sp64, half64), zkw)


def _inproj_b(x2d, nw, w_b, w_tail, tabs128, tabs64, tm, d_b, d_kv, d_qi, emit_vt):
    m, d = x2d.shape
    (c128, sm128, sp128), half128 = tabs128
    (c64, sm64, sp64), half64 = tabs64
    t_blocks = c128.shape[0] // tm
    row = lambda i: (i, 0)
    tab = lambda i: (i % t_blocks, 0)
    fixed = lambda i: (0, 0)
    n_b = w_b.shape[0]
    out_specs = [
        pl.BlockSpec((tm, d_b), row),
        pl.BlockSpec((tm, d_kv), row),
        pl.BlockSpec((tm, d_kv), row),
        pl.BlockSpec((tm, d_b), row),
        pl.BlockSpec((tm, d_qi), row),
        pl.BlockSpec((tm, LANES), row),
    ]
    out_shape = [
        jax.ShapeDtypeStruct((m, d_b), BF16),
        jax.ShapeDtypeStruct((m, d_kv), F32),
        jax.ShapeDtypeStruct((m, d_kv), F32),
        jax.ShapeDtypeStruct((m, d_b), F32),
        jax.ShapeDtypeStruct((m, d_qi), BF16),
        jax.ShapeDtypeStruct((m, LANES), F32),
    ]
    if emit_vt:
        out_specs.append(pl.BlockSpec((d_kv, tm), lambda i: (0, i)))
        out_shape.append(jax.ShapeDtypeStruct((d_kv, m), BF16))
    return pl.pallas_call(
        functools.partial(_inproj_b_kernel, half128=half128, half64=half64,
                          d_b=d_b, d_kv=d_kv, d_qi=d_qi),
        grid=(m // tm,),
        in_specs=[
            pl.BlockSpec((tm, d), row),
            pl.BlockSpec((1, d), fixed),
            pl.BlockSpec((n_b, d), fixed, pipeline_mode=pl.Buffered(1)),
            pl.BlockSpec((LANES, d), fixed),
        ] + [pl.BlockSpec((tm, LANES), tab)] * 6,
        out_specs=out_specs,
        out_shape=out_shape,
        compiler_params=_cparams(1),
        name="inproj_b",
    )(x2d, nw, w_b, w_tail, c128, sm128, sp128, c64, sm64, sp64)


def _flash_init(m_ref, l_ref, acc_ref):
    m_ref[...] = jnp.full(m_ref.shape, NEG_BIG, F32)
    l_ref[...] = jnp.zeros(l_ref.shape, F32)
    acc_ref[...] = jnp.zeros(acc_ref.shape, F32)


def _flash_step(s, v_bf, m_ref, l_ref, acc_ref, axis=1):
    m_prev = m_ref[...]
    m_new = jnp.maximum(m_prev, jnp.max(s, axis=axis, keepdims=True))
    alpha = jnp.exp2(m_prev - m_new)
    p = jnp.exp2(s - m_new)
    l_ref[...] = alpha * l_ref[...] + jnp.sum(p, axis=axis, keepdims=True)
    pv = _dot(p.astype(BF16), v_bf) if axis == 1 else _dot(v_bf, p.astype(BF16))
    acc_ref[...] = alpha * acc_ref[...] + pv
    m_ref[...] = m_new


def _flash_step_multi(s_list, v_list, m_ref, l_ref, acc_ref):
    m_prev = m_ref[...]
    m_new = m_prev
    for s in s_list:
        m_new = jnp.maximum(m_new, jnp.max(s, axis=1, keepdims=True))
    alpha = jnp.exp2(m_prev - m_new)
    l_new = alpha * l_ref[...]
    acc = alpha * acc_ref[...]
    for s, v_bf in zip(s_list, v_list):
        p = jnp.exp2(s - m_new)
        l_new = l_new + jnp.sum(p, axis=1, keepdims=True)
        acc = acc + _dot(p.astype(BF16), v_bf)
    l_ref[...] = l_new
    acc_ref[...] = acc
    m_ref[...] = m_new


def _lambda_full(lamp, lam_init):
    s1 = jnp.sum(lamp[0:1, :] * lamp[1:2, :], axis=-1, keepdims=True)
    s2 = jnp.sum(lamp[2:3, :] * lamp[3:4, :], axis=-1, keepdims=True)
    return jnp.exp(s1) - jnp.exp(s2) + lam_init


def _diff_attn_kernel(q_ref, k_ref, vt_ref, g_ref, subw_ref, lamp_ref, o_ref,
                      kbf, sb, mb, lb, accb, *, tq, n_hs, lam_init):
    i = pl.program_id(2)

    @pl.when(i == 0)
    def _():
        kbf[...] = k_ref[0].astype(BF16)

    q = q_ref[0]
    lane = lax.broadcasted_iota(I32, (tq, HEAD_DIM), 1)
    qm = []
    for h in range(n_hs):
        qh = q[:, h * HEAD_DIM:(h + 1) * HEAD_DIM]
        zero = jnp.zeros_like(qh)
        qm.append((jnp.where(lane < DH_A, qh, zero), jnp.where(lane >= DH_A, qh, zero)))
        for mp in range(2):
            _flash_init(mb.at[h, mp], lb.at[h, mp], accb.at[h, mp])

    def scores(j, slot):
        off = pl.multiple_of(j * tq, tq)
        for h in range(n_hs):
            kb = kbf[pl.ds(off, tq), h * HEAD_DIM:(h + 1) * HEAD_DIM]
            for mp in range(2):
                sb[h, mp, slot] = _dot_nt(kb, qm[h][mp])

    def softmax_pv(j, slot, masked):
        off = pl.multiple_of(j * tq, tq)
        for h in range(n_hs):
            vtb = vt_ref[h * HEAD_DIM:(h + 1) * HEAD_DIM, pl.ds(off, tq)]
            for mp in range(2):
                s = sb[h, mp, slot]
                if masked:
                    kidx = lax.broadcasted_iota(I32, s.shape, 0)
                    qidx = lax.broadcasted_iota(I32, s.shape, 1)
                    s = jnp.where(kidx <= qidx, s, NEG_BIG)
                _flash_step(s, vtb, mb.at[h, mp], lb.at[h, mp], accb.at[h, mp], axis=0)

    scores(0, 0)

    def pair(jj, carry):
        j = 2 * jj
        scores(j + 1, 1)
        softmax_pv(j, 0, False)
        scores(j + 2, 0)
        softmax_pv(j + 1, 1, False)
        return carry

    lax.fori_loop(0, i // 2, pair, 0)

    @pl.when(i % 2 == 1)
    def _():
        scores(i, 1)
        softmax_pv(i - 1, 0, False)
        softmax_pv(i, 1, True)

    @pl.when(i % 2 == 0)
    def _():
        softmax_pv(i, 0, True)

    lam = _lambda_full(lamp_ref[...], lam_init)
    g = g_ref[0]
    for h in range(n_hs):
        sl = slice(h * HEAD_DIM, (h + 1) * HEAD_DIM)
        o_t = accb[h, 0] / lb[h, 0] - lam * (accb[h, 1] / lb[h, 1])
        o = _rms_rows(o_t.T, subw_ref[...]) * (1.0 - lam_init)
        o_ref[0, :, sl] = (o * g[:, sl]).astype(BF16)


def _diff_attn(qa, ka, va_t, sga, subw, lamp, lam_init, tq):
    b, t, d_a = qa.shape
    n_h = d_a // HEAD_DIM
    n_hs = DIFF_HEADS_PER_STEP if n_h % DIFF_HEADS_PER_STEP == 0 else 1
    hw = n_hs * HEAD_DIM
    qmap = lambda bi, h, i: (bi, i, h)
    kmap = lambda bi, h, i: (bi, 0, h)
    fixed = lambda bi, h, i: (0, 0)
    return pl.pallas_call(
        functools.partial(_diff_attn_kernel, tq=tq, n_hs=n_hs, lam_init=lam_init),
        grid=(b, n_h // n_hs, t // tq),
        in_specs=[
            pl.BlockSpec((1, tq, hw), qmap),
            pl.BlockSpec((1, t, hw), kmap),
            pl.BlockSpec((hw, t), lambda bi, h, i: (h, bi)),
            pl.BlockSpec((1, tq, hw), qmap),
            pl.BlockSpec((1, HEAD_DIM), fixed),
            pl.BlockSpec((4, DH_A), fixed),
        ],
        out_specs=pl.BlockSpec((1, tq, hw), qmap),
        out_shape=jax.ShapeDtypeStruct((b, t, d_a), BF16),
        scratch_shapes=[
            pltpu.VMEM((t, hw), BF16),
            pltpu.VMEM((n_hs, 2, 2, tq, tq), F32),
            pltpu.VMEM((n_hs, 2, 1, tq), F32), pltpu.VMEM((n_hs, 2, 1, tq), F32),
            pltpu.VMEM((n_hs, 2, HEAD_DIM, tq), F32),
        ],
        compiler_params=_cparams(3),
        name="diff_attn",
    )(qa, ka, va_t, sga, subw, lamp)


def _key_to_float(key):
    key = jnp.maximum(key, NEG_INF_KEY)
    return lax.bitcast_convert_type(key ^ ((key >> 31) & 0x7FFFFFFF), F32)


def _fold(m, axis):
    if axis == 1:
        n = m.shape[1] // LANES
        acc = m[:, 0:LANES]
        for j in range(1, n):
            acc = acc + m[:, j * LANES:(j + 1) * LANES]
        return acc
    return jnp.sum(m.reshape(m.shape[0] // SUBLANES, SUBLANES, m.shape[1]), axis=0)


def _count(key_ref, n_chunks, chunk, pred, axis):
    other = key_ref.shape[1 - axis]
    acc_shape = (other, LANES) if axis == 1 else (SUBLANES, other)

    def body(c, acc):
        off = pl.multiple_of(c * chunk, chunk)
        kc = key_ref[:, pl.ds(off, chunk)] if axis == 1 else key_ref[pl.ds(off, chunk), :]
        return acc + _fold(pred(kc, off, slice(None)), axis)

    acc = lax.fori_loop(0, n_chunks, body, jnp.zeros(acc_shape, F32))
    return jnp.sum(acc, axis=axis, keepdims=True)


def _valu_counter(key_ref, n_chunks, chunk, axis):
    return lambda preds: [_count(key_ref, n_chunks, chunk, p, axis) for p in preds]


def _prefix_key(pu):
    key = lax.shift_left(pu, 16) ^ INT_MIN
    return key | ((key >> 31) & 0xFFFF)


def _count_bf16(hb_ref, n_chunks, chunk, cand):
    tq = hb_ref.shape[1]
    pack = 2 * SUBLANES
    candb = jnp.broadcast_to(cand.astype(BF16), (pack, tq))
    one, zero = jnp.ones((), BF16), jnp.zeros((), BF16)

    def body(c, acc):
        off = pl.multiple_of(c * chunk, chunk)
        kc = hb_ref[pl.ds(off, chunk), :].reshape(chunk // pack, pack, tq)
        part = jnp.where(kc[0] >= candb, one, zero)
        for r in range(1, chunk // pack):
            part = part + jnp.where(kc[r] >= candb, one, zero)
        return acc + part.astype(F32)

    acc = lax.fori_loop(0, n_chunks, body, jnp.zeros((pack, tq), F32))
    return jnp.sum(acc, axis=0, keepdims=True)


def _select_topk(key_ref, p_ref, counter, k_sel, idx_bits, axis, coarse_count=None):
    k_f = float(k_sel)
    q_shape = p_ref.shape

    def count_ge(cand):
        (cnt,) = counter([lambda kc, off, cs: jnp.where(kc >= cand[:, cs], 1.0, 0.0)])
        return cnt

    if coarse_count is None:
        def bit_body(it, carry):
            tu, cnt_tu = carry
            cand_u = tu | lax.shift_left(jnp.int32(1), 31 - it)
            cnt = count_ge(_key_to_float(cand_u ^ INT_MIN))
            ok = cnt >= k_f
            return jnp.where(ok, cand_u, tu), jnp.where(ok, cnt, cnt_tu)

        lowest = jnp.full(q_shape, -jnp.inf, F32)
        tu, cnt_ge = lax.fori_loop(0, 32, bit_body, (jnp.zeros(q_shape, I32), count_ge(lowest)))
        key = tu ^ INT_MIN
    else:
        def coarse_body(it, pu):
            cand_p = pu | lax.shift_left(jnp.int32(1), 15 - it)
            cnt = coarse_count(_key_to_float(_prefix_key(cand_p)))
            return jnp.where(cnt >= k_f, cand_p, pu)

        pu = lax.fori_loop(0, 16, coarse_body, jnp.zeros(q_shape, I32))
        base = _prefix_key(pu) - 2 ** 16

        def fine_body(it, carry):
            x, cnt_x = carry
            cand_x = x | lax.shift_left(jnp.int32(1), 16 - it)
            cnt = count_ge(_key_to_float(base + cand_x))
            ok = cnt >= k_f
            return jnp.where(ok, cand_x, x), jnp.where(ok, cnt, cnt_x)

        x, cnt_ge = lax.fori_loop(0, 17, fine_body,
                                  (jnp.zeros(q_shape, I32), count_ge(_key_to_float(base))))
        key = base + x

    thr = _key_to_float(key)
    p_ref[...] = jnp.full(q_shape, 2 ** 30, I32)
    has_ties = jnp.max(cnt_ge) > k_f

    @pl.when(has_ties)
    def _():
        (n_gt,) = counter([lambda kc, off, cs: jnp.where(kc > thr[:, cs], 1.0, 0.0)])
        need = k_f - n_gt

        def idx_body(it, p):
            bit = lax.shift_left(jnp.int32(1), idx_bits - 1 - it)
            cand = p | bit

            def pred(kc, off, cs):
                idx = off + lax.broadcasted_iota(I32, kc.shape, axis)
                return jnp.where(kc == thr[:, cs], jnp.where(idx < cand[:, cs], 1.0, 0.0), 0.0)

            (cnt,) = counter([pred])
            return jnp.where(cnt < need, cand, p)

        p_ref[...] = lax.fori_loop(0, idx_bits, idx_body, jnp.zeros(q_shape, I32))

    return thr, has_ties


def _selected_bias(kc, off, thr, p_max, axis):
    idx = off + lax.broadcasted_iota(I32, kc.shape, axis)
    tie = jnp.where(kc == thr, jnp.where(idx <= p_max, 1.0, 0.0), 0.0)
    sel = jnp.where(kc > thr, 1.0, tie)
    sel = jnp.where(kc == -jnp.inf, 0.0, sel)
    return jnp.where(sel > 0.5, 0.0, NEG_BIG)


def _index_select_kernel(qi_ref, ki_ref, wit_ref, bias_ref, key_ref, hb_ref, p_ref, *, tq, k_sel, idx_bits):
    i = pl.program_id(1)
    n_total = bias_ref.shape[1] // tq
    wit = wit_ref[0]

    lane = lax.broadcasted_iota(I32, (tq, LANES), 1)
    qf = qi_ref[0].astype(F32)
    qh = []
    for h in range(H_IDX):
        grp = qf[:, (h // 2) * LANES:(h // 2 + 1) * LANES]
        if h % 2 == 1:
            grp = pltpu.roll(grp, D_IDX, axis=1)
        qh.append(jnp.where(lane < D_IDX, grp, 0.0).astype(BF16))

    def score_body(c, carry):
        off = pl.multiple_of(c * tq, tq)
        kc = jnp.where(lane < D_IDX, ki_ref[0, pl.ds(off, tq), :], 0.0).astype(BF16)
        sc = jnp.zeros((tq, tq), F32)
        for h in range(H_IDX):
            sc = sc + jnp.maximum(_dot_nt(kc, qh[h]), 0.0) * wit[h:h + 1, :]
        kidx = lax.broadcasted_iota(I32, sc.shape, 0) + off
        qidx = lax.broadcasted_iota(I32, sc.shape, 1) + i * tq
        sc = jnp.where(kidx <= qidx, sc, -jnp.inf)
        key_ref[pl.ds(off, tq), :] = sc
        hb_ref[pl.ds(off, tq), :] = sc.astype(BF16)
        return carry

    lax.fori_loop(0, i + 1, score_body, 0)

    thr, has_ties = _select_topk(key_ref, p_ref, _valu_counter(key_ref, i + 1, tq, 0), k_sel, idx_bits,
                                 axis=0, coarse_count=functools.partial(_count_bf16, hb_ref, i + 1, tq))
    p_max = p_ref[...]

    def out_body(c, carry):
        off = pl.multiple_of(c * tq, tq)
        kc = key_ref[pl.ds(off, tq), :]
        bias_ref[0, pl.ds(off, tq), :] = _selected_bias(kc, off, thr, p_max, 0).astype(BF16)
        return carry

    def out_body_no_ties(c, carry):
        off = pl.multiple_of(c * tq, tq)
        kc = key_ref[pl.ds(off, tq), :]
        keep = jnp.where(kc == -jnp.inf, NEG_BIG, 0.0)
        bias_ref[0, pl.ds(off, tq), :] = jnp.where(kc >= thr, keep, NEG_BIG).astype(BF16)
        return carry

    @pl.when(has_ties)
    def _():
        lax.fori_loop(0, i + 1, out_body, 0)

    @pl.when(jnp.logical_not(has_ties))
    def _():
        lax.fori_loop(0, i + 1, out_body_no_ties, 0)

    def fill_body(c, carry):
        off = pl.multiple_of(c * tq, tq)
        bias_ref[0, pl.ds(off, tq), :] = jnp.full((tq, tq), NEG_BIG, BF16)
        return carry

    lax.fori_loop(i + 1, n_total, fill_body, 0)


def _index_select(qi, kiw, wi_t, tq, k_sel):
    b, t, d_qi = qi.shape
    assert d_qi == H_IDX * D_IDX and 2 * D_IDX == LANES and kiw.shape[2] == LANES
    idx_bits = max(1, (t - 1).bit_length())
    return pl.pallas_call(
        functools.partial(_index_select_kernel, tq=tq, k_sel=k_sel, idx_bits=idx_bits),
        grid=(b, t // tq),
        in_specs=[
            pl.BlockSpec((1, tq, d_qi), lambda bi, i: (bi, i, 0)),
            pl.BlockSpec((1, t, LANES), lambda bi, i: (bi, 0, 0)),
            pl.BlockSpec((1, H_IDX, tq), lambda bi, i: (bi, 0, i)),
        ],
        out_specs=pl.BlockSpec((1, t, tq), lambda bi, i: (bi, 0, i)),
        out_shape=jax.ShapeDtypeStruct((b, t, t), BF16),
        scratch_shapes=[pltpu.VMEM((t, tq), F32), pltpu.VMEM((t, tq), BF16), pltpu.VMEM((1, tq), I32)],
        compiler_params=_cparams(2),
        name="index_select",
    )(qi, kiw, wi_t)


def _dsa_attn_kernel(q_ref, k_ref, vt_ref, bias_ref, g_ref, o_ref, kbf, sb, mb, lb, accb,
                     *, tq, group, n_kvs):
    i = pl.program_id(2)

    @pl.when(i == 0)
    def _():
        kbf[...] = k_ref[0].astype(BF16)

    q = q_ref[0]
    q4 = []
    for n in range(n_kvs):
        q4.append(jnp.concatenate(
            [q[:, (n * group + h) * HEAD_DIM:(n * group + h + 1) * HEAD_DIM] for h in range(group)],
            axis=0))
        _flash_init(mb.at[n], lb.at[n], accb.at[n])

    def scores(j, slot):
        off = pl.multiple_of(j * tq, tq)
        for n in range(n_kvs):
            kb = kbf[pl.ds(off, tq), n * HEAD_DIM:(n + 1) * HEAD_DIM]
            sb[n, slot] = _dot_nt(kb, q4[n])

    def softmax_pv(j, slot):
        off = pl.multiple_of(j * tq, tq)
        bias = bias_ref[0, pl.ds(off, tq), :].astype(F32)
        bias_g = jnp.concatenate([bias] * group, axis=1)
        for n in range(n_kvs):
            vtb = vt_ref[n * HEAD_DIM:(n + 1) * HEAD_DIM, pl.ds(off, tq)]
            _flash_step(sb[n, slot] + bias_g, vtb, mb.at[n], lb.at[n], accb.at[n], axis=0)

    scores(0, 0)

    def pair(jj, carry):
        j = 2 * jj
        scores(j + 1, 1)
        softmax_pv(j, 0)
        scores(jnp.minimum(j + 2, i), 0)
        softmax_pv(j + 1, 1)
        return carry

    lax.fori_loop(0, (i + 1) // 2, pair, 0)

    @pl.when(i % 2 == 0)
    def _():
        softmax_pv(i, 0)

    g = g_ref[0]
    for n in range(n_kvs):
        o_t = accb[n] / lb[n]
        for h in range(group):
            sl = slice((n * group + h) * HEAD_DIM, (n * group + h + 1) * HEAD_DIM)
            o_ref[0, :, sl] = (o_t[:, h * tq:(h + 1) * tq].T * g[:, sl]).astype(BF16)


def _dsa_attn(qs, ks, vs_t, bias_t, sgs, tq):
    b, t, d_b = qs.shape
    n_kv = ks.shape[2] // HEAD_DIM
    group = d_b // HEAD_DIM // n_kv
    n_kvs = DSA_KV_HEADS_PER_STEP if n_kv % DSA_KV_HEADS_PER_STEP == 0 else 1
    gw = n_kvs * group * HEAD_DIM
    kw = n_kvs * HEAD_DIM
    qmap = lambda bi, n, i: (bi, i, n)
    kmap = lambda bi, n, i: (bi, 0, n)
    return pl.pallas_call(
        functools.partial(_dsa_attn_kernel, tq=tq, group=group, n_kvs=n_kvs),
        grid=(b, n_kv // n_kvs, t // tq),
        in_specs=[
            pl.BlockSpec((1, tq, gw), qmap),
            pl.BlockSpec((1, t, kw), kmap),
            pl.BlockSpec((kw, t), lambda bi, n, i: (n, bi)),
            pl.BlockSpec((1, t, tq), lambda bi, n, i: (bi, 0, i)),
            pl.BlockSpec((1, tq, gw), qmap),
        ],
        out_specs=pl.BlockSpec((1, tq, gw), qmap),
        out_shape=jax.ShapeDtypeStruct((b, t, d_b), BF16),
        scratch_shapes=[
            pltpu.VMEM((t, kw), BF16),
            pltpu.VMEM((n_kvs, 2, tq, group * tq), F32),
            pltpu.VMEM((n_kvs, 1, group * tq), F32), pltpu.VMEM((n_kvs, 1, group * tq), F32),
            pltpu.VMEM((n_kvs, HEAD_DIM, group * tq), F32),
        ],
        compiler_params=_cparams(3),
        name="dsa_attn",
    )(qs, ks, vs_t, bias_t, sgs)


def _out_kernel(x_ref, a1_ref, a2_ref, p_ref, wo_ref, pw_ref, wg_ref, bg_ref, we_ref, y_ref, *, d_a):
    o = _dot(a1_ref[...], wo_ref[0:d_a, :]) + _dot(a2_ref[...], wo_ref[d_a:, :])
    x1 = x_ref[...] + _rms_rows(o, pw_ref[...])
    z = _dot(x1.astype(BF16), wg_ref[...]) + bg_ref[...]
    gate = 1.0 / (1.0 + jnp.exp(-z))
    y_ref[...] = x1 + gate * _dot(p_ref[...].astype(BF16), we_ref[...])


def _out_proj(x2d, a1, a2, p2d, wo, pw, wg, bg, we, tm):
    m, d = x2d.shape
    d_a = a1.shape[1]
    d_mix = wo.shape[0]
    d_ple = p2d.shape[1]
    row = lambda i: (i, 0)
    fixed = lambda i: (0, 0)
    single = dict(pipeline_mode=pl.Buffered(1))
    return pl.pallas_call(
        functools.partial(_out_kernel, d_a=d_a),
        grid=(m // tm,),
        in_specs=[
            pl.BlockSpec((tm, d), row),
            pl.BlockSpec((tm, d_a), row),
            pl.BlockSpec((tm, d_mix - d_a), row),
            pl.BlockSpec((tm, d_ple), row),
            pl.BlockSpec((d_mix, d), fixed, **single),
            pl.BlockSpec((1, d), fixed),
            pl.BlockSpec((d, d), fixed, **single),
            pl.BlockSpec((1, d), fixed),
            pl.BlockSpec((d_ple, d), fixed, **single),
        ],
        out_specs=pl.BlockSpec((tm, d), row),
        out_shape=jax.ShapeDtypeStruct((m, d), F32),
        compiler_params=_cparams(1),
        name="out_proj",
    )(x2d, a1, a2, p2d, wo, pw, wg, bg, we)


def _sample_scores_kernel(pt_ref, qi_ref, wi_ref, *rest, n_group):
    del pt_ref
    page_refs, out_ref = rest[:n_group], rest[n_group]
    q4 = qi_ref[0]
    w = wi_ref[0]
    for r in range(n_group):
        kp_t = page_refs[r][0].astype(BF16)
        rel = jnp.maximum(_dot(q4, kp_t), 0.0)
        out_ref[0, r:r + 1, :] = jnp.sum(rel * w, axis=0, keepdims=True)


def _sample_scores(page_table_flat, qi4, wi3, cache_idx_t, n_pages, n_group):
    nb = qi4.shape[0]
    page = cache_idx_t.shape[2]

    def page_map(r):
        return lambda bi, g, pt: (pt[bi * n_pages + g * n_group + r], 0, 0)

    grid_spec = pltpu.PrefetchScalarGridSpec(
        num_scalar_prefetch=1,
        grid=(nb, n_pages // n_group),
        in_specs=[
            pl.BlockSpec((1, H_IDX, D_IDX), lambda bi, g, pt: (bi, 0, 0)),
            pl.BlockSpec((1, H_IDX, 1), lambda bi, g, pt: (bi, 0, 0)),
        ] + [pl.BlockSpec((1, D_IDX, page), page_map(r)) for r in range(n_group)],
        out_specs=pl.BlockSpec((1, n_group, page), lambda bi, g, pt: (bi, g, 0)),
    )
    return pl.pallas_call(
        functools.partial(_sample_scores_kernel, n_group=n_group),
        grid_spec=grid_spec,
        out_shape=jax.ShapeDtypeStruct((nb, n_pages, page), F32),
        compiler_params=_cparams(2),
        name="sample_scores",
    )(page_table_flat, qi4, wi3, *([cache_idx_t] * n_group))


def _sample_select_kernel(sc_ref, qi_ref, kit_ref, wi_ref, bias_ref, key_ref, p_ref, *, k_sel, idx_bits):
    rows, s_past = sc_ref.shape
    s_all = key_ref.shape[1]
    prod = qi_ref[...].astype(F32) * kit_ref[...].astype(F32)
    lane = lax.broadcasted_iota(I32, prod.shape, 1)
    wi = wi_ref[...]
    new = jnp.zeros((rows, 1), F32)
    for h in range(H_IDX):
        dot_h = jnp.sum(jnp.where(lane // D_IDX == h, prod, 0.0), axis=-1, keepdims=True)
        new = new + jnp.maximum(dot_h, 0.0) * wi[:, h:h + 1]
    key_ref[:, 0:s_past] = sc_ref[...]
    tail_lane = lax.broadcasted_iota(I32, (rows, s_all - s_past), 1)
    key_ref[:, s_past:s_all] = jnp.where(tail_lane == 0, new, -jnp.inf)
    thr, _ = _select_topk(key_ref, p_ref, _valu_counter(key_ref, 1, s_all, 1), k_sel, idx_bits, axis=1)
    bias_ref[...] = _selected_bias(key_ref[...], 0, thr, p_ref[...], 1)


def _sample_select(scores2d, qi, ki_tiled, wi, k_sel):
    rows, s_past = scores2d.shape
    s_all = s_past + LANES
    idx_bits = max(1, (s_all - 1).bit_length())
    return pl.pallas_call(
        functools.partial(_sample_select_kernel, k_sel=k_sel, idx_bits=idx_bits),
        out_shape=jax.ShapeDtypeStruct((rows, s_all), F32),
        scratch_shapes=[pltpu.VMEM((rows, s_all), F32), pltpu.VMEM((rows, 1), I32)],
        compiler_params=pltpu.CompilerParams(vmem_limit_bytes=VMEM_LIMIT),
        name="sample_select",
    )(scores2d, qi, ki_tiled, wi)


def _head_rows(row, n_rows, rows_per_head):
    return jnp.concatenate(
        [row[:, (c // rows_per_head) * HEAD_DIM:(c // rows_per_head + 1) * HEAD_DIM]
         for c in range(n_rows)], axis=0)


def _sample_attn_kernel(pt_ref, qa_ref, qs_ref, kan_ref, van_ref, ksn_ref, vsn_ref, bnew_ref,
                        ga_ref, gs_ref, subw_ref, lamp_ref, *rest, n_pg, n_ha, n_hb, group, lam_init):
    del pt_ref
    kd_refs, vd_refs = rest[0:n_pg], rest[n_pg:2 * n_pg]
    ks_refs, vs_refs = rest[2 * n_pg:3 * n_pg], rest[3 * n_pg:4 * n_pg]
    bias_ref, oa_ref, ob_ref, qd, qsb, md, ld, accd, ms, ls, accs = rest[4 * n_pg:]
    p = pl.program_id(1)
    n_p = pl.num_programs(1)
    n_kv = n_hb // group

    @pl.when(p == 0)
    def _():
        qd_f = _head_rows(qa_ref[0].astype(F32), 2 * n_ha, 2)
        rd = lax.broadcasted_iota(I32, qd_f.shape, 0)
        ln = lax.broadcasted_iota(I32, qd_f.shape, 1)
        qd_f = jnp.where(ln // DH_A == rd % 2, qd_f, 0.0)
        qd[...] = qd_f.astype(BF16)
        qs_f = _head_rows(qs_ref[0].astype(F32), n_hb, 1)
        qsb[...] = qs_f.astype(BF16)
        kan = _head_rows(kan_ref[0].astype(BF16).astype(F32), 2 * n_ha, 2)
        van = _head_rows(van_ref[0].astype(BF16).astype(F32), 2 * n_ha, 2)
        md[...] = jnp.sum(qd_f * kan, axis=-1, keepdims=True)
        ld[...] = jnp.ones(ld.shape, F32)
        accd[...] = van
        ksn = _head_rows(ksn_ref[0].astype(BF16).astype(F32), n_hb, group)
        vsn = _head_rows(vsn_ref[0].astype(BF16).astype(F32), n_hb, group)
        ms[...] = jnp.sum(qs_f * ksn, axis=-1, keepdims=True) + bnew_ref[0]
        ls[...] = jnp.ones(ls.shape, F32)
        accs[...] = vsn

    sd_list, vd_list, ss_list, vs_list = [], [], [], []
    for r in range(n_pg):
        kd = kd_refs[r][0].astype(BF16)
        sd = _dot_nt(qd[...], kd)
        rq = lax.broadcasted_iota(I32, sd.shape, 0)
        ck = lax.broadcasted_iota(I32, sd.shape, 1)
        sd_list.append(jnp.where(ck % n_ha == rq // 2, sd, NEG_BIG))
        vd_list.append(vd_refs[r][0].astype(BF16))
        ksp = ks_refs[r][0].astype(BF16)
        ss = _dot_nt(qsb[...], ksp) + bias_ref[0, r]
        rq = lax.broadcasted_iota(I32, ss.shape, 0)
        ck = lax.broadcasted_iota(I32, ss.shape, 1)
        ss_list.append(jnp.where(ck % n_kv == rq // group, ss, NEG_BIG))
        vs_list.append(vs_refs[r][0].astype(BF16))
    _flash_step_multi(sd_list, vd_list, md, ld, accd)
    _flash_step_multi(ss_list, vs_list, ms, ls, accs)

    @pl.when(p == n_p - 1)
    def _():
        lam = _lambda_full(lamp_ref[...], lam_init)
        od = accd[...] / ld[...]
        ga = ga_ref[0]
        subw = subw_ref[...]
        for h in range(n_ha):
            sl = slice(h * HEAD_DIM, (h + 1) * HEAD_DIM)
            o = od[2 * h:2 * h + 1, :] - lam * od[2 * h + 1:2 * h + 2, :]
            o = _rms_rows(o, subw) * (1.0 - lam_init)
            oa_ref[0, :, sl] = (o * ga[:, sl]).astype(BF16)
        os_ = accs[...] / ls[...]
        gs = gs_ref[0]
        for h in range(n_hb):
            sl = slice(h * HEAD_DIM, (h + 1) * HEAD_DIM)
            ob_ref[0, :, sl] = (os_[h:h + 1, :] * gs[:, sl]).astype(BF16)


def _sample_attn(page_table_flat, qa, qs, ka_new, va_new, ks_new, vs_new, bias_new, sga, sgs,
                 subw, lamp, cdk, cdv, csk, csv, bias_pages, n_pages, lam_init):
    nb, _, d_a = qa.shape
    d_b = qs.shape[2]
    d_kv = ks_new.shape[2]
    n_ha = d_a // HEAD_DIM
    n_hb = d_b // HEAD_DIM
    n_kv = d_kv // HEAD_DIM
    group = n_hb // n_kv
    rows_d = cdk.shape[1]
    rows_s = csk.shape[1]
    per_b = lambda bi, p, pt: (bi, 0, 0)
    fixed = lambda bi, p, pt: (0, 0)
    n_pg = SAMPLE_PAGES_PER_STEP if n_pages % SAMPLE_PAGES_PER_STEP == 0 else 1

    def paged(r):
        return lambda bi, p, pt: (pt[bi * n_pages + p * n_pg + r], 0, 0)

    grid_spec = pltpu.PrefetchScalarGridSpec(
        num_scalar_prefetch=1,
        grid=(nb, n_pages // n_pg),
        in_specs=[
            pl.BlockSpec((1, 1, d_a), per_b),
            pl.BlockSpec((1, 1, d_b), per_b),
            pl.BlockSpec((1, 1, d_a), per_b),
            pl.BlockSpec((1, 1, d_a), per_b),
            pl.BlockSpec((1, 1, d_kv), per_b),
            pl.BlockSpec((1, 1, d_kv), per_b),
            pl.BlockSpec((1, 1, 1), per_b),
            pl.BlockSpec((1, 1, d_a), per_b),
            pl.BlockSpec((1, 1, d_b), per_b),
            pl.BlockSpec((1, HEAD_DIM), fixed),
            pl.BlockSpec((4, DH_A), fixed),
        ] + [pl.BlockSpec((1, rows_d, HEAD_DIM), paged(r)) for r in range(n_pg)] * 2
        + [pl.BlockSpec((1, rows_s, HEAD_DIM), paged(r)) for r in range(n_pg)] * 2
        + [pl.BlockSpec((1, n_pg, 1, rows_s), lambda bi, p, pt: (bi, p, 0, 0))],
        out_specs=[pl.BlockSpec((1, 1, d_a), per_b), pl.BlockSpec((1, 1, d_b), per_b)],
        scratch_shapes=[
            pltpu.VMEM((2 * n_ha, HEAD_DIM), BF16), pltpu.VMEM((n_hb, HEAD_DIM), BF16),
            pltpu.VMEM((2 * n_ha, 1), F32), pltpu.VMEM((2 * n_ha, 1), F32),
            pltpu.VMEM((2 * n_ha, HEAD_DIM), F32),
            pltpu.VMEM((n_hb, 1), F32), pltpu.VMEM((n_hb, 1), F32), pltpu.VMEM((n_hb, HEAD_DIM), F32),
        ],
    )
    return pl.pallas_call(
        functools.partial(_sample_attn_kernel, n_pg=n_pg, n_ha=n_ha, n_hb=n_hb, group=group,
                          lam_init=lam_init),
        grid_spec=grid_spec,
        out_shape=[jax.ShapeDtypeStruct((nb, 1, d_a), BF16), jax.ShapeDtypeStruct((nb, 1, d_b), BF16)],
        compiler_params=_cparams(2),
        name="sample_attn",
    )(page_table_flat, qa, qs, ka_new, va_new, ks_new, vs_new, bias_new, sga, sgs, subw, lamp,
      *([cdk] * n_pg + [cdv] * n_pg + [csk] * n_pg + [csv] * n_pg), bias_pages)


def _row_tile(m, pref):
    return pref if m % pref == 0 else m


def _inproj_all(x2d, pos, wts, tm, emit_vt):
    tabs64 = _rope_tables(pos, DH_A)
    tabs128 = _rope_tables(pos, HEAD_DIM)
    d_a, d_b, d_kv, d_qi = wts["d_a"], wts["d_b"], wts["d_kv"], wts["d_qi"]
    outs_a = _inproj_a(x2d, wts["pre_w"], wts["w_a"], tabs64, tm, d_a, emit_vt)
    outs_b = _inproj_b(x2d, wts["pre_w"], wts["w_b"], wts["w_tail"], tabs128, tabs64, tm,
                       d_b, d_kv, d_qi, emit_vt)
    return outs_a, outs_b


def kernel(x_prompt, x_sample, p_prompt, p_sample, cache_diff_k, cache_diff_v, cache_dsa_k, cache_dsa_v, cache_idx_k, page_table, pre_norm_w, post_norm_w, w_in, lam_q1, lam_k1, lam_q2, lam_k2, diff_norm_w, w_out, w_ple_gate, b_ple_gate, w_ple_proj):
    depth = w_in.shape[0]
    assert depth == 1, "single-layer stack only"
    bsz, t_p, d = x_prompt.shape
    nb, t_s, _ = x_sample.shape
    assert t_s == 1, "one new token per sample sequence"
    n_pages = page_table.shape[1]
    n_pool = cache_diff_k.shape[1]
    page = cache_diff_k.shape[2]
    n_ha = cache_diff_k.shape[3]
    n_kv = cache_dsa_k.shape[3]
    d_a = n_ha * HEAD_DIM
    d_kv = n_kv * HEAD_DIM
    d_b = d - d_a
    d_qi = H_IDX * D_IDX
    past_len = n_pages * page
    lam_init = 0.8 - 0.6 * math.exp(-0.3 * 0)
    n_in = w_in.shape[2]

    n_main = 4 * d_a + 2 * d_b + 2 * d_kv + d_qi
    assert n_in - n_main == D_IDX + H_IDX
    w_a = jnp.swapaxes(w_in[0, :, :4 * d_a], 0, 1).astype(BF16)
    w_b = jnp.swapaxes(w_in[0, :, 4 * d_a:n_main], 0, 1).astype(BF16)
    w_tail = jnp.pad(jnp.swapaxes(w_in[0, :, n_main:], 0, 1),
                     ((0, LANES - (n_in - n_main)), (0, 0))).astype(BF16)
    wts = dict(pre_w=pre_norm_w[0][None, :], w_a=w_a, w_b=w_b, w_tail=w_tail,
               d_a=d_a, d_b=d_b, d_kv=d_kv, d_qi=d_qi)
    wo = w_out[0].astype(BF16)
    wg = w_ple_gate[0].astype(BF16)
    we = w_ple_proj[0].astype(BF16)
    pw = post_norm_w[0][None, :]
    bg = b_ple_gate[0][None, :]
    subw = diff_norm_w[0][None, :]
    lamp = jnp.stack([lam_q1[0], lam_k1[0], lam_q2[0], lam_k2[0]], axis=0)

    m_p = bsz * t_p
    tm = _row_tile(t_p, ROW_TILE)
    tq_diff = _row_tile(t_p, DIFF_TILE)
    tq_dsa = _row_tile(t_p, DSA_TILE)
    tq_idx = _row_tile(t_p, IDX_TILE)
    xp2 = x_prompt.reshape(m_p, d)
    pos_p = jnp.arange(t_p, dtype=I32)
    (qa, ka, va, sga, va_t), (qs, ks, vs, sgs, qi, kiw, vs_t) = _inproj_all(xp2, pos_p, wts, tm, True)
    r3 = lambda a: a.reshape(bsz, t_p, a.shape[-1])
    a1 = _diff_attn(r3(qa), r3(ka), va_t, r3(sga), subw, lamp, lam_init, tq_diff)
    ki = kiw[:, :D_IDX]
    wi = kiw[:, D_IDX:D_IDX + H_IDX]
    wi_t = wi.reshape(bsz, t_p, H_IDX).transpose(0, 2, 1)
    k_sel_p = min(TOPK_MAX, t_p // 4)
    bias_t = _index_select(r3(qi), r3(kiw), wi_t, tq_idx, k_sel_p)
    a2 = _dsa_attn(r3(qs), r3(ks), vs_t, bias_t, r3(sgs), tq_dsa)
    y_p = _out_proj(xp2, a1.reshape(m_p, d_a), a2.reshape(m_p, d_b), p_prompt[0].reshape(m_p, -1),
                    wo, pw, wg, bg, we, tm)

    xs2 = x_sample.reshape(nb, d)
    pos_s = jnp.full((nb,), past_len, dtype=I32)
    (qa_s, ka_s, va_s, sga_s), (qs_s, ks_s, vs_s, sgs_s, qi_s, kiw_s) = _inproj_all(xs2, pos_s, wts, nb, False)
    ki_s = kiw_s[:, :D_IDX]
    wi_s = kiw_s[:, D_IDX:D_IDX + H_IDX]
    pt_flat = page_table.reshape(-1)
    n_group = SCORE_PAGES_PER_STEP if n_pages % SCORE_PAGES_PER_STEP == 0 else 1
    cache_idx_t = jnp.swapaxes(cache_idx_k.reshape(n_pool, page, D_IDX), 1, 2)
    scores = _sample_scores(pt_flat, qi_s.reshape(nb, H_IDX, D_IDX), wi_s.reshape(nb, H_IDX, 1),
                            cache_idx_t, n_pages, n_group)
    k_sel_s = min(TOPK_MAX, (past_len + t_s) // 4)
    bias_s = _sample_select(scores.reshape(nb, past_len), qi_s,
                            jnp.tile(ki_s.astype(BF16), (1, H_IDX)), wi_s, k_sel_s)
    bias_pages = jnp.repeat(bias_s[:, :past_len], n_kv, axis=1).reshape(nb, n_pages, 1, page * n_kv)
    bias_new = bias_s[:, past_len:past_len + 1].reshape(nb, 1, 1)
    e3 = lambda a: a.reshape(nb, 1, a.shape[-1])
    a1_s, a2_s = _sample_attn(
        pt_flat, e3(qa_s), e3(qs_s), e3(ka_s), e3(va_s), e3(ks_s), e3(vs_s), bias_new,
        e3(sga_s), e3(sgs_s), subw, lamp,
        cache_diff_k.reshape(n_pool, page * n_ha, HEAD_DIM), cache_diff_v.reshape(n_pool, page * n_ha, HEAD_DIM),
        cache_dsa_k.reshape(n_pool, page * n_kv, HEAD_DIM), cache_dsa_v.reshape(n_pool, page * n_kv, HEAD_DIM),
        bias_pages, n_pages, lam_init)
    y_s = _out_proj(xs2, a1_s.reshape(nb, d_a), a2_s.reshape(nb, d_b), p_sample[0].reshape(nb, -1),
                    wo, pw, wg, bg, we, nb)

    return (
        y_p.reshape(bsz, t_p, d), y_s.reshape(nb, t_s, d),
        ka.reshape(1, bsz, t_p, n_ha, HEAD_DIM), va.reshape(1, bsz, t_p, n_ha, HEAD_DIM),
        ks.reshape(1, bsz, t_p, n_kv, HEAD_DIM), vs.reshape(1, bsz, t_p, n_kv, HEAD_DIM),
        ki.reshape(1, bsz, t_p, D_IDX),
        ka_s.reshape(1, nb, t_s, n_ha, HEAD_DIM), va_s.reshape(1, nb, t_s, n_ha, HEAD_DIM),
        ks_s.reshape(1, nb, t_s, n_kv, HEAD_DIM), vs_s.reshape(1, nb, t_s, n_kv, HEAD_DIM),
        ki_s.reshape(1, nb, t_s, D_IDX),
    )
```

```python
import functools
import math

import jax
import jax.numpy as jnp
from jax import lax
from jax.experimental import pallas as pl
from jax.experimental.pallas import tpu as pltpu

F32 = jnp.float32
BF16 = jnp.bfloat16
I32 = jnp.int32

LANES = 128
SUBLANES = 8
HEAD_DIM = 128
DH_A = HEAD_DIM // 2
D_IDX = 64
H_IDX = 4
TOPK_MAX = 256
ROPE_THETA = 500000.0
ROPE_FRAC = 4
RMS_EPS = 1e-6
NEG_BIG = -1e30
INT_MIN = -(2 ** 31)
NEG_INF_KEY = INT_MIN + 0x7FFFFF
VMEM_LIMIT = 56 * 1024 * 1024
ROW_TILE = 256
LOG2E = math.log2(math.e)
DIFF_TILE = 512
DIFF_HEADS_PER_STEP = 2
DSA_TILE = 256
DSA_KV_HEADS_PER_STEP = 1
IDX_TILE = 512
SAMPLE_PAGES_PER_STEP = 8
SCORE_PAGES_PER_STEP = 32


def _cparams(n_axes):
    return pltpu.CompilerParams(
        dimension_semantics=("arbitrary",) * n_axes, vmem_limit_bytes=VMEM_LIMIT)


def _dot(a, b):
    return jnp.dot(a, b, preferred_element_type=F32)


def _dot_nt(a, b):
    return lax.dot_general(a, b, (((1,), (1,)), ((), ())), preferred_element_type=F32)


def _rope_tables(pos, d):
    r = d // ROPE_FRAC
    half = r // 2
    inv = ROPE_THETA ** (-(2.0 / r) * jnp.arange(half, dtype=F32))
    ang = pos.astype(F32)[:, None] * inv[None, :]
    cos, sin = jnp.cos(ang), jnp.sin(ang)
    t = pos.shape[0]
    ones = jnp.ones((t, d - r), F32)
    zeros_h = jnp.zeros((t, half), F32)
    zeros_r = jnp.zeros((t, d - r), F32)
    c = jnp.concatenate([cos, cos, ones], axis=-1)
    sm = jnp.concatenate([-sin, zeros_h, zeros_r], axis=-1)
    sp = jnp.concatenate([zeros_h, sin, zeros_r], axis=-1)
    rep = LANES // d
    return tuple(jnp.tile(a, (1, rep)) for a in (c, sm, sp)), half


def _rope_chunk(z, c, sm, sp, half):
    return (z * c + pltpu.roll(z, LANES - half, axis=1) * sm
            + pltpu.roll(z, half, axis=1) * sp)


def _silu(z):
    return z * (1.0 / (1.0 + jnp.exp(-z)))


def _rms_rows(x, w):
    return x * lax.rsqrt(jnp.mean(x * x, axis=-1, keepdims=True) + RMS_EPS) * w


def _inproj_a_kernel(x_ref, nw_ref, w_ref, c_ref, sm_ref, sp_ref,
                     qa_ref, ka_ref, va_ref, ga_ref, *maybe_vt_ref, half, d_a):
    h = _rms_rows(x_ref[...], nw_ref[...]).astype(BF16)
    c, sm, sp = c_ref[...], sm_ref[...], sp_ref[...]
    n_chunks = d_a // LANES
    zq = _dot_nt(h, w_ref[0:d_a, :])
    for j in range(n_chunks):
        sl = slice(j * LANES, (j + 1) * LANES)
        qa_ref[:, sl] = (_rope_chunk(zq[:, sl], c, sm, sp, half) * (LOG2E / math.sqrt(DH_A))).astype(BF16)
    zk = _dot_nt(h, w_ref[d_a:2 * d_a, :])
    for j in range(n_chunks):
        sl = slice(j * LANES, (j + 1) * LANES)
        ka_ref[:, sl] = _rope_chunk(zk[:, sl], c, sm, sp, half)
    zv = _dot_nt(h, w_ref[2 * d_a:3 * d_a, :])
    va_ref[...] = zv
    if maybe_vt_ref:
        maybe_vt_ref[0][...] = zv.T.astype(BF16)
    ga_ref[...] = _silu(_dot_nt(h, w_ref[3 * d_a:4 * d_a, :]))


def _inproj_a(x2d, nw, w_a, tabs64, tm, d_a, emit_vt):
    m, d = x2d.shape
    (c, sm, sp), half = tabs64
    t_blocks = c.shape[0] // tm
    row = lambda i: (i, 0)
    tab = lambda i: (i % t_blocks, 0)
    fixed = lambda i: (0, 0)
    out_specs = [pl.BlockSpec((tm, d_a), row)] * 4
    out_shape = [
        jax.ShapeDtypeStruct((m, d_a), BF16),
        jax.ShapeDtypeStruct((m, d_a), F32),
        jax.ShapeDtypeStruct((m, d_a), F32),
        jax.ShapeDtypeStruct((m, d_a), F32),
    ]
    if emit_vt:
        out_specs.append(pl.BlockSpec((d_a, tm), lambda i: (0, i)))
        out_shape.append(jax.ShapeDtypeStruct((d_a, m), BF16))
    return pl.pallas_call(
        functools.partial(_inproj_a_kernel, half=half, d_a=d_a),
        grid=(m // tm,),
        in_specs=[
            pl.BlockSpec((tm, d), row),
            pl.BlockSpec((1, d), fixed),
            pl.BlockSpec((4 * d_a, d), fixed, pipeline_mode=pl.Buffered(1)),
            pl.BlockSpec((tm, LANES), tab),
            pl.BlockSpec((tm, LANES), tab),
            pl.BlockSpec((tm, LANES), tab),
        ],
        out_specs=out_specs,
        out_shape=out_shape,
        compiler_params=_cparams(1),
        name="inproj_a",
    )(x2d, nw, w_a, c, sm, sp)


def _inproj_b_kernel(x_ref, nw_ref, w_ref, wt_ref, c128_ref, sm128_ref, sp128_ref,
                     c64_ref, sm64_ref, sp64_ref,
                     qs_ref, ks_ref, vs_ref, gs_ref, qi_ref, kiw_ref, *maybe_vt_ref,
                     half128, half64, d_b, d_kv, d_qi):
    h = _rms_rows(x_ref[...], nw_ref[...]).astype(BF16)
    c128, sm128, sp128 = c128_ref[...], sm128_ref[...], sp128_ref[...]
    c64, sm64, sp64 = c64_ref[...], sm64_ref[...], sp64_ref[...]
    o = 0
    zq = _dot_nt(h, w_ref[o:o + d_b, :])
    for j in range(d_b // LANES):
        sl = slice(j * LANES, (j + 1) * LANES)
        qs_ref[:, sl] = (_rope_chunk(zq[:, sl], c128, sm128, sp128, half128)
                         * (LOG2E / math.sqrt(HEAD_DIM))).astype(BF16)
    o += d_b
    zk = _dot_nt(h, w_ref[o:o + d_kv, :])
    for j in range(d_kv // LANES):
        sl = slice(j * LANES, (j + 1) * LANES)
        ks_ref[:, sl] = _rope_chunk(zk[:, sl], c128, sm128, sp128, half128)
    o += d_kv
    zv = _dot_nt(h, w_ref[o:o + d_kv, :])
    vs_ref[...] = zv
    if maybe_vt_ref:
        maybe_vt_ref[0][...] = zv.T.astype(BF16)
    o += d_kv
    gs_ref[...] = _silu(_dot_nt(h, w_ref[o:o + d_b, :]))
    o += d_b
    zi = _dot_nt(h, w_ref[o:o + d_qi, :])
    for j in range(d_qi // LANES):
        sl = slice(j * LANES, (j + 1) * LANES)
        qi_ref[:, sl] = _rope_chunk(zi[:, sl], c64, sm64, sp64, half64).astype(BF16)
    zkw = _dot_nt(h, wt_ref[...])
    lane = lax.broadcasted_iota(I32, zkw.shape, 1)
    kiw_ref[...] = jnp.where(lane < D_IDX, _rope_chunk(zkw, c64, sm64, sp64, half64), zkw)


def _inproj_b(x2d, nw, w_b, w_tail, tabs128, tabs64, tm, d_b, d_kv, d_qi, emit_vt):
    m, d = x2d.shape
    (c128, sm128, sp128), half128 = tabs128
    (c64, sm64, sp64), half64 = tabs64
    t_blocks = c128.shape[0] // tm
    row = lambda i: (i, 0)
    tab = lambda i: (i % t_blocks, 0)
    fixed = lambda i: (0, 0)
    n_b = w_b.shape[0]
    out_specs = [
        pl.BlockSpec((tm, d_b), row),
        pl.BlockSpec((tm, d_kv), row),
        pl.BlockSpec((tm, d_kv), row),
        pl.BlockSpec((tm, d_b), row),
        pl.BlockSpec((tm, d_qi), row),
        pl.BlockSpec((tm, LANES), row),
    ]
    out_shape = [
        jax.ShapeDtypeStruct((m, d_b), BF16),
        jax.ShapeDtypeStruct((m, d_kv), F32),
        jax.ShapeDtypeStruct((m, d_kv), F32),
        jax.ShapeDtypeStruct((m, d_b), F32),
        jax.ShapeDtypeStruct((m, d_qi), BF16),
        jax.ShapeDtypeStruct((m, LANES), F32),
    ]
    if emit_vt:
        out_specs.append(pl.BlockSpec((d_kv, tm), lambda i: (0, i)))
        out_shape.append(jax.ShapeDtypeStruct((d_kv, m), BF16))
    return pl.pallas_call(
        functools.partial(_inproj_b_kernel, half128=half128, half64=half64,
                          d_b=d_b, d_kv=d_kv, d_qi=d_qi),
        grid=(m // tm,),
        in_specs=[
            pl.BlockSpec((tm, d), row),
            pl.BlockSpec((1, d), fixed),
            pl.BlockSpec((n_b, d), fixed, pipeline_mode=pl.Buffered(1)),
            pl.BlockSpec((LANES, d), fixed),
        ] + [pl.BlockSpec((tm, LANES), tab)] * 6,
        out_specs=out_specs,
        out_shape=out_shape,
        compiler_params=_cparams(1),
        name="inproj_b",
    )(x2d, nw, w_b, w_tail, c128, sm128, sp128, c64, sm64, sp64)


def _flash_init(m_ref, l_ref, acc_ref):
    m_ref[...] = jnp.full(m_ref.shape, NEG_BIG, F32)
    l_ref[...] = jnp.zeros(l_ref.shape, F32)
    acc_ref[...] = jnp.zeros(acc_ref.shape, F32)


def _flash_step(s, v_bf, m_ref, l_ref, acc_ref, axis=1):
    m_prev = m_ref[...]
    m_new = jnp.maximum(m_prev, jnp.max(s, axis=axis, keepdims=True))
    alpha = jnp.exp2(m_prev - m_new)
    p = jnp.exp2(s - m_new)
    l_ref[...] = alpha * l_ref[...] + jnp.sum(p, axis=axis, keepdims=True)
    pv = _dot(p.astype(BF16), v_bf) if axis == 1 else _dot(v_bf, p.astype(BF16))
    acc_ref[...] = alpha * acc_ref[...] + pv
    m_ref[...] = m_new


def _flash_step_multi(s_list, v_list, m_ref, l_ref, acc_ref):
    m_prev = m_ref[...]
    m_new = m_prev
    for s in s_list:
        m_new = jnp.maximum(m_new, jnp.max(s, axis=1, keepdims=True))
    alpha = jnp.exp2(m_prev - m_new)
    l_new = alpha * l_ref[...]
    acc = alpha * acc_ref[...]
    for s, v_bf in zip(s_list, v_list):
        p = jnp.exp2(s - m_new)
        l_new = l_new + jnp.sum(p, axis=1, keepdims=True)
        acc = acc + _dot(p.astype(BF16), v_bf)
    l_ref[...] = l_new
    acc_ref[...] = acc
    m_ref[...] = m_new


def _lambda_full(lamp, lam_init):
    s1 = jnp.sum(lamp[0:1, :] * lamp[1:2, :], axis=-1, keepdims=True)
    s2 = jnp.sum(lamp[2:3, :] * lamp[3:4, :], axis=-1, keepdims=True)
    return jnp.exp(s1) - jnp.exp(s2) + lam_init


def _diff_attn_kernel(q_ref, k_ref, vt_ref, g_ref, subw_ref, lamp_ref, o_ref,
                      kbf, sb, mb, lb, accb, *, tq, n_hs, lam_init):
    i = pl.program_id(2)

    @pl.when(i == 0)
    def _():
        kbf[...] = k_ref[0].astype(BF16)

    q = q_ref[0]
    lane = lax.broadcasted_iota(I32, (tq, HEAD_DIM), 1)
    qm = []
    for h in range(n_hs):
        qh = q[:, h * HEAD_DIM:(h + 1) * HEAD_DIM]
        zero = jnp.zeros_like(qh)
        qm.append((jnp.where(lane < DH_A, qh, zero), jnp.where(lane >= DH_A, qh, zero)))
        for mp in range(2):
            _flash_init(mb.at[h, mp], lb.at[h, mp], accb.at[h, mp])

    def scores(j, slot):
        off = pl.multiple_of(j * tq, tq)
        for h in range(n_hs):
            kb = kbf[pl.ds(off, tq), h * HEAD_DIM:(h + 1) * HEAD_DIM]
            for mp in range(2):
                sb[h, mp, slot] = _dot_nt(kb, qm[h][mp])

    def softmax_pv(j, slot, masked):
        off = pl.multiple_of(j * tq, tq)
        for h in range(n_hs):
            vtb = vt_ref[h * HEAD_DIM:(h + 1) * HEAD_DIM, pl.ds(off, tq)]
            for mp in range(2):
                s = sb[h, mp, slot]
                if masked:
                    kidx = lax.broadcasted_iota(I32, s.shape, 0)
                    qidx = lax.broadcasted_iota(I32, s.shape, 1)
                    s = jnp.where(kidx <= qidx, s, NEG_BIG)
                _flash_step(s, vtb, mb.at[h, mp], lb.at[h, mp], accb.at[h, mp], axis=0)

    scores(0, 0)

    def pair(jj, carry):
        j = 2 * jj
        scores(j + 1, 1)
        softmax_pv(j, 0, False)
        scores(j + 2, 0)
        softmax_pv(j + 1, 1, False)
        return carry

    lax.fori_loop(0, i // 2, pair, 0)

    @pl.when(i % 2 == 1)
    def _():
        scores(i, 1)
        softmax_pv(i - 1, 0, False)
        softmax_pv(i, 1, True)

    @pl.when(i % 2 == 0)
    def _():
        softmax_pv(i, 0, True)

    lam = _lambda_full(lamp_ref[...], lam_init)
    g = g_ref[0]
    for h in range(n_hs):
        sl = slice(h * HEAD_DIM, (h + 1) * HEAD_DIM)
        o_t = accb[h, 0] / lb[h, 0] - lam * (accb[h, 1] / lb[h, 1])
        o = _rms_rows(o_t.T, subw_ref[...]) * (1.0 - lam_init)
        o_ref[0, :, sl] = (o * g[:, sl]).astype(BF16)


def _diff_attn(qa, ka, va_t, sga, subw, lamp, lam_init, tq):
    b, t, d_a = qa.shape
    n_h = d_a // HEAD_DIM
    n_hs = DIFF_HEADS_PER_STEP if n_h % DIFF_HEADS_PER_STEP == 0 else 1
    hw = n_hs * HEAD_DIM
    qmap = lambda bi, h, i: (bi, i, h)
    kmap = lambda bi, h, i: (bi, 0, h)
    fixed = lambda bi, h, i: (0, 0)
    return pl.pallas_call(
        functools.partial(_diff_attn_kernel, tq=tq, n_hs=n_hs, lam_init=lam_init),
        grid=(b, n_h // n_hs, t // tq),
        in_specs=[
            pl.BlockSpec((1, tq, hw), qmap),
            pl.BlockSpec((1, t, hw), kmap),
            pl.BlockSpec((hw, t), lambda bi, h, i: (h, bi)),
            pl.BlockSpec((1, tq, hw), qmap),
            pl.BlockSpec((1, HEAD_DIM), fixed),
            pl.BlockSpec((4, DH_A), fixed),
        ],
        out_specs=pl.BlockSpec((1, tq, hw), qmap),
        out_shape=jax.ShapeDtypeStruct((b, t, d_a), BF16),
        scratch_shapes=[
            pltpu.VMEM((t, hw), BF16),
            pltpu.VMEM((n_hs, 2, 2, tq, tq), F32),
            pltpu.VMEM((n_hs, 2, 1, tq), F32), pltpu.VMEM((n_hs, 2, 1, tq), F32),
            pltpu.VMEM((n_hs, 2, HEAD_DIM, tq), F32),
        ],
        compiler_params=_cparams(3),
        name="diff_attn",
    )(qa, ka, va_t, sga, subw, lamp)


def _key_to_float(key):
    key = jnp.maximum(key, NEG_INF_KEY)
    return lax.bitcast_convert_type(key ^ ((key >> 31) & 0x7FFFFFFF), F32)


def _fold(m, axis):
    if axis == 1:
        n = m.shape[1] // LANES
        acc = m[:, 0:LANES]
        for j in range(1, n):
            acc = acc + m[:, j * LANES:(j + 1) * LANES]
        return acc
    return jnp.sum(m.reshape(m.shape[0] // SUBLANES, SUBLANES, m.shape[1]), axis=0)


def _count(key_ref, n_chunks, chunk, pred, axis):
    other = key_ref.shape[1 - axis]
    acc_shape = (other, LANES) if axis == 1 else (SUBLANES, other)

    def body(c, acc):
        off = pl.multiple_of(c * chunk, chunk)
        kc = key_ref[:, pl.ds(off, chunk)] if axis == 1 else key_ref[pl.ds(off, chunk), :]
        return acc + _fold(pred(kc, off, slice(None)), axis)

    acc = lax.fori_loop(0, n_chunks, body, jnp.zeros(acc_shape, F32))
    return jnp.sum(acc, axis=axis, keepdims=True)


def _valu_counter(key_ref, n_chunks, chunk, axis):
    return lambda preds: [_count(key_ref, n_chunks, chunk, p, axis) for p in preds]


def _prefix_key(pu):
    key = lax.shift_left(pu, 16) ^ INT_MIN
    return key | ((key >> 31) & 0xFFFF)


def _count_bf16(hb_ref, n_chunks, chunk, cand):
    tq = hb_ref.shape[1]
    pack = 2 * SUBLANES
    candb = jnp.broadcast_to(cand.astype(BF16), (pack, tq))
    one, zero = jnp.ones((), BF16), jnp.zeros((), BF16)

    def body(c, acc):
        off = pl.multiple_of(c * chunk, chunk)
        kc = hb_ref[pl.ds(off, chunk), :].reshape(chunk // pack, pack, tq)
        part = jnp.where(kc[0] >= candb, one, zero)
        for r in range(1, chunk // pack):
            part = part + jnp.where(kc[r] >= candb, one, zero)
        return acc + part.astype(F32)

    acc = lax.fori_loop(0, n_chunks, body, jnp.zeros((pack, tq), F32))
    return jnp.sum(acc, axis=0, keepdims=True)


def _select_topk(key_ref, p_ref, counter, k_sel, idx_bits, axis, coarse_count=None, q_shape=None):
    k_f = float(k_sel)
    q_shape = p_ref.shape if p_ref is not None else q_shape

    def count_ge(cand):
        (cnt,) = counter([lambda kc, off, cs: jnp.where(kc >= cand[:, cs], 1.0, 0.0)])
        return cnt

    if coarse_count is None:
        def bit_body(it, carry):
            tu, cnt_tu = carry
            cand_u = tu | lax.shift_left(jnp.int32(1), 31 - it)
            cnt = count_ge(_key_to_float(cand_u ^ INT_MIN))
            ok = cnt >= k_f
            return jnp.where(ok, cand_u, tu), jnp.where(ok, cnt, cnt_tu)

        lowest = jnp.full(q_shape, -jnp.inf, F32)
        tu, cnt_ge = lax.fori_loop(0, 32, bit_body, (jnp.zeros(q_shape, I32), count_ge(lowest)))
        key = tu ^ INT_MIN
    else:
        def coarse_body(it, pu):
            cand_p = pu | lax.shift_left(jnp.int32(1), 15 - it)
            cnt = coarse_count(_key_to_float(_prefix_key(cand_p)))
            return jnp.where(cnt >= k_f, cand_p, pu)

        pu = lax.fori_loop(0, 16, coarse_body, jnp.zeros(q_shape, I32))
        base = _prefix_key(pu) - 2 ** 16

        def fine_body(it, carry):
            x, cnt_x = carry
            cand_x = x | lax.shift_left(jnp.int32(1), 16 - it)
            cnt = count_ge(_key_to_float(base + cand_x))
            ok = cnt >= k_f
            return jnp.where(ok, cand_x, x), jnp.where(ok, cnt, cnt_x)

        x, cnt_ge = lax.fori_loop(0, 17, fine_body,
                                  (jnp.zeros(q_shape, I32), count_ge(_key_to_float(base))))
        key = base + x

    thr = _key_to_float(key)
    has_ties = jnp.max(cnt_ge) > k_f
    if p_ref is None:
        return thr, has_ties
    p_ref[...] = jnp.full(q_shape, 2 ** 30, I32)

    @pl.when(has_ties)
    def _():
        (n_gt,) = counter([lambda kc, off, cs: jnp.where(kc > thr[:, cs], 1.0, 0.0)])
        need = k_f - n_gt

        def idx_body(it, p):
            bit = lax.shift_left(jnp.int32(1), idx_bits - 1 - it)
            cand = p | bit

            def pred(kc, off, cs):
                idx = off + lax.broadcasted_iota(I32, kc.shape, axis)
                return jnp.where(kc == thr[:, cs], jnp.where(idx < cand[:, cs], 1.0, 0.0), 0.0)

            (cnt,) = counter([pred])
            return jnp.where(cnt < need, cand, p)

        p_ref[...] = lax.fori_loop(0, idx_bits, idx_body, jnp.zeros(q_shape, I32))

    return thr, has_ties


def _selected_bias(kc, off, thr, p_max, axis):
    idx = off + lax.broadcasted_iota(I32, kc.shape, axis)
    tie = jnp.where(kc == thr, jnp.where(idx <= p_max, 1.0, 0.0), 0.0)
    sel = jnp.where(kc > thr, 1.0, tie)
    sel = jnp.where(kc == -jnp.inf, 0.0, sel)
    return jnp.where(sel > 0.5, 0.0, NEG_BIG)


def _index_select_kernel(qi_ref, ki_ref, wit_ref, bias_ref, key_ref, hb_ref, *, tq, k_sel, idx_bits):
    i = pl.program_id(1)
    n_total = bias_ref.shape[1] // tq
    wit = wit_ref[0]

    lane = lax.broadcasted_iota(I32, (tq, LANES), 1)
    qf = qi_ref[0].astype(F32)
    qh = []
    for h in range(H_IDX):
        grp = qf[:, (h // 2) * LANES:(h // 2 + 1) * LANES]
        if h % 2 == 1:
            grp = pltpu.roll(grp, D_IDX, axis=1)
        qh.append(jnp.where(lane < D_IDX, grp, 0.0).astype(BF16))

    def score_body(c, carry):
        off = pl.multiple_of(c * tq, tq)
        kc = jnp.where(lane < D_IDX, ki_ref[0, pl.ds(off, tq), :], 0.0).astype(BF16)
        sc = jnp.zeros((tq, tq), F32)
        for h in range(H_IDX):
            sc = sc + jnp.maximum(_dot_nt(kc, qh[h]), 0.0) * wit[h:h + 1, :]
        kidx = lax.broadcasted_iota(I32, sc.shape, 0) + off
        qidx = lax.broadcasted_iota(I32, sc.shape, 1) + i * tq
        sc = jnp.where(kidx <= qidx, sc, -jnp.inf)
        key_ref[pl.ds(off, tq), :] = sc
        hb_ref[pl.ds(off, tq), :] = sc.astype(BF16)
        return carry

    lax.fori_loop(0, i + 1, score_body, 0)

    counter = _valu_counter(key_ref, i + 1, tq, 0)
    thr, has_ties = _select_topk(key_ref, None, counter, k_sel, idx_bits, axis=0, q_shape=(1, tq),
                                 coarse_count=functools.partial(_count_bf16, hb_ref, i + 1, tq))

    def out_with_ties():
        (n_gt,) = counter([lambda kc, off, cs: jnp.where(kc > thr[:, cs], 1.0, 0.0)])
        need = float(k_sel) - n_gt
        r = lax.broadcasted_iota(I32, (tq, tq), 0)
        col = lax.broadcasted_iota(I32, (tq, tq), 1)
        tri = jnp.where(col <= r, 1.0, 0.0).astype(BF16)

        def out_body(c, seen):
            off = pl.multiple_of(c * tq, tq)
            kc = key_ref[pl.ds(off, tq), :]
            eq = jnp.where(kc == thr, 1.0, 0.0)
            rank = _dot(tri, eq.astype(BF16)) + seen
            sel = jnp.where(kc > thr, 1.0, jnp.where(rank <= need, eq, 0.0))
            sel = jnp.where(kc == -jnp.inf, 0.0, sel)
            bias_ref[0, pl.ds(off, tq), :] = jnp.where(sel > 0.5, 0.0, NEG_BIG).astype(BF16)
            return rank[tq - 1:tq, :]

        lax.fori_loop(0, i + 1, out_body, jnp.zeros((1, tq), F32))

    def out_body_no_ties(c, carry):
        off = pl.multiple_of(c * tq, tq)
        kc = key_ref[pl.ds(off, tq), :]
        keep = jnp.where(kc == -jnp.inf, NEG_BIG, 0.0)
        bias_ref[0, pl.ds(off, tq), :] = jnp.where(kc >= thr, keep, NEG_BIG).astype(BF16)
        return carry

    @pl.when(has_ties)
    def _():
        out_with_ties()

    @pl.when(jnp.logical_not(has_ties))
    def _():
        lax.fori_loop(0, i + 1, out_body_no_ties, 0)

    def fill_body(c, carry):
        off = pl.multiple_of(c * tq, tq)
        bias_ref[0, pl.ds(off, tq), :] = jnp.full((tq, tq), NEG_BIG, BF16)
        return carry

    lax.fori_loop(i + 1, n_total, fill_body, 0)


def _index_select(qi, kiw, wi_t, tq, k_sel):
    b, t, d_qi = qi.shape
    assert d_qi == H_IDX * D_IDX and 2 * D_IDX == LANES and kiw.shape[2] == LANES
    idx_bits = max(1, (t - 1).bit_length())
    return pl.pallas_call(
        functools.partial(_index_select_kernel, tq=tq, k_sel=k_sel, idx_bits=idx_bits),
        grid=(b, t // tq),
        in_specs=[
            pl.BlockSpec((1, tq, d_qi), lambda bi, i: (bi, i, 0)),
            pl.BlockSpec((1, t, LANES), lambda bi, i: (bi, 0, 0)),
            pl.BlockSpec((1, H_IDX, tq), lambda bi, i: (bi, 0, i)),
        ],
        out_specs=pl.BlockSpec((1, t, tq), lambda bi, i: (bi, 0, i)),
        out_shape=jax.ShapeDtypeStruct((b, t, t), BF16),
        scratch_shapes=[pltpu.VMEM((t, tq), F32), pltpu.VMEM((t, tq), BF16)],
        compiler_params=_cparams(2),
        name="index_select",
    )(qi, kiw, wi_t)


def _dsa_attn_kernel(q_ref, k_ref, vt_ref, bias_ref, g_ref, o_ref, kbf, sb, mb, lb, accb,
                     *, tq, group, n_kvs):
    i = pl.program_id(2)

    @pl.when(i == 0)
    def _():
        kbf[...] = k_ref[0].astype(BF16)

    q = q_ref[0]
    q4 = []
    for n in range(n_kvs):
        q4.append(jnp.concatenate(
            [q[:, (n * group + h) * HEAD_DIM:(n * group + h + 1) * HEAD_DIM] for h in range(group)],
            axis=0))
        _flash_init(mb.at[n], lb.at[n], accb.at[n])

    def scores(j, slot):
        off = pl.multiple_of(j * tq, tq)
        for n in range(n_kvs):
            kb = kbf[pl.ds(off, tq), n * HEAD_DIM:(n + 1) * HEAD_DIM]
            sb[n, slot] = _dot_nt(kb, q4[n])

    def softmax_pv(j, slot):
        off = pl.multiple_of(j * tq, tq)
        bias = bias_ref[0, pl.ds(off, tq), :].astype(F32)
        bias_g = jnp.concatenate([bias] * group, axis=1)
        for n in range(n_kvs):
            vtb = vt_ref[n * HEAD_DIM:(n + 1) * HEAD_DIM, pl.ds(off, tq)]
            _flash_step(sb[n, slot] + bias_g, vtb, mb.at[n], lb.at[n], accb.at[n], axis=0)

    scores(0, 0)

    def pair(jj, carry):
        j = 2 * jj
        scores(j + 1, 1)
        softmax_pv(j, 0)
        scores(jnp.minimum(j + 2, i), 0)
        softmax_pv(j + 1, 1)
        return carry

    lax.fori_loop(0, (i + 1) // 2, pair, 0)

    @pl.when(i % 2 == 0)
    def _():
        softmax_pv(i, 0)

    g = g_ref[0]
    for n in range(n_kvs):
        o_t = accb[n] / lb[n]
        for h in range(group):
            sl = slice((n * group + h) * HEAD_DIM, (n * group + h + 1) * HEAD_DIM)
            o_ref[0, :, sl] = (o_t[:, h * tq:(h + 1) * tq].T * g[:, sl]).astype(BF16)


def _dsa_attn(qs, ks, vs_t, bias_t, sgs, tq):
    b, t, d_b = qs.shape
    n_kv = ks.shape[2] // HEAD_DIM
    group = d_b // HEAD_DIM // n_kv
    n_kvs = DSA_KV_HEADS_PER_STEP if n_kv % DSA_KV_HEADS_PER_STEP == 0 else 1
    gw = n_kvs * group * HEAD_DIM
    kw = n_kvs * HEAD_DIM
    qmap = lambda bi, n, i: (bi, i, n)
    kmap = lambda bi, n, i: (bi, 0, n)
    return pl.pallas_call(
        functools.partial(_dsa_attn_kernel, tq=tq, group=group, n_kvs=n_kvs),
        grid=(b, n_kv // n_kvs, t // tq),
        in_specs=[
            pl.BlockSpec((1, tq, gw), qmap),
            pl.BlockSpec((1, t, kw), kmap),
            pl.BlockSpec((kw, t), lambda bi, n, i: (n, bi)),
            pl.BlockSpec((1, t, tq), lambda bi, n, i: (bi, 0, i)),
            pl.BlockSpec((1, tq, gw), qmap),
        ],
        out_specs=pl.BlockSpec((1, tq, gw), qmap),
        out_shape=jax.ShapeDtypeStruct((b, t, d_b), BF16),
        scratch_shapes=[
            pltpu.VMEM((t, kw), BF16),
            pltpu.VMEM((n_kvs, 2, tq, group * tq), F32),
            pltpu.VMEM((n_kvs, 1, group * tq), F32), pltpu.VMEM((n_kvs, 1, group * tq), F32),
            pltpu.VMEM((n_kvs, HEAD_DIM, group * tq), F32),
        ],
        compiler_params=_cparams(3),
        name="dsa_attn",
    )(qs, ks, vs_t, bias_t, sgs)


def _out_kernel(x_ref, a1_ref, a2_ref, p_ref, wo_ref, pw_ref, wg_ref, bg_ref, we_ref, y_ref, *, d_a):
    o = _dot(a1_ref[...], wo_ref[0:d_a, :]) + _dot(a2_ref[...], wo_ref[d_a:, :])
    x1 = x_ref[...] + _rms_rows(o, pw_ref[...])
    z = _dot(x1.astype(BF16), wg_ref[...]) + bg_ref[...]
    gate = 1.0 / (1.0 + jnp.exp(-z))
    y_ref[...] = x1 + gate * _dot(p_ref[...].astype(BF16), we_ref[...])


def _out_proj(x2d, a1, a2, p2d, wo, pw, wg, bg, we, tm):
    m, d = x2d.shape
    d_a = a1.shape[1]
    d_mix = wo.shape[0]
    d_ple = p2d.shape[1]
    row = lambda i: (i, 0)
    fixed = lambda i: (0, 0)
    single = dict(pipeline_mode=pl.Buffered(1))
    return pl.pallas_call(
        functools.partial(_out_kernel, d_a=d_a),
        grid=(m // tm,),
        in_specs=[
            pl.BlockSpec((tm, d), row),
            pl.BlockSpec((tm, d_a), row),
            pl.BlockSpec((tm, d_mix - d_a), row),
            pl.BlockSpec((tm, d_ple), row),
            pl.BlockSpec((d_mix, d), fixed, **single),
            pl.BlockSpec((1, d), fixed),
            pl.BlockSpec((d, d), fixed, **single),
            pl.BlockSpec((1, d), fixed),
            pl.BlockSpec((d_ple, d), fixed, **single),
        ],
        out_specs=pl.BlockSpec((tm, d), row),
        out_shape=jax.ShapeDtypeStruct((m, d), F32),
        compiler_params=_cparams(1),
        name="out_proj",
    )(x2d, a1, a2, p2d, wo, pw, wg, bg, we)


def _sample_scores_kernel(pt_ref, qi_ref, wi_ref, *rest, n_group):
    del pt_ref
    page_refs, out_ref = rest[:n_group], rest[n_group]
    q4 = qi_ref[0]
    w = wi_ref[0]
    for r in range(n_group):
        kp_t = page_refs[r][0].astype(BF16)
        rel = jnp.maximum(_dot(q4, kp_t), 0.0)
        out_ref[0, r:r + 1, :] = jnp.sum(rel * w, axis=0, keepdims=True)


def _sample_scores(page_table_flat, qi4, wi3, cache_idx_t, n_pages, n_group):
    nb = qi4.shape[0]
    page = cache_idx_t.shape[2]

    def page_map(r):
        return lambda bi, g, pt: (pt[bi * n_pages + g * n_group + r], 0, 0)

    grid_spec = pltpu.PrefetchScalarGridSpec(
        num_scalar_prefetch=1,
        grid=(nb, n_pages // n_group),
        in_specs=[
            pl.BlockSpec((1, H_IDX, D_IDX), lambda bi, g, pt: (bi, 0, 0)),
            pl.BlockSpec((1, H_IDX, 1), lambda bi, g, pt: (bi, 0, 0)),
        ] + [pl.BlockSpec((1, D_IDX, page), page_map(r)) for r in range(n_group)],
        out_specs=pl.BlockSpec((1, n_group, page), lambda bi, g, pt: (bi, g, 0)),
    )
    return pl.pallas_call(
        functools.partial(_sample_scores_kernel, n_group=n_group),
        grid_spec=grid_spec,
        out_shape=jax.ShapeDtypeStruct((nb, n_pages, page), F32),
        compiler_params=_cparams(2),
        name="sample_scores",
    )(page_table_flat, qi4, wi3, *([cache_idx_t] * n_group))


def _sample_select_kernel(sc_ref, qi_ref, kit_ref, wi_ref, bias_ref, key_ref, p_ref, *, k_sel, idx_bits):
    rows, s_past = sc_ref.shape
    s_all = key_ref.shape[1]
    prod = qi_ref[...].astype(F32) * kit_ref[...].astype(F32)
    lane = lax.broadcasted_iota(I32, prod.shape, 1)
    wi = wi_ref[...]
    new = jnp.zeros((rows, 1), F32)
    for h in range(H_IDX):
        dot_h = jnp.sum(jnp.where(lane // D_IDX == h, prod, 0.0), axis=-1, keepdims=True)
        new = new + jnp.maximum(dot_h, 0.0) * wi[:, h:h + 1]
    key_ref[:, 0:s_past] = sc_ref[...]
    tail_lane = lax.broadcasted_iota(I32, (rows, s_all - s_past), 1)
    key_ref[:, s_past:s_all] = jnp.where(tail_lane == 0, new, -jnp.inf)
    thr, _ = _select_topk(key_ref, p_ref, _valu_counter(key_ref, 1, s_all, 1), k_sel, idx_bits, axis=1)
    bias_ref[...] = _selected_bias(key_ref[...], 0, thr, p_ref[...], 1)


def _sample_select(scores2d, qi, ki_tiled, wi, k_sel):
    rows, s_past = scores2d.shape
    s_all = s_past + LANES
    idx_bits = max(1, (s_all - 1).bit_length())
    return pl.pallas_call(
        functools.partial(_sample_select_kernel, k_sel=k_sel, idx_bits=idx_bits),
        out_shape=jax.ShapeDtypeStruct((rows, s_all), F32),
        scratch_shapes=[pltpu.VMEM((rows, s_all), F32), pltpu.VMEM((rows, 1), I32)],
        compiler_params=pltpu.CompilerParams(vmem_limit_bytes=VMEM_LIMIT),
        name="sample_select",
    )(scores2d, qi, ki_tiled, wi)


def _head_rows(row, n_rows, rows_per_head):
    return jnp.concatenate(
        [row[:, (c // rows_per_head) * HEAD_DIM:(c // rows_per_head + 1) * HEAD_DIM]
         for c in range(n_rows)], axis=0)


def _sample_attn_kernel(pt_ref, qa_ref, qs_ref, kan_ref, van_ref, ksn_ref, vsn_ref, bnew_ref,
                        ga_ref, gs_ref, subw_ref, lamp_ref, *rest, n_pg, n_ha, n_hb, group, lam_init):
    del pt_ref
    kd_refs, vd_refs = rest[0:n_pg], rest[n_pg:2 * n_pg]
    ks_refs, vs_refs = rest[2 * n_pg:3 * n_pg], rest[3 * n_pg:4 * n_pg]
    bias_ref, oa_ref, ob_ref, qd, qsb, md, ld, accd, ms, ls, accs = rest[4 * n_pg:]
    p = pl.program_id(1)
    n_p = pl.num_programs(1)
    n_kv = n_hb // group

    @pl.when(p == 0)
    def _():
        qd_f = _head_rows(qa_ref[0].astype(F32), 2 * n_ha, 2)
        rd = lax.broadcasted_iota(I32, qd_f.shape, 0)
        ln = lax.broadcasted_iota(I32, qd_f.shape, 1)
        qd_f = jnp.where(ln // DH_A == rd % 2, qd_f, 0.0)
        qd[...] = qd_f.astype(BF16)
        qs_f = _head_rows(qs_ref[0].astype(F32), n_hb, 1)
        qsb[...] = qs_f.astype(BF16)
        kan = _head_rows(kan_ref[0].astype(BF16).astype(F32), 2 * n_ha, 2)
        van = _head_rows(van_ref[0].astype(BF16).astype(F32), 2 * n_ha, 2)
        md[...] = jnp.sum(qd_f * kan, axis=-1, keepdims=True)
        ld[...] = jnp.ones(ld.shape, F32)
        accd[...] = van
        ksn = _head_rows(ksn_ref[0].astype(BF16).astype(F32), n_hb, group)
        vsn = _head_rows(vsn_ref[0].astype(BF16).astype(F32), n_hb, group)
        ms[...] = jnp.sum(qs_f * ksn, axis=-1, keepdims=True) + bnew_ref[0]
        ls[...] = jnp.ones(ls.shape, F32)
        accs[...] = vsn

    sd_list, vd_list, ss_list, vs_list = [], [], [], []
    for r in range(n_pg):
        kd = kd_refs[r][0].astype(BF16)
        sd = _dot_nt(qd[...], kd)
        rq = lax.broadcasted_iota(I32, sd.shape, 0)
        ck = lax.broadcasted_iota(I32, sd.shape, 1)
        sd_list.append(jnp.where(ck % n_ha == rq // 2, sd, NEG_BIG))
        vd_list.append(vd_refs[r][0].astype(BF16))
        ksp = ks_refs[r][0].astype(BF16)
        ss = _dot_nt(qsb[...], ksp) + bias_ref[0, r]
        rq = lax.broadcasted_iota(I32, ss.shape, 0)
        ck = lax.broadcasted_iota(I32, ss.shape, 1)
        ss_list.append(jnp.where(ck % n_kv == rq // group, ss, NEG_BIG))
        vs_list.append(vs_refs[r][0].astype(BF16))
    _flash_step_multi(sd_list, vd_list, md, ld, accd)
    _flash_step_multi(ss_list, vs_list, ms, ls, accs)

    @pl.when(p == n_p - 1)
    def _():
        lam = _lambda_full(lamp_ref[...], lam_init)
        od = accd[...] / ld[...]
        ga = ga_ref[0]
        subw = subw_ref[...]
        for h in range(n_ha):
            sl = slice(h * HEAD_DIM, (h + 1) * HEAD_DIM)
            o = od[2 * h:2 * h + 1, :] - lam * od[2 * h + 1:2 * h + 2, :]
            o = _rms_rows(o, subw) * (1.0 - lam_init)
            oa_ref[0, :, sl] = (o * ga[:, sl]).astype(BF16)
        os_ = accs[...] / ls[...]
        gs = gs_ref[0]
        for h in range(n_hb):
            sl = slice(h * HEAD_DIM, (h + 1) * HEAD_DIM)
            ob_ref[0, :, sl] = (os_[h:h + 1, :] * gs[:, sl]).astype(BF16)


def _sample_attn(page_table_flat, qa, qs, ka_new, va_new, ks_new, vs_new, bias_new, sga, sgs,
                 subw, lamp, cdk, cdv, csk, csv, bias_pages, n_pages, lam_init):
    nb, _, d_a = qa.shape
    d_b = qs.shape[2]
    d_kv = ks_new.shape[2]
    n_ha = d_a // HEAD_DIM
    n_hb = d_b // HEAD_DIM
    n_kv = d_kv // HEAD_DIM
    group = n_hb // n_kv
    rows_d = cdk.shape[1]
    rows_s = csk.shape[1]
    per_b = lambda bi, p, pt: (bi, 0, 0)
    fixed = lambda bi, p, pt: (0, 0)
    n_pg = SAMPLE_PAGES_PER_STEP if n_pages % SAMPLE_PAGES_PER_STEP == 0 else 1

    def paged(r):
        return lambda bi, p, pt: (pt[bi * n_pages + p * n_pg + r], 0, 0)

    grid_spec = pltpu.PrefetchScalarGridSpec(
        num_scalar_prefetch=1,
        grid=(nb, n_pages // n_pg),
        in_specs=[
            pl.BlockSpec((1, 1, d_a), per_b),
            pl.BlockSpec((1, 1, d_b), per_b),
            pl.BlockSpec((1, 1, d_a), per_b),
            pl.BlockSpec((1, 1, d_a), per_b),
            pl.BlockSpec((1, 1, d_kv), per_b),
            pl.BlockSpec((1, 1, d_kv), per_b),
            pl.BlockSpec((1, 1, 1), per_b),
            pl.BlockSpec((1, 1, d_a), per_b),
            pl.BlockSpec((1, 1, d_b), per_b),
            pl.BlockSpec((1, HEAD_DIM), fixed),
            pl.BlockSpec((4, DH_A), fixed),
        ] + [pl.BlockSpec((1, rows_d, HEAD_DIM), paged(r)) for r in range(n_pg)] * 2
        + [pl.BlockSpec((1, rows_s, HEAD_DIM), paged(r)) for r in range(n_pg)] * 2
        + [pl.BlockSpec((1, n_pg, 1, rows_s), lambda bi, p, pt: (bi, p, 0, 0))],
        out_specs=[pl.BlockSpec((1, 1, d_a), per_b), pl.BlockSpec((1, 1, d_b), per_b)],
        scratch_shapes=[
            pltpu.VMEM((2 * n_ha, HEAD_DIM), BF16), pltpu.VMEM((n_hb, HEAD_DIM), BF16),
            pltpu.VMEM((2 * n_ha, 1), F32), pltpu.VMEM((2 * n_ha, 1), F32),
            pltpu.VMEM((2 * n_ha, HEAD_DIM), F32),
            pltpu.VMEM((n_hb, 1), F32), pltpu.VMEM((n_hb, 1), F32), pltpu.VMEM((n_hb, HEAD_DIM), F32),
        ],
    )
    return pl.pallas_call(
        functools.partial(_sample_attn_kernel, n_pg=n_pg, n_ha=n_ha, n_hb=n_hb, group=group,
                          lam_init=lam_init),
        grid_spec=grid_spec,
        out_shape=[jax.ShapeDtypeStruct((nb, 1, d_a), BF16), jax.ShapeDtypeStruct((nb, 1, d_b), BF16)],
        compiler_params=_cparams(2),
        name="sample_attn",
    )(page_table_flat, qa, qs, ka_new, va_new, ks_new, vs_new, bias_new, sga, sgs, subw, lamp,
      *([cdk] * n_pg + [cdv] * n_pg + [csk] * n_pg + [csv] * n_pg), bias_pages)


def _row_tile(m, pref):
    return pref if m % pref == 0 else m


def _inproj_all(x2d, pos, wts, tm, emit_vt):
    tabs64 = _rope_tables(pos, DH_A)
    tabs128 = _rope_tables(pos, HEAD_DIM)
    d_a, d_b, d_kv, d_qi = wts["d_a"], wts["d_b"], wts["d_kv"], wts["d_qi"]
    outs_a = _inproj_a(x2d, wts["pre_w"], wts["w_a"], tabs64, tm, d_a, emit_vt)
    outs_b = _inproj_b(x2d, wts["pre_w"], wts["w_b"], wts["w_tail"], tabs128, tabs64, tm,
                       d_b, d_kv, d_qi, emit_vt)
    return outs_a, outs_b


def kernel(x_prompt, x_sample, p_prompt, p_sample, cache_diff_k, cache_diff_v, cache_dsa_k, cache_dsa_v, cache_idx_k, page_table, pre_norm_w, post_norm_w, w_in, lam_q1, lam_k1, lam_q2, lam_k2, diff_norm_w, w_out, w_ple_gate, b_ple_gate, w_ple_proj):
    depth = w_in.shape[0]
    assert depth == 1, "single-layer stack only"
    bsz, t_p, d = x_prompt.shape
    nb, t_s, _ = x_sample.shape
    assert t_s == 1, "one new token per sample sequence"
    n_pages = page_table.shape[1]
    n_pool = cache_diff_k.shape[1]
    page = cache_diff_k.shape[2]
    n_ha = cache_diff_k.shape[3]
    n_kv = cache_dsa_k.shape[3]
    d_a = n_ha * HEAD_DIM
    d_kv = n_kv * HEAD_DIM
    d_b = d - d_a
    d_qi = H_IDX * D_IDX
    past_len = n_pages * page
    lam_init = 0.8 - 0.6 * math.exp(-0.3 * 0)
    n_in = w_in.shape[2]

    n_main = 4 * d_a + 2 * d_b + 2 * d_kv + d_qi
    assert n_in - n_main == D_IDX + H_IDX
    w_a = jnp.swapaxes(w_in[0, :, :4 * d_a], 0, 1).astype(BF16)
    w_b = jnp.swapaxes(w_in[0, :, 4 * d_a:n_main], 0, 1).astype(BF16)
    w_tail = jnp.pad(jnp.swapaxes(w_in[0, :, n_main:], 0, 1),
                     ((0, LANES - (n_in - n_main)), (0, 0))).astype(BF16)
    wts = dict(pre_w=pre_norm_w[0][None, :], w_a=w_a, w_b=w_b, w_tail=w_tail,
               d_a=d_a, d_b=d_b, d_kv=d_kv, d_qi=d_qi)
    wo = w_out[0].astype(BF16)
    wg = w_ple_gate[0].astype(BF16)
    we = w_ple_proj[0].astype(BF16)
    pw = post_norm_w[0][None, :]
    bg = b_ple_gate[0][None, :]
    subw = diff_norm_w[0][None, :]
    lamp = jnp.stack([lam_q1[0], lam_k1[0], lam_q2[0], lam_k2[0]], axis=0)

    m_p = bsz * t_p
    tm = _row_tile(t_p, ROW_TILE)
    tq_diff = _row_tile(t_p, DIFF_TILE)
    tq_dsa = _row_tile(t_p, DSA_TILE)
    tq_idx = _row_tile(t_p, IDX_TILE)
    xp2 = x_prompt.reshape(m_p, d)
    pos_p = jnp.arange(t_p, dtype=I32)
    (qa, ka, va, sga, va_t), (qs, ks, vs, sgs, qi, kiw, vs_t) = _inproj_all(xp2, pos_p, wts, tm, True)
    r3 = lambda a: a.reshape(bsz, t_p, a.shape[-1])
    a1 = _diff_attn(r3(qa), r3(ka), va_t, r3(sga), subw, lamp, lam_init, tq_diff)
    ki = kiw[:, :D_IDX]
    wi = kiw[:, D_IDX:D_IDX + H_IDX]
    wi_t = wi.reshape(bsz, t_p, H_IDX).transpose(0, 2, 1)
    k_sel_p = min(TOPK_MAX, t_p // 4)
    bias_t = _index_select(r3(qi), r3(kiw), wi_t, tq_idx, k_sel_p)
    a2 = _dsa_attn(r3(qs), r3(ks), vs_t, bias_t, r3(sgs), tq_dsa)
    y_p = _out_proj(xp2, a1.reshape(m_p, d_a), a2.reshape(m_p, d_b), p_prompt[0].reshape(m_p, -1),
                    wo, pw, wg, bg, we, tm)

    xs2 = x_sample.reshape(nb, d)
    pos_s = jnp.full((nb,), past_len, dtype=I32)
    (qa_s, ka_s, va_s, sga_s), (qs_s, ks_s, vs_s, sgs_s, qi_s, kiw_s) = _inproj_all(xs2, pos_s, wts, nb, False)
    ki_s = kiw_s[:, :D_IDX]
    wi_s = kiw_s[:, D_IDX:D_IDX + H_IDX]
    pt_flat = page_table.reshape(-1)
    n_group = SCORE_PAGES_PER_STEP if n_pages % SCORE_PAGES_PER_STEP == 0 else 1
    cache_idx_t = jnp.swapaxes(cache_idx_k.reshape(n_pool, page, D_IDX), 1, 2)
    scores = _sample_scores(pt_flat, qi_s.reshape(nb, H_IDX, D_IDX), wi_s.reshape(nb, H_IDX, 1),
                            cache_idx_t, n_pages, n_group)
    k_sel_s = min(TOPK_MAX, (past_len + t_s) // 4)
    bias_s = _sample_select(scores.reshape(nb, past_len), qi_s,
                            jnp.tile(ki_s.astype(BF16), (1, H_IDX)), wi_s, k_sel_s)
    bias_pages = jnp.repeat(bias_s[:, :past_len], n_kv, axis=1).reshape(nb, n_pages, 1, page * n_kv)
    bias_new = bias_s[:, past_len:past_len + 1].reshape(nb, 1, 1)
    e3 = lambda a: a.reshape(nb, 1, a.shape[-1])
    a1_s, a2_s = _sample_attn(
        pt_flat, e3(qa_s), e3(qs_s), e3(ka_s), e3(va_s), e3(ks_s), e3(vs_s), bias_new,
        e3(sga_s), e3(sgs_s), subw, lamp,
        cache_diff_k.reshape(n_pool, page * n_ha, HEAD_DIM), cache_diff_v.reshape(n_pool, page * n_ha, HEAD_DIM),
        cache_dsa_k.reshape(n_pool, page * n_kv, HEAD_DIM), cache_dsa_v.reshape(n_pool, page * n_kv, HEAD_DIM),
        bias_pages, n_pages, lam_init)
    y_s = _out_proj(xs2, a1_s.reshape(nb, d_a), a2_s.reshape(nb, d_b), p_sample[0].reshape(nb, -1),
                    wo, pw, wg, bg, we, nb)

    return (
        y_p.reshape(bsz, t_p, d), y_s.reshape(nb, t_s, d),
        ka.reshape(1, bsz, t_p, n_ha, HEAD_DIM), va.reshape(1, bsz, t_p, n_ha, HEAD_DIM),
        ks.reshape(1, bsz, t_p, n_kv, HEAD_DIM), vs.reshape(1, bsz, t_p, n_kv, HEAD_DIM),
        ki.reshape(1, bsz, t_p, D_IDX),
        ka_s.reshape(1, nb, t_s, n_ha, HEAD_DIM), va_s.reshape(1, nb, t_s, n_ha, HEAD_DIM),
        ks_s.reshape(1, nb, t_s, n_kv, HEAD_DIM), vs_s.reshape(1, nb, t_s, n_kv, HEAD_DIM),
        ki_s.reshape(1, nb, t_s, D_IDX),
    )
```

```python
import functools
import math

import jax
import jax.numpy as jnp
from jax import lax
from jax.experimental import pallas as pl
from jax.experimental.pallas import tpu as pltpu

F32 = jnp.float32
BF16 = jnp.bfloat16
I32 = jnp.int32

LANES = 128
SUBLANES = 8
HEAD_DIM = 128
DH_A = HEAD_DIM // 2
D_IDX = 64
H_IDX = 4
TOPK_MAX = 256
ROPE_THETA = 500000.0
ROPE_FRAC = 4
RMS_EPS = 1e-6
NEG_BIG = -1e30
INT_MIN = -(2 ** 31)
NEG_INF_KEY = INT_MIN + 0x7FFFFF
VMEM_LIMIT = 56 * 1024 * 1024
ROW_TILE = 256
LOG2E = math.log2(math.e)
DIFF_TILE = 512
DIFF_HEADS_PER_STEP = 2
DSA_TILE = 256
DSA_KV_HEADS_PER_STEP = 1
IDX_TILE = 512
TIE_BLOCK = 256
SAMPLE_PAGES_PER_STEP = 8
SCORE_PAGES_PER_STEP = 32


def _cparams(n_axes):
    return pltpu.CompilerParams(
        dimension_semantics=("arbitrary",) * n_axes, vmem_limit_bytes=VMEM_LIMIT)


def _dot(a, b):
    return jnp.dot(a, b, preferred_element_type=F32)


def _dot_nt(a, b):
    return lax.dot_general(a, b, (((1,), (1,)), ((), ())), preferred_element_type=F32)


def _rope_tables(pos, d):
    r = d // ROPE_FRAC
    half = r // 2
    inv = ROPE_THETA ** (-(2.0 / r) * jnp.arange(half, dtype=F32))
    ang = pos.astype(F32)[:, None] * inv[None, :]
    cos, sin = jnp.cos(ang), jnp.sin(ang)
    t = pos.shape[0]
    ones = jnp.ones((t, d - r), F32)
    zeros_h = jnp.zeros((t, half), F32)
    zeros_r = jnp.zeros((t, d - r), F32)
    c = jnp.concatenate([cos, cos, ones], axis=-1)
    sm = jnp.concatenate([-sin, zeros_h, zeros_r], axis=-1)
    sp = jnp.concatenate([zeros_h, sin, zeros_r], axis=-1)
    rep = LANES // d
    return tuple(jnp.tile(a, (1, rep)) for a in (c, sm, sp)), half


def _rope_chunk(z, c, sm, sp, half):
    return (z * c + pltpu.roll(z, LANES - half, axis=1) * sm
            + pltpu.roll(z, half, axis=1) * sp)


def _silu(z):
    return z * (1.0 / (1.0 + jnp.exp(-z)))


def _rms_rows(x, w):
    return x * lax.rsqrt(jnp.mean(x * x, axis=-1, keepdims=True) + RMS_EPS) * w


def _inproj_a_kernel(x_ref, nw_ref, w_ref, c_ref, sm_ref, sp_ref,
                     qa_ref, ka_ref, va_ref, ga_ref, *maybe_vt_ref, half, d_a):
    h = _rms_rows(x_ref[...], nw_ref[...]).astype(BF16)
    c, sm, sp = c_ref[...], sm_ref[...], sp_ref[...]
    n_chunks = d_a // LANES
    zq = _dot_nt(h, w_ref[0:d_a, :])
    for j in range(n_chunks):
        sl = slice(j * LANES, (j + 1) * LANES)
        qa_ref[:, sl] = (_rope_chunk(zq[:, sl], c, sm, sp, half) * (LOG2E / math.sqrt(DH_A))).astype(BF16)
    zk = _dot_nt(h, w_ref[d_a:2 * d_a, :])
    for j in range(n_chunks):
        sl = slice(j * LANES, (j + 1) * LANES)
        ka_ref[:, sl] = _rope_chunk(zk[:, sl], c, sm, sp, half)
    zv = _dot_nt(h, w_ref[2 * d_a:3 * d_a, :])
    va_ref[...] = zv
    if maybe_vt_ref:
        maybe_vt_ref[0][...] = zv.T.astype(BF16)
    ga_ref[...] = _silu(_dot_nt(h, w_ref[3 * d_a:4 * d_a, :]))


def _inproj_a(x2d, nw, w_a, tabs64, tm, d_a, emit_vt):
    m, d = x2d.shape
    (c, sm, sp), half = tabs64
    t_blocks = c.shape[0] // tm
    row = lambda i: (i, 0)
    tab = lambda i: (i % t_blocks, 0)
    fixed = lambda i: (0, 0)
    out_specs = [pl.BlockSpec((tm, d_a), row)] * 4
    out_shape = [
        jax.ShapeDtypeStruct((m, d_a), BF16),
        jax.ShapeDtypeStruct((m, d_a), F32),
        jax.ShapeDtypeStruct((m, d_a), F32),
        jax.ShapeDtypeStruct((m, d_a), F32),
    ]
    if emit_vt:
        out_specs.append(pl.BlockSpec((d_a, tm), lambda i: (0, i)))
        out_shape.append(jax.ShapeDtypeStruct((d_a, m), BF16))
    return pl.pallas_call(
        functools.partial(_inproj_a_kernel, half=half, d_a=d_a),
        grid=(m // tm,),
        in_specs=[
            pl.BlockSpec((tm, d), row),
            pl.BlockSpec((1, d), fixed),
            pl.BlockSpec((4 * d_a, d), fixed, pipeline_mode=pl.Buffered(1)),
            pl.BlockSpec((tm, LANES), tab),
            pl.BlockSpec((tm, LANES), tab),
            pl.BlockSpec((tm, LANES), tab),
        ],
        out_specs=out_specs,
        out_shape=out_shape,
        compiler_params=_cparams(1),
        name="inproj_a",
    )(x2d, nw, w_a, c, sm, sp)


def _inproj_b_kernel(x_ref, nw_ref, w_ref, wt_ref, c128_ref, sm128_ref, sp128_ref,
                     c64_ref, sm64_ref, sp64_ref,
                     qs_ref, ks_ref, vs_ref, gs_ref, qi_ref, kiw_ref, *maybe_vt_ref,
                     half128, half64, d_b, d_kv, d_qi):
    h = _rms_rows(x_ref[...], nw_ref[...]).astype(BF16)
    c128, sm128, sp128 = c128_ref[...], sm128_ref[...], sp128_ref[...]
    c64, sm64, sp64 = c64_ref[...], sm64_ref[...], sp64_ref[...]
    o = 0
    zq = _dot_nt(h, w_ref[o:o + d_b, :])
    for j in range(d_b // LANES):
        sl = slice(j * LANES, (j + 1) * LANES)
        qs_ref[:, sl] = (_rope_chunk(zq[:, sl], c128, sm128, sp128, half128)
                         * (LOG2E / math.sqrt(HEAD_DIM))).astype(BF16)
    o += d_b
    zk = _dot_nt(h, w_ref[o:o + d_kv, :])
    for j in range(d_kv // LANES):
        sl = slice(j * LANES, (j + 1) * LANES)
        ks_ref[:, sl] = _rope_chunk(zk[:, sl], c128, sm128, sp128, half128)
    o += d_kv
    zv = _dot_nt(h, w_ref[o:o + d_kv, :])
    vs_ref[...] = zv
    if maybe_vt_ref:
        maybe_vt_ref[0][...] = zv.T.astype(BF16)
    o += d_kv
    gs_ref[...] = _silu(_dot_nt(h, w_ref[o:o + d_b, :]))
    o += d_b
    zi = _dot_nt(h, w_ref[o:o + d_qi, :])
    for j in range(d_qi // LANES):
        sl = slice(j * LANES, (j + 1) * LANES)
        qi_ref[:, sl] = _rope_chunk(zi[:, sl], c64, sm64, sp64, half64).astype(BF16)
    zkw = _dot_nt(h, wt_ref[...])
    lane = lax.broadcasted_iota(I32, zkw.shape, 1)
    kiw_ref[...] = jnp.where(lane < D_IDX, _rope_chunk(zkw, c64, sm64, sp64, half64), zkw)


def _inproj_b(x2d, nw, w_b, w_tail, tabs128, tabs64, tm, d_b, d_kv, d_qi, emit_vt):
    m, d = x2d.shape
    (c128, sm128, sp128), half128 = tabs128
    (c64, sm64, sp64), half64 = tabs64
    t_blocks = c128.shape[0] // tm
    row = lambda i: (i, 0)
    tab = lambda i: (i % t_blocks, 0)
    fixed = lambda i: (0, 0)
    n_b = w_b.shape[0]
    out_specs = [
        pl.BlockSpec((tm, d_b), row),
        pl.BlockSpec((tm, d_kv), row),
        pl.BlockSpec((tm, d_kv), row),
        pl.BlockSpec((tm, d_b), row),
        pl.BlockSpec((tm, d_qi), row),
        pl.BlockSpec((tm, LANES), row),
    ]
    out_shape = [
        jax.ShapeDtypeStruct((m, d_b), BF16),
        jax.ShapeDtypeStruct((m, d_kv), F32),
        jax.ShapeDtypeStruct((m, d_kv), F32),
        jax.ShapeDtypeStruct((m, d_b), F32),
        jax.ShapeDtypeStruct((m, d_qi), BF16),
        jax.ShapeDtypeStruct((m, LANES), F32),
    ]
    if emit_vt:
        out_specs.append(pl.BlockSpec((d_kv, tm), lambda i: (0, i)))
        out_shape.append(jax.ShapeDtypeStruct((d_kv, m), BF16))
    return pl.pallas_call(
        functools.partial(_inproj_b_kernel, half128=half128, half64=half64,
                          d_b=d_b, d_kv=d_kv, d_qi=d_qi),
        grid=(m // tm,),
        in_specs=[
            pl.BlockSpec((tm, d), row),
            pl.BlockSpec((1, d), fixed),
            pl.BlockSpec((n_b, d), fixed, pipeline_mode=pl.Buffered(1)),
            pl.BlockSpec((LANES, d), fixed),
        ] + [pl.BlockSpec((tm, LANES), tab)] * 6,
        out_specs=out_specs,
        out_shape=out_shape,
        compiler_params=_cparams(1),
        name="inproj_b",
    )(x2d, nw, w_b, w_tail, c128, sm128, sp128, c64, sm64, sp64)


def _flash_init(m_ref, l_ref, acc_ref):
    m_ref[...] = jnp.full(m_ref.shape, NEG_BIG, F32)
    l_ref[...] = jnp.zeros(l_ref.shape, F32)
    acc_ref[...] = jnp.zeros(acc_ref.shape, F32)


def _flash_step(s, v_bf, m_ref, l_ref, acc_ref, axis=1):
    m_prev = m_ref[...]
    m_new = jnp.maximum(m_prev, jnp.max(s, axis=axis, keepdims=True))
    alpha = jnp.exp2(m_prev - m_new)
    p = jnp.exp2(s - m_new)
    l_ref[...] = alpha * l_ref[...] + jnp.sum(p, axis=axis, keepdims=True)
    pv = _dot(p.astype(BF16), v_bf) if axis == 1 else _dot(v_bf, p.astype(BF16))
    acc_ref[...] = alpha * acc_ref[...] + pv
    m_ref[...] = m_new


def _flash_step_multi(s_list, v_list, m_ref, l_ref, acc_ref):
    m_prev = m_ref[...]
    m_new = m_prev
    for s in s_list:
        m_new = jnp.maximum(m_new, jnp.max(s, axis=1, keepdims=True))
    alpha = jnp.exp2(m_prev - m_new)
    l_new = alpha * l_ref[...]
    acc = alpha * acc_ref[...]
    for s, v_bf in zip(s_list, v_list):
        p = jnp.exp2(s - m_new)
        l_new = l_new + jnp.sum(p, axis=1, keepdims=True)
        acc = acc + _dot(p.astype(BF16), v_bf)
    l_ref[...] = l_new
    acc_ref[...] = acc
    m_ref[...] = m_new


def _lambda_full(lamp, lam_init):
    s1 = jnp.sum(lamp[0:1, :] * lamp[1:2, :], axis=-1, keepdims=True)
    s2 = jnp.sum(lamp[2:3, :] * lamp[3:4, :], axis=-1, keepdims=True)
    return jnp.exp(s1) - jnp.exp(s2) + lam_init


def _diff_attn_kernel(q_ref, k_ref, vt_ref, g_ref, subw_ref, lamp_ref, o_ref,
                      kbf, sb, mb, lb, accb, *, tq, n_hs, lam_init):
    i = pl.program_id(2)

    @pl.when(i == 0)
    def _():
        kbf[...] = k_ref[0].astype(BF16)

    q = q_ref[0]
    lane = lax.broadcasted_iota(I32, (tq, HEAD_DIM), 1)
    qm = []
    for h in range(n_hs):
        qh = q[:, h * HEAD_DIM:(h + 1) * HEAD_DIM]
        zero = jnp.zeros_like(qh)
        qm.append((jnp.where(lane < DH_A, qh, zero), jnp.where(lane >= DH_A, qh, zero)))
        for mp in range(2):
            _flash_init(mb.at[h, mp], lb.at[h, mp], accb.at[h, mp])

    def scores(j, slot):
        off = pl.multiple_of(j * tq, tq)
        for h in range(n_hs):
            kb = kbf[pl.ds(off, tq), h * HEAD_DIM:(h + 1) * HEAD_DIM]
            for mp in range(2):
                sb[h, mp, slot] = _dot_nt(kb, qm[h][mp])

    def softmax_pv(j, slot, masked):
        off = pl.multiple_of(j * tq, tq)
        for h in range(n_hs):
            vtb = vt_ref[h * HEAD_DIM:(h + 1) * HEAD_DIM, pl.ds(off, tq)]
            for mp in range(2):
                s = sb[h, mp, slot]
                if masked:
                    kidx = lax.broadcasted_iota(I32, s.shape, 0)
                    qidx = lax.broadcasted_iota(I32, s.shape, 1)
                    s = jnp.where(kidx <= qidx, s, NEG_BIG)
                _flash_step(s, vtb, mb.at[h, mp], lb.at[h, mp], accb.at[h, mp], axis=0)

    scores(0, 0)

    def pair(jj, carry):
        j = 2 * jj
        scores(j + 1, 1)
        softmax_pv(j, 0, False)
        scores(j + 2, 0)
        softmax_pv(j + 1, 1, False)
        return carry

    lax.fori_loop(0, i // 2, pair, 0)

    @pl.when(i % 2 == 1)
    def _():
        scores(i, 1)
        softmax_pv(i - 1, 0, False)
        softmax_pv(i, 1, True)

    @pl.when(i % 2 == 0)
    def _():
        softmax_pv(i, 0, True)

    lam = _lambda_full(lamp_ref[...], lam_init)
    g = g_ref[0]
    for h in range(n_hs):
        sl = slice(h * HEAD_DIM, (h + 1) * HEAD_DIM)
        o_t = accb[h, 0] / lb[h, 0] - lam * (accb[h, 1] / lb[h, 1])
        o = _rms_rows(o_t.T, subw_ref[...]) * (1.0 - lam_init)
        o_ref[0, :, sl] = (o * g[:, sl]).astype(BF16)


def _diff_attn(qa, ka, va_t, sga, subw, lamp, lam_init, tq):
    b, t, d_a = qa.shape
    n_h = d_a // HEAD_DIM
    n_hs = DIFF_HEADS_PER_STEP if n_h % DIFF_HEADS_PER_STEP == 0 else 1
    hw = n_hs * HEAD_DIM
    qmap = lambda bi, h, i: (bi, i, h)
    kmap = lambda bi, h, i: (bi, 0, h)
    fixed = lambda bi, h, i: (0, 0)
    return pl.pallas_call(
        functools.partial(_diff_attn_kernel, tq=tq, n_hs=n_hs, lam_init=lam_init),
        grid=(b, n_h // n_hs, t // tq),
        in_specs=[
            pl.BlockSpec((1, tq, hw), qmap),
            pl.BlockSpec((1, t, hw), kmap),
            pl.BlockSpec((hw, t), lambda bi, h, i: (h, bi)),
            pl.BlockSpec((1, tq, hw), qmap),
            pl.BlockSpec((1, HEAD_DIM), fixed),
            pl.BlockSpec((4, DH_A), fixed),
        ],
        out_specs=pl.BlockSpec((1, tq, hw), qmap),
        out_shape=jax.ShapeDtypeStruct((b, t, d_a), BF16),
        scratch_shapes=[
            pltpu.VMEM((t, hw), BF16),
            pltpu.VMEM((n_hs, 2, 2, tq, tq), F32),
            pltpu.VMEM((n_hs, 2, 1, tq), F32), pltpu.VMEM((n_hs, 2, 1, tq), F32),
            pltpu.VMEM((n_hs, 2, HEAD_DIM, tq), F32),
        ],
        compiler_params=_cparams(3),
        name="diff_attn",
    )(qa, ka, va_t, sga, subw, lamp)


def _key_to_float(key):
    key = jnp.maximum(key, NEG_INF_KEY)
    return lax.bitcast_convert_type(key ^ ((key >> 31) & 0x7FFFFFFF), F32)


def _fold(m, axis):
    if axis == 1:
        n = m.shape[1] // LANES
        acc = m[:, 0:LANES]
        for j in range(1, n):
            acc = acc + m[:, j * LANES:(j + 1) * LANES]
        return acc
    return jnp.sum(m.reshape(m.shape[0] // SUBLANES, SUBLANES, m.shape[1]), axis=0)


def _count(key_ref, n_chunks, chunk, pred, axis):
    other = key_ref.shape[1 - axis]
    acc_shape = (other, LANES) if axis == 1 else (SUBLANES, other)

    def body(c, acc):
        off = pl.multiple_of(c * chunk, chunk)
        kc = key_ref[:, pl.ds(off, chunk)] if axis == 1 else key_ref[pl.ds(off, chunk), :]
        return acc + _fold(pred(kc, off, slice(None)), axis)

    acc = lax.fori_loop(0, n_chunks, body, jnp.zeros(acc_shape, F32))
    return jnp.sum(acc, axis=axis, keepdims=True)


def _valu_counter(key_ref, n_chunks, chunk, axis):
    return lambda preds: [_count(key_ref, n_chunks, chunk, p, axis) for p in preds]


def _prefix_key(pu):
    key = lax.shift_left(pu, 16) ^ INT_MIN
    return key | ((key >> 31) & 0xFFFF)


def _count_bf16(hb_ref, n_chunks, chunk, cand):
    tq = hb_ref.shape[1]
    pack = 2 * SUBLANES
    candb = jnp.broadcast_to(cand.astype(BF16), (pack, tq))
    one, zero = jnp.ones((), BF16), jnp.zeros((), BF16)

    def body(c, acc):
        off = pl.multiple_of(c * chunk, chunk)
        kc = hb_ref[pl.ds(off, chunk), :].reshape(chunk // pack, pack, tq)
        part = jnp.where(kc[0] >= candb, one, zero)
        for r in range(1, chunk // pack):
            part = part + jnp.where(kc[r] >= candb, one, zero)
        return acc + part.astype(F32)

    acc = lax.fori_loop(0, n_chunks, body, jnp.zeros((pack, tq), F32))
    return jnp.sum(acc, axis=0, keepdims=True)


def _select_topk(counter, k_sel, axis, q_shape, coarse_count=None, tie_index=None):
    k_f = float(k_sel)

    def count_ge(cand):
        (cnt,) = counter([lambda kc, off, cs: jnp.where(kc >= cand[:, cs], 1.0, 0.0)])
        return cnt

    if coarse_count is None:
        def bit_body(it, carry):
            tu, cnt_tu = carry
            cand_u = tu | lax.shift_left(jnp.int32(1), 31 - it)
            cnt = count_ge(_key_to_float(cand_u ^ INT_MIN))
            ok = cnt >= k_f
            return jnp.where(ok, cand_u, tu), jnp.where(ok, cnt, cnt_tu)

        lowest = jnp.full(q_shape, -jnp.inf, F32)
        tu, cnt_ge = lax.fori_loop(0, 32, bit_body, (jnp.zeros(q_shape, I32), count_ge(lowest)))
        key = tu ^ INT_MIN
    else:
        def coarse_body(it, pu):
            cand_p = pu | lax.shift_left(jnp.int32(1), 15 - it)
            cnt = coarse_count(_key_to_float(_prefix_key(cand_p)))
            return jnp.where(cnt >= k_f, cand_p, pu)

        pu = lax.fori_loop(0, 16, coarse_body, jnp.zeros(q_shape, I32))
        base = _prefix_key(pu) - 2 ** 16

        def fine_body(it, carry):
            x, cnt_x = carry
            cand_x = x | lax.shift_left(jnp.int32(1), 16 - it)
            cnt = count_ge(_key_to_float(base + cand_x))
            ok = cnt >= k_f
            return jnp.where(ok, cand_x, x), jnp.where(ok, cnt, cnt_x)

        x, cnt_ge = lax.fori_loop(0, 17, fine_body,
                                  (jnp.zeros(q_shape, I32), count_ge(_key_to_float(base))))
        key = base + x

    thr = _key_to_float(key)
    has_ties = jnp.max(cnt_ge) > k_f
    if tie_index is None:
        return thr, has_ties
    p_ref, idx_bits = tie_index
    p_ref[...] = jnp.full(q_shape, 2 ** 30, I32)

    @pl.when(has_ties)
    def _():
        (n_gt,) = counter([lambda kc, off, cs: jnp.where(kc > thr[:, cs], 1.0, 0.0)])
        need = k_f - n_gt

        def idx_body(it, p):
            bit = lax.shift_left(jnp.int32(1), idx_bits - 1 - it)
            cand = p | bit

            def pred(kc, off, cs):
                idx = off + lax.broadcasted_iota(I32, kc.shape, axis)
                return jnp.where(kc == thr[:, cs], jnp.where(idx < cand[:, cs], 1.0, 0.0), 0.0)

            (cnt,) = counter([pred])
            return jnp.where(cnt < need, cand, p)

        p_ref[...] = lax.fori_loop(0, idx_bits, idx_body, jnp.zeros(q_shape, I32))

    return thr, has_ties


def _selected_bias(kc, off, thr, p_max, axis):
    idx = off + lax.broadcasted_iota(I32, kc.shape, axis)
    tie = jnp.where(kc == thr, jnp.where(idx <= p_max, 1.0, 0.0), 0.0)
    sel = jnp.where(kc > thr, 1.0, tie)
    sel = jnp.where(kc == -jnp.inf, 0.0, sel)
    return jnp.where(sel > 0.5, 0.0, NEG_BIG)


def _index_select_kernel(qi_ref, ki_ref, wit_ref, bias_ref, key_ref, hb_ref, *, tq, k_sel):
    i = pl.program_id(1)
    n_total = bias_ref.shape[1] // tq
    wit = wit_ref[0]

    lane = lax.broadcasted_iota(I32, (tq, LANES), 1)
    qf = qi_ref[0].astype(F32)
    qh = []
    for h in range(H_IDX):
        grp = qf[:, (h // 2) * LANES:(h // 2 + 1) * LANES]
        if h % 2 == 1:
            grp = pltpu.roll(grp, D_IDX, axis=1)
        qh.append(jnp.where(lane < D_IDX, grp, 0.0).astype(BF16))

    def score_body(c, carry):
        off = pl.multiple_of(c * tq, tq)
        kc = jnp.where(lane < D_IDX, ki_ref[0, pl.ds(off, tq), :], 0.0).astype(BF16)
        sc = jnp.zeros((tq, tq), F32)
        for h in range(H_IDX):
            sc = sc + jnp.maximum(_dot_nt(kc, qh[h]), 0.0) * wit[h:h + 1, :]
        kidx = lax.broadcasted_iota(I32, sc.shape, 0) + off
        qidx = lax.broadcasted_iota(I32, sc.shape, 1) + i * tq
        sc = jnp.where(kidx <= qidx, sc, -jnp.inf)
        key_ref[pl.ds(off, tq), :] = sc
        hb_ref[pl.ds(off, tq), :] = sc.astype(BF16)
        return carry

    lax.fori_loop(0, i + 1, score_body, 0)

    counter = _valu_counter(key_ref, i + 1, tq, 0)
    thr, has_ties = _select_topk(counter, k_sel, axis=0, q_shape=(1, tq),
                                 coarse_count=functools.partial(_count_bf16, hb_ref, i + 1, tq))

    def out_with_ties():
        (n_gt,) = counter([lambda kc, off, cs: jnp.where(kc > thr[:, cs], 1.0, 0.0)])
        need = float(k_sel) - n_gt
        sub = TIE_BLOCK if tq % TIE_BLOCK == 0 else tq
        r = lax.broadcasted_iota(I32, (sub, sub), 0)
        col = lax.broadcasted_iota(I32, (sub, sub), 1)
        tri = jnp.where(col <= r, 1.0, 0.0).astype(BF16)

        def out_body(c, seen):
            off = pl.multiple_of(c * tq, tq)
            kc = key_ref[pl.ds(off, tq), :]
            eq = jnp.where(kc == thr, 1.0, 0.0)
            eq_bf = eq.astype(BF16)
            parts = []
            for j in range(tq // sub):
                parts.append(_dot(tri, eq_bf[j * sub:(j + 1) * sub, :]) + seen)
                seen = parts[-1][sub - 1:sub, :]
            rank = jnp.concatenate(parts, axis=0)
            sel = jnp.where(kc > thr, 1.0, jnp.where(rank <= need, eq, 0.0))
            sel = jnp.where(kc == -jnp.inf, 0.0, sel)
            bias_ref[0, pl.ds(off, tq), :] = jnp.where(sel > 0.5, 0.0, NEG_BIG).astype(BF16)
            return rank[tq - 1:tq, :]

        lax.fori_loop(0, i + 1, out_body, jnp.zeros((1, tq), F32))

    def out_body_no_ties(c, carry):
        off = pl.multiple_of(c * tq, tq)
        kc = key_ref[pl.ds(off, tq), :]
        keep = jnp.where(kc == -jnp.inf, NEG_BIG, 0.0)
        bias_ref[0, pl.ds(off, tq), :] = jnp.where(kc >= thr, keep, NEG_BIG).astype(BF16)
        return carry

    @pl.when(has_ties)
    def _():
        out_with_ties()

    @pl.when(jnp.logical_not(has_ties))
    def _():
        lax.fori_loop(0, i + 1, out_body_no_ties, 0)

    def fill_body(c, carry):
        off = pl.multiple_of(c * tq, tq)
        bias_ref[0, pl.ds(off, tq), :] = jnp.full((tq, tq), NEG_BIG, BF16)
        return carry

    lax.fori_loop(i + 1, n_total, fill_body, 0)


def _index_select(qi, kiw, wi_t, tq, k_sel):
    b, t, d_qi = qi.shape
    assert d_qi == H_IDX * D_IDX and 2 * D_IDX == LANES and kiw.shape[2] == LANES
    return pl.pallas_call(
        functools.partial(_index_select_kernel, tq=tq, k_sel=k_sel),
        grid=(b, t // tq),
        in_specs=[
            pl.BlockSpec((1, tq, d_qi), lambda bi, i: (bi, i, 0)),
            pl.BlockSpec((1, t, LANES), lambda bi, i: (bi, 0, 0)),
            pl.BlockSpec((1, H_IDX, tq), lambda bi, i: (bi, 0, i)),
        ],
        out_specs=pl.BlockSpec((1, t, tq), lambda bi, i: (bi, 0, i)),
        out_shape=jax.ShapeDtypeStruct((b, t, t), BF16),
        scratch_shapes=[pltpu.VMEM((t, tq), F32), pltpu.VMEM((t, tq), BF16)],
        compiler_params=_cparams(2),
        name="index_select",
    )(qi, kiw, wi_t)


def _dsa_attn_kernel(q_ref, k_ref, vt_ref, bias_ref, g_ref, o_ref, kbf, sb, mb, lb, accb,
                     *, tq, group, n_kvs):
    i = pl.program_id(2)

    @pl.when(i == 0)
    def _():
        kbf[...] = k_ref[0].astype(BF16)

    q = q_ref[0]
    q4 = []
    for n in range(n_kvs):
        q4.append(jnp.concatenate(
            [q[:, (n * group + h) * HEAD_DIM:(n * group + h + 1) * HEAD_DIM] for h in range(group)],
            axis=0))
        _flash_init(mb.at[n], lb.at[n], accb.at[n])

    def scores(j, slot):
        off = pl.multiple_of(j * tq, tq)
        for n in range(n_kvs):
            kb = kbf[pl.ds(off, tq), n * HEAD_DIM:(n + 1) * HEAD_DIM]
            sb[n, slot] = _dot_nt(kb, q4[n])

    def softmax_pv(j, slot):
        off = pl.multiple_of(j * tq, tq)
        bias = bias_ref[0, pl.ds(off, tq), :].astype(F32)
        bias_g = jnp.concatenate([bias] * group, axis=1)
        for n in range(n_kvs):
            vtb = vt_ref[n * HEAD_DIM:(n + 1) * HEAD_DIM, pl.ds(off, tq)]
            _flash_step(sb[n, slot] + bias_g, vtb, mb.at[n], lb.at[n], accb.at[n], axis=0)

    scores(0, 0)

    def pair(jj, carry):
        j = 2 * jj
        scores(j + 1, 1)
        softmax_pv(j, 0)
        scores(jnp.minimum(j + 2, i), 0)
        softmax_pv(j + 1, 1)
        return carry

    lax.fori_loop(0, (i + 1) // 2, pair, 0)

    @pl.when(i % 2 == 0)
    def _():
        softmax_pv(i, 0)

    g = g_ref[0]
    for n in range(n_kvs):
        o_t = accb[n] / lb[n]
        for h in range(group):
            sl = slice((n * group + h) * HEAD_DIM, (n * group + h + 1) * HEAD_DIM)
            o_ref[0, :, sl] = (o_t[:, h * tq:(h + 1) * tq].T * g[:, sl]).astype(BF16)


def _dsa_attn(qs, ks, vs_t, bias_t, sgs, tq):
    b, t, d_b = qs.shape
    n_kv = ks.shape[2] // HEAD_DIM
    group = d_b // HEAD_DIM // n_kv
    n_kvs = DSA_KV_HEADS_PER_STEP if n_kv % DSA_KV_HEADS_PER_STEP == 0 else 1
    gw = n_kvs * group * HEAD_DIM
    kw = n_kvs * HEAD_DIM
    qmap = lambda bi, n, i: (bi, i, n)
    kmap = lambda bi, n, i: (bi, 0, n)
    return pl.pallas_call(
        functools.partial(_dsa_attn_kernel, tq=tq, group=group, n_kvs=n_kvs),
        grid=(b, n_kv // n_kvs, t // tq),
        in_specs=[
            pl.BlockSpec((1, tq, gw), qmap),
            pl.BlockSpec((1, t, kw), kmap),
            pl.BlockSpec((kw, t), lambda bi, n, i: (n, bi)),
            pl.BlockSpec((1, t, tq), lambda bi, n, i: (bi, 0, i)),
            pl.BlockSpec((1, tq, gw), qmap),
        ],
        out_specs=pl.BlockSpec((1, tq, gw), qmap),
        out_shape=jax.ShapeDtypeStruct((b, t, d_b), BF16),
        scratch_shapes=[
            pltpu.VMEM((t, kw), BF16),
            pltpu.VMEM((n_kvs, 2, tq, group * tq), F32),
            pltpu.VMEM((n_kvs, 1, group * tq), F32), pltpu.VMEM((n_kvs, 1, group * tq), F32),
            pltpu.VMEM((n_kvs, HEAD_DIM, group * tq), F32),
        ],
        compiler_params=_cparams(3),
        name="dsa_attn",
    )(qs, ks, vs_t, bias_t, sgs)


def _out_kernel(x_ref, a1_ref, a2_ref, p_ref, wo_ref, pw_ref, wg_ref, bg_ref, we_ref, y_ref, *, d_a):
    o = _dot(a1_ref[...], wo_ref[0:d_a, :]) + _dot(a2_ref[...], wo_ref[d_a:, :])
    x1 = x_ref[...] + _rms_rows(o, pw_ref[...])
    z = _dot(x1.astype(BF16), wg_ref[...]) + bg_ref[...]
    gate = 1.0 / (1.0 + jnp.exp(-z))
    y_ref[...] = x1 + gate * _dot(p_ref[...].astype(BF16), we_ref[...])


def _out_proj(x2d, a1, a2, p2d, wo, pw, wg, bg, we, tm):
    m, d = x2d.shape
    d_a = a1.shape[1]
    d_mix = wo.shape[0]
    d_ple = p2d.shape[1]
    row = lambda i: (i, 0)
    fixed = lambda i: (0, 0)
    single = dict(pipeline_mode=pl.Buffered(1))
    return pl.pallas_call(
        functools.partial(_out_kernel, d_a=d_a),
        grid=(m // tm,),
        in_specs=[
            pl.BlockSpec((tm, d), row),
            pl.BlockSpec((tm, d_a), row),
            pl.BlockSpec((tm, d_mix - d_a), row),
            pl.BlockSpec((tm, d_ple), row),
            pl.BlockSpec((d_mix, d), fixed, **single),
            pl.BlockSpec((1, d), fixed),
            pl.BlockSpec((d, d), fixed, **single),
            pl.BlockSpec((1, d), fixed),
            pl.BlockSpec((d_ple, d), fixed, **single),
        ],
        out_specs=pl.BlockSpec((tm, d), row),
        out_shape=jax.ShapeDtypeStruct((m, d), F32),
        compiler_params=_cparams(1),
        name="out_proj",
    )(x2d, a1, a2, p2d, wo, pw, wg, bg, we)


def _sample_scores_kernel(pt_ref, qi_ref, wi_ref, *rest, n_group):
    del pt_ref
    page_refs, out_ref = rest[:n_group], rest[n_group]
    q4 = qi_ref[0]
    w = wi_ref[0]
    for r in range(n_group):
        kp_t = page_refs[r][0].astype(BF16)
        rel = jnp.maximum(_dot(q4, kp_t), 0.0)
        out_ref[0, r:r + 1, :] = jnp.sum(rel * w, axis=0, keepdims=True)


def _sample_scores(page_table_flat, qi4, wi3, cache_idx_t, n_pages, n_group):
    nb = qi4.shape[0]
    page = cache_idx_t.shape[2]

    def page_map(r):
        return lambda bi, g, pt: (pt[bi * n_pages + g * n_group + r], 0, 0)

    grid_spec = pltpu.PrefetchScalarGridSpec(
        num_scalar_prefetch=1,
        grid=(nb, n_pages // n_group),
        in_specs=[
            pl.BlockSpec((1, H_IDX, D_IDX), lambda bi, g, pt: (bi, 0, 0)),
            pl.BlockSpec((1, H_IDX, 1), lambda bi, g, pt: (bi, 0, 0)),
        ] + [pl.BlockSpec((1, D_IDX, page), page_map(r)) for r in range(n_group)],
        out_specs=pl.BlockSpec((1, n_group, page), lambda bi, g, pt: (bi, g, 0)),
    )
    return pl.pallas_call(
        functools.partial(_sample_scores_kernel, n_group=n_group),
        grid_spec=grid_spec,
        out_shape=jax.ShapeDtypeStruct((nb, n_pages, page), F32),
        compiler_params=_cparams(2),
        name="sample_scores",
    )(page_table_flat, qi4, wi3, *([cache_idx_t] * n_group))


def _sample_select_kernel(sc_ref, qi_ref, kit_ref, wi_ref, bias_ref, key_ref, p_ref, *, k_sel, idx_bits):
    rows, s_past = sc_ref.shape
    s_all = key_ref.shape[1]
    prod = qi_ref[...].astype(F32) * kit_ref[...].astype(F32)
    lane = lax.broadcasted_iota(I32, prod.shape, 1)
    wi = wi_ref[...]
    new = jnp.zeros((rows, 1), F32)
    for h in range(H_IDX):
        dot_h = jnp.sum(jnp.where(lane // D_IDX == h, prod, 0.0), axis=-1, keepdims=True)
        new = new + jnp.maximum(dot_h, 0.0) * wi[:, h:h + 1]
    key_ref[:, 0:s_past] = sc_ref[...]
    tail_lane = lax.broadcasted_iota(I32, (rows, s_all - s_past), 1)
    key_ref[:, s_past:s_all] = jnp.where(tail_lane == 0, new, -jnp.inf)
    thr, _ = _select_topk(_valu_counter(key_ref, 1, s_all, 1), k_sel, axis=1, q_shape=p_ref.shape,
                          tie_index=(p_ref, idx_bits))
    bias_ref[...] = _selected_bias(key_ref[...], 0, thr, p_ref[...], 1)


def _sample_select(scores2d, qi, ki_tiled, wi, k_sel):
    rows, s_past = scores2d.shape
    s_all = s_past + LANES
    idx_bits = max(1, (s_all - 1).bit_length())
    return pl.pallas_call(
        functools.partial(_sample_select_kernel, k_sel=k_sel, idx_bits=idx_bits),
        out_shape=jax.ShapeDtypeStruct((rows, s_all), F32),
        scratch_shapes=[pltpu.VMEM((rows, s_all), F32), pltpu.VMEM((rows, 1), I32)],
        compiler_params=pltpu.CompilerParams(vmem_limit_bytes=VMEM_LIMIT),
        name="sample_select",
    )(scores2d, qi, ki_tiled, wi)


def _head_rows(row, n_rows, rows_per_head):
    return jnp.concatenate(
        [row[:, (c // rows_per_head) * HEAD_DIM:(c // rows_per_head + 1) * HEAD_DIM]
         for c in range(n_rows)], axis=0)


def _sample_attn_kernel(pt_ref, qa_ref, qs_ref, kan_ref, van_ref, ksn_ref, vsn_ref, bnew_ref,
                        ga_ref, gs_ref, subw_ref, lamp_ref, *rest, n_pg, n_ha, n_hb, group, lam_init):
    del pt_ref
    kd_refs, vd_refs = rest[0:n_pg], rest[n_pg:2 * n_pg]
    ks_refs, vs_refs = rest[2 * n_pg:3 * n_pg], rest[3 * n_pg:4 * n_pg]
    bias_ref, oa_ref, ob_ref, qd, qsb, md, ld, accd, ms, ls, accs = rest[4 * n_pg:]
    p = pl.program_id(1)
    n_p = pl.num_programs(1)
    n_kv = n_hb // group

    @pl.when(p == 0)
    def _():
        qd_f = _head_rows(qa_ref[0].astype(F32), 2 * n_ha, 2)
        rd = lax.broadcasted_iota(I32, qd_f.shape, 0)
        ln = lax.broadcasted_iota(I32, qd_f.shape, 1)
        qd_f = jnp.where(ln // DH_A == rd % 2, qd_f, 0.0)
        qd[...] = qd_f.astype(BF16)
        qs_f = _head_rows(qs_ref[0].astype(F32), n_hb, 1)
        qsb[...] = qs_f.astype(BF16)
        kan = _head_rows(kan_ref[0].astype(BF16).astype(F32), 2 * n_ha, 2)
        van = _head_rows(van_ref[0].astype(BF16).astype(F32), 2 * n_ha, 2)
        md[...] = jnp.sum(qd_f * kan, axis=-1, keepdims=True)
        ld[...] = jnp.ones(ld.shape, F32)
        accd[...] = van
        ksn = _head_rows(ksn_ref[0].astype(BF16).astype(F32), n_hb, group)
        vsn = _head_rows(vsn_ref[0].astype(BF16).astype(F32), n_hb, group)
        ms[...] = jnp.sum(qs_f * ksn, axis=-1, keepdims=True) + bnew_ref[0]
        ls[...] = jnp.ones(ls.shape, F32)
        accs[...] = vsn

    sd_list, vd_list, ss_list, vs_list = [], [], [], []
    for r in range(n_pg):
        kd = kd_refs[r][0].astype(BF16)
        sd = _dot_nt(qd[...], kd)
        rq = lax.broadcasted_iota(I32, sd.shape, 0)
        ck = lax.broadcasted_iota(I32, sd.shape, 1)
        sd_list.append(jnp.where(ck % n_ha == rq // 2, sd, NEG_BIG))
        vd_list.append(vd_refs[r][0].astype(BF16))
        ksp = ks_refs[r][0].astype(BF16)
        ss = _dot_nt(qsb[...], ksp) + bias_ref[0, r]
        rq = lax.broadcasted_iota(I32, ss.shape, 0)
        ck = lax.broadcasted_iota(I32, ss.shape, 1)
        ss_list.append(jnp.where(ck % n_kv == rq // group, ss, NEG_BIG))
        vs_list.append(vs_refs[r][0].astype(BF16))
    _flash_step_multi(sd_list, vd_list, md, ld, accd)
    _flash_step_multi(ss_list, vs_list, ms, ls, accs)

    @pl.when(p == n_p - 1)
    def _():
        lam = _lambda_full(lamp_ref[...], lam_init)
        od = accd[...] / ld[...]
        ga = ga_ref[0]
        subw = subw_ref[...]
        for h in range(n_ha):
            sl = slice(h * HEAD_DIM, (h + 1) * HEAD_DIM)
            o = od[2 * h:2 * h + 1, :] - lam * od[2 * h + 1:2 * h + 2, :]
            o = _rms_rows(o, subw) * (1.0 - lam_init)
            oa_ref[0, :, sl] = (o * ga[:, sl]).astype(BF16)
        os_ = accs[...] / ls[...]
        gs = gs_ref[0]
        for h in range(n_hb):
            sl = slice(h * HEAD_DIM, (h + 1) * HEAD_DIM)
            ob_ref[0, :, sl] = (os_[h:h + 1, :] * gs[:, sl]).astype(BF16)


def _sample_attn(page_table_flat, qa, qs, ka_new, va_new, ks_new, vs_new, bias_new, sga, sgs,
                 subw, lamp, cdk, cdv, csk, csv, bias_pages, n_pages, lam_init):
    nb, _, d_a = qa.shape
    d_b = qs.shape[2]
    d_kv = ks_new.shape[2]
    n_ha = d_a // HEAD_DIM
    n_hb = d_b // HEAD_DIM
    n_kv = d_kv // HEAD_DIM
    group = n_hb // n_kv
    rows_d = cdk.shape[1]
    rows_s = csk.shape[1]
    per_b = lambda bi, p, pt: (bi, 0, 0)
    fixed = lambda bi, p, pt: (0, 0)
    n_pg = SAMPLE_PAGES_PER_STEP if n_pages % SAMPLE_PAGES_PER_STEP == 0 else 1

    def paged(r):
        return lambda bi, p, pt: (pt[bi * n_pages + p * n_pg + r], 0, 0)

    grid_spec = pltpu.PrefetchScalarGridSpec(
        num_scalar_prefetch=1,
        grid=(nb, n_pages // n_pg),
        in_specs=[
            pl.BlockSpec((1, 1, d_a), per_b),
            pl.BlockSpec((1, 1, d_b), per_b),
            pl.BlockSpec((1, 1, d_a), per_b),
            pl.BlockSpec((1, 1, d_a), per_b),
            pl.BlockSpec((1, 1, d_kv), per_b),
            pl.BlockSpec((1, 1, d_kv), per_b),
            pl.BlockSpec((1, 1, 1), per_b),
            pl.BlockSpec((1, 1, d_a), per_b),
            pl.BlockSpec((1, 1, d_b), per_b),
            pl.BlockSpec((1, HEAD_DIM), fixed),
            pl.BlockSpec((4, DH_A), fixed),
        ] + [pl.BlockSpec((1, rows_d, HEAD_DIM), paged(r)) for r in range(n_pg)] * 2
        + [pl.BlockSpec((1, rows_s, HEAD_DIM), paged(r)) for r in range(n_pg)] * 2
        + [pl.BlockSpec((1, n_pg, 1, rows_s), lambda bi, p, pt: (bi, p, 0, 0))],
        out_specs=[pl.BlockSpec((1, 1, d_a), per_b), pl.BlockSpec((1, 1, d_b), per_b)],
        scratch_shapes=[
            pltpu.VMEM((2 * n_ha, HEAD_DIM), BF16), pltpu.VMEM((n_hb, HEAD_DIM), BF16),
            pltpu.VMEM((2 * n_ha, 1), F32), pltpu.VMEM((2 * n_ha, 1), F32),
            pltpu.VMEM((2 * n_ha, HEAD_DIM), F32),
            pltpu.VMEM((n_hb, 1), F32), pltpu.VMEM((n_hb, 1), F32), pltpu.VMEM((n_hb, HEAD_DIM), F32),
        ],
    )
    return pl.pallas_call(
        functools.partial(_sample_attn_kernel, n_pg=n_pg, n_ha=n_ha, n_hb=n_hb, group=group,
                          lam_init=lam_init),
        grid_spec=grid_spec,
        out_shape=[jax.ShapeDtypeStruct((nb, 1, d_a), BF16), jax.ShapeDtypeStruct((nb, 1, d_b), BF16)],
        compiler_params=_cparams(2),
        name="sample_attn",
    )(page_table_flat, qa, qs, ka_new, va_new, ks_new, vs_new, bias_new, sga, sgs, subw, lamp,
      *([cdk] * n_pg + [cdv] * n_pg + [csk] * n_pg + [csv] * n_pg), bias_pages)


def _row_tile(m, pref):
    return pref if m % pref == 0 else m


def _inproj_all(x2d, pos, wts, tm, emit_vt):
    tabs64 = _rope_tables(pos, DH_A)
    tabs128 = _rope_tables(pos, HEAD_DIM)
    d_a, d_b, d_kv, d_qi = wts["d_a"], wts["d_b"], wts["d_kv"], wts["d_qi"]
    outs_a = _inproj_a(x2d, wts["pre_w"], wts["w_a"], tabs64, tm, d_a, emit_vt)
    outs_b = _inproj_b(x2d, wts["pre_w"], wts["w_b"], wts["w_tail"], tabs128, tabs64, tm,
                       d_b, d_kv, d_qi, emit_vt)
    return outs_a, outs_b


def kernel(x_prompt, x_sample, p_prompt, p_sample, cache_diff_k, cache_diff_v, cache_dsa_k, cache_dsa_v, cache_idx_k, page_table, pre_norm_w, post_norm_w, w_in, lam_q1, lam_k1, lam_q2, lam_k2, diff_norm_w, w_out, w_ple_gate, b_ple_gate, w_ple_proj):
    depth = w_in.shape[0]
    assert depth == 1, "single-layer stack only"
    bsz, t_p, d = x_prompt.shape
    nb, t_s, _ = x_sample.shape
    assert t_s == 1, "one new token per sample sequence"
    n_pages = page_table.shape[1]
    n_pool = cache_diff_k.shape[1]
    page = cache_diff_k.shape[2]
    n_ha = cache_diff_k.shape[3]
    n_kv = cache_dsa_k.shape[3]
    d_a = n_ha * HEAD_DIM
    d_kv = n_kv * HEAD_DIM
    d_b = d - d_a
    d_qi = H_IDX * D_IDX
    past_len = n_pages * page
    lam_init = 0.8 - 0.6 * math.exp(-0.3 * 0)
    n_in = w_in.shape[2]

    n_main = 4 * d_a + 2 * d_b + 2 * d_kv + d_qi
    assert n_in - n_main == D_IDX + H_IDX
    w_a = jnp.swapaxes(w_in[0, :, :4 * d_a], 0, 1).astype(BF16)
    w_b = jnp.swapaxes(w_in[0, :, 4 * d_a:n_main], 0, 1).astype(BF16)
    w_tail = jnp.pad(jnp.swapaxes(w_in[0, :, n_main:], 0, 1),
                     ((0, LANES - (n_in - n_main)), (0, 0))).astype(BF16)
    wts = dict(pre_w=pre_norm_w[0][None, :], w_a=w_a, w_b=w_b, w_tail=w_tail,
               d_a=d_a, d_b=d_b, d_kv=d_kv, d_qi=d_qi)
    wo = w_out[0].astype(BF16)
    wg = w_ple_gate[0].astype(BF16)
    we = w_ple_proj[0].astype(BF16)
    pw = post_norm_w[0][None, :]
    bg = b_ple_gate[0][None, :]
    subw = diff_norm_w[0][None, :]
    lamp = jnp.stack([lam_q1[0], lam_k1[0], lam_q2[0], lam_k2[0]], axis=0)

    m_p = bsz * t_p
    tm = _row_tile(t_p, ROW_TILE)
    tq_diff = _row_tile(t_p, DIFF_TILE)
    tq_dsa = _row_tile(t_p, DSA_TILE)
    tq_idx = _row_tile(t_p, IDX_TILE)
    xp2 = x_prompt.reshape(m_p, d)
    pos_p = jnp.arange(t_p, dtype=I32)
    (qa, ka, va, sga, va_t), (qs, ks, vs, sgs, qi, kiw, vs_t) = _inproj_all(xp2, pos_p, wts, tm, True)
    r3 = lambda a: a.reshape(bsz, t_p, a.shape[-1])
    a1 = _diff_attn(r3(qa), r3(ka), va_t, r3(sga), subw, lamp, lam_init, tq_diff)
    ki = kiw[:, :D_IDX]
    wi = kiw[:, D_IDX:D_IDX + H_IDX]
    wi_t = wi.reshape(bsz, t_p, H_IDX).transpose(0, 2, 1)
    k_sel_p = min(TOPK_MAX, t_p // 4)
    bias_t = _index_select(r3(qi), r3(kiw), wi_t, tq_idx, k_sel_p)
    a2 = _dsa_attn(r3(qs), r3(ks), vs_t, bias_t, r3(sgs), tq_dsa)
    y_p = _out_proj(xp2, a1.reshape(m_p, d_a), a2.reshape(m_p, d_b), p_prompt[0].reshape(m_p, -1),
                    wo, pw, wg, bg, we, tm)

    xs2 = x_sample.reshape(nb, d)
    pos_s = jnp.full((nb,), past_len, dtype=I32)
    (qa_s, ka_s, va_s, sga_s), (qs_s, ks_s, vs_s, sgs_s, qi_s, kiw_s) = _inproj_all(xs2, pos_s, wts, nb, False)
    ki_s = kiw_s[:, :D_IDX]
    wi_s = kiw_s[:, D_IDX:D_IDX + H_IDX]
    pt_flat = page_table.reshape(-1)
    n_group = SCORE_PAGES_PER_STEP if n_pages % SCORE_PAGES_PER_STEP == 0 else 1
    cache_idx_t = jnp.swapaxes(cache_idx_k.reshape(n_pool, page, D_IDX), 1, 2)
    scores = _sample_scores(pt_flat, qi_s.reshape(nb, H_IDX, D_IDX), wi_s.reshape(nb, H_IDX, 1),
                            cache_idx_t, n_pages, n_group)
    k_sel_s = min(TOPK_MAX, (past_len + t_s) // 4)
    bias_s = _sample_select(scores.reshape(nb, past_len), qi_s,
                            jnp.tile(ki_s.astype(BF16), (1, H_IDX)), wi_s, k_sel_s)
    bias_pages = jnp.repeat(bias_s[:, :past_len], n_kv, axis=1).reshape(nb, n_pages, 1, page * n_kv)
    bias_new = bias_s[:, past_len:past_len + 1].reshape(nb, 1, 1)
    e3 = lambda a: a.reshape(nb, 1, a.shape[-1])
    a1_s, a2_s = _sample_attn(
        pt_flat, e3(qa_s), e3(qs_s), e3(ka_s), e3(va_s), e3(ks_s), e3(vs_s), bias_new,
        e3(sga_s), e3(sgs_s), subw, lamp,
        cache_diff_k.reshape(n_pool, page * n_ha, HEAD_DIM), cache_diff_v.reshape(n_pool, page * n_ha, HEAD_DIM),
        cache_dsa_k.reshape(n_pool, page * n_kv, HEAD_DIM), cache_dsa_v.reshape(n_pool, page * n_kv, HEAD_DIM),
        bias_pages, n_pages, lam_init)
    y_s = _out_proj(xs2, a1_s.reshape(nb, d_a), a2_s.reshape(nb, d_b), p_sample[0].reshape(nb, -1),
                    wo, pw, wg, bg, we, nb)

    return (
        y_p.reshape(bsz, t_p, d), y_s.reshape(nb, t_s, d),
        ka.reshape(1, bsz, t_p, n_ha, HEAD_DIM), va.reshape(1, bsz, t_p, n_ha, HEAD_DIM),
        ks.reshape(1, bsz, t_p, n_kv, HEAD_DIM), vs.reshape(1, bsz, t_p, n_kv, HEAD_DIM),
        ki.reshape(1, bsz, t_p, D_IDX),
        ka_s.reshape(1, nb, t_s, n_ha, HEAD_DIM), va_s.reshape(1, nb, t_s, n_ha, HEAD_DIM),
        ks_s.reshape(1, nb, t_s, n_kv, HEAD_DIM), vs_s.reshape(1, nb, t_s, n_kv, HEAD_DIM),
        ki_s.reshape(1, nb, t_s, D_IDX),
    )
```

```python
import functools
import math

import jax
import jax.numpy as jnp
from jax import lax
from jax.experimental import pallas as pl
from jax.experimental.pallas import tpu as pltpu

F32 = jnp.float32
BF16 = jnp.bfloat16
I32 = jnp.int32

LANES = 128
SUBLANES = 8
HEAD_DIM = 128
DH_A = HEAD_DIM // 2
D_IDX = 64
H_IDX = 4
TOPK_MAX = 256
ROPE_THETA = 500000.0
ROPE_FRAC = 4
RMS_EPS = 1e-6
NEG_BIG = -1e30
INT_MIN = -(2 ** 31)
NEG_INF_KEY = INT_MIN + 0x7FFFFF
VMEM_LIMIT = 56 * 1024 * 1024
ROW_TILE = 256
LOG2E = math.log2(math.e)
DIFF_TILE = 512
DIFF_HEADS_PER_STEP = 2
DIFF_SOFTMAX_WIDTH = 256
DSA_TILE = 256
DSA_KV_HEADS_PER_STEP = 1
IDX_TILE = 512
TIE_BLOCK = 256
SAMPLE_PAGES_PER_STEP = 8
SCORE_PAGES_PER_STEP = 32


def _cparams(n_axes):
    return pltpu.CompilerParams(
        dimension_semantics=("arbitrary",) * n_axes, vmem_limit_bytes=VMEM_LIMIT)


def _dot(a, b):
    return jnp.dot(a, b, preferred_element_type=F32)


def _dot_nt(a, b):
    return lax.dot_general(a, b, (((1,), (1,)), ((), ())), preferred_element_type=F32)


def _rope_tables(pos, d):
    r = d // ROPE_FRAC
    half = r // 2
    inv = ROPE_THETA ** (-(2.0 / r) * jnp.arange(half, dtype=F32))
    ang = pos.astype(F32)[:, None] * inv[None, :]
    cos, sin = jnp.cos(ang), jnp.sin(ang)
    t = pos.shape[0]
    ones = jnp.ones((t, d - r), F32)
    zeros_h = jnp.zeros((t, half), F32)
    zeros_r = jnp.zeros((t, d - r), F32)
    c = jnp.concatenate([cos, cos, ones], axis=-1)
    sm = jnp.concatenate([-sin, zeros_h, zeros_r], axis=-1)
    sp = jnp.concatenate([zeros_h, sin, zeros_r], axis=-1)
    rep = LANES // d
    return tuple(jnp.tile(a, (1, rep)) for a in (c, sm, sp)), half


def _rope_chunk(z, c, sm, sp, half):
    return (z * c + pltpu.roll(z, LANES - half, axis=1) * sm
            + pltpu.roll(z, half, axis=1) * sp)


def _silu(z):
    return z * (1.0 / (1.0 + jnp.exp(-z)))


def _rms_rows(x, w):
    return x * lax.rsqrt(jnp.mean(x * x, axis=-1, keepdims=True) + RMS_EPS) * w


def _inproj_a_kernel(x_ref, nw_ref, w_ref, c_ref, sm_ref, sp_ref,
                     qa_ref, ka_ref, va_ref, ga_ref, *maybe_vt_ref, half, d_a):
    h = _rms_rows(x_ref[...], nw_ref[...]).astype(BF16)
    c, sm, sp = c_ref[...], sm_ref[...], sp_ref[...]
    n_chunks = d_a // LANES
    zq = _dot_nt(h, w_ref[0:d_a, :])
    for j in range(n_chunks):
        sl = slice(j * LANES, (j + 1) * LANES)
        qa_ref[:, sl] = (_rope_chunk(zq[:, sl], c, sm, sp, half) * (LOG2E / math.sqrt(DH_A))).astype(BF16)
    zk = _dot_nt(h, w_ref[d_a:2 * d_a, :])
    for j in range(n_chunks):
        sl = slice(j * LANES, (j + 1) * LANES)
        ka_ref[:, sl] = _rope_chunk(zk[:, sl], c, sm, sp, half)
    zv = _dot_nt(h, w_ref[2 * d_a:3 * d_a, :])
    va_ref[...] = zv
    if maybe_vt_ref:
        maybe_vt_ref[0][...] = zv.T.astype(BF16)
    ga_ref[...] = _silu(_dot_nt(h, w_ref[3 * d_a:4 * d_a, :]))


def _inproj_a(x2d, nw, w_a, tabs64, tm, d_a, emit_vt):
    m, d = x2d.shape
    (c, sm, sp), half = tabs64
    t_blocks = c.shape[0] // tm
    row = lambda i: (i, 0)
    tab = lambda i: (i % t_blocks, 0)
    fixed = lambda i: (0, 0)
    out_specs = [pl.BlockSpec((tm, d_a), row)] * 4
    out_shape = [
        jax.ShapeDtypeStruct((m, d_a), BF16),
        jax.ShapeDtypeStruct((m, d_a), F32),
        jax.ShapeDtypeStruct((m, d_a), F32),
        jax.ShapeDtypeStruct((m, d_a), F32),
    ]
    if emit_vt:
        out_specs.append(pl.BlockSpec((d_a, tm), lambda i: (0, i)))
        out_shape.append(jax.ShapeDtypeStruct((d_a, m), BF16))
    return pl.pallas_call(
        functools.partial(_inproj_a_kernel, half=half, d_a=d_a),
        grid=(m // tm,),
        in_specs=[
            pl.BlockSpec((tm, d), row),
            pl.BlockSpec((1, d), fixed),
            pl.BlockSpec((4 * d_a, d), fixed, pipeline_mode=pl.Buffered(1)),
            pl.BlockSpec((tm, LANES), tab),
            pl.BlockSpec((tm, LANES), tab),
            pl.BlockSpec((tm, LANES), tab),
        ],
        out_specs=out_specs,
        out_shape=out_shape,
        compiler_params=_cparams(1),
        name="inproj_a",
    )(x2d, nw, w_a, c, sm, sp)


def _inproj_b_kernel(x_ref, nw_ref, w_ref, wt_ref, c128_ref, sm128_ref, sp128_ref,
                     c64_ref, sm64_ref, sp64_ref,
                     qs_ref, ks_ref, vs_ref, gs_ref, qi_ref, kiw_ref, *maybe_vt_ref,
                     half128, half64, d_b, d_kv, d_qi):
    h = _rms_rows(x_ref[...], nw_ref[...]).astype(BF16)
    c128, sm128, sp128 = c128_ref[...], sm128_ref[...], sp128_ref[...]
    c64, sm64, sp64 = c64_ref[...], sm64_ref[...], sp64_ref[...]
    o = 0
    zq = _dot_nt(h, w_ref[o:o + d_b, :])
    for j in range(d_b // LANES):
        sl = slice(j * LANES, (j + 1) * LANES)
        qs_ref[:, sl] = (_rope_chunk(zq[:, sl], c128, sm128, sp128, half128)
                         * (LOG2E / math.sqrt(HEAD_DIM))).astype(BF16)
    o += d_b
    zk = _dot_nt(h, w_ref[o:o + d_kv, :])
    for j in range(d_kv // LANES):
        sl = slice(j * LANES, (j + 1) * LANES)
        ks_ref[:, sl] = _rope_chunk(zk[:, sl], c128, sm128, sp128, half128)
    o += d_kv
    zv = _dot_nt(h, w_ref[o:o + d_kv, :])
    vs_ref[...] = zv
    if maybe_vt_ref:
        maybe_vt_ref[0][...] = zv.T.astype(BF16)
    o += d_kv
    gs_ref[...] = _silu(_dot_nt(h, w_ref[o:o + d_b, :]))
    o += d_b
    zi = _dot_nt(h, w_ref[o:o + d_qi, :])
    for j in range(d_qi // LANES):
        sl = slice(j * LANES, (j + 1) * LANES)
        qi_ref[:, sl] = _rope_chunk(zi[:, sl], c64, sm64, sp64, half64).astype(BF16)
    zkw = _dot_nt(h, wt_ref[...])
    lane = lax.broadcasted_iota(I32, zkw.shape, 1)
    kiw_ref[...] = jnp.where(lane < D_IDX, _rope_chunk(zkw, c64, sm64, sp64, half64), zkw)


def _inproj_b(x2d, nw, w_b, w_tail, tabs128, tabs64, tm, d_b, d_kv, d_qi, emit_vt):
    m, d = x2d.shape
    (c128, sm128, sp128), half128 = tabs128
    (c64, sm64, sp64), half64 = tabs64
    t_blocks = c128.shape[0] // tm
    row = lambda i: (i, 0)
    tab = lambda i: (i % t_blocks, 0)
    fixed = lambda i: (0, 0)
    n_b = w_b.shape[0]
    out_specs = [
        pl.BlockSpec((tm, d_b), row),
        pl.BlockSpec((tm, d_kv), row),
        pl.BlockSpec((tm, d_kv), row),
        pl.BlockSpec((tm, d_b), row),
        pl.BlockSpec((tm, d_qi), row),
        pl.BlockSpec((tm, LANES), row),
    ]
    out_shape = [
        jax.ShapeDtypeStruct((m, d_b), BF16),
        jax.ShapeDtypeStruct((m, d_kv), F32),
        jax.ShapeDtypeStruct((m, d_kv), F32),
        jax.ShapeDtypeStruct((m, d_b), F32),
        jax.ShapeDtypeStruct((m, d_qi), BF16),
        jax.ShapeDtypeStruct((m, LANES), F32),
    ]
    if emit_vt:
        out_specs.append(pl.BlockSpec((d_kv, tm), lambda i: (0, i)))
        out_shape.append(jax.ShapeDtypeStruct((d_kv, m), BF16))
    return pl.pallas_call(
        functools.partial(_inproj_b_kernel, half128=half128, half64=half64,
                          d_b=d_b, d_kv=d_kv, d_qi=d_qi),
        grid=(m // tm,),
        in_specs=[
            pl.BlockSpec((tm, d), row),
            pl.BlockSpec((1, d), fixed),
            pl.BlockSpec((n_b, d), fixed, pipeline_mode=pl.Buffered(1)),
            pl.BlockSpec((LANES, d), fixed),
        ] + [pl.BlockSpec((tm, LANES), tab)] * 6,
        out_specs=out_specs,
        out_shape=out_shape,
        compiler_params=_cparams(1),
        name="inproj_b",
    )(x2d, nw, w_b, w_tail, c128, sm128, sp128, c64, sm64, sp64)


def _flash_init(m_ref, l_ref, acc_ref):
    m_ref[...] = jnp.full(m_ref.shape, NEG_BIG, F32)
    l_ref[...] = jnp.zeros(l_ref.shape, F32)
    acc_ref[...] = jnp.zeros(acc_ref.shape, F32)


def _flash_step(s, v_bf, m_ref, l_ref, acc_ref, axis=1):
    m_prev = m_ref[...]
    m_new = jnp.maximum(m_prev, jnp.max(s, axis=axis, keepdims=True))
    alpha = jnp.exp2(m_prev - m_new)
    p = jnp.exp2(s - m_new)
    l_ref[...] = alpha * l_ref[...] + jnp.sum(p, axis=axis, keepdims=True)
    pv = _dot(p.astype(BF16), v_bf) if axis == 1 else _dot(v_bf, p.astype(BF16))
    acc_ref[...] = alpha * acc_ref[...] + pv
    m_ref[...] = m_new


def _flash_step_multi(s_list, v_list, m_ref, l_ref, acc_ref):
    m_prev = m_ref[...]
    m_new = m_prev
    for s in s_list:
        m_new = jnp.maximum(m_new, jnp.max(s, axis=1, keepdims=True))
    alpha = jnp.exp2(m_prev - m_new)
    l_new = alpha * l_ref[...]
    acc = alpha * acc_ref[...]
    for s, v_bf in zip(s_list, v_list):
        p = jnp.exp2(s - m_new)
        l_new = l_new + jnp.sum(p, axis=1, keepdims=True)
        acc = acc + _dot(p.astype(BF16), v_bf)
    l_ref[...] = l_new
    acc_ref[...] = acc
    m_ref[...] = m_new


def _lambda_full(lamp, lam_init):
    s1 = jnp.sum(lamp[0:1, :] * lamp[1:2, :], axis=-1, keepdims=True)
    s2 = jnp.sum(lamp[2:3, :] * lamp[3:4, :], axis=-1, keepdims=True)
    return jnp.exp(s1) - jnp.exp(s2) + lam_init


def _diff_attn_kernel(q_ref, k_ref, vt_ref, g_ref, subw_ref, lamp_ref, o_ref,
                      kbf, sb, mb, lb, accb, *, tq, n_hs, lam_init):
    i = pl.program_id(2)

    @pl.when(i == 0)
    def _():
        kbf[...] = k_ref[0].astype(BF16)

    q = q_ref[0]
    lane = lax.broadcasted_iota(I32, (tq, HEAD_DIM), 1)
    qm = []
    for h in range(n_hs):
        qh = q[:, h * HEAD_DIM:(h + 1) * HEAD_DIM]
        zero = jnp.zeros_like(qh)
        qm.append((jnp.where(lane < DH_A, qh, zero), jnp.where(lane >= DH_A, qh, zero)))
        for mp in range(2):
            _flash_init(mb.at[h, mp], lb.at[h, mp], accb.at[h, mp])

    def scores(j, slot):
        off = pl.multiple_of(j * tq, tq)
        for h in range(n_hs):
            kb = kbf[pl.ds(off, tq), h * HEAD_DIM:(h + 1) * HEAD_DIM]
            for mp in range(2):
                sb[h, mp, slot] = _dot_nt(kb, qm[h][mp])

    qw = DIFF_SOFTMAX_WIDTH if tq % DIFF_SOFTMAX_WIDTH == 0 else tq

    def softmax_pv(j, slot, masked):
        off = pl.multiple_of(j * tq, tq)
        for h in range(n_hs):
            vtb = vt_ref[h * HEAD_DIM:(h + 1) * HEAD_DIM, pl.ds(off, tq)]
            for mp in range(2):
                for q0 in range(0, tq, qw):
                    qs = slice(q0, q0 + qw)
                    k_end = q0 + qw if masked else tq
                    s = sb[h, mp, slot, 0:k_end, qs]
                    if masked:
                        kidx = lax.broadcasted_iota(I32, s.shape, 0)
                        qidx = lax.broadcasted_iota(I32, s.shape, 1) + q0
                        s = jnp.where(kidx <= qidx, s, NEG_BIG)
                    _flash_step(s, vtb[:, 0:k_end], mb.at[h, mp, :, qs], lb.at[h, mp, :, qs],
                                accb.at[h, mp, :, qs], axis=0)

    scores(0, 0)

    def pair(jj, carry):
        j = 2 * jj
        scores(j + 1, 1)
        softmax_pv(j, 0, False)
        scores(j + 2, 0)
        softmax_pv(j + 1, 1, False)
        return carry

    lax.fori_loop(0, i // 2, pair, 0)

    @pl.when(i % 2 == 1)
    def _():
        scores(i, 1)
        softmax_pv(i - 1, 0, False)
        softmax_pv(i, 1, True)

    @pl.when(i % 2 == 0)
    def _():
        softmax_pv(i, 0, True)

    lam = _lambda_full(lamp_ref[...], lam_init)
    g = g_ref[0]
    for h in range(n_hs):
        sl = slice(h * HEAD_DIM, (h + 1) * HEAD_DIM)
        o_t = accb[h, 0] / lb[h, 0] - lam * (accb[h, 1] / lb[h, 1])
        o = _rms_rows(o_t.T, subw_ref[...]) * (1.0 - lam_init)
        o_ref[0, :, sl] = (o * g[:, sl]).astype(BF16)


def _diff_attn(qa, ka, va_t, sga, subw, lamp, lam_init, tq):
    b, t, d_a = qa.shape
    n_h = d_a // HEAD_DIM
    n_hs = DIFF_HEADS_PER_STEP if n_h % DIFF_HEADS_PER_STEP == 0 else 1
    hw = n_hs * HEAD_DIM
    qmap = lambda bi, h, i: (bi, i, h)
    kmap = lambda bi, h, i: (bi, 0, h)
    fixed = lambda bi, h, i: (0, 0)
    return pl.pallas_call(
        functools.partial(_diff_attn_kernel, tq=tq, n_hs=n_hs, lam_init=lam_init),
        grid=(b, n_h // n_hs, t // tq),
        in_specs=[
            pl.BlockSpec((1, tq, hw), qmap),
            pl.BlockSpec((1, t, hw), kmap),
            pl.BlockSpec((hw, t), lambda bi, h, i: (h, bi)),
            pl.BlockSpec((1, tq, hw), qmap),
            pl.BlockSpec((1, HEAD_DIM), fixed),
            pl.BlockSpec((4, DH_A), fixed),
        ],
        out_specs=pl.BlockSpec((1, tq, hw), qmap),
        out_shape=jax.ShapeDtypeStruct((b, t, d_a), BF16),
        scratch_shapes=[
            pltpu.VMEM((t, hw), BF16),
            pltpu.VMEM((n_hs, 2, 2, tq, tq), F32),
            pltpu.VMEM((n_hs, 2, 1, tq), F32), pltpu.VMEM((n_hs, 2, 1, tq), F32),
            pltpu.VMEM((n_hs, 2, HEAD_DIM, tq), F32),
        ],
        compiler_params=_cparams(3),
        name="diff_attn",
    )(qa, ka, va_t, sga, subw, lamp)


def _key_to_float(key):
    key = jnp.maximum(key, NEG_INF_KEY)
    return lax.bitcast_convert_type(key ^ ((key >> 31) & 0x7FFFFFFF), F32)


def _fold(m, axis):
    if axis == 1:
        n = m.shape[1] // LANES
        acc = m[:, 0:LANES]
        for j in range(1, n):
            acc = acc + m[:, j * LANES:(j + 1) * LANES]
        return acc
    return jnp.sum(m.reshape(m.shape[0] // SUBLANES, SUBLANES, m.shape[1]), axis=0)


def _count(key_ref, n_chunks, chunk, pred, axis):
    other = key_ref.shape[1 - axis]
    acc_shape = (other, LANES) if axis == 1 else (SUBLANES, other)

    def body(c, acc):
        off = pl.multiple_of(c * chunk, chunk)
        kc = key_ref[:, pl.ds(off, chunk)] if axis == 1 else key_ref[pl.ds(off, chunk), :]
        return acc + _fold(pred(kc, off, slice(None)), axis)

    acc = lax.fori_loop(0, n_chunks, body, jnp.zeros(acc_shape, F32))
    return jnp.sum(acc, axis=axis, keepdims=True)


def _valu_counter(key_ref, n_chunks, chunk, axis):
    return lambda preds: [_count(key_ref, n_chunks, chunk, p, axis) for p in preds]


def _prefix_key(pu):
    key = lax.shift_left(pu, 16) ^ INT_MIN
    return key | ((key >> 31) & 0xFFFF)


def _count_bf16(hb_ref, n_chunks, chunk, cand):
    tq = hb_ref.shape[1]
    pack = 2 * SUBLANES
    candb = jnp.broadcast_to(cand.astype(BF16), (pack, tq))
    one, zero = jnp.ones((), BF16), jnp.zeros((), BF16)

    def body(c, acc):
        off = pl.multiple_of(c * chunk, chunk)
        kc = hb_ref[pl.ds(off, chunk), :].reshape(chunk // pack, pack, tq)
        part = jnp.where(kc[0] >= candb, one, zero)
        for r in range(1, chunk // pack):
            part = part + jnp.where(kc[r] >= candb, one, zero)
        return acc + part.astype(F32)

    acc = lax.fori_loop(0, n_chunks, body, jnp.zeros((pack, tq), F32))
    return jnp.sum(acc, axis=0, keepdims=True)


def _select_topk(counter, k_sel, axis, q_shape, coarse_count=None, tie_index=None):
    k_f = float(k_sel)

    def count_ge(cand):
        (cnt,) = counter([lambda kc, off, cs: jnp.where(kc >= cand[:, cs], 1.0, 0.0)])
        return cnt

    if coarse_count is None:
        def bit_body(it, carry):
            tu, cnt_tu = carry
            cand_u = tu | lax.shift_left(jnp.int32(1), 31 - it)
            cnt = count_ge(_key_to_float(cand_u ^ INT_MIN))
            ok = cnt >= k_f
            return jnp.where(ok, cand_u, tu), jnp.where(ok, cnt, cnt_tu)

        lowest = jnp.full(q_shape, -jnp.inf, F32)
        tu, cnt_ge = lax.fori_loop(0, 32, bit_body, (jnp.zeros(q_shape, I32), count_ge(lowest)))
        key = tu ^ INT_MIN
    else:
        def coarse_body(it, pu):
            cand_p = pu | lax.shift_left(jnp.int32(1), 15 - it)
            cnt = coarse_count(_key_to_float(_prefix_key(cand_p)))
            return jnp.where(cnt >= k_f, cand_p, pu)

        pu = lax.fori_loop(0, 16, coarse_body, jnp.zeros(q_shape, I32))
        base = _prefix_key(pu) - 2 ** 16

        def fine_body(it, carry):
            x, cnt_x = carry
            cand_x = x | lax.shift_left(jnp.int32(1), 16 - it)
            cnt = count_ge(_key_to_float(base + cand_x))
            ok = cnt >= k_f
            return jnp.where(ok, cand_x, x), jnp.where(ok, cnt, cnt_x)

        x, cnt_ge = lax.fori_loop(0, 17, fine_body,
                                  (jnp.zeros(q_shape, I32), count_ge(_key_to_float(base))))
        key = base + x

    thr = _key_to_float(key)
    has_ties = jnp.max(cnt_ge) > k_f
    if tie_index is None:
        return thr, has_ties
    p_ref, idx_bits = tie_index
    p_ref[...] = jnp.full(q_shape, 2 ** 30, I32)

    @pl.when(has_ties)
    def _():
        (n_gt,) = counter([lambda kc, off, cs: jnp.where(kc > thr[:, cs], 1.0, 0.0)])
        need = k_f - n_gt

        def idx_body(it, p):
            bit = lax.shift_left(jnp.int32(1), idx_bits - 1 - it)
            cand = p | bit

            def pred(kc, off, cs):
                idx = off + lax.broadcasted_iota(I32, kc.shape, axis)
                return jnp.where(kc == thr[:, cs], jnp.where(idx < cand[:, cs], 1.0, 0.0), 0.0)

            (cnt,) = counter([pred])
            return jnp.where(cnt < need, cand, p)

        p_ref[...] = lax.fori_loop(0, idx_bits, idx_body, jnp.zeros(q_shape, I32))

    return thr, has_ties


def _selected_bias(kc, off, thr, p_max, axis):
    idx = off + lax.broadcasted_iota(I32, kc.shape, axis)
    tie = jnp.where(kc == thr, jnp.where(idx <= p_max, 1.0, 0.0), 0.0)
    sel = jnp.where(kc > thr, 1.0, tie)
    sel = jnp.where(kc == -jnp.inf, 0.0, sel)
    return jnp.where(sel > 0.5, 0.0, NEG_BIG)


def _index_select_kernel(qi_ref, ki_ref, wit_ref, bias_ref, key_ref, hb_ref, *, tq, k_sel):
    i = pl.program_id(1)
    n_total = bias_ref.shape[1] // tq
    wit = wit_ref[0]

    lane = lax.broadcasted_iota(I32, (tq, LANES), 1)
    qf = qi_ref[0].astype(F32)
    qh = []
    for h in range(H_IDX):
        grp = qf[:, (h // 2) * LANES:(h // 2 + 1) * LANES]
        if h % 2 == 1:
            grp = pltpu.roll(grp, D_IDX, axis=1)
        qh.append(jnp.where(lane < D_IDX, grp, 0.0).astype(BF16))

    def score_body(c, carry):
        off = pl.multiple_of(c * tq, tq)
        kc = jnp.where(lane < D_IDX, ki_ref[0, pl.ds(off, tq), :], 0.0).astype(BF16)
        sc = jnp.zeros((tq, tq), F32)
        for h in range(H_IDX):
            sc = sc + jnp.maximum(_dot_nt(kc, qh[h]), 0.0) * wit[h:h + 1, :]
        kidx = lax.broadcasted_iota(I32, sc.shape, 0) + off
        qidx = lax.broadcasted_iota(I32, sc.shape, 1) + i * tq
        sc = jnp.where(kidx <= qidx, sc, -jnp.inf)
        key_ref[pl.ds(off, tq), :] = sc
        hb_ref[pl.ds(off, tq), :] = sc.astype(BF16)
        return carry

    lax.fori_loop(0, i + 1, score_body, 0)

    counter = _valu_counter(key_ref, i + 1, tq, 0)
    thr, has_ties = _select_topk(counter, k_sel, axis=0, q_shape=(1, tq),
                                 coarse_count=functools.partial(_count_bf16, hb_ref, i + 1, tq))

    def out_with_ties():
        (n_gt,) = counter([lambda kc, off, cs: jnp.where(kc > thr[:, cs], 1.0, 0.0)])
        need = float(k_sel) - n_gt
        sub = TIE_BLOCK if tq % TIE_BLOCK == 0 else tq
        r = lax.broadcasted_iota(I32, (sub, sub), 0)
        col = lax.broadcasted_iota(I32, (sub, sub), 1)
        tri = jnp.where(col <= r, 1.0, 0.0).astype(BF16)

        def out_body(c, seen):
            off = pl.multiple_of(c * tq, tq)
            kc = key_ref[pl.ds(off, tq), :]
            eq = jnp.where(kc == thr, 1.0, 0.0)
            eq_bf = eq.astype(BF16)
            parts = []
            for j in range(tq // sub):
                parts.append(_dot(tri, eq_bf[j * sub:(j + 1) * sub, :]) + seen)
                seen = parts[-1][sub - 1:sub, :]
            rank = jnp.concatenate(parts, axis=0)
            sel = jnp.where(kc > thr, 1.0, jnp.where(rank <= need, eq, 0.0))
            sel = jnp.where(kc == -jnp.inf, 0.0, sel)
            bias_ref[0, pl.ds(off, tq), :] = jnp.where(sel > 0.5, 0.0, NEG_BIG).astype(BF16)
            return rank[tq - 1:tq, :]

        lax.fori_loop(0, i + 1, out_body, jnp.zeros((1, tq), F32))

    def out_body_no_ties(c, carry):
        off = pl.multiple_of(c * tq, tq)
        kc = key_ref[pl.ds(off, tq), :]
        keep = jnp.where(kc == -jnp.inf, NEG_BIG, 0.0)
        bias_ref[0, pl.ds(off, tq), :] = jnp.where(kc >= thr, keep, NEG_BIG).astype(BF16)
        return carry

    @pl.when(has_ties)
    def _():
        out_with_ties()

    @pl.when(jnp.logical_not(has_ties))
    def _():
        lax.fori_loop(0, i + 1, out_body_no_ties, 0)

    def fill_body(c, carry):
        off = pl.multiple_of(c * tq, tq)
        bias_ref[0, pl.ds(off, tq), :] = jnp.full((tq, tq), NEG_BIG, BF16)
        return carry

    lax.fori_loop(i + 1, n_total, fill_body, 0)


def _index_select(qi, kiw, wi_t, tq, k_sel):
    b, t, d_qi = qi.shape
    assert d_qi == H_IDX * D_IDX and 2 * D_IDX == LANES and kiw.shape[2] == LANES
    return pl.pallas_call(
        functools.partial(_index_select_kernel, tq=tq, k_sel=k_sel),
        grid=(b, t // tq),
        in_specs=[
            pl.BlockSpec((1, tq, d_qi), lambda bi, i: (bi, i, 0)),
            pl.BlockSpec((1, t, LANES), lambda bi, i: (bi, 0, 0)),
            pl.BlockSpec((1, H_IDX, tq), lambda bi, i: (bi, 0, i)),
        ],
        out_specs=pl.BlockSpec((1, t, tq), lambda bi, i: (bi, 0, i)),
        out_shape=jax.ShapeDtypeStruct((b, t, t), BF16),
        scratch_shapes=[pltpu.VMEM((t, tq), F32), pltpu.VMEM((t, tq), BF16)],
        compiler_params=_cparams(2),
        name="index_select",
    )(qi, kiw, wi_t)


def _dsa_attn_kernel(q_ref, k_ref, vt_ref, bias_ref, g_ref, o_ref, kbf, sb, mb, lb, accb,
                     *, tq, group, n_kvs):
    i = pl.program_id(2)

    @pl.when(i == 0)
    def _():
        kbf[...] = k_ref[0].astype(BF16)

    q = q_ref[0]
    q4 = []
    for n in range(n_kvs):
        q4.append(jnp.concatenate(
            [q[:, (n * group + h) * HEAD_DIM:(n * group + h + 1) * HEAD_DIM] for h in range(group)],
            axis=0))
        _flash_init(mb.at[n], lb.at[n], accb.at[n])

    def scores(j, slot):
        off = pl.multiple_of(j * tq, tq)
        for n in range(n_kvs):
            kb = kbf[pl.ds(off, tq), n * HEAD_DIM:(n + 1) * HEAD_DIM]
            sb[n, slot] = _dot_nt(kb, q4[n])

    def softmax_pv(j, slot):
        off = pl.multiple_of(j * tq, tq)
        bias = bias_ref[0, pl.ds(off, tq), :].astype(F32)
        bias_g = jnp.concatenate([bias] * group, axis=1)
        for n in range(n_kvs):
            vtb = vt_ref[n * HEAD_DIM:(n + 1) * HEAD_DIM, pl.ds(off, tq)]
            _flash_step(sb[n, slot] + bias_g, vtb, mb.at[n], lb.at[n], accb.at[n], axis=0)

    scores(0, 0)

    def pair(jj, carry):
        j = 2 * jj
        scores(j + 1, 1)
        softmax_pv(j, 0)
        scores(jnp.minimum(j + 2, i), 0)
        softmax_pv(j + 1, 1)
        return carry

    lax.fori_loop(0, (i + 1) // 2, pair, 0)

    @pl.when(i % 2 == 0)
    def _():
        softmax_pv(i, 0)

    g = g_ref[0]
    for n in range(n_kvs):
        o_t = accb[n] / lb[n]
        for h in range(group):
            sl = slice((n * group + h) * HEAD_DIM, (n * group + h + 1) * HEAD_DIM)
            o_ref[0, :, sl] = (o_t[:, h * tq:(h + 1) * tq].T * g[:, sl]).astype(BF16)


def _dsa_attn(qs, ks, vs_t, bias_t, sgs, tq):
    b, t, d_b = qs.shape
    n_kv = ks.shape[2] // HEAD_DIM
    group = d_b // HEAD_DIM // n_kv
    n_kvs = DSA_KV_HEADS_PER_STEP if n_kv % DSA_KV_HEADS_PER_STEP == 0 else 1
    gw = n_kvs * group * HEAD_DIM
    kw = n_kvs * HEAD_DIM
    qmap = lambda bi, n, i: (bi, i, n)
    kmap = lambda bi, n, i: (bi, 0, n)
    return pl.pallas_call(
        functools.partial(_dsa_attn_kernel, tq=tq, group=group, n_kvs=n_kvs),
        grid=(b, n_kv // n_kvs, t // tq),
        in_specs=[
            pl.BlockSpec((1, tq, gw), qmap),
            pl.BlockSpec((1, t, kw), kmap),
            pl.BlockSpec((kw, t), lambda bi, n, i: (n, bi)),
            pl.BlockSpec((1, t, tq), lambda bi, n, i: (bi, 0, i)),
            pl.BlockSpec((1, tq, gw), qmap),
        ],
        out_specs=pl.BlockSpec((1, tq, gw), qmap),
        out_shape=jax.ShapeDtypeStruct((b, t, d_b), BF16),
        scratch_shapes=[
            pltpu.VMEM((t, kw), BF16),
            pltpu.VMEM((n_kvs, 2, tq, group * tq), F32),
            pltpu.VMEM((n_kvs, 1, group * tq), F32), pltpu.VMEM((n_kvs, 1, group * tq), F32),
            pltpu.VMEM((n_kvs, HEAD_DIM, group * tq), F32),
        ],
        compiler_params=_cparams(3),
        name="dsa_attn",
    )(qs, ks, vs_t, bias_t, sgs)


def _out_kernel(x_ref, a1_ref, a2_ref, p_ref, wo_ref, pw_ref, wg_ref, bg_ref, we_ref, y_ref, *, d_a):
    o = _dot(a1_ref[...], wo_ref[0:d_a, :]) + _dot(a2_ref[...], wo_ref[d_a:, :])
    x1 = x_ref[...] + _rms_rows(o, pw_ref[...])
    z = _dot(x1.astype(BF16), wg_ref[...]) + bg_ref[...]
    gate = 1.0 / (1.0 + jnp.exp(-z))
    y_ref[...] = x1 + gate * _dot(p_ref[...].astype(BF16), we_ref[...])


def _out_proj(x2d, a1, a2, p2d, wo, pw, wg, bg, we, tm):
    m, d = x2d.shape
    d_a = a1.shape[1]
    d_mix = wo.shape[0]
    d_ple = p2d.shape[1]
    row = lambda i: (i, 0)
    fixed = lambda i: (0, 0)
    single = dict(pipeline_mode=pl.Buffered(1))
    return pl.pallas_call(
        functools.partial(_out_kernel, d_a=d_a),
        grid=(m // tm,),
        in_specs=[
            pl.BlockSpec((tm, d), row),
            pl.BlockSpec((tm, d_a), row),
            pl.BlockSpec((tm, d_mix - d_a), row),
            pl.BlockSpec((tm, d_ple), row),
            pl.BlockSpec((d_mix, d), fixed, **single),
            pl.BlockSpec((1, d), fixed),
            pl.BlockSpec((d, d), fixed, **single),
            pl.BlockSpec((1, d), fixed),
            pl.BlockSpec((d_ple, d), fixed, **single),
        ],
        out_specs=pl.BlockSpec((tm, d), row),
        out_shape=jax.ShapeDtypeStruct((m, d), F32),
        compiler_params=_cparams(1),
        name="out_proj",
    )(x2d, a1, a2, p2d, wo, pw, wg, bg, we)


def _sample_scores_kernel(pt_ref, qi_ref, wi_ref, *rest, n_group):
    del pt_ref
    page_refs, out_ref = rest[:n_group], rest[n_group]
    q4 = qi_ref[0]
    w = wi_ref[0]
    for r in range(n_group):
        kp_t = page_refs[r][0].astype(BF16)
        rel = jnp.maximum(_dot(q4, kp_t), 0.0)
        out_ref[0, r:r + 1, :] = jnp.sum(rel * w, axis=0, keepdims=True)


def _sample_scores(page_table_flat, qi4, wi3, cache_idx_t, n_pages, n_group):
    nb = qi4.shape[0]
    page = cache_idx_t.shape[2]

    def page_map(r):
        return lambda bi, g, pt: (pt[bi * n_pages + g * n_group + r], 0, 0)

    grid_spec = pltpu.PrefetchScalarGridSpec(
        num_scalar_prefetch=1,
        grid=(nb, n_pages // n_group),
        in_specs=[
            pl.BlockSpec((1, H_IDX, D_IDX), lambda bi, g, pt: (bi, 0, 0)),
            pl.BlockSpec((1, H_IDX, 1), lambda bi, g, pt: (bi, 0, 0)),
        ] + [pl.BlockSpec((1, D_IDX, page), page_map(r)) for r in range(n_group)],
        out_specs=pl.BlockSpec((1, n_group, page), lambda bi, g, pt: (bi, g, 0)),
    )
    return pl.pallas_call(
        functools.partial(_sample_scores_kernel, n_group=n_group),
        grid_spec=grid_spec,
        out_shape=jax.ShapeDtypeStruct((nb, n_pages, page), F32),
        compiler_params=_cparams(2),
        name="sample_scores",
    )(page_table_flat, qi4, wi3, *([cache_idx_t] * n_group))


def _sample_select_kernel(sc_ref, qi_ref, kit_ref, wi_ref, bias_ref, key_ref, p_ref, *, k_sel, idx_bits):
    rows, s_past = sc_ref.shape
    s_all = key_ref.shape[1]
    prod = qi_ref[...].astype(F32) * kit_ref[...].astype(F32)
    lane = lax.broadcasted_iota(I32, prod.shape, 1)
    wi = wi_ref[...]
    new = jnp.zeros((rows, 1), F32)
    for h in range(H_IDX):
        dot_h = jnp.sum(jnp.where(lane // D_IDX == h, prod, 0.0), axis=-1, keepdims=True)
        new = new + jnp.maximum(dot_h, 0.0) * wi[:, h:h + 1]
    key_ref[:, 0:s_past] = sc_ref[...]
    tail_lane = lax.broadcasted_iota(I32, (rows, s_all - s_past), 1)
    key_ref[:, s_past:s_all] = jnp.where(tail_lane == 0, new, -jnp.inf)
    thr, _ = _select_topk(_valu_counter(key_ref, 1, s_all, 1), k_sel, axis=1, q_shape=p_ref.shape,
                          tie_index=(p_ref, idx_bits))
    bias_ref[...] = _selected_bias(key_ref[...], 0, thr, p_ref[...], 1)


def _sample_select(scores2d, qi, ki_tiled, wi, k_sel):
    rows, s_past = scores2d.shape
    s_all = s_past + LANES
    idx_bits = max(1, (s_all - 1).bit_length())
    return pl.pallas_call(
        functools.partial(_sample_select_kernel, k_sel=k_sel, idx_bits=idx_bits),
        out_shape=jax.ShapeDtypeStruct((rows, s_all), F32),
        scratch_shapes=[pltpu.VMEM((rows, s_all), F32), pltpu.VMEM((rows, 1), I32)],
        compiler_params=pltpu.CompilerParams(vmem_limit_bytes=VMEM_LIMIT),
        name="sample_select",
    )(scores2d, qi, ki_tiled, wi)


def _head_rows(row, n_rows, rows_per_head):
    return jnp.concatenate(
        [row[:, (c // rows_per_head) * HEAD_DIM:(c // rows_per_head + 1) * HEAD_DIM]
         for c in range(n_rows)], axis=0)


def _sample_attn_kernel(pt_ref, qa_ref, qs_ref, kan_ref, van_ref, ksn_ref, vsn_ref, bnew_ref,
                        ga_ref, gs_ref, subw_ref, lamp_ref, *rest, n_pg, n_ha, n_hb, group, lam_init):
    del pt_ref
    kd_refs, vd_refs = rest[0:n_pg], rest[n_pg:2 * n_pg]
    ks_refs, vs_refs = rest[2 * n_pg:3 * n_pg], rest[3 * n_pg:4 * n_pg]
    bias_ref, oa_ref, ob_ref, qd, qsb, md, ld, accd, ms, ls, accs = rest[4 * n_pg:]
    p = pl.program_id(1)
    n_p = pl.num_programs(1)
    n_kv = n_hb // group

    @pl.when(p == 0)
    def _():
        qd_f = _head_rows(qa_ref[0].astype(F32), 2 * n_ha, 2)
        rd = lax.broadcasted_iota(I32, qd_f.shape, 0)
        ln = lax.broadcasted_iota(I32, qd_f.shape, 1)
        qd_f = jnp.where(ln // DH_A == rd % 2, qd_f, 0.0)
        qd[...] = qd_f.astype(BF16)
        qs_f = _head_rows(qs_ref[0].astype(F32), n_hb, 1)
        qsb[...] = qs_f.astype(BF16)
        kan = _head_rows(kan_ref[0].astype(BF16).astype(F32), 2 * n_ha, 2)
        van = _head_rows(van_ref[0].astype(BF16).astype(F32), 2 * n_ha, 2)
        md[...] = jnp.sum(qd_f * kan, axis=-1, keepdims=True)
        ld[...] = jnp.ones(ld.shape, F32)
        accd[...] = van
        ksn = _head_rows(ksn_ref[0].astype(BF16).astype(F32), n_hb, group)
        vsn = _head_rows(vsn_ref[0].astype(BF16).astype(F32), n_hb, group)
        ms[...] = jnp.sum(qs_f * ksn, axis=-1, keepdims=True) + bnew_ref[0]
        ls[...] = jnp.ones(ls.shape, F32)
        accs[...] = vsn

    sd_list, vd_list, ss_list, vs_list = [], [], [], []
    for r in range(n_pg):
        kd = kd_refs[r][0].astype(BF16)
        sd = _dot_nt(qd[...], kd)
        rq = lax.broadcasted_iota(I32, sd.shape, 0)
        ck = lax.broadcasted_iota(I32, sd.shape, 1)
        sd_list.append(jnp.where(ck % n_ha == rq // 2, sd, NEG_BIG))
        vd_list.append(vd_refs[r][0].astype(BF16))
        ksp = ks_refs[r][0].astype(BF16)
        ss = _dot_nt(qsb[...], ksp) + bias_ref[0, r]
        rq = lax.broadcasted_iota(I32, ss.shape, 0)
        ck = lax.broadcasted_iota(I32, ss.shape, 1)
        ss_list.append(jnp.where(ck % n_kv == rq // group, ss, NEG_BIG))
        vs_list.append(vs_refs[r][0].astype(BF16))
    _flash_step_multi(sd_list, vd_list, md, ld, accd)
    _flash_step_multi(ss_list, vs_list, ms, ls, accs)

    @pl.when(p == n_p - 1)
    def _():
        lam = _lambda_full(lamp_ref[...], lam_init)
        od = accd[...] / ld[...]
        ga = ga_ref[0]
        subw = subw_ref[...]
        for h in range(n_ha):
            sl = slice(h * HEAD_DIM, (h + 1) * HEAD_DIM)
            o = od[2 * h:2 * h + 1, :] - lam * od[2 * h + 1:2 * h + 2, :]
            o = _rms_rows(o, subw) * (1.0 - lam_init)
            oa_ref[0, :, sl] = (o * ga[:, sl]).astype(BF16)
        os_ = accs[...] / ls[...]
        gs = gs_ref[0]
        for h in range(n_hb):
            sl = slice(h * HEAD_DIM, (h + 1) * HEAD_DIM)
            ob_ref[0, :, sl] = (os_[h:h + 1, :] * gs[:, sl]).astype(BF16)


def _sample_attn(page_table_flat, qa, qs, ka_new, va_new, ks_new, vs_new, bias_new, sga, sgs,
                 subw, lamp, cdk, cdv, csk, csv, bias_pages, n_pages, lam_init):
    nb, _, d_a = qa.shape
    d_b = qs.shape[2]
    d_kv = ks_new.shape[2]
    n_ha = d_a // HEAD_DIM
    n_hb = d_b // HEAD_DIM
    n_kv = d_kv // HEAD_DIM
    group = n_hb // n_kv
    rows_d = cdk.shape[1]
    rows_s = csk.shape[1]
    per_b = lambda bi, p, pt: (bi, 0, 0)
    fixed = lambda bi, p, pt: (0, 0)
    n_pg = SAMPLE_PAGES_PER_STEP if n_pages % SAMPLE_PAGES_PER_STEP == 0 else 1

    def paged(r):
        return lambda bi, p, pt: (pt[bi * n_pages + p * n_pg + r], 0, 0)

    grid_spec = pltpu.PrefetchScalarGridSpec(
        num_scalar_prefetch=1,
        grid=(nb, n_pages // n_pg),
        in_specs=[
            pl.BlockSpec((1, 1, d_a), per_b),
            pl.BlockSpec((1, 1, d_b), per_b),
            pl.BlockSpec((1, 1, d_a), per_b),
            pl.BlockSpec((1, 1, d_a), per_b),
            pl.BlockSpec((1, 1, d_kv), per_b),
            pl.BlockSpec((1, 1, d_kv), per_b),
            pl.BlockSpec((1, 1, 1), per_b),
            pl.BlockSpec((1, 1, d_a), per_b),
            pl.BlockSpec((1, 1, d_b), per_b),
            pl.BlockSpec((1, HEAD_DIM), fixed),
            pl.BlockSpec((4, DH_A), fixed),
        ] + [pl.BlockSpec((1, rows_d, HEAD_DIM), paged(r)) for r in range(n_pg)] * 2
        + [pl.BlockSpec((1, rows_s, HEAD_DIM), paged(r)) for r in range(n_pg)] * 2
        + [pl.BlockSpec((1, n_pg, 1, rows_s), lambda bi, p, pt: (bi, p, 0, 0))],
        out_specs=[pl.BlockSpec((1, 1, d_a), per_b), pl.BlockSpec((1, 1, d_b), per_b)],
        scratch_shapes=[
            pltpu.VMEM((2 * n_ha, HEAD_DIM), BF16), pltpu.VMEM((n_hb, HEAD_DIM), BF16),
            pltpu.VMEM((2 * n_ha, 1), F32), pltpu.VMEM((2 * n_ha, 1), F32),
            pltpu.VMEM((2 * n_ha, HEAD_DIM), F32),
            pltpu.VMEM((n_hb, 1), F32), pltpu.VMEM((n_hb, 1), F32), pltpu.VMEM((n_hb, HEAD_DIM), F32),
        ],
    )
    return pl.pallas_call(
        functools.partial(_sample_attn_kernel, n_pg=n_pg, n_ha=n_ha, n_hb=n_hb, group=group,
                          lam_init=lam_init),
        grid_spec=grid_spec,
        out_shape=[jax.ShapeDtypeStruct((nb, 1, d_a), BF16), jax.ShapeDtypeStruct((nb, 1, d_b), BF16)],
        compiler_params=_cparams(2),
        name="sample_attn",
    )(page_table_flat, qa, qs, ka_new, va_new, ks_new, vs_new, bias_new, sga, sgs, subw, lamp,
      *([cdk] * n_pg + [cdv] * n_pg + [csk] * n_pg + [csv] * n_pg), bias_pages)


def _row_tile(m, pref):
    return pref if m % pref == 0 else m


def _inproj_all(x2d, pos, wts, tm, emit_vt):
    tabs64 = _rope_tables(pos, DH_A)
    tabs128 = _rope_tables(pos, HEAD_DIM)
    d_a, d_b, d_kv, d_qi = wts["d_a"], wts["d_b"], wts["d_kv"], wts["d_qi"]
    outs_a = _inproj_a(x2d, wts["pre_w"], wts["w_a"], tabs64, tm, d_a, emit_vt)
    outs_b = _inproj_b(x2d, wts["pre_w"], wts["w_b"], wts["w_tail"], tabs128, tabs64, tm,
                       d_b, d_kv, d_qi, emit_vt)
    return outs_a, outs_b


def kernel(x_prompt, x_sample, p_prompt, p_sample, cache_diff_k, cache_diff_v, cache_dsa_k, cache_dsa_v, cache_idx_k, page_table, pre_norm_w, post_norm_w, w_in, lam_q1, lam_k1, lam_q2, lam_k2, diff_norm_w, w_out, w_ple_gate, b_ple_gate, w_ple_proj):
    depth = w_in.shape[0]
    assert depth == 1, "single-layer stack only"
    bsz, t_p, d = x_prompt.shape
    nb, t_s, _ = x_sample.shape
    assert t_s == 1, "one new token per sample sequence"
    n_pages = page_table.shape[1]
    n_pool = cache_diff_k.shape[1]
    page = cache_diff_k.shape[2]
    n_ha = cache_diff_k.shape[3]
    n_kv = cache_dsa_k.shape[3]
    d_a = n_ha * HEAD_DIM
    d_kv = n_kv * HEAD_DIM
    d_b = d - d_a
    d_qi = H_IDX * D_IDX
    past_len = n_pages * page
    lam_init = 0.8 - 0.6 * math.exp(-0.3 * 0)
    n_in = w_in.shape[2]

    n_main = 4 * d_a + 2 * d_b + 2 * d_kv + d_qi
    assert n_in - n_main == D_IDX + H_IDX
    w_a = jnp.swapaxes(w_in[0, :, :4 * d_a], 0, 1).astype(BF16)
    w_b = jnp.swapaxes(w_in[0, :, 4 * d_a:n_main], 0, 1).astype(BF16)
    w_tail = jnp.pad(jnp.swapaxes(w_in[0, :, n_main:], 0, 1),
                     ((0, LANES - (n_in - n_main)), (0, 0))).astype(BF16)
    wts = dict(pre_w=pre_norm_w[0][None, :], w_a=w_a, w_b=w_b, w_tail=w_tail,
               d_a=d_a, d_b=d_b, d_kv=d_kv, d_qi=d_qi)
    wo = w_out[0].astype(BF16)
    wg = w_ple_gate[0].astype(BF16)
    we = w_ple_proj[0].astype(BF16)
    pw = post_norm_w[0][None, :]
    bg = b_ple_gate[0][None, :]
    subw = diff_norm_w[0][None, :]
    lamp = jnp.stack([lam_q1[0], lam_k1[0], lam_q2[0], lam_k2[0]], axis=0)

    m_p = bsz * t_p
    tm = _row_tile(t_p, ROW_TILE)
    tq_diff = _row_tile(t_p, DIFF_TILE)
    tq_dsa = _row_tile(t_p, DSA_TILE)
    tq_idx = _row_tile(t_p, IDX_TILE)
    xp2 = x_prompt.reshape(m_p, d)
    pos_p = jnp.arange(t_p, dtype=I32)
    (qa, ka, va, sga, va_t), (qs, ks, vs, sgs, qi, kiw, vs_t) = _inproj_all(xp2, pos_p, wts, tm, True)
    r3 = lambda a: a.reshape(bsz, t_p, a.shape[-1])
    a1 = _diff_attn(r3(qa), r3(ka), va_t, r3(sga), subw, lamp, lam_init, tq_diff)
    ki = kiw[:, :D_IDX]
    wi = kiw[:, D_IDX:D_IDX + H_IDX]
    wi_t = wi.reshape(bsz, t_p, H_IDX).transpose(0, 2, 1)
    k_sel_p = min(TOPK_MAX, t_p // 4)
    bias_t = _index_select(r3(qi), r3(kiw), wi_t, tq_idx, k_sel_p)
    a2 = _dsa_attn(r3(qs), r3(ks), vs_t, bias_t, r3(sgs), tq_dsa)
    y_p = _out_proj(xp2, a1.reshape(m_p, d_a), a2.reshape(m_p, d_b), p_prompt[0].reshape(m_p, -1),
                    wo, pw, wg, bg, we, tm)

    xs2 = x_sample.reshape(nb, d)
    pos_s = jnp.full((nb,), past_len, dtype=I32)
    (qa_s, ka_s, va_s, sga_s), (qs_s, ks_s, vs_s, sgs_s, qi_s, kiw_s) = _inproj_all(xs2, pos_s, wts, nb, False)
    ki_s = kiw_s[:, :D_IDX]
    wi_s = kiw_s[:, D_IDX:D_IDX + H_IDX]
    pt_flat = page_table.reshape(-1)
    n_group = SCORE_PAGES_PER_STEP if n_pages % SCORE_PAGES_PER_STEP == 0 else 1
    cache_idx_t = jnp.swapaxes(cache_idx_k.reshape(n_pool, page, D_IDX), 1, 2)
    scores = _sample_scores(pt_flat, qi_s.reshape(nb, H_IDX, D_IDX), wi_s.reshape(nb, H_IDX, 1),
                            cache_idx_t, n_pages, n_group)
    k_sel_s = min(TOPK_MAX, (past_len + t_s) // 4)
    bias_s = _sample_select(scores.reshape(nb, past_len), qi_s,
                            jnp.tile(ki_s.astype(BF16), (1, H_IDX)), wi_s, k_sel_s)
    bias_pages = jnp.repeat(bias_s[:, :past_len], n_kv, axis=1).reshape(nb, n_pages, 1, page * n_kv)
    bias_new = bias_s[:, past_len:past_len + 1].reshape(nb, 1, 1)
    e3 = lambda a: a.reshape(nb, 1, a.shape[-1])
    a1_s, a2_s = _sample_attn(
        pt_flat, e3(qa_s), e3(qs_s), e3(ka_s), e3(va_s), e3(ks_s), e3(vs_s), bias_new,
        e3(sga_s), e3(sgs_s), subw, lamp,
        cache_diff_k.reshape(n_pool, page * n_ha, HEAD_DIM), cache_diff_v.reshape(n_pool, page * n_ha, HEAD_DIM),
        cache_dsa_k.reshape(n_pool, page * n_kv, HEAD_DIM), cache_dsa_v.reshape(n_pool, page * n_kv, HEAD_DIM),
        bias_pages, n_pages, lam_init)
    y_s = _out_proj(xs2, a1_s.reshape(nb, d_a), a2_s.reshape(nb, d_b), p_sample[0].reshape(nb, -1),
                    wo, pw, wg, bg, we, nb)

    return (
        y_p.reshape(bsz, t_p, d), y_s.reshape(nb, t_s, d),
        ka.reshape(1, bsz, t_p, n_ha, HEAD_DIM), va.reshape(1, bsz, t_p, n_ha, HEAD_DIM),
        ks.reshape(1, bsz, t_p, n_kv, HEAD_DIM), vs.reshape(1, bsz, t_p, n_kv, HEAD_DIM),
        ki.reshape(1, bsz, t_p, D_IDX),
        ka_s.reshape(1, nb, t_s, n_ha, HEAD_DIM), va_s.reshape(1, nb, t_s, n_ha, HEAD_DIM),
        ks_s.reshape(1, nb, t_s, n_kv, HEAD_DIM), vs_s.reshape(1, nb, t_s, n_kv, HEAD_DIM),
        ki_s.reshape(1, nb, t_s, D_IDX),
    )
```

```python
import functools
import math

import jax
import jax.numpy as jnp
from jax import lax
from jax.experimental import pallas as pl
from jax.experimental.pallas import tpu as pltpu

F32 = jnp.float32
BF16 = jnp.bfloat16
I32 = jnp.int32

LANES = 128
SUBLANES = 8
HEAD_DIM = 128
DH_A = HEAD_DIM // 2
D_IDX = 64
H_IDX = 4
TOPK_MAX = 256
ROPE_THETA = 500000.0
ROPE_FRAC = 4
RMS_EPS = 1e-6
NEG_BIG = -1e30
INT_MIN = -(2 ** 31)
NEG_INF_KEY = INT_MIN + 0x7FFFFF
VMEM_LIMIT = 56 * 1024 * 1024
ROW_TILE = 256
LOG2E = math.log2(math.e)
DIFF_TILE = 512
DIFF_HEADS_PER_STEP = 2
DIFF_SOFTMAX_WIDTH = 256
DSA_TILE = 256
DSA_KV_HEADS_PER_STEP = 1
IDX_TILE = 512
TIE_BLOCK = 256
SAMPLE_PAGES_PER_STEP = 8
SCORE_PAGES_PER_STEP = 32


def _cparams(n_axes):
    return pltpu.CompilerParams(
        dimension_semantics=("arbitrary",) * n_axes, vmem_limit_bytes=VMEM_LIMIT)


def _dot(a, b):
    return jnp.dot(a, b, preferred_element_type=F32)


def _dot_nt(a, b):
    return lax.dot_general(a, b, (((1,), (1,)), ((), ())), preferred_element_type=F32)


def _rope_tables(pos, d):
    r = d // ROPE_FRAC
    half = r // 2
    inv = ROPE_THETA ** (-(2.0 / r) * jnp.arange(half, dtype=F32))
    ang = pos.astype(F32)[:, None] * inv[None, :]
    cos, sin = jnp.cos(ang), jnp.sin(ang)
    t = pos.shape[0]
    ones = jnp.ones((t, d - r), F32)
    zeros_h = jnp.zeros((t, half), F32)
    zeros_r = jnp.zeros((t, d - r), F32)
    c = jnp.concatenate([cos, cos, ones], axis=-1)
    sm = jnp.concatenate([-sin, zeros_h, zeros_r], axis=-1)
    sp = jnp.concatenate([zeros_h, sin, zeros_r], axis=-1)
    rep = LANES // d
    return tuple(jnp.tile(a, (1, rep)) for a in (c, sm, sp)), half


def _rope_chunk(z, c, sm, sp, half):
    return (z * c + pltpu.roll(z, LANES - half, axis=1) * sm
            + pltpu.roll(z, half, axis=1) * sp)


def _silu(z):
    return z * (1.0 / (1.0 + jnp.exp(-z)))


def _rms_rows(x, w):
    return x * lax.rsqrt(jnp.mean(x * x, axis=-1, keepdims=True) + RMS_EPS) * w


def _inproj_a_kernel(x_ref, nw_ref, w_ref, c_ref, sm_ref, sp_ref,
                     qa_ref, ka_ref, va_ref, ga_ref, *maybe_vt_ref, half, d_a):
    h = _rms_rows(x_ref[...], nw_ref[...]).astype(BF16)
    c, sm, sp = c_ref[...], sm_ref[...], sp_ref[...]
    n_chunks = d_a // LANES
    zq = _dot_nt(h, w_ref[0:d_a, :])
    for j in range(n_chunks):
        sl = slice(j * LANES, (j + 1) * LANES)
        qa_ref[:, sl] = (_rope_chunk(zq[:, sl], c, sm, sp, half) * (LOG2E / math.sqrt(DH_A))).astype(BF16)
    zk = _dot_nt(h, w_ref[d_a:2 * d_a, :])
    for j in range(n_chunks):
        sl = slice(j * LANES, (j + 1) * LANES)
        ka_ref[:, sl] = _rope_chunk(zk[:, sl], c, sm, sp, half)
    zv = _dot_nt(h, w_ref[2 * d_a:3 * d_a, :])
    va_ref[...] = zv
    if maybe_vt_ref:
        maybe_vt_ref[0][...] = zv.T.astype(BF16)
    ga_ref[...] = _silu(_dot_nt(h, w_ref[3 * d_a:4 * d_a, :]))


def _inproj_a(x2d, nw, w_a, tabs64, tm, d_a, emit_vt):
    m, d = x2d.shape
    (c, sm, sp), half = tabs64
    t_blocks = c.shape[0] // tm
    row = lambda i: (i, 0)
    tab = lambda i: (i % t_blocks, 0)
    fixed = lambda i: (0, 0)
    out_specs = [pl.BlockSpec((tm, d_a), row)] * 4
    out_shape = [
        jax.ShapeDtypeStruct((m, d_a), BF16),
        jax.ShapeDtypeStruct((m, d_a), F32),
        jax.ShapeDtypeStruct((m, d_a), F32),
        jax.ShapeDtypeStruct((m, d_a), F32),
    ]
    if emit_vt:
        out_specs.append(pl.BlockSpec((d_a, tm), lambda i: (0, i)))
        out_shape.append(jax.ShapeDtypeStruct((d_a, m), BF16))
    return pl.pallas_call(
        functools.partial(_inproj_a_kernel, half=half, d_a=d_a),
        grid=(m // tm,),
        in_specs=[
            pl.BlockSpec((tm, d), row),
            pl.BlockSpec((1, d), fixed),
            pl.BlockSpec((4 * d_a, d), fixed, pipeline_mode=pl.Buffered(1)),
            pl.BlockSpec((tm, LANES), tab),
            pl.BlockSpec((tm, LANES), tab),
            pl.BlockSpec((tm, LANES), tab),
        ],
        out_specs=out_specs,
        out_shape=out_shape,
        compiler_params=_cparams(1),
        name="inproj_a",
    )(x2d, nw, w_a, c, sm, sp)


def _inproj_b_kernel(x_ref, nw_ref, w_ref, wt_ref, c128_ref, sm128_ref, sp128_ref,
                     c64_ref, sm64_ref, sp64_ref,
                     qs_ref, ks_ref, vs_ref, gs_ref, qi_ref, kiw_ref, *maybe_vt_ref,
                     half128, half64, d_b, d_kv, d_qi):
    h = _rms_rows(x_ref[...], nw_ref[...]).astype(BF16)
    c128, sm128, sp128 = c128_ref[...], sm128_ref[...], sp128_ref[...]
    c64, sm64, sp64 = c64_ref[...], sm64_ref[...], sp64_ref[...]
    o = 0
    zq = _dot_nt(h, w_ref[o:o + d_b, :])
    for j in range(d_b // LANES):
        sl = slice(j * LANES, (j + 1) * LANES)
        qs_ref[:, sl] = (_rope_chunk(zq[:, sl], c128, sm128, sp128, half128)
                         * (LOG2E / math.sqrt(HEAD_DIM))).astype(BF16)
    o += d_b
    zk = _dot_nt(h, w_ref[o:o + d_kv, :])
    for j in range(d_kv // LANES):
        sl = slice(j * LANES, (j + 1) * LANES)
        ks_ref[:, sl] = _rope_chunk(zk[:, sl], c128, sm128, sp128, half128)
    o += d_kv
    zv = _dot_nt(h, w_ref[o:o + d_kv, :])
    vs_ref[...] = zv
    if maybe_vt_ref:
        maybe_vt_ref[0][...] = zv.T.astype(BF16)
    o += d_kv
    gs_ref[...] = _silu(_dot_nt(h, w_ref[o:o + d_b, :]))
    o += d_b
    zi = _dot_nt(h, w_ref[o:o + d_qi, :])
    for j in range(d_qi // LANES):
        sl = slice(j * LANES, (j + 1) * LANES)
        qi_ref[:, sl] = _rope_chunk(zi[:, sl], c64, sm64, sp64, half64).astype(BF16)
    zkw = _dot_nt(h, wt_ref[...])
    lane = lax.broadcasted_iota(I32, zkw.shape, 1)
    kiw_ref[...] = jnp.where(lane < D_IDX, _rope_chunk(zkw, c64, sm64, sp64, half64), zkw)


def _inproj_b(x2d, nw, w_b, w_tail, tabs128, tabs64, tm, d_b, d_kv, d_qi, emit_vt):
    m, d = x2d.shape
    (c128, sm128, sp128), half128 = tabs128
    (c64, sm64, sp64), half64 = tabs64
    t_blocks = c128.shape[0] // tm
    row = lambda i: (i, 0)
    tab = lambda i: (i % t_blocks, 0)
    fixed = lambda i: (0, 0)
    n_b = w_b.shape[0]
    out_specs = [
        pl.BlockSpec((tm, d_b), row),
        pl.BlockSpec((tm, d_kv), row),
        pl.BlockSpec((tm, d_kv), row),
        pl.BlockSpec((tm, d_b), row),
        pl.BlockSpec((tm, d_qi), row),
        pl.BlockSpec((tm, LANES), row),
    ]
    out_shape = [
        jax.ShapeDtypeStruct((m, d_b), BF16),
        jax.ShapeDtypeStruct((m, d_kv), F32),
        jax.ShapeDtypeStruct((m, d_kv), F32),
        jax.ShapeDtypeStruct((m, d_b), F32),
        jax.ShapeDtypeStruct((m, d_qi), BF16),
        jax.ShapeDtypeStruct((m, LANES), F32),
    ]
    if emit_vt:
        out_specs.append(pl.BlockSpec((d_kv, tm), lambda i: (0, i)))
        out_shape.append(jax.ShapeDtypeStruct((d_kv, m), BF16))
    return pl.pallas_call(
        functools.partial(_inproj_b_kernel, half128=half128, half64=half64,
                          d_b=d_b, d_kv=d_kv, d_qi=d_qi),
        grid=(m // tm,),
        in_specs=[
            pl.BlockSpec((tm, d), row),
            pl.BlockSpec((1, d), fixed),
            pl.BlockSpec((n_b, d), fixed, pipeline_mode=pl.Buffered(1)),
            pl.BlockSpec((LANES, d), fixed),
        ] + [pl.BlockSpec((tm, LANES), tab)] * 6,
        out_specs=out_specs,
        out_shape=out_shape,
        compiler_params=_cparams(1),
        name="inproj_b",
    )(x2d, nw, w_b, w_tail, c128, sm128, sp128, c64, sm64, sp64)


def _flash_init(m_ref, l_ref, acc_ref):
    m_ref[...] = jnp.full(m_ref.shape, NEG_BIG, F32)
    l_ref[...] = jnp.zeros(l_ref.shape, F32)
    acc_ref[...] = jnp.zeros(acc_ref.shape, F32)


def _flash_step(s, v_bf, m_ref, l_ref, acc_ref, axis=1):
    m_prev = m_ref[...]
    m_new = jnp.maximum(m_prev, jnp.max(s, axis=axis, keepdims=True))
    alpha = jnp.exp2(m_prev - m_new)
    p = jnp.exp2(s - m_new)
    l_ref[...] = alpha * l_ref[...] + jnp.sum(p, axis=axis, keepdims=True)
    pv = _dot(p.astype(BF16), v_bf) if axis == 1 else _dot(v_bf, p.astype(BF16))
    acc_ref[...] = alpha * acc_ref[...] + pv
    m_ref[...] = m_new


def _flash_step_multi(s_list, v_list, m_ref, l_ref, acc_ref):
    m_prev = m_ref[...]
    m_new = m_prev
    for s in s_list:
        m_new = jnp.maximum(m_new, jnp.max(s, axis=1, keepdims=True))
    alpha = jnp.exp2(m_prev - m_new)
    l_new = alpha * l_ref[...]
    acc = alpha * acc_ref[...]
    for s, v_bf in zip(s_list, v_list):
        p = jnp.exp2(s - m_new)
        l_new = l_new + jnp.sum(p, axis=1, keepdims=True)
        acc = acc + _dot(p.astype(BF16), v_bf)
    l_ref[...] = l_new
    acc_ref[...] = acc
    m_ref[...] = m_new


def _lambda_full(lamp, lam_init):
    s1 = jnp.sum(lamp[0:1, :] * lamp[1:2, :], axis=-1, keepdims=True)
    s2 = jnp.sum(lamp[2:3, :] * lamp[3:4, :], axis=-1, keepdims=True)
    return jnp.exp(s1) - jnp.exp(s2) + lam_init


def _diff_attn_kernel(q_ref, k_ref, vt_ref, g_ref, subw_ref, lamp_ref, o_ref,
                      kbf, sb, mb, lb, accb, *, tq, n_hs, lam_init):
    i = pl.program_id(2)

    @pl.when(i == 0)
    def _():
        kbf[...] = k_ref[0].astype(BF16)

    q = q_ref[0]
    lane = lax.broadcasted_iota(I32, (tq, HEAD_DIM), 1)
    qm = []
    for h in range(n_hs):
        qh = q[:, h * HEAD_DIM:(h + 1) * HEAD_DIM]
        zero = jnp.zeros_like(qh)
        qm.append((jnp.where(lane < DH_A, qh, zero), jnp.where(lane >= DH_A, qh, zero)))
        for mp in range(2):
            _flash_init(mb.at[h, mp], lb.at[h, mp], accb.at[h, mp])

    def scores(j, slot):
        off = pl.multiple_of(j * tq, tq)
        for h in range(n_hs):
            kb = kbf[pl.ds(off, tq), h * HEAD_DIM:(h + 1) * HEAD_DIM]
            for mp in range(2):
                sb[h, mp, slot] = _dot_nt(kb, qm[h][mp])

    qw = DIFF_SOFTMAX_WIDTH if tq % DIFF_SOFTMAX_WIDTH == 0 else tq

    def softmax_pv(j, slot, masked):
        off = pl.multiple_of(j * tq, tq)
        for h in range(n_hs):
            vtb = vt_ref[h * HEAD_DIM:(h + 1) * HEAD_DIM, pl.ds(off, tq)]
            for mp in range(2):
                for q0 in range(0, tq, qw):
                    qs = slice(q0, q0 + qw)
                    k_end = q0 + qw if masked else tq
                    s = sb[h, mp, slot, 0:k_end, qs]
                    if masked:
                        kidx = lax.broadcasted_iota(I32, s.shape, 0)
                        qidx = lax.broadcasted_iota(I32, s.shape, 1) + q0
                        s = jnp.where(kidx <= qidx, s, NEG_BIG)
                    _flash_step(s, vtb[:, 0:k_end], mb.at[h, mp, :, qs], lb.at[h, mp, :, qs],
                                accb.at[h, mp, :, qs], axis=0)

    scores(0, 0)

    def pair(jj, carry):
        j = 2 * jj
        scores(j + 1, 1)
        softmax_pv(j, 0, False)
        scores(j + 2, 0)
        softmax_pv(j + 1, 1, False)
        return carry

    lax.fori_loop(0, i // 2, pair, 0)

    @pl.when(i % 2 == 1)
    def _():
        scores(i, 1)
        softmax_pv(i - 1, 0, False)
        softmax_pv(i, 1, True)

    @pl.when(i % 2 == 0)
    def _():
        softmax_pv(i, 0, True)

    lam = _lambda_full(lamp_ref[...], lam_init)
    g = g_ref[0]
    for h in range(n_hs):
        sl = slice(h * HEAD_DIM, (h + 1) * HEAD_DIM)
        o_t = accb[h, 0] / lb[h, 0] - lam * (accb[h, 1] / lb[h, 1])
        o = _rms_rows(o_t.T, subw_ref[...]) * (1.0 - lam_init)
        o_ref[0, :, sl] = (o * g[:, sl]).astype(BF16)


def _diff_attn(qa, ka, va_t, sga, subw, lamp, lam_init, tq):
    b, t, d_a = qa.shape
    n_h = d_a // HEAD_DIM
    n_hs = DIFF_HEADS_PER_STEP if n_h % DIFF_HEADS_PER_STEP == 0 else 1
    hw = n_hs * HEAD_DIM
    qmap = lambda bi, h, i: (bi, i, h)
    kmap = lambda bi, h, i: (bi, 0, h)
    fixed = lambda bi, h, i: (0, 0)
    return pl.pallas_call(
        functools.partial(_diff_attn_kernel, tq=tq, n_hs=n_hs, lam_init=lam_init),
        grid=(b, n_h // n_hs, t // tq),
        in_specs=[
            pl.BlockSpec((1, tq, hw), qmap),
            pl.BlockSpec((1, t, hw), kmap),
            pl.BlockSpec((hw, t), lambda bi, h, i: (h, bi)),
            pl.BlockSpec((1, tq, hw), qmap),
            pl.BlockSpec((1, HEAD_DIM), fixed),
            pl.BlockSpec((4, DH_A), fixed),
        ],
        out_specs=pl.BlockSpec((1, tq, hw), qmap),
        out_shape=jax.ShapeDtypeStruct((b, t, d_a), BF16),
        scratch_shapes=[
            pltpu.VMEM((t, hw), BF16),
            pltpu.VMEM((n_hs, 2, 2, tq, tq), F32),
            pltpu.VMEM((n_hs, 2, 1, tq), F32), pltpu.VMEM((n_hs, 2, 1, tq), F32),
            pltpu.VMEM((n_hs, 2, HEAD_DIM, tq), F32),
        ],
        compiler_params=_cparams(3),
        name="diff_attn",
    )(qa, ka, va_t, sga, subw, lamp)


def _key_to_float(key):
    key = jnp.maximum(key, NEG_INF_KEY)
    return lax.bitcast_convert_type(key ^ ((key >> 31) & 0x7FFFFFFF), F32)


def _fold(m, axis):
    if axis == 1:
        n = m.shape[1] // LANES
        acc = m[:, 0:LANES]
        for j in range(1, n):
            acc = acc + m[:, j * LANES:(j + 1) * LANES]
        return acc
    return jnp.sum(m.reshape(m.shape[0] // SUBLANES, SUBLANES, m.shape[1]), axis=0)


def _count(key_ref, n_chunks, chunk, pred, axis):
    other = key_ref.shape[1 - axis]
    acc_shape = (other, LANES) if axis == 1 else (SUBLANES, other)

    def body(c, acc):
        off = pl.multiple_of(c * chunk, chunk)
        kc = key_ref[:, pl.ds(off, chunk)] if axis == 1 else key_ref[pl.ds(off, chunk), :]
        return acc + _fold(pred(kc, off, slice(None)), axis)

    acc = lax.fori_loop(0, n_chunks, body, jnp.zeros(acc_shape, F32))
    return jnp.sum(acc, axis=axis, keepdims=True)


def _valu_counter(key_ref, n_chunks, chunk, axis):
    return lambda preds: [_count(key_ref, n_chunks, chunk, p, axis) for p in preds]


def _prefix_key(pu):
    key = lax.shift_left(pu, 16) ^ INT_MIN
    return key | ((key >> 31) & 0xFFFF)


def _count_bf16(hb_ref, n_chunks, chunk, cand):
    tq = hb_ref.shape[1]
    pack = 2 * SUBLANES
    candb = jnp.broadcast_to(cand.astype(BF16), (pack, tq))
    one, zero = jnp.ones((), BF16), jnp.zeros((), BF16)

    def body(c, acc):
        off = pl.multiple_of(c * chunk, chunk)
        kc = hb_ref[pl.ds(off, chunk), :].reshape(chunk // pack, pack, tq)
        part = jnp.where(kc[0] >= candb, one, zero)
        for r in range(1, chunk // pack):
            part = part + jnp.where(kc[r] >= candb, one, zero)
        return acc + part.astype(F32)

    acc = lax.fori_loop(0, n_chunks, body, jnp.zeros((pack, tq), F32))
    return jnp.sum(acc, axis=0, keepdims=True)


def _select_topk(counter, k_sel, axis, q_shape, coarse_count=None, tie_index=None):
    k_f = float(k_sel)

    def count_ge(cand):
        (cnt,) = counter([lambda kc, off, cs: jnp.where(kc >= cand[:, cs], 1.0, 0.0)])
        return cnt

    if coarse_count is None:
        def bit_body(it, carry):
            tu, cnt_tu = carry
            cand_u = tu | lax.shift_left(jnp.int32(1), 31 - it)
            cnt = count_ge(_key_to_float(cand_u ^ INT_MIN))
            ok = cnt >= k_f
            return jnp.where(ok, cand_u, tu), jnp.where(ok, cnt, cnt_tu)

        lowest = jnp.full(q_shape, -jnp.inf, F32)
        tu, cnt_ge = lax.fori_loop(0, 32, bit_body, (jnp.zeros(q_shape, I32), count_ge(lowest)))
        key = tu ^ INT_MIN
    else:
        def coarse_body(it, pu):
            cand_p = pu | lax.shift_left(jnp.int32(1), 15 - it)
            cnt = coarse_count(_key_to_float(_prefix_key(cand_p)))
            return jnp.where(cnt >= k_f, cand_p, pu)

        pu = lax.fori_loop(0, 16, coarse_body, jnp.zeros(q_shape, I32))
        base = _prefix_key(pu) - 2 ** 16

        def fine_body(it, carry):
            x, cnt_x = carry
            cand_x = x | lax.shift_left(jnp.int32(1), 16 - it)
            cnt = count_ge(_key_to_float(base + cand_x))
            ok = cnt >= k_f
            return jnp.where(ok, cand_x, x), jnp.where(ok, cnt, cnt_x)

        x, cnt_ge = lax.fori_loop(0, 17, fine_body,
                                  (jnp.zeros(q_shape, I32), jnp.full(q_shape, 2.0 ** 30, F32)))
        key = base + x

    thr = _key_to_float(key)
    has_ties = jnp.max(cnt_ge) > k_f
    if tie_index is None:
        return thr, has_ties
    p_ref, idx_bits = tie_index
    p_ref[...] = jnp.full(q_shape, 2 ** 30, I32)

    @pl.when(has_ties)
    def _():
        (n_gt,) = counter([lambda kc, off, cs: jnp.where(kc > thr[:, cs], 1.0, 0.0)])
        need = k_f - n_gt

        def idx_body(it, p):
            bit = lax.shift_left(jnp.int32(1), idx_bits - 1 - it)
            cand = p | bit

            def pred(kc, off, cs):
                idx = off + lax.broadcasted_iota(I32, kc.shape, axis)
                return jnp.where(kc == thr[:, cs], jnp.where(idx < cand[:, cs], 1.0, 0.0), 0.0)

            (cnt,) = counter([pred])
            return jnp.where(cnt < need, cand, p)

        p_ref[...] = lax.fori_loop(0, idx_bits, idx_body, jnp.zeros(q_shape, I32))

    return thr, has_ties


def _selected_bias(kc, off, thr, p_max, axis):
    idx = off + lax.broadcasted_iota(I32, kc.shape, axis)
    tie = jnp.where(kc == thr, jnp.where(idx <= p_max, 1.0, 0.0), 0.0)
    sel = jnp.where(kc > thr, 1.0, tie)
    sel = jnp.where(kc == -jnp.inf, 0.0, sel)
    return jnp.where(sel > 0.5, 0.0, NEG_BIG)


def _index_select_kernel(qi_ref, ki_ref, wit_ref, bias_ref, key_ref, hb_ref, *, tq, k_sel):
    i = pl.program_id(1)
    n_total = bias_ref.shape[1] // tq
    wit = wit_ref[0]

    lane = lax.broadcasted_iota(I32, (tq, LANES), 1)
    qf = qi_ref[0].astype(F32)
    qh = []
    for h in range(H_IDX):
        grp = qf[:, (h // 2) * LANES:(h // 2 + 1) * LANES]
        if h % 2 == 1:
            grp = pltpu.roll(grp, D_IDX, axis=1)
        qh.append(jnp.where(lane < D_IDX, grp, 0.0).astype(BF16))

    def score_body(c, carry):
        off = pl.multiple_of(c * tq, tq)
        kc = jnp.where(lane < D_IDX, ki_ref[0, pl.ds(off, tq), :], 0.0).astype(BF16)
        sc = jnp.zeros((tq, tq), F32)
        for h in range(H_IDX):
            sc = sc + jnp.maximum(_dot_nt(kc, qh[h]), 0.0) * wit[h:h + 1, :]
        kidx = lax.broadcasted_iota(I32, sc.shape, 0) + off
        qidx = lax.broadcasted_iota(I32, sc.shape, 1) + i * tq
        sc = jnp.where(kidx <= qidx, sc, -jnp.inf)
        key_ref[pl.ds(off, tq), :] = sc
        hb_ref[pl.ds(off, tq), :] = sc.astype(BF16)
        return carry

    lax.fori_loop(0, i + 1, score_body, 0)

    counter = _valu_counter(key_ref, i + 1, tq, 0)
    thr, has_ties = _select_topk(counter, k_sel, axis=0, q_shape=(1, tq),
                                 coarse_count=functools.partial(_count_bf16, hb_ref, i + 1, tq))

    def out_with_ties():
        (n_gt,) = counter([lambda kc, off, cs: jnp.where(kc > thr[:, cs], 1.0, 0.0)])
        need = float(k_sel) - n_gt
        sub = TIE_BLOCK if tq % TIE_BLOCK == 0 else tq
        r = lax.broadcasted_iota(I32, (sub, sub), 0)
        col = lax.broadcasted_iota(I32, (sub, sub), 1)
        tri = jnp.where(col <= r, 1.0, 0.0).astype(BF16)

        def out_body(c, seen):
            off = pl.multiple_of(c * tq, tq)
            kc = key_ref[pl.ds(off, tq), :]
            eq = jnp.where(kc == thr, 1.0, 0.0)
            eq_bf = eq.astype(BF16)
            parts = []
            for j in range(tq // sub):
                parts.append(_dot(tri, eq_bf[j * sub:(j + 1) * sub, :]) + seen)
                seen = parts[-1][sub - 1:sub, :]
            rank = jnp.concatenate(parts, axis=0)
            sel = jnp.where(kc > thr, 1.0, jnp.where(rank <= need, eq, 0.0))
            sel = jnp.where(kc == -jnp.inf, 0.0, sel)
            bias_ref[0, pl.ds(off, tq), :] = jnp.where(sel > 0.5, 0.0, NEG_BIG).astype(BF16)
            return rank[tq - 1:tq, :]

        lax.fori_loop(0, i + 1, out_body, jnp.zeros((1, tq), F32))

    def out_body_no_ties(c, carry):
        off = pl.multiple_of(c * tq, tq)
        kc = key_ref[pl.ds(off, tq), :]
        keep = jnp.where(kc == -jnp.inf, NEG_BIG, 0.0)
        bias_ref[0, pl.ds(off, tq), :] = jnp.where(kc >= thr, keep, NEG_BIG).astype(BF16)
        return carry

    @pl.when(has_ties)
    def _():
        out_with_ties()

    @pl.when(jnp.logical_not(has_ties))
    def _():
        lax.fori_loop(0, i + 1, out_body_no_ties, 0)

    def fill_body(c, carry):
        off = pl.multiple_of(c * tq, tq)
        bias_ref[0, pl.ds(off, tq), :] = jnp.full((tq, tq), NEG_BIG, BF16)
        return carry

    lax.fori_loop(i + 1, n_total, fill_body, 0)


def _index_select(qi, kiw, wi_t, tq, k_sel):
    b, t, d_qi = qi.shape
    assert d_qi == H_IDX * D_IDX and 2 * D_IDX == LANES and kiw.shape[2] == LANES
    return pl.pallas_call(
        functools.partial(_index_select_kernel, tq=tq, k_sel=k_sel),
        grid=(b, t // tq),
        in_specs=[
            pl.BlockSpec((1, tq, d_qi), lambda bi, i: (bi, i, 0)),
            pl.BlockSpec((1, t, LANES), lambda bi, i: (bi, 0, 0)),
            pl.BlockSpec((1, H_IDX, tq), lambda bi, i: (bi, 0, i)),
        ],
        out_specs=pl.BlockSpec((1, t, tq), lambda bi, i: (bi, 0, i)),
        out_shape=jax.ShapeDtypeStruct((b, t, t), BF16),
        scratch_shapes=[pltpu.VMEM((t, tq), F32), pltpu.VMEM((t, tq), BF16)],
        compiler_params=_cparams(2),
        name="index_select",
    )(qi, kiw, wi_t)


def _dsa_attn_kernel(q_ref, k_ref, vt_ref, bias_ref, g_ref, o_ref, kbf, sb, mb, lb, accb,
                     *, tq, group, n_kvs):
    i = pl.program_id(2)

    @pl.when(i == 0)
    def _():
        kbf[...] = k_ref[0].astype(BF16)

    q = q_ref[0]
    q4 = []
    for n in range(n_kvs):
        q4.append(jnp.concatenate(
            [q[:, (n * group + h) * HEAD_DIM:(n * group + h + 1) * HEAD_DIM] for h in range(group)],
            axis=0))
        _flash_init(mb.at[n], lb.at[n], accb.at[n])

    def scores(j, slot):
        off = pl.multiple_of(j * tq, tq)
        for n in range(n_kvs):
            kb = kbf[pl.ds(off, tq), n * HEAD_DIM:(n + 1) * HEAD_DIM]
            sb[n, slot] = _dot_nt(kb, q4[n])

    def softmax_pv(j, slot):
        off = pl.multiple_of(j * tq, tq)
        bias = bias_ref[0, pl.ds(off, tq), :].astype(F32)
        bias_g = jnp.concatenate([bias] * group, axis=1)
        for n in range(n_kvs):
            vtb = vt_ref[n * HEAD_DIM:(n + 1) * HEAD_DIM, pl.ds(off, tq)]
            _flash_step(sb[n, slot] + bias_g, vtb, mb.at[n], lb.at[n], accb.at[n], axis=0)

    scores(0, 0)

    def pair(jj, carry):
        j = 2 * jj
        scores(j + 1, 1)
        softmax_pv(j, 0)
        scores(jnp.minimum(j + 2, i), 0)
        softmax_pv(j + 1, 1)
        return carry

    lax.fori_loop(0, (i + 1) // 2, pair, 0)

    @pl.when(i % 2 == 0)
    def _():
        softmax_pv(i, 0)

    g = g_ref[0]
    for n in range(n_kvs):
        o_t = accb[n] / lb[n]
        for h in range(group):
            sl = slice((n * group + h) * HEAD_DIM, (n * group + h + 1) * HEAD_DIM)
            o_ref[0, :, sl] = (o_t[:, h * tq:(h + 1) * tq].T * g[:, sl]).astype(BF16)


def _dsa_attn(qs, ks, vs_t, bias_t, sgs, tq):
    b, t, d_b = qs.shape
    n_kv = ks.shape[2] // HEAD_DIM
    group = d_b // HEAD_DIM // n_kv
    n_kvs = DSA_KV_HEADS_PER_STEP if n_kv % DSA_KV_HEADS_PER_STEP == 0 else 1
    gw = n_kvs * group * HEAD_DIM
    kw = n_kvs * HEAD_DIM
    qmap = lambda bi, n, i: (bi, i, n)
    kmap = lambda bi, n, i: (bi, 0, n)
    return pl.pallas_call(
        functools.partial(_dsa_attn_kernel, tq=tq, group=group, n_kvs=n_kvs),
        grid=(b, n_kv // n_kvs, t // tq),
        in_specs=[
            pl.BlockSpec((1, tq, gw), qmap),
            pl.BlockSpec((1, t, kw), kmap),
            pl.BlockSpec((kw, t), lambda bi, n, i: (n, bi)),
            pl.BlockSpec((1, t, tq), lambda bi, n, i: (bi, 0, i)),
            pl.BlockSpec((1, tq, gw), qmap),
        ],
        out_specs=pl.BlockSpec((1, tq, gw), qmap),
        out_shape=jax.ShapeDtypeStruct((b, t, d_b), BF16),
        scratch_shapes=[
            pltpu.VMEM((t, kw), BF16),
            pltpu.VMEM((n_kvs, 2, tq, group * tq), F32),
            pltpu.VMEM((n_kvs, 1, group * tq), F32), pltpu.VMEM((n_kvs, 1, group * tq), F32),
            pltpu.VMEM((n_kvs, HEAD_DIM, group * tq), F32),
        ],
        compiler_params=_cparams(3),
        name="dsa_attn",
    )(qs, ks, vs_t, bias_t, sgs)


def _out_kernel(x_ref, a1_ref, a2_ref, p_ref, wo_ref, pw_ref, wg_ref, bg_ref, we_ref, y_ref, *, d_a):
    o = _dot(a1_ref[...], wo_ref[0:d_a, :]) + _dot(a2_ref[...], wo_ref[d_a:, :])
    x1 = x_ref[...] + _rms_rows(o, pw_ref[...])
    z = _dot(x1.astype(BF16), wg_ref[...]) + bg_ref[...]
    gate = 1.0 / (1.0 + jnp.exp(-z))
    y_ref[...] = x1 + gate * _dot(p_ref[...].astype(BF16), we_ref[...])


def _out_proj(x2d, a1, a2, p2d, wo, pw, wg, bg, we, tm):
    m, d = x2d.shape
    d_a = a1.shape[1]
    d_mix = wo.shape[0]
    d_ple = p2d.shape[1]
    row = lambda i: (i, 0)
    fixed = lambda i: (0, 0)
    single = dict(pipeline_mode=pl.Buffered(1))
    return pl.pallas_call(
        functools.partial(_out_kernel, d_a=d_a),
        grid=(m // tm,),
        in_specs=[
            pl.BlockSpec((tm, d), row),
            pl.BlockSpec((tm, d_a), row),
            pl.BlockSpec((tm, d_mix - d_a), row),
            pl.BlockSpec((tm, d_ple), row),
            pl.BlockSpec((d_mix, d), fixed, **single),
            pl.BlockSpec((1, d), fixed),
            pl.BlockSpec((d, d), fixed, **single),
            pl.BlockSpec((1, d), fixed),
            pl.BlockSpec((d_ple, d), fixed, **single),
        ],
        out_specs=pl.BlockSpec((tm, d), row),
        out_shape=jax.ShapeDtypeStruct((m, d), F32),
        compiler_params=_cparams(1),
        name="out_proj",
    )(x2d, a1, a2, p2d, wo, pw, wg, bg, we)


def _sample_scores_kernel(pt_ref, qi_ref, wi_ref, *rest, n_group):
    del pt_ref
    page_refs, out_ref = rest[:n_group], rest[n_group]
    q4 = qi_ref[0]
    w = wi_ref[0]
    for r in range(n_group):
        kp_t = page_refs[r][0].astype(BF16)
        rel = jnp.maximum(_dot(q4, kp_t), 0.0)
        out_ref[0, r:r + 1, :] = jnp.sum(rel * w, axis=0, keepdims=True)


def _sample_scores(page_table_flat, qi4, wi3, cache_idx_t, n_pages, n_group):
    nb = qi4.shape[0]
    page = cache_idx_t.shape[2]

    def page_map(r):
        return lambda bi, g, pt: (pt[bi * n_pages + g * n_group + r], 0, 0)

    grid_spec = pltpu.PrefetchScalarGridSpec(
        num_scalar_prefetch=1,
        grid=(nb, n_pages // n_group),
        in_specs=[
            pl.BlockSpec((1, H_IDX, D_IDX), lambda bi, g, pt: (bi, 0, 0)),
            pl.BlockSpec((1, H_IDX, 1), lambda bi, g, pt: (bi, 0, 0)),
        ] + [pl.BlockSpec((1, D_IDX, page), page_map(r)) for r in range(n_group)],
        out_specs=pl.BlockSpec((1, n_group, page), lambda bi, g, pt: (bi, g, 0)),
    )
    return pl.pallas_call(
        functools.partial(_sample_scores_kernel, n_group=n_group),
        grid_spec=grid_spec,
        out_shape=jax.ShapeDtypeStruct((nb, n_pages, page), F32),
        compiler_params=_cparams(2),
        name="sample_scores",
    )(page_table_flat, qi4, wi3, *([cache_idx_t] * n_group))


def _sample_select_kernel(sc_ref, qi_ref, kit_ref, wi_ref, bias_ref, key_ref, p_ref, *, k_sel, idx_bits):
    rows, s_past = sc_ref.shape
    s_all = key_ref.shape[1]
    prod = qi_ref[...].astype(F32) * kit_ref[...].astype(F32)
    lane = lax.broadcasted_iota(I32, prod.shape, 1)
    wi = wi_ref[...]
    new = jnp.zeros((rows, 1), F32)
    for h in range(H_IDX):
        dot_h = jnp.sum(jnp.where(lane // D_IDX == h, prod, 0.0), axis=-1, keepdims=True)
        new = new + jnp.maximum(dot_h, 0.0) * wi[:, h:h + 1]
    key_ref[:, 0:s_past] = sc_ref[...]
    tail_lane = lax.broadcasted_iota(I32, (rows, s_all - s_past), 1)
    key_ref[:, s_past:s_all] = jnp.where(tail_lane == 0, new, -jnp.inf)
    thr, _ = _select_topk(_valu_counter(key_ref, 1, s_all, 1), k_sel, axis=1, q_shape=p_ref.shape,
                          tie_index=(p_ref, idx_bits))
    bias_ref[...] = _selected_bias(key_ref[...], 0, thr, p_ref[...], 1)


def _sample_select(scores2d, qi, ki_tiled, wi, k_sel):
    rows, s_past = scores2d.shape
    s_all = s_past + LANES
    idx_bits = max(1, (s_all - 1).bit_length())
    return pl.pallas_call(
        functools.partial(_sample_select_kernel, k_sel=k_sel, idx_bits=idx_bits),
        out_shape=jax.ShapeDtypeStruct((rows, s_all), F32),
        scratch_shapes=[pltpu.VMEM((rows, s_all), F32), pltpu.VMEM((rows, 1), I32)],
        compiler_params=pltpu.CompilerParams(vmem_limit_bytes=VMEM_LIMIT),
        name="sample_select",
    )(scores2d, qi, ki_tiled, wi)


def _head_rows(row, n_rows, rows_per_head):
    return jnp.concatenate(
        [row[:, (c // rows_per_head) * HEAD_DIM:(c // rows_per_head + 1) * HEAD_DIM]
         for c in range(n_rows)], axis=0)


def _sample_attn_kernel(pt_ref, qa_ref, qs_ref, kan_ref, van_ref, ksn_ref, vsn_ref, bnew_ref,
                        ga_ref, gs_ref, subw_ref, lamp_ref, *rest, n_pg, n_ha, n_hb, group, lam_init):
    del pt_ref
    kd_refs, vd_refs = rest[0:n_pg], rest[n_pg:2 * n_pg]
    ks_refs, vs_refs = rest[2 * n_pg:3 * n_pg], rest[3 * n_pg:4 * n_pg]
    bias_ref, oa_ref, ob_ref, qd, qsb, md, ld, accd, ms, ls, accs = rest[4 * n_pg:]
    p = pl.program_id(1)
    n_p = pl.num_programs(1)
    n_kv = n_hb // group

    @pl.when(p == 0)
    def _():
        qd_f = _head_rows(qa_ref[0].astype(F32), 2 * n_ha, 2)
        rd = lax.broadcasted_iota(I32, qd_f.shape, 0)
        ln = lax.broadcasted_iota(I32, qd_f.shape, 1)
        qd_f = jnp.where(ln // DH_A == rd % 2, qd_f, 0.0)
        qd[...] = qd_f.astype(BF16)
        qs_f = _head_rows(qs_ref[0].astype(F32), n_hb, 1)
        qsb[...] = qs_f.astype(BF16)
        kan = _head_rows(kan_ref[0].astype(BF16).astype(F32), 2 * n_ha, 2)
        van = _head_rows(van_ref[0].astype(BF16).astype(F32), 2 * n_ha, 2)
        md[...] = jnp.sum(qd_f * kan, axis=-1, keepdims=True)
        ld[...] = jnp.ones(ld.shape, F32)
        accd[...] = van
        ksn = _head_rows(ksn_ref[0].astype(BF16).astype(F32), n_hb, group)
        vsn = _head_rows(vsn_ref[0].astype(BF16).astype(F32), n_hb, group)
        ms[...] = jnp.sum(qs_f * ksn, axis=-1, keepdims=True) + bnew_ref[0]
        ls[...] = jnp.ones(ls.shape, F32)
        accs[...] = vsn

    sd_list, vd_list, ss_list, vs_list = [], [], [], []
    for r in range(n_pg):
        kd = kd_refs[r][0].astype(BF16)
        sd = _dot_nt(qd[...], kd)
        rq = lax.broadcasted_iota(I32, sd.shape, 0)
        ck = lax.broadcasted_iota(I32, sd.shape, 1)
        sd_list.append(jnp.where(ck % n_ha == rq // 2, sd, NEG_BIG))
        vd_list.append(vd_refs[r][0].astype(BF16))
        ksp = ks_refs[r][0].astype(BF16)
        ss = _dot_nt(qsb[...], ksp) + bias_ref[0, r]
        rq = lax.broadcasted_iota(I32, ss.shape, 0)
        ck = lax.broadcasted_iota(I32, ss.shape, 1)
        ss_list.append(jnp.where(ck % n_kv == rq // group, ss, NEG_BIG))
        vs_list.append(vs_refs[r][0].astype(BF16))
    _flash_step_multi(sd_list, vd_list, md, ld, accd)
    _flash_step_multi(ss_list, vs_list, ms, ls, accs)

    @pl.when(p == n_p - 1)
    def _():
        lam = _lambda_full(lamp_ref[...], lam_init)
        od = accd[...] / ld[...]
        ga = ga_ref[0]
        subw = subw_ref[...]
        for h in range(n_ha):
            sl = slice(h * HEAD_DIM, (h + 1) * HEAD_DIM)
            o = od[2 * h:2 * h + 1, :] - lam * od[2 * h + 1:2 * h + 2, :]
            o = _rms_rows(o, subw) * (1.0 - lam_init)
            oa_ref[0, :, sl] = (o * ga[:, sl]).astype(BF16)
        os_ = accs[...] / ls[...]
        gs = gs_ref[0]
        for h in range(n_hb):
            sl = slice(h * HEAD_DIM, (h + 1) * HEAD_DIM)
            ob_ref[0, :, sl] = (os_[h:h + 1, :] * gs[:, sl]).astype(BF16)


def _sample_attn(page_table_flat, qa, qs, ka_new, va_new, ks_new, vs_new, bias_new, sga, sgs,
                 subw, lamp, cdk, cdv, csk, csv, bias_pages, n_pages, lam_init):
    nb, _, d_a = qa.shape
    d_b = qs.shape[2]
    d_kv = ks_new.shape[2]
    n_ha = d_a // HEAD_DIM
    n_hb = d_b // HEAD_DIM
    n_kv = d_kv // HEAD_DIM
    group = n_hb // n_kv
    rows_d = cdk.shape[1]
    rows_s = csk.shape[1]
    per_b = lambda bi, p, pt: (bi, 0, 0)
    fixed = lambda bi, p, pt: (0, 0)
    n_pg = SAMPLE_PAGES_PER_STEP if n_pages % SAMPLE_PAGES_PER_STEP == 0 else 1

    def paged(r):
        return lambda bi, p, pt: (pt[bi * n_pages + p * n_pg + r], 0, 0)

    grid_spec = pltpu.PrefetchScalarGridSpec(
        num_scalar_prefetch=1,
        grid=(nb, n_pages // n_pg),
        in_specs=[
            pl.BlockSpec((1, 1, d_a), per_b),
            pl.BlockSpec((1, 1, d_b), per_b),
            pl.BlockSpec((1, 1, d_a), per_b),
            pl.BlockSpec((1, 1, d_a), per_b),
            pl.BlockSpec((1, 1, d_kv), per_b),
            pl.BlockSpec((1, 1, d_kv), per_b),
            pl.BlockSpec((1, 1, 1), per_b),
            pl.BlockSpec((1, 1, d_a), per_b),
            pl.BlockSpec((1, 1, d_b), per_b),
            pl.BlockSpec((1, HEAD_DIM), fixed),
            pl.BlockSpec((4, DH_A), fixed),
        ] + [pl.BlockSpec((1, rows_d, HEAD_DIM), paged(r)) for r in range(n_pg)] * 2
        + [pl.BlockSpec((1, rows_s, HEAD_DIM), paged(r)) for r in range(n_pg)] * 2
        + [pl.BlockSpec((1, n_pg, 1, rows_s), lambda bi, p, pt: (bi, p, 0, 0))],
        out_specs=[pl.BlockSpec((1, 1, d_a), per_b), pl.BlockSpec((1, 1, d_b), per_b)],
        scratch_shapes=[
            pltpu.VMEM((2 * n_ha, HEAD_DIM), BF16), pltpu.VMEM((n_hb, HEAD_DIM), BF16),
            pltpu.VMEM((2 * n_ha, 1), F32), pltpu.VMEM((2 * n_ha, 1), F32),
            pltpu.VMEM((2 * n_ha, HEAD_DIM), F32),
            pltpu.VMEM((n_hb, 1), F32), pltpu.VMEM((n_hb, 1), F32), pltpu.VMEM((n_hb, HEAD_DIM), F32),
        ],
    )
    return pl.pallas_call(
        functools.partial(_sample_attn_kernel, n_pg=n_pg, n_ha=n_ha, n_hb=n_hb, group=group,
                          lam_init=lam_init),
        grid_spec=grid_spec,
        out_shape=[jax.ShapeDtypeStruct((nb, 1, d_a), BF16), jax.ShapeDtypeStruct((nb, 1, d_b), BF16)],
        compiler_params=_cparams(2),
        name="sample_attn",
    )(page_table_flat, qa, qs, ka_new, va_new, ks_new, vs_new, bias_new, sga, sgs, subw, lamp,
      *([cdk] * n_pg + [cdv] * n_pg + [csk] * n_pg + [csv] * n_pg), bias_pages)


def _row_tile(m, pref):
    return pref if m % pref == 0 else m


def _inproj_all(x2d, pos, wts, tm, emit_vt):
    tabs64 = _rope_tables(pos, DH_A)
    tabs128 = _rope_tables(pos, HEAD_DIM)
    d_a, d_b, d_kv, d_qi = wts["d_a"], wts["d_b"], wts["d_kv"], wts["d_qi"]
    outs_a = _inproj_a(x2d, wts["pre_w"], wts["w_a"], tabs64, tm, d_a, emit_vt)
    outs_b = _inproj_b(x2d, wts["pre_w"], wts["w_b"], wts["w_tail"], tabs128, tabs64, tm,
                       d_b, d_kv, d_qi, emit_vt)
    return outs_a, outs_b


def kernel(x_prompt, x_sample, p_prompt, p_sample, cache_diff_k, cache_diff_v, cache_dsa_k, cache_dsa_v, cache_idx_k, page_table, pre_norm_w, post_norm_w, w_in, lam_q1, lam_k1, lam_q2, lam_k2, diff_norm_w, w_out, w_ple_gate, b_ple_gate, w_ple_proj):
    depth = w_in.shape[0]
    assert depth == 1, "single-layer stack only"
    bsz, t_p, d = x_prompt.shape
    nb, t_s, _ = x_sample.shape
    assert t_s == 1, "one new token per sample sequence"
    n_pages = page_table.shape[1]
    n_pool = cache_diff_k.shape[1]
    page = cache_diff_k.shape[2]
    n_ha = cache_diff_k.shape[3]
    n_kv = cache_dsa_k.shape[3]
    d_a = n_ha * HEAD_DIM
    d_kv = n_kv * HEAD_DIM
    d_b = d - d_a
    d_qi = H_IDX * D_IDX
    past_len = n_pages * page
    lam_init = 0.8 - 0.6 * math.exp(-0.3 * 0)
    n_in = w_in.shape[2]

    n_main = 4 * d_a + 2 * d_b + 2 * d_kv + d_qi
    assert n_in - n_main == D_IDX + H_IDX
    w_a = jnp.swapaxes(w_in[0, :, :4 * d_a], 0, 1).astype(BF16)
    w_b = jnp.swapaxes(w_in[0, :, 4 * d_a:n_main], 0, 1).astype(BF16)
    w_tail = jnp.pad(jnp.swapaxes(w_in[0, :, n_main:], 0, 1),
                     ((0, LANES - (n_in - n_main)), (0, 0))).astype(BF16)
    wts = dict(pre_w=pre_norm_w[0][None, :], w_a=w_a, w_b=w_b, w_tail=w_tail,
               d_a=d_a, d_b=d_b, d_kv=d_kv, d_qi=d_qi)
    wo = w_out[0].astype(BF16)
    wg = w_ple_gate[0].astype(BF16)
    we = w_ple_proj[0].astype(BF16)
    pw = post_norm_w[0][None, :]
    bg = b_ple_gate[0][None, :]
    subw = diff_norm_w[0][None, :]
    lamp = jnp.stack([lam_q1[0], lam_k1[0], lam_q2[0], lam_k2[0]], axis=0)

    m_p = bsz * t_p
    tm = _row_tile(t_p, ROW_TILE)
    tq_diff = _row_tile(t_p, DIFF_TILE)
    tq_dsa = _row_tile(t_p, DSA_TILE)
    tq_idx = _row_tile(t_p, IDX_TILE)
    xp2 = x_prompt.reshape(m_p, d)
    pos_p = jnp.arange(t_p, dtype=I32)
    (qa, ka, va, sga, va_t), (qs, ks, vs, sgs, qi, kiw, vs_t) = _inproj_all(xp2, pos_p, wts, tm, True)
    r3 = lambda a: a.reshape(bsz, t_p, a.shape[-1])
    a1 = _diff_attn(r3(qa), r3(ka), va_t, r3(sga), subw, lamp, lam_init, tq_diff)
    ki = kiw[:, :D_IDX]
    wi = kiw[:, D_IDX:D_IDX + H_IDX]
    wi_t = wi.reshape(bsz, t_p, H_IDX).transpose(0, 2, 1)
    k_sel_p = min(TOPK_MAX, t_p // 4)
    bias_t = _index_select(r3(qi), r3(kiw), wi_t, tq_idx, k_sel_p)
    a2 = _dsa_attn(r3(qs), r3(ks), vs_t, bias_t, r3(sgs), tq_dsa)
    y_p = _out_proj(xp2, a1.reshape(m_p, d_a), a2.reshape(m_p, d_b), p_prompt[0].reshape(m_p, -1),
                    wo, pw, wg, bg, we, tm)

    xs2 = x_sample.reshape(nb, d)
    pos_s = jnp.full((nb,), past_len, dtype=I32)
    (qa_s, ka_s, va_s, sga_s), (qs_s, ks_s, vs_s, sgs_s, qi_s, kiw_s) = _inproj_all(xs2, pos_s, wts, nb, False)
    ki_s = kiw_s[:, :D_IDX]
    wi_s = kiw_s[:, D_IDX:D_IDX + H_IDX]
    pt_flat = page_table.reshape(-1)
    n_group = SCORE_PAGES_PER_STEP if n_pages % SCORE_PAGES_PER_STEP == 0 else 1
    cache_idx_t = jnp.swapaxes(cache_idx_k.reshape(n_pool, page, D_IDX), 1, 2)
    scores = _sample_scores(pt_flat, qi_s.reshape(nb, H_IDX, D_IDX), wi_s.reshape(nb, H_IDX, 1),
                            cache_idx_t, n_pages, n_group)
    k_sel_s = min(TOPK_MAX, (past_len + t_s) // 4)
    bias_s = _sample_select(scores.reshape(nb, past_len), qi_s,
                            jnp.tile(ki_s.astype(BF16), (1, H_IDX)), wi_s, k_sel_s)
    bias_pages = jnp.repeat(bias_s[:, :past_len], n_kv, axis=1).reshape(nb, n_pages, 1, page * n_kv)
    bias_new = bias_s[:, past_len:past_len + 1].reshape(nb, 1, 1)
    e3 = lambda a: a.reshape(nb, 1, a.shape[-1])
    a1_s, a2_s = _sample_attn(
        pt_flat, e3(qa_s), e3(qs_s), e3(ka_s), e3(va_s), e3(ks_s), e3(vs_s), bias_new,
        e3(sga_s), e3(sgs_s), subw, lamp,
        cache_diff_k.reshape(n_pool, page * n_ha, HEAD_DIM), cache_diff_v.reshape(n_pool, page * n_ha, HEAD_DIM),
        cache_dsa_k.reshape(n_pool, page * n_kv, HEAD_DIM), cache_dsa_v.reshape(n_pool, page * n_kv, HEAD_DIM),
        bias_pages, n_pages, lam_init)
    y_s = _out_proj(xs2, a1_s.reshape(nb, d_a), a2_s.reshape(nb, d_b), p_sample[0].reshape(nb, -1),
                    wo, pw, wg, bg, we, nb)

    return (
        y_p.reshape(bsz, t_p, d), y_s.reshape(nb, t_s, d),
        ka.reshape(1, bsz, t_p, n_ha, HEAD_DIM), va.reshape(1, bsz, t_p, n_ha, HEAD_DIM),
        ks.reshape(1, bsz, t_p, n_kv, HEAD_DIM), vs.reshape(1, bsz, t_p, n_kv, HEAD_DIM),
        ki.reshape(1, bsz, t_p, D_IDX),
        ka_s.reshape(1, nb, t_s, n_ha, HEAD_DIM), va_s.reshape(1, nb, t_s, n_ha, HEAD_DIM),
        ks_s.reshape(1, nb, t_s, n_kv, HEAD_DIM), vs_s.reshape(1, nb, t_s, n_kv, HEAD_DIM),
        ki_s.reshape(1, nb, t_s, D_IDX),
    )
```
